```python
import jax
import jax.numpy as jnp
from jax import lax
import numpy as np

D_MODEL = 2048
BATCH = 2
SEQ = 4096
DEPTH = 1
DEC_BATCH = 32
DEC_SEQ = 4
PAST_LEN = 16384
PAGE_SIZE = 128

HEAD_DIM = 128
NSA_HEADS = D_MODEL // (2 * HEAD_DIM)
NSA_KV_HEADS = 2
NSA_REP = NSA_HEADS // NSA_KV_HEADS
FOX_HEADS = D_MODEL // (4 * HEAD_DIM)
MEM_HEADS = 4
MEM_LEN = 256
CMP_BLOCK = 32
CMP_STRIDE = 16
CMP_HIDDEN = 256
SLC_BLOCK = 64
SLC_TOPK = 16
WINDOW = 512
Q_BLOCK = 128
ROT_DIM = HEAD_DIM // 4
ROPE_THETA = 500000.0
D_FF = 5632
CONV_W = 3
LN_EPS = 1e-5
FORGET_BIAS_INIT = 3.0
DN_ALPHA = (2 * DEPTH) ** 0.25
DN_BETA = (8 * DEPTH) ** -0.25
ATTN_SCALE = HEAD_DIM ** -0.5
NEG_INF = -1e30
FORCE_BONUS = 1e3
KV_W = 2 * NSA_KV_HEADS * HEAD_DIM
IN_WIDTHS = (NSA_HEADS * HEAD_DIM, KV_W, KV_W, KV_W, 3 * NSA_HEADS,
             3 * FOX_HEADS * HEAD_DIM, FOX_HEADS, MEM_HEADS * HEAD_DIM)
D_IN = sum(IN_WIDTHS)
MIX_W = (NSA_HEADS + FOX_HEADS + MEM_HEADS) * HEAD_DIM

kernel_name = 'hymba_nsa_fox_mem_convffn_step'


def layer_norm(x, g, b):
    xf = x.astype(jnp.float32)
    xc = xf - jnp.mean(xf, -1, keepdims=True)
    var = jnp.mean(xc * xc, -1, keepdims=True)
    return (xc * lax.rsqrt(var + LN_EPS) * g.astype(jnp.float32) + b.astype(jnp.float32)).astype(x.dtype)


def masked_softmax(s, mask):
    s = jnp.where(mask, s, NEG_INF)
    m = jnp.max(s, -1, keepdims=True)
    e = jnp.where(mask, jnp.exp(s - m), 0.0)
    return e / jnp.maximum(jnp.sum(e, -1, keepdims=True), 1e-30)


def partial_rope(x, pos):
    half = ROT_DIM // 2
    inv = jnp.power(ROPE_THETA, -jnp.arange(half, dtype=jnp.float32) * (2.0 / ROT_DIM))
    ang = pos.astype(jnp.float32)[:, None] * inv[None, :]
    cos = jnp.cos(ang)[:, None, :]
    sin = jnp.sin(ang)[:, None, :]
    x1 = x[..., :half].astype(jnp.float32)
    x2 = x[..., half:ROT_DIM].astype(jnp.float32)
    rot = jnp.concatenate([x1 * cos - x2 * sin, x1 * sin + x2 * cos], -1).astype(x.dtype)
    return jnp.concatenate([rot, x[..., ROT_DIM:]], -1)


def input_projection(x, w_in, b_gate, b_forget):
    B, T, _ = x.shape
    offs = []
    acc = 0
    for w in IN_WIDTHS[:-1]:
        acc += w
        offs.append(acc)
    p = jnp.einsum('btd,de->bte', x, w_in)
    q, kv_c, kv_s, kv_w, g, fqkv, f, qm = jnp.split(p, offs, axis=-1)
    kv_c = kv_c.reshape(B, T, 2, NSA_KV_HEADS, HEAD_DIM)
    kv_s = kv_s.reshape(B, T, 2, NSA_KV_HEADS, HEAD_DIM)
    kv_w = kv_w.reshape(B, T, 2, NSA_KV_HEADS, HEAD_DIM)
    gates = jax.nn.sigmoid((g + b_gate).astype(jnp.float32)).reshape(B, T, NSA_KV_HEADS, NSA_REP, 3)
    fqkv = fqkv.reshape(B, T, 3, FOX_HEADS, HEAD_DIM)
    logf = jax.nn.log_sigmoid((f + b_forget).astype(jnp.float32))
    return (q.reshape(B, T, NSA_HEADS, HEAD_DIM), kv_c[:, :, 0], kv_c[:, :, 1],
            kv_s[:, :, 0], kv_s[:, :, 1], kv_w[:, :, 0], kv_w[:, :, 1], gates,
            fqkv[:, :, 0], fqkv[:, :, 1], fqkv[:, :, 2], logf,
            qm.reshape(B, T, MEM_HEADS, HEAD_DIM))


def nsa_compress(k, w1, b1, w2, pos_emb):
    B, T, G, _ = k.shape
    kb = k.reshape(B, T // CMP_STRIDE, CMP_STRIDE, G, HEAD_DIM)
    first = jnp.einsum('bnlgd,ldm->bngm', kb, w1[:CMP_STRIDE])
    second = jnp.einsum('bnlgd,ldm->bngm', kb, w1[CMP_STRIDE:])
    pos_term = jnp.einsum('ld,ldm->m', pos_emb, w1) + b1
    h = jax.nn.gelu(first[:, :-1] + second[:, 1:] + pos_term)
    return jnp.einsum('bngm,md->bngd', h, w2)


def selection_blocks(k):
    B, T, G, D = k.shape
    ns = -(-T // SLC_BLOCK)
    k = jnp.pad(k, ((0, 0), (0, ns * SLC_BLOCK - T), (0, 0), (0, 0)))
    return k.reshape(B, ns, SLC_BLOCK, G, D).transpose(0, 3, 1, 2, 4)


def nsa_attend(q_raw, q_rot, q_pos, kc, vc, k_sb, v_sb, k_w, v_w, w_pos, gates):
    B, Q = q_raw.shape[:2]
    G = k_sb.shape[1]
    NC = kc.shape[1]
    NS = k_sb.shape[2]
    c_start = jnp.arange(NC, dtype=jnp.int32) * CMP_STRIDE
    mask_c = (c_start + CMP_BLOCK - 1)[None, :] <= q_pos[:, None]
    s_c = jnp.einsum('bqgrd,bngd->bgrqn', q_raw, kc).astype(jnp.float32) * ATTN_SCALE
    p_c = masked_softmax(s_c, mask_c)
    o_cmp = jnp.einsum('bgrqn,bngd->bqgrd', p_c.astype(vc.dtype), vc)
    b_start = jnp.arange(NS, dtype=jnp.int32) * SLC_BLOCK
    overlap = ((c_start[:, None] < b_start[None, :] + SLC_BLOCK) &
               (c_start[:, None] + CMP_BLOCK > b_start[None, :])).astype(jnp.float32)
    imp = jnp.einsum('bgrqn,ns->bgqs', p_c, overlap)
    blk = jnp.arange(NS, dtype=jnp.int32)[None, :]
    cur = (q_pos // SLC_BLOCK)[:, None]
    forced = (blk == 0) | (blk == cur) | (blk == cur - 1)
    valid = b_start[None, :] <= q_pos[:, None]
    score = jnp.where(valid, imp + jnp.where(forced, FORCE_BONUS, 0.0), NEG_INF)
    n_sel = min(SLC_TOPK, NS)
    _, idx = lax.top_k(score, n_sel)
    bi = jnp.arange(B)[:, None, None, None]
    gi = jnp.arange(G)[None, :, None, None]
    k_g = k_sb[bi, gi, idx]
    v_g = v_sb[bi, gi, idx]
    tok = idx[..., None] * SLC_BLOCK + jnp.arange(SLC_BLOCK, dtype=jnp.int32)
    mask_s = (tok <= q_pos[None, None, :, None, None]).reshape(B, G, 1, Q, n_sel * SLC_BLOCK)
    s_s = jnp.einsum('bqgrd,bgqkld->bgrqkl', q_rot, k_g).astype(jnp.float32) * ATTN_SCALE
    R = s_s.shape[2]
    p_s = masked_softmax(s_s.reshape(B, G, R, Q, n_sel * SLC_BLOCK), mask_s)
    p_s = p_s.reshape(B, G, R, Q, n_sel, SLC_BLOCK)
    o_slc = jnp.einsum('bgrqkl,bgqkld->bqgrd', p_s.astype(v_g.dtype), v_g)
    dist = q_pos[:, None] - w_pos[None, :]
    mask_w = (dist >= 0) & (dist < WINDOW) & (w_pos[None, :] >= 0)
    s_w = jnp.einsum('bqgrd,bwgd->bgrqw', q_rot, k_w).astype(jnp.float32) * ATTN_SCALE
    p_w = masked_softmax(s_w, mask_w)
    o_swa = jnp.einsum('bgrqw,bwgd->bqgrd', p_w.astype(v_w.dtype), v_w)
    o = gates[..., 0:1] * o_cmp + gates[..., 1:2] * o_slc + gates[..., 2:3] * o_swa
    return o.astype(q_raw.dtype)


def reverse_decay(logf):
    return lax.cumsum(logf, axis=1, reverse=True) - logf


def fox_attend(q, d_q, q_pos, k, v, d_k, k_pos):
    s = jnp.einsum('bqhd,bkhd->bhqk', q, k).astype(jnp.float32) * ATTN_SCALE
    bias = jnp.swapaxes(d_k, 1, 2)[:, :, None, :] - jnp.swapaxes(d_q, 1, 2)[:, :, :, None]
    p = masked_softmax(s + bias, k_pos[None, :] <= q_pos[:, None])
    return jnp.einsum('bhqk,bkhd->bqhd', p.astype(v.dtype), v)


def mem_attend(q, mk, mv):
    s = jnp.einsum('bqhd,bmhd->bhqm', q, mk).astype(jnp.float32) * ATTN_SCALE
    p = jax.nn.softmax(s, axis=-1)
    return jnp.einsum('bhqm,bmhd->bqhd', p.astype(mv.dtype), mv)


def residual_and_ffn(x, o_nsa, o_fox, o_mem, conv_buf, w_out, ln1_g, ln1_b,
                     w_up, conv_w, conv_b, w_down, ln2_g, ln2_b):
    B, T, _ = x.shape
    mix = jnp.concatenate([o_nsa.reshape(B, T, -1), o_fox.reshape(B, T, -1),
                           o_mem.reshape(B, T, -1)], -1)
    h = layer_norm(DN_ALPHA * x + jnp.einsum('bte,ed->btd', mix, w_out), ln1_g, ln1_b)
    a, b = jnp.split(jnp.einsum('btd,df->btf', h, w_up), 2, axis=-1)
    a_ext = jnp.concatenate([conv_buf.astype(a.dtype), a], 1)
    c = conv_b
    for j in range(CONV_W):
        c = c + a_ext[:, j:j + T] * conv_w[j]
    y = jnp.einsum('btf,fd->btd', jax.nn.silu(c) * b, w_down)
    out = layer_norm(DN_ALPHA * h + y, ln2_g, ln2_b)
    return out, a_ext[:, -(CONV_W - 1):]


def to_blocks(a, nb):
    return jnp.moveaxis(a.reshape((a.shape[0], nb, Q_BLOCK) + a.shape[2:]), 1, 0)


def from_blocks(a):
    a = jnp.moveaxis(a, 0, 1)
    return a.reshape((a.shape[0], a.shape[1] * a.shape[2]) + a.shape[3:])


def gather_pages(pool, l, page_table):
    g = pool[l, page_table]
    return g.reshape((page_table.shape[0], page_table.shape[1] * pool.shape[2]) + pool.shape[3:])


def prompt_layer(x, mem, lw):
    (w_in, b_gate, b_forget, cmp_w1, cmp_b1, cmp_w2, cmp_pos, w_mem_kv, w_out,
     ln1_g, ln1_b, w_up, conv_w, conv_b, w_down, ln2_g, ln2_b) = lw
    B, T, _ = x.shape
    nb = T // Q_BLOCK
    pos = jnp.arange(T, dtype=jnp.int32)
    (q, k_cmp, v_cmp, k_slc, v_slc, k_swa, v_swa, gates, fq, fk, fv, logf,
     q_mem) = input_projection(x, w_in, b_gate, b_forget)
    q_rot = partial_rope(q, pos)
    k_slc = partial_rope(k_slc, pos)
    k_swa = partial_rope(k_swa, pos)
    kc = nsa_compress(k_cmp, cmp_w1[0], cmp_b1[0], cmp_w2[0], cmp_pos[0])
    vc = nsa_compress(v_cmp, cmp_w1[1], cmp_b1[1], cmp_w2[1], cmp_pos[1])
    ksb = selection_blocks(k_slc)
    vsb = selection_blocks(v_slc)
    pad = ((0, 0), (WINDOW, 0), (0, 0), (0, 0))
    kw_pad = jnp.pad(k_swa, pad)
    vw_pad = jnp.pad(v_swa, pad)
    grp = lambda a: a.reshape(B, T, NSA_KV_HEADS, NSA_REP, HEAD_DIM)

    def nsa_step(args):
        i, qr, qo, g = args
        s0 = i * Q_BLOCK
        q_pos = s0 + jnp.arange(Q_BLOCK, dtype=jnp.int32)
        kw = lax.dynamic_slice_in_dim(kw_pad, s0, WINDOW + Q_BLOCK, axis=1)
        vw = lax.dynamic_slice_in_dim(vw_pad, s0, WINDOW + Q_BLOCK, axis=1)
        w_pos = s0 - WINDOW + jnp.arange(WINDOW + Q_BLOCK, dtype=jnp.int32)
        return nsa_attend(qr, qo, q_pos, kc, vc, ksb, vsb, kw, vw, w_pos, g)

    blk = jnp.arange(nb, dtype=jnp.int32)
    o_nsa = from_blocks(lax.map(nsa_step, (blk, to_blocks(grp(q), nb),
                                           to_blocks(grp(q_rot), nb), to_blocks(gates, nb))))
    d_f = reverse_decay(logf)

    def fox_step(args):
        i, qb, db = args
        q_pos = i * Q_BLOCK + jnp.arange(Q_BLOCK, dtype=jnp.int32)
        return fox_attend(qb, db, q_pos, fk, fv, d_f, pos)

    o_fox = from_blocks(lax.map(fox_step, (blk, to_blocks(fq, nb), to_blocks(d_f, nb))))
    mem_kv = jnp.einsum('bmd,de->bme', mem, w_mem_kv).reshape(B, mem.shape[1], 2, MEM_HEADS, HEAD_DIM)
    o_mem = mem_attend(q_mem, mem_kv[:, :, 0], mem_kv[:, :, 1])
    conv0 = jnp.zeros((B, CONV_W - 1, D_FF), x.dtype)
    y, conv_state = residual_and_ffn(x, o_nsa, o_fox, o_mem, conv0, w_out, ln1_g, ln1_b,
                                     w_up, conv_w, conv_b, w_down, ln2_g, ln2_b)
    n_win = min(WINDOW, T)
    states = (jnp.stack([k_cmp, v_cmp], 2), jnp.stack([k_slc, v_slc], 2),
              jnp.stack([fk, fv], 2), logf.astype(x.dtype),
              jnp.stack([k_swa, v_swa], 2)[:, T - n_win:], mem_kv, conv_state)
    return y, states


def sample_layer(x, l, c_cmp, c_slc, c_fox, c_logf, c_swa, c_mem, s_conv, page_table, lw):
    (w_in, b_gate, b_forget, cmp_w1, cmp_b1, cmp_w2, cmp_pos, w_mem_kv, w_out,
     ln1_g, ln1_b, w_up, conv_w, conv_b, w_down, ln2_g, ln2_b) = lw
    B, S, _ = x.shape
    past = page_table.shape[1] * c_cmp.shape[2]
    t_all = past + S
    q_pos = past + jnp.arange(S, dtype=jnp.int32)
    (q, k_cmp, v_cmp, k_slc, v_slc, k_swa, v_swa, gates, fq, fk, fv, logf,
     q_mem) = input_projection(x, w_in, b_gate, b_forget)
    q_rot = partial_rope(q, q_pos)
    k_slc = partial_rope(k_slc, q_pos)
    k_swa = partial_rope(k_swa, q_pos)
    past_cmp = gather_pages(c_cmp, l, page_table)
    n16 = (t_all // CMP_STRIDE) * CMP_STRIDE
    k_cmp_all = jnp.concatenate([past_cmp[:, :, 0], k_cmp], 1)[:, :n16]
    v_cmp_all = jnp.concatenate([past_cmp[:, :, 1], v_cmp], 1)[:, :n16]
    kc = nsa_compress(k_cmp_all, cmp_w1[0], cmp_b1[0], cmp_w2[0], cmp_pos[0])
    vc = nsa_compress(v_cmp_all, cmp_w1[1], cmp_b1[1], cmp_w2[1], cmp_pos[1])
    past_slc = gather_pages(c_slc, l, page_table)
    ksb = selection_blocks(jnp.concatenate([past_slc[:, :, 0], k_slc], 1))
    vsb = selection_blocks(jnp.concatenate([past_slc[:, :, 1], v_slc], 1))
    swa = c_swa[l]
    w_buf = swa.shape[1]
    kw = jnp.concatenate([swa[:, :, 0], k_swa], 1)
    vw = jnp.concatenate([swa[:, :, 1], v_swa], 1)
    w_pos = past - w_buf + jnp.arange(w_buf + S, dtype=jnp.int32)
    grp = lambda a: a.reshape(B, S, NSA_KV_HEADS, NSA_REP, HEAD_DIM)
    o_nsa = nsa_attend(grp(q), grp(q_rot), q_pos, kc, vc, ksb, vsb, kw, vw, w_pos, gates)
    past_fox = gather_pages(c_fox, l, page_table)
    past_logf = gather_pages(c_logf, l, page_table).astype(jnp.float32)
    d_all = reverse_decay(jnp.concatenate([past_logf, logf], 1))
    k_all = jnp.concatenate([past_fox[:, :, 0], fk], 1)
    v_all = jnp.concatenate([past_fox[:, :, 1], fv], 1)
    o_fox = fox_attend(fq, d_all[:, past:], q_pos, k_all, v_all, d_all,
                       jnp.arange(t_all, dtype=jnp.int32))
    mem = c_mem[l]
    o_mem = mem_attend(q_mem, mem[:, :, 0], mem[:, :, 1])
    y, conv_state = residual_and_ffn(x, o_nsa, o_fox, o_mem, s_conv[l], w_out, ln1_g, ln1_b,
                                     w_up, conv_w, conv_b, w_down, ln2_g, ln2_b)
    new_swa = jnp.concatenate([swa, jnp.stack([k_swa, v_swa], 2).astype(swa.dtype)], 1)[:, S:]
    states = (jnp.stack([k_cmp, v_cmp], 2), jnp.stack([k_slc, v_slc], 2),
              jnp.stack([fk, fv], 2), logf.astype(x.dtype), new_swa, conv_state)
    return y, states


def setup_inputs(seed: int = 0) -> dict:
    key = jax.random.key(seed)
    ks = jax.random.split(key, 32)
    f32 = jnp.float32
    n_pages = PAST_LEN // PAGE_SIZE
    n_pool = (5 * DEC_BATCH * n_pages) // 4
    swa_buf = min(WINDOW, PAST_LEN)
    nrm = lambda k, shape: jax.random.normal(k, shape, f32)
    page_table = jax.random.permutation(ks[10], n_pool)[:DEC_BATCH * n_pages]
    page_table = page_table.reshape(DEC_BATCH, n_pages).astype(jnp.int32)
    return {
        'x_prompt': nrm(ks[0], (BATCH, SEQ, D_MODEL)),
        'x_sample': nrm(ks[1], (DEC_BATCH, DEC_SEQ, D_MODEL)),
        'mem_prompt': nrm(ks[2], (BATCH, MEM_LEN, D_MODEL)),
        'cache_nsa_cmp': nrm(ks[3], (DEPTH, n_pool, PAGE_SIZE, 2, NSA_KV_HEADS, HEAD_DIM)),
        'cache_nsa_slc': nrm(ks[4], (DEPTH, n_pool, PAGE_SIZE, 2, NSA_KV_HEADS, HEAD_DIM)),
        'cache_fox_kv': nrm(ks[5], (DEPTH, n_pool, PAGE_SIZE, 2, FOX_HEADS, HEAD_DIM)),
        'cache_fox_logf': jax.nn.log_sigmoid(FORGET_BIAS_INIT + nrm(ks[6], (DEPTH, n_pool, PAGE_SIZE, FOX_HEADS))),
        'cache_nsa_swa': nrm(ks[7], (DEPTH, DEC_BATCH, swa_buf, 2, NSA_KV_HEADS, HEAD_DIM)),
        'cache_mem': nrm(ks[8], (DEPTH, DEC_BATCH, MEM_LEN, 2, MEM_HEADS, HEAD_DIM)),
        'state_conv': nrm(ks[9], (DEPTH, DEC_BATCH, CONV_W - 1, D_FF)),
        'page_table': page_table,
        'w_in': nrm(ks[11], (DEPTH, D_MODEL, D_IN)) * D_MODEL ** -0.5,
        'b_gate': 0.1 * nrm(ks[12], (DEPTH, 3 * NSA_HEADS)),
        'b_forget': FORGET_BIAS_INIT + 0.5 * nrm(ks[13], (DEPTH, FOX_HEADS)),
        'cmp_w1': nrm(ks[14], (DEPTH, 2, CMP_BLOCK, HEAD_DIM, CMP_HIDDEN)) * (CMP_BLOCK * HEAD_DIM) ** -0.5,
        'cmp_b1': 0.02 * nrm(ks[15], (DEPTH, 2, CMP_HIDDEN)),
        'cmp_w2': nrm(ks[16], (DEPTH, 2, CMP_HIDDEN, HEAD_DIM)) * CMP_HIDDEN ** -0.5,
        'cmp_pos': 0.5 * nrm(ks[17], (DEPTH, 2, CMP_BLOCK, HEAD_DIM)),
        'w_mem_kv': nrm(ks[18], (DEPTH, D_MODEL, 2 * MEM_HEADS * HEAD_DIM)) * D_MODEL ** -0.5,
        'w_out': nrm(ks[19], (DEPTH, MIX_W, D_MODEL)) * (MIX_W ** -0.5 * DN_BETA),
        'ln1_g': 1.0 + 0.02 * nrm(ks[20], (DEPTH, D_MODEL)),
        'ln1_b': 0.02 * nrm(ks[21], (DEPTH, D_MODEL)),
        'w_up': nrm(ks[22], (DEPTH, D_MODEL, 2 * D_FF)) * D_MODEL ** -0.5,
        'conv_w': nrm(ks[23], (DEPTH, CONV_W, D_FF)) * CONV_W ** -0.5,
        'conv_b': 0.02 * nrm(ks[24], (DEPTH, D_FF)),
        'w_down': nrm(ks[25], (DEPTH, D_FF, D_MODEL)) * (D_FF ** -0.5 * DN_BETA),
        'ln2_g': 1.0 + 0.02 * nrm(ks[26], (DEPTH, D_MODEL)),
        'ln2_b': 0.02 * nrm(ks[27], (DEPTH, D_MODEL)),
    }


def reference(x_prompt, x_sample, mem_prompt, cache_nsa_cmp, cache_nsa_slc, cache_fox_kv,
              cache_fox_logf, cache_nsa_swa, cache_mem, state_conv, page_table,
              w_in, b_gate, b_forget, cmp_w1, cmp_b1, cmp_w2, cmp_pos, w_mem_kv, w_out,
              ln1_g, ln1_b, w_up, conv_w, conv_b, w_down, ln2_g, ln2_b):
    yp = x_prompt
    ys = x_sample
    acc_p = [[] for _ in range(7)]
    acc_s = [[] for _ in range(6)]
    for l in range(DEPTH):
        lw = (w_in[l], b_gate[l], b_forget[l], cmp_w1[l], cmp_b1[l], cmp_w2[l], cmp_pos[l],
              w_mem_kv[l], w_out[l], ln1_g[l], ln1_b[l], w_up[l], conv_w[l], conv_b[l],
              w_down[l], ln2_g[l], ln2_b[l])
        yp, st_p = prompt_layer(yp, mem_prompt, lw)
        ys, st_s = sample_layer(ys, l, cache_nsa_cmp, cache_nsa_slc, cache_fox_kv, cache_fox_logf,
                                cache_nsa_swa, cache_mem, state_conv, page_table, lw)
        for lst, a in zip(acc_p, st_p):
            lst.append(a)
        for lst, a in zip(acc_s, st_s):
            lst.append(a)
    p_nsa_cmp, p_nsa_slc, p_fox_kv, p_fox_logf, p_nsa_swa, p_mem, p_conv = [jnp.stack(a, 0) for a in acc_p]
    s_nsa_cmp, s_nsa_slc, s_fox_kv, s_fox_logf, s_nsa_swa, s_conv = [jnp.stack(a, 0) for a in acc_s]
    return (yp, ys, p_nsa_cmp, p_nsa_slc, p_fox_kv, p_fox_logf, p_nsa_swa, p_mem, p_conv,
            s_nsa_cmp, s_nsa_slc, s_fox_kv, s_fox_logf, s_nsa_swa, s_conv)
```

```python
import functools

import jax
import jax.numpy as jnp
import numpy as np
from jax import lax
from jax.experimental import pallas as pl
from jax.experimental.pallas import tpu as pltpu

HEAD_DIM = 128
NSA_KV_HEADS = 2
NSA_REP = 4
NSA_HEADS = NSA_KV_HEADS * NSA_REP
FOX_HEADS = 4
MEM_HEADS = 4
CMP_BLOCK = 32
CMP_STRIDE = 16
CMP_HIDDEN = 256
SLC_BLOCK = 64
SLC_TOPK = 16
WINDOW = 512
Q_BLOCK = 128
ROT_DIM = HEAD_DIM // 4
ROPE_THETA = 500000.0
CONV_W = 3
LN_EPS = 1e-5
ATTN_SCALE = HEAD_DIM ** -0.5
NEG_INF = -1e30
FORCE_BONUS = 1e3

LANE = 128
SUBLANE = 8
SROW = SUBLANE
PAGES_PER_STEP = 16
VMEM_LIMIT = 56 * 1024 * 1024

COL_Q = 0
COL_KVC = 8
COL_KVS = 12
COL_KVW = 16
COL_FOX = 20
COL_QM = 32
MAIN_W = 36 * LANE
IN_TILE = 512

BF = jnp.bfloat16
F32 = jnp.float32


def _dot(a, b):
    return jnp.dot(a.astype(BF), b.astype(BF), preferred_element_type=F32)


def _dot_nt(a, b):
    return lax.dot_general(a.astype(BF), b.astype(BF), (((1,), (1,)), ((), ())),
                           preferred_element_type=F32)


def _dot_exact01(x, m01):
    hi = x.astype(BF)
    r1 = x - hi.astype(F32)
    mid = r1.astype(BF)
    lo = (r1 - mid.astype(F32)).astype(BF)
    m = m01.astype(BF)
    d = lambda a: jnp.dot(a, m, preferred_element_type=F32)
    return d(hi) + d(mid) + d(lo)


def _cparams(sem):
    return pltpu.CompilerParams(dimension_semantics=sem, vmem_limit_bytes=VMEM_LIMIT)


def _masked_softmax(s, mask):
    s = jnp.where(mask, s, NEG_INF)
    m = jnp.max(s, -1, keepdims=True)
    e = jnp.where(mask, jnp.exp(s - m), 0.0)
    return e / jnp.maximum(jnp.sum(e, -1, keepdims=True), 1e-30)


def _online_update(carry, s, mask, v):
    m, l, acc = carry
    if mask is not None:
        s = jnp.where(mask, s, NEG_INF)
    m_new = jnp.maximum(m, jnp.max(s, -1, keepdims=True))
    p = jnp.exp(s - m_new)
    if mask is not None:
        p = jnp.where(mask, p, 0.0)
    alpha = jnp.exp(m - m_new)
    l_new = alpha * l + jnp.sum(p, -1, keepdims=True)
    lead = p.shape[:-1]
    pv = _dot(p.reshape((-1, p.shape[-1])), v).reshape(lead + (v.shape[-1],))
    return m_new, l_new, alpha * acc + pv


def _rope_tile(x, c, s):
    lane = lax.broadcasted_iota(jnp.int32, x.shape, 1)
    half = ROT_DIM // 2
    swapped = jnp.where(lane < half, pltpu.roll(x, LANE - half, 1), pltpu.roll(x, half, 1))
    return x * c + swapped * s


def _topk_select(score, n_sel, n_valid_cols):
    col = lax.broadcasted_iota(jnp.int32, score.shape, 1)
    rank = jnp.zeros(score.shape, F32)
    for i in range(n_valid_cols):
        ci = score[:, i:i + 1]
        rank = rank + jnp.where(col > i, jnp.where(ci >= score, 1.0, 0.0),
                                jnp.where(ci > score, 1.0, 0.0))
    return jnp.where(rank < n_sel, 1.0, 0.0)


def _inproj_kernel(x_ref, w_ref, ws_ref, bs_ref, c_ref, s_ref,
                   p_ref, qrot_ref, sig_ref, lsig_ref, xb_ref):
    j = pl.program_id(1)

    @pl.when(j == 0)
    def _():
        xb_ref[...] = x_ref[...].astype(BF)
        z = jnp.dot(xb_ref[...], ws_ref[...], preferred_element_type=F32) + bs_ref[...]
        sig_ref[...] = jax.nn.sigmoid(z[:, :2 * LANE])
        zf = z[:, 2 * LANE:]
        lsig_ref[...] = jnp.minimum(zf, 0.0) - jnp.log1p(jnp.exp(-jnp.abs(zf)))

    acc = jnp.dot(xb_ref[...], w_ref[...], preferred_element_type=F32)
    c = c_ref[...]
    s = s_ref[...]
    heads = IN_TILE // LANE

    @pl.when(j < 2)
    def _():
        p_ref[...] = acc
        for h in range(heads):
            qrot_ref[:, h * LANE:(h + 1) * LANE] = _rope_tile(acc[:, h * LANE:(h + 1) * LANE], c, s)

    @pl.when((j == COL_KVS // heads) | (j == COL_KVW // heads))
    def _():
        for h in range(heads):
            t = acc[:, h * LANE:(h + 1) * LANE]
            p_ref[:, h * LANE:(h + 1) * LANE] = _rope_tile(t, c, s) if h < NSA_KV_HEADS else t

    @pl.when((j >= 2) & (j != COL_KVS // heads) & (j != COL_KVW // heads))
    def _():
        p_ref[...] = acc


def _input_projection(x2d, w_main, w_small, b_small, rope_c, rope_s, tm):
    m, d = x2d.shape
    nj = MAIN_W // IN_TILE
    return pl.pallas_call(
        _inproj_kernel,
        grid=(m // tm, nj),
        in_specs=[
            pl.BlockSpec((tm, d), lambda i, j: (i, 0)),
            pl.BlockSpec((d, IN_TILE), lambda i, j: (0, j)),
            pl.BlockSpec((d, 3 * LANE), lambda i, j: (0, 0)),
            pl.BlockSpec((1, 3 * LANE), lambda i, j: (0, 0)),
            pl.BlockSpec((tm, LANE), lambda i, j: (i, 0)),
            pl.BlockSpec((tm, LANE), lambda i, j: (i, 0)),
        ],
        out_specs=[
            pl.BlockSpec((tm, IN_TILE), lambda i, j: (i, j)),
            pl.BlockSpec((tm, IN_TILE), lambda i, j: (i, jnp.minimum(j, 1))),
            pl.BlockSpec((tm, 2 * LANE), lambda i, j: (i, 0)),
            pl.BlockSpec((tm, LANE), lambda i, j: (i, 0)),
        ],
        out_shape=[
            jax.ShapeDtypeStruct((m, MAIN_W), F32),
            jax.ShapeDtypeStruct((m, NSA_HEADS * HEAD_DIM), F32),
            jax.ShapeDtypeStruct((m, 2 * LANE), F32),
            jax.ShapeDtypeStruct((m, LANE), F32),
        ],
        scratch_shapes=[pltpu.VMEM((tm, d), BF)],
        compiler_params=_cparams(("arbitrary", "arbitrary")),
        name="input_projection",
    )(x2d, w_main, w_small, b_small, rope_c, rope_s)


def _gelu_tanh(x):
    k = np.sqrt(2.0 / np.pi).astype(np.float32)
    return x * (0.5 * (1.0 + jnp.tanh(k * (x + 0.044715 * (x ** 3)))))


def _compress_body(page_refs, w1_ref, pe_ref, b1_ref, w2_ref, out_ref, carry_ref, first_step):
    rows = [r.shape[0] // CMP_STRIDE for r in page_refs[0]]
    nb = sum(rows)

    @pl.when(first_step)
    def _():
        carry_ref[...] = jnp.zeros(carry_ref.shape, F32)

    row = lax.broadcasted_iota(jnp.int32, (nb, CMP_HIDDEN), 0)
    for kv in range(2):
        xs = []
        for g in range(NSA_KV_HEADS):
            per_l = []
            for l in range(CMP_STRIDE):
                parts = [r[pl.ds(l, n, stride=CMP_STRIDE), :]
                         for r, n in zip(page_refs[kv * NSA_KV_HEADS + g], rows)]
                per_l.append(parts[0] if len(parts) == 1 else jnp.concatenate(parts, 0))
            xs.append(jnp.concatenate(per_l, 1).astype(BF))
        xs.append(pe_ref[kv].astype(BF))
        fs = jnp.dot(jnp.concatenate(xs, 0), w1_ref[kv], preferred_element_type=F32)
        pos = (fs[2 * nb:2 * nb + 1, :CMP_HIDDEN] + fs[2 * nb + 1:2 * nb + 2, CMP_HIDDEN:]
               + b1_ref[kv])
        for g in range(NSA_KV_HEADS):
            col = (kv * NSA_KV_HEADS + g) * HEAD_DIM
            first = fs[g * nb:(g + 1) * nb, :CMP_HIDDEN]
            second = fs[g * nb:(g + 1) * nb, CMP_HIDDEN:]
            slot = kv * NSA_KV_HEADS + g
            prev = jnp.where(row == 0, carry_ref[slot], pltpu.roll(first, 1, 0))
            carry_ref[slot] = first[nb - 1:nb, :]
            h = _gelu_tanh(prev + second + pos)
            out_ref[:, col:col + HEAD_DIM] = jnp.dot(h.astype(BF), w2_ref[kv],
                                                     preferred_element_type=F32)


N_KV_SLOTS = 2 * NSA_KV_HEADS


def _compress_prompt_kernel(*refs):
    slots = [[r] for r in refs[:N_KV_SLOTS]]
    w1_ref, pe_ref, b1_ref, w2_ref, out_ref, carry_ref = refs[N_KV_SLOTS:]
    _compress_body(slots, w1_ref, pe_ref, b1_ref, w2_ref, out_ref, carry_ref,
                   pl.program_id(1) == 0)


def _compress_paged_kernel(pt_ref, *refs):
    del pt_ref
    n = N_KV_SLOTS * PAGES_PER_STEP
    slots = [list(refs[s * PAGES_PER_STEP:(s + 1) * PAGES_PER_STEP]) for s in range(N_KV_SLOTS)]
    w1_ref, pe_ref, b1_ref, w2_ref, out_ref, carry_ref = refs[n:]
    _compress_body(slots, w1_ref, pe_ref, b1_ref, w2_ref, out_ref, carry_ref,
                   pl.program_id(1) == 0)


def _compress_weight_specs():
    const3 = (lambda *a: (0, 0, 0))
    return [
        pl.BlockSpec((2, CMP_STRIDE * HEAD_DIM, 2 * CMP_HIDDEN), const3),
        pl.BlockSpec((2, SUBLANE, CMP_STRIDE * HEAD_DIM), const3),
        pl.BlockSpec((2, 1, CMP_HIDDEN), const3),
        pl.BlockSpec((2, CMP_HIDDEN, HEAD_DIM), const3),
    ]


def _compress_prompt(p_main, batch, seq, cw):
    chunk = min(seq, 2048)
    nchunk = seq // chunk
    kvw = 2 * NSA_KV_HEADS * HEAD_DIM
    return pl.pallas_call(
        _compress_prompt_kernel,
        grid=(batch, nchunk),
        in_specs=[pl.BlockSpec((chunk, HEAD_DIM), (lambda b, c, s=s: (b * nchunk + c, COL_KVC + s)))
                  for s in range(N_KV_SLOTS)] + _compress_weight_specs(),
        out_specs=pl.BlockSpec((None, chunk // CMP_STRIDE, kvw), lambda b, c: (b, c, 0)),
        out_shape=jax.ShapeDtypeStruct((batch, seq // CMP_STRIDE, kvw), F32),
        scratch_shapes=[pltpu.VMEM((2 * NSA_KV_HEADS, 1, CMP_HIDDEN), F32)],
        compiler_params=_cparams(("arbitrary", "arbitrary")),
        name="nsa_compress_prompt",
    )(*([p_main] * N_KV_SLOTS), *cw)


def _page_specs(page_rows, width, chunk_of, col=0):
    def spec(i):
        return pl.BlockSpec((None, page_rows, width),
                            lambda b, c, pt: (pt[b, chunk_of(c) * PAGES_PER_STEP + i], 0, col))
    return [spec(i) for i in range(PAGES_PER_STEP)]


def _compress_paged(cache2d, page_table, cw):
    n_pool, page, kvw = cache2d.shape
    dec_batch, n_pages = page_table.shape
    nchunk = n_pages // PAGES_PER_STEP
    rows = PAGES_PER_STEP * page // CMP_STRIDE
    page_specs = []
    for s in range(N_KV_SLOTS):
        page_specs += _page_specs(page, HEAD_DIM, lambda c: c, col=s)
    return pl.pallas_call(
        _compress_paged_kernel,
        grid_spec=pltpu.PrefetchScalarGridSpec(
            num_scalar_prefetch=1,
            grid=(dec_batch, nchunk),
            in_specs=page_specs + _compress_weight_specs(),
            out_specs=pl.BlockSpec((None, rows, kvw), lambda b, c, pt: (b, c, 0)),
            scratch_shapes=[pltpu.VMEM((2 * NSA_KV_HEADS, 1, CMP_HIDDEN), F32)],
        ),
        out_shape=jax.ShapeDtypeStruct((dec_batch, nchunk * rows, kvw), F32),
        compiler_params=_cparams(("arbitrary", "arbitrary")),
        name="nsa_compress_paged",
    )(page_table, *([cache2d] * (N_KV_SLOTS * PAGES_PER_STEP)), *cw)


def _overlap_matrix(n_rows, n_blocks):
    n = lax.broadcasted_iota(jnp.int32, (n_rows, n_blocks), 0)
    s = lax.broadcasted_iota(jnp.int32, (n_rows, n_blocks), 1)
    c0 = (n - 1) * CMP_STRIDE
    hit = (n >= 1) & (c0 < s * SLC_BLOCK + SLC_BLOCK) & (c0 + CMP_BLOCK > s * SLC_BLOCK)
    return jnp.where(hit, 1.0, 0.0)


def _block_scores(imp, q_pos, n_blocks):
    blk = lax.broadcasted_iota(jnp.int32, imp.shape, 1)
    cur = q_pos // SLC_BLOCK
    forced = (blk == 0) | (blk == cur) | (blk == cur - 1)
    valid = (blk * SLC_BLOCK <= q_pos) & (blk < n_blocks)
    return jnp.where(valid, imp + jnp.where(forced, FORCE_BONUS, 0.0), NEG_INF)


def _stack_heads(ref, n):
    return jnp.concatenate([ref[:, r * HEAD_DIM:(r + 1) * HEAD_DIM] for r in range(n)], 0)


def _nsa_prompt_kernel(qraw_ref, qrot_ref, kc_ref, vc_ref, ks_ref, vs_ref, kw_ref, vw_ref,
                       gate_ref, o_ref, *, seq, kchunk):
    i = pl.program_id(2)
    q0 = i * Q_BLOCK
    nc = kc_ref.shape[0]
    ns = seq // SLC_BLOCK
    rep = NSA_REP
    t_col = lax.broadcasted_iota(jnp.int32, (Q_BLOCK, 1), 0) + q0

    q_raw = _stack_heads(qraw_ref, rep)
    s_c = (_dot_nt(q_raw, kc_ref[...]) * ATTN_SCALE).reshape(rep, Q_BLOCK, nc)
    n_row = lax.broadcasted_iota(jnp.int32, (1, nc), 1)
    vis = (n_row >= 1) & (n_row * CMP_STRIDE + (CMP_BLOCK - CMP_STRIDE - 1) <= t_col)
    p_c = _masked_softmax(s_c, vis[None])
    o_cmp = _dot(p_c.reshape(rep * Q_BLOCK, nc), vc_ref[...])
    imp = _dot_exact01(jnp.sum(p_c, 0), _overlap_matrix(nc, ns))
    sel = _topk_select(_block_scores(imp, t_col, ns), min(SLC_TOPK, ns), ns)
    sel_b = sel.astype(BF)

    q_rot = _stack_heads(qrot_ref, rep).astype(BF)
    blk_row = lax.broadcasted_iota(jnp.int32, (ns, kchunk), 0)
    k_lane = lax.broadcasted_iota(jnp.int32, (ns, kchunk), 1)
    k_lane1 = lax.broadcasted_iota(jnp.int32, (1, kchunk), 1)

    def slc_step(c, carry):
        k0 = pl.multiple_of(c * kchunk, kchunk)
        s = (_dot_nt(q_rot, ks_ref[pl.ds(k0, kchunk), :]) * ATTN_SCALE).reshape(rep, Q_BLOCK, kchunk)
        expand = jnp.where(blk_row == (k0 + k_lane) // SLC_BLOCK, 1.0, 0.0).astype(BF)
        chosen = jnp.dot(sel_b, expand, preferred_element_type=F32)
        mask = (chosen > 0.5) & (k0 + k_lane1 <= t_col)
        return _online_update(carry, s, mask[None], vs_ref[pl.ds(k0, kchunk), :])

    init = (jnp.full((rep, Q_BLOCK, 1), NEG_INF, F32), jnp.zeros((rep, Q_BLOCK, 1), F32),
            jnp.zeros((rep, Q_BLOCK, HEAD_DIM), F32))
    n_steps = (q0 + Q_BLOCK + kchunk - 1) // kchunk
    _, l_s, acc_s = lax.fori_loop(0, n_steps, slc_step, init)
    o_slc = (acc_s / jnp.maximum(l_s, 1e-30)).reshape(rep * Q_BLOCK, HEAD_DIM)

    wk = WINDOW + Q_BLOCK
    w0 = pl.multiple_of(jnp.maximum(q0 - WINDOW, 0), Q_BLOCK)
    s_w = (_dot_nt(q_rot, kw_ref[pl.ds(w0, wk), :]) * ATTN_SCALE).reshape(rep, Q_BLOCK, wk)
    dist = t_col - (w0 + lax.broadcasted_iota(jnp.int32, (1, wk), 1))
    p_w = _masked_softmax(s_w, ((dist >= 0) & (dist < WINDOW))[None])
    o_swa = _dot(p_w.reshape(rep * Q_BLOCK, wk), vw_ref[pl.ds(w0, wk), :])

    gt = gate_ref[...]
    for r in range(rep):
        rows = slice(r * Q_BLOCK, (r + 1) * Q_BLOCK)
        o_ref[:, r * HEAD_DIM:(r + 1) * HEAD_DIM] = (
            gt[:, 3 * r:3 * r + 1] * o_cmp[rows] + gt[:, 3 * r + 1:3 * r + 2] * o_slc[rows]
            + gt[:, 3 * r + 2:3 * r + 3] * o_swa[rows])


def _nsa_prompt(p_main, q_rot, kc_all, sig, batch, seq):
    nq = seq // Q_BLOCK
    gw = NSA_REP * HEAD_DIM
    kchunk = min(512, seq)
    col = lambda base, kv: (lambda b, g, i: (b, base + kv * NSA_KV_HEADS + g))
    return pl.pallas_call(
        functools.partial(_nsa_prompt_kernel, seq=seq, kchunk=kchunk),
        grid=(batch, NSA_KV_HEADS, nq),
        in_specs=[
            pl.BlockSpec((Q_BLOCK, gw), lambda b, g, i: (b * nq + i, g)),
            pl.BlockSpec((Q_BLOCK, gw), lambda b, g, i: (b * nq + i, g)),
            pl.BlockSpec((None, seq // CMP_STRIDE, HEAD_DIM), lambda b, g, i: (b, 0, g)),
            pl.BlockSpec((None, seq // CMP_STRIDE, HEAD_DIM), lambda b, g, i: (b, 0, NSA_KV_HEADS + g)),
            pl.BlockSpec((seq, HEAD_DIM), col(COL_KVS, 0)),
            pl.BlockSpec((seq, HEAD_DIM), col(COL_KVS, 1)),
            pl.BlockSpec((seq, HEAD_DIM), col(COL_KVW, 0)),
            pl.BlockSpec((seq, HEAD_DIM), col(COL_KVW, 1)),
            pl.BlockSpec((Q_BLOCK, LANE), lambda b, g, i: (b * nq + i, g)),
        ],
        out_specs=pl.BlockSpec((Q_BLOCK, gw), lambda b, g, i: (b * nq + i, g)),
        out_shape=jax.ShapeDtypeStruct((batch * seq, NSA_HEADS * HEAD_DIM), F32),
        compiler_params=_cparams(("arbitrary", "arbitrary", "arbitrary")),
        name="nsa_attention_prompt",
    )(p_main, q_rot, kc_all, kc_all, p_main, p_main, p_main, p_main, sig)


def _suffix_sum_lanes(x):
    n = x.shape[-1]
    lane = lax.broadcasted_iota(jnp.int32, x.shape, x.ndim - 1)
    k = 1
    while k < n:
        x = x + jnp.where(lane < n - k, pltpu.roll(x, n - k, x.ndim - 1), 0.0)
        k *= 2
    return x


def _decay_prompt_kernel(lf_ref, d_ref):
    lt = lf_ref[...].T
    top = lt[:SUBLANE]
    d_ref[...] = _suffix_sum_lanes(top) - top


def _decay_prompt(lsig, batch, seq):
    return pl.pallas_call(
        _decay_prompt_kernel,
        grid=(batch,),
        in_specs=[pl.BlockSpec((seq, LANE), lambda b: (b, 0))],
        out_specs=pl.BlockSpec((None, SUBLANE, seq), lambda b: (b, 0, 0)),
        out_shape=jax.ShapeDtypeStruct((batch, SUBLANE, seq), F32),
        compiler_params=_cparams(("arbitrary",)),
        name="fox_decay_prompt",
    )(lsig)


def _fox_prompt_kernel(q_ref, k_ref, v_ref, d_ref, o_ref, *, tq, tk):
    h = pl.program_id(1)
    i = pl.program_id(2)
    q0 = i * tq
    q = q_ref[...].astype(BF)
    t_col = lax.broadcasted_iota(jnp.int32, (tq, 1), 0) + q0
    k_lane = lax.broadcasted_iota(jnp.int32, (1, tk), 1)

    def step(c, carry):
        k0 = pl.multiple_of(c * tk, tk)
        s = _dot_nt(q, k_ref[pl.ds(k0, tk), :]) * ATTN_SCALE + d_ref[pl.ds(h, 1), pl.ds(k0, tk)]
        return _online_update(carry, s, k0 + k_lane <= t_col, v_ref[pl.ds(k0, tk), :])

    init = (jnp.full((tq, 1), NEG_INF, F32), jnp.zeros((tq, 1), F32),
            jnp.zeros((tq, HEAD_DIM), F32))
    _, l, acc = lax.fori_loop(0, (q0 + tq + tk - 1) // tk, step, init)
    o_ref[...] = acc / jnp.maximum(l, 1e-30)


def _fox_prompt(p_main, decay, batch, seq):
    tq = min(512, seq)
    tk = min(512, seq)
    nq = seq // tq
    return pl.pallas_call(
        functools.partial(_fox_prompt_kernel, tq=tq, tk=tk),
        grid=(batch, FOX_HEADS, nq),
        in_specs=[
            pl.BlockSpec((tq, HEAD_DIM), lambda b, h, i: (b * nq + i, COL_FOX + h)),
            pl.BlockSpec((seq, HEAD_DIM), lambda b, h, i: (b, COL_FOX + FOX_HEADS + h)),
            pl.BlockSpec((seq, HEAD_DIM), lambda b, h, i: (b, COL_FOX + 2 * FOX_HEADS + h)),
            pl.BlockSpec((None, SUBLANE, seq), lambda b, h, i: (b, 0, 0)),
        ],
        out_specs=pl.BlockSpec((tq, HEAD_DIM), lambda b, h, i: (b * nq + i, h)),
        out_shape=jax.ShapeDtypeStruct((batch * seq, FOX_HEADS * HEAD_DIM), F32),
        compiler_params=_cparams(("arbitrary", "arbitrary", "arbitrary")),
        name="fox_attention_prompt",
    )(p_main, p_main, p_main, decay)


def _mem_attn_kernel(q_ref, k_ref, v_ref, o_ref):
    s = _dot_nt(q_ref[...], k_ref[...]) * ATTN_SCALE
    m = jnp.max(s, -1, keepdims=True)
    e = jnp.exp(s - m)
    p = e / jnp.sum(e, -1, keepdims=True)
    o_ref[...] = _dot(p, v_ref[...])


def _mem_attention(p_main, mem_kv2d, batch, rows_per_batch, tq):
    nq = rows_per_batch // tq
    mlen = mem_kv2d.shape[0] // batch
    return pl.pallas_call(
        _mem_attn_kernel,
        grid=(batch, MEM_HEADS, nq),
        in_specs=[
            pl.BlockSpec((tq, HEAD_DIM), lambda b, h, i: (b * nq + i, COL_QM + h)),
            pl.BlockSpec((mlen, HEAD_DIM), lambda b, h, i: (b, h)),
            pl.BlockSpec((mlen, HEAD_DIM), lambda b, h, i: (b, MEM_HEADS + h)),
        ],
        out_specs=pl.BlockSpec((tq, HEAD_DIM), lambda b, h, i: (b * nq + i, h)),
        out_shape=jax.ShapeDtypeStruct((batch * rows_per_batch, MEM_HEADS * HEAD_DIM), F32),
        compiler_params=_cparams(("arbitrary", "arbitrary", "arbitrary")),
        name="mem_attention",
    )(p_main, mem_kv2d, mem_kv2d)


def _matmul_kernel(x_ref, w_ref, o_ref):
    o_ref[...] = jnp.dot(x_ref[...].astype(BF), w_ref[...], preferred_element_type=F32)


def _matmul(x2d, w_bf, tm, tn):
    m, k = x2d.shape
    n = w_bf.shape[1]
    return pl.pallas_call(
        _matmul_kernel,
        grid=(m // tm, n // tn),
        in_specs=[pl.BlockSpec((tm, k), lambda i, j: (i, 0)),
                  pl.BlockSpec((k, tn), lambda i, j: (0, j))],
        out_specs=pl.BlockSpec((tm, tn), lambda i, j: (i, j)),
        out_shape=jax.ShapeDtypeStruct((m, n), F32),
        compiler_params=_cparams(("arbitrary", "arbitrary")),
        name="projection_matmul",
    )(x2d, w_bf)


def _layer_norm(z, g, b):
    zc = z - jnp.mean(z, -1, keepdims=True)
    var = jnp.mean(zc * zc, -1, keepdims=True)
    return zc * lax.rsqrt(var + LN_EPS) * g + b


def _outproj_kernel(on_ref, of_ref, om_ref, x_ref, w_ref, g_ref, b_ref, h_ref, *, alpha):
    mix = jnp.concatenate([on_ref[...].astype(BF), of_ref[...].astype(BF),
                           om_ref[...].astype(BF)], 1)
    y = jnp.dot(mix, w_ref[...], preferred_element_type=F32)
    h_ref[...] = _layer_norm(alpha * x_ref[...] + y, g_ref[...], b_ref[...])


def _out_projection(o_nsa, o_fox, o_mem, x2d, w_out, g, b, tm, alpha):
    m, d = x2d.shape
    row = lambda w: pl.BlockSpec((tm, w), lambda i: (i, 0))
    const = lambda shape: pl.BlockSpec(shape, lambda i: (0, 0))
    return pl.pallas_call(
        functools.partial(_outproj_kernel, alpha=alpha),
        grid=(m // tm,),
        in_specs=[row(o_nsa.shape[1]), row(o_fox.shape[1]), row(o_mem.shape[1]), row(d),
                  const(w_out.shape), const((1, d)), const((1, d))],
        out_specs=row(d),
        out_shape=jax.ShapeDtypeStruct((m, d), F32),
        compiler_params=_cparams(("arbitrary",)),
        name="out_projection_ln",
    )(o_nsa, o_fox, o_mem, x2d, w_out, g, b)


def _ffn_core(h_ref, wa_ref, wb_ref, cw_ref, cb_ref, wd_ref, g_ref, b_ref, y_ref, acc_ref,
              shifted, alpha):
    j = pl.program_id(1)
    hb = h_ref[...].astype(BF)
    a = jnp.dot(hb, wa_ref[...], preferred_element_type=F32)
    gate_in = jnp.dot(hb, wb_ref[...], preferred_element_type=F32)
    a1, a2 = shifted(a)
    cw = cw_ref[...]
    c = cb_ref[...] + a2 * cw[0:1] + a1 * cw[1:2] + a * cw[2:3]
    act = (c * jax.nn.sigmoid(c)) * gate_in
    part = jnp.dot(act.astype(BF), wd_ref[...], preferred_element_type=F32)

    @pl.when(j == 0)
    def _():
        acc_ref[...] = part

    @pl.when(j > 0)
    def _():
        acc_ref[...] += part

    @pl.when(j == pl.num_programs(1) - 1)
    def _():
        y_ref[...] = _layer_norm(alpha * h_ref[...] + acc_ref[...], g_ref[...], b_ref[...])
    return a


def _ffn_prompt_kernel(h_ref, wa_ref, wb_ref, cw_ref, cb_ref, wd_ref, g_ref, b_ref,
                       y_ref, tail_ref, acc_ref, halo_ref, *, tiles_per_seq, alpha):
    i = pl.program_id(0)
    j = pl.program_id(1)
    tm = h_ref.shape[0]
    seq_start = (i % tiles_per_seq) == 0

    @pl.when(seq_start)
    def _():
        halo_ref[j] = jnp.zeros(halo_ref.shape[1:], F32)

    def shifted(a):
        row = lax.broadcasted_iota(jnp.int32, a.shape, 0)
        halo = halo_ref[j]
        h1 = halo[SUBLANE - 1:SUBLANE]
        h2 = halo[SUBLANE - 2:SUBLANE - 1]
        a1 = jnp.where(row == 0, h1, pltpu.roll(a, 1, 0))
        a2 = jnp.where(row == 0, h2, jnp.where(row == 1, h1, pltpu.roll(a, 2, 0)))
        return a1, a2

    a = _ffn_core(h_ref, wa_ref, wb_ref, cw_ref, cb_ref, wd_ref, g_ref, b_ref, y_ref, acc_ref,
                  shifted, alpha)
    halo_ref[j] = a[tm - SUBLANE:]
    tail_ref[...] = a[tm - SUBLANE:]


def _ffn_sample_kernel(h_ref, wa_ref, wb_ref, cw_ref, cb_ref, wd_ref, g_ref, b_ref, f1_ref, f2_ref,
                       y_ref, a_ref, acc_ref, *, alpha):
    def shifted(a):
        s = lax.broadcasted_iota(jnp.int32, a.shape, 0) % SROW
        a1 = jnp.where(s >= 1, pltpu.roll(a, 1, 0), 0.0) + f1_ref[...]
        a2 = jnp.where(s >= 2, pltpu.roll(a, 2, 0), 0.0) + f2_ref[...]
        return a1, a2

    a_ref[...] = _ffn_core(h_ref, wa_ref, wb_ref, cw_ref, cb_ref, wd_ref, g_ref, b_ref, y_ref,
                           acc_ref, shifted, alpha)


def _ffn_specs(tm, d, tf, nf):
    return [
        pl.BlockSpec((tm, d), lambda i, j: (i, 0)),
        pl.BlockSpec((d, tf), lambda i, j: (0, j)),
        pl.BlockSpec((d, tf), lambda i, j: (0, nf + j)),
        pl.BlockSpec((SUBLANE, tf), lambda i, j: (0, j)),
        pl.BlockSpec((1, tf), lambda i, j: (0, j)),
        pl.BlockSpec((tf, d), lambda i, j: (j, 0)),
        pl.BlockSpec((1, d), lambda i, j: (0, 0)),
        pl.BlockSpec((1, d), lambda i, j: (0, 0)),
    ]


def _ffn_prompt(h2d, fw, seq, tm, tf, alpha):
    w_up, conv_w8, conv_b, w_down, g, b = fw
    m, d = h2d.shape
    dff = w_down.shape[0]
    nf = dff // tf
    return pl.pallas_call(
        functools.partial(_ffn_prompt_kernel, tiles_per_seq=seq // tm, alpha=alpha),
        grid=(m // tm, nf),
        in_specs=_ffn_specs(tm, d, tf, nf),
        out_specs=[pl.BlockSpec((tm, d), lambda i, j: (i, 0)),
                   pl.BlockSpec((None, SUBLANE, tf), lambda i, j: (i, 0, j))],
        out_shape=[jax.ShapeDtypeStruct((m, d), F32),
                   jax.ShapeDtypeStruct((m // tm, SUBLANE, dff), F32)],
        scratch_shapes=[pltpu.VMEM((tm, d), F32), pltpu.VMEM((nf, SUBLANE, tf), F32)],
        compiler_params=_cparams(("arbitrary", "arbitrary")),
        name="conv_ffn_prompt",
    )(h2d, w_up, w_up, conv_w8, conv_b, w_down, g, b)


def _ffn_sample(h2d, fw, fill1, fill2, tf, alpha):
    w_up, conv_w8, conv_b, w_down, g, b = fw
    m, d = h2d.shape
    dff = w_down.shape[0]
    nf = dff // tf
    return pl.pallas_call(
        functools.partial(_ffn_sample_kernel, alpha=alpha),
        grid=(1, nf),
        in_specs=_ffn_specs(m, d, tf, nf) + [pl.BlockSpec((m, tf), lambda i, j: (0, j)),
                                             pl.BlockSpec((m, tf), lambda i, j: (0, j))],
        out_specs=[pl.BlockSpec((m, d), lambda i, j: (0, 0)),
                   pl.BlockSpec((m, tf), lambda i, j: (0, j))],
        out_shape=[jax.ShapeDtypeStruct((m, d), F32), jax.ShapeDtypeStruct((m, dff), F32)],
        scratch_shapes=[pltpu.VMEM((m, d), F32)],
        compiler_params=_cparams(("arbitrary", "arbitrary")),
        name="conv_ffn_sample",
    )(h2d, w_up, w_up, conv_w8, conv_b, w_down, g, b, fill1, fill2)


def _nsa_sample_a_kernel(qraw_ref, qrot_ref, kc_ref, vc_ref, kwc_ref, vwc_ref, kwn_ref, vwn_ref,
                         gate_ref, part_ref, sel_ref, *, past, n_blocks):
    rep = NSA_REP
    nc = kc_ref.shape[0]
    nsp = sel_ref.shape[-1]
    s_col = lax.broadcasted_iota(jnp.int32, (SROW, 1), 0) + past

    q_raw = _stack_heads(qraw_ref, rep)
    s_c = (_dot_nt(q_raw, kc_ref[...]) * ATTN_SCALE).reshape(rep, SROW, nc)
    n_row = lax.broadcasted_iota(jnp.int32, (1, nc), 1)
    vis = (n_row >= 1) & (n_row * CMP_STRIDE + (CMP_BLOCK - CMP_STRIDE - 1) <= s_col)
    p_c = _masked_softmax(s_c, vis[None])
    o_cmp = _dot(p_c.reshape(rep * SROW, nc), vc_ref[...])
    imp = _dot_exact01(jnp.sum(p_c, 0), _overlap_matrix(nc, nsp))
    sel_ref[...] = _topk_select(_block_scores(imp, s_col, n_blocks), min(SLC_TOPK, n_blocks), n_blocks)

    q_rot = _stack_heads(qrot_ref, rep)
    wbuf = kwc_ref.shape[0]
    keys = jnp.concatenate([kwc_ref[...], kwn_ref[...]], 0)
    vals = jnp.concatenate([vwc_ref[...], vwn_ref[...]], 0)
    w_pos = past - wbuf + lax.broadcasted_iota(jnp.int32, (1, wbuf + SROW), 1)
    dist = s_col - w_pos
    mask = (dist >= 0) & (dist < WINDOW) & (w_pos >= 0)
    s_w = (_dot_nt(q_rot, keys) * ATTN_SCALE).reshape(rep, SROW, wbuf + SROW)
    p_w = _masked_softmax(s_w, mask[None])
    o_swa = _dot(p_w.reshape(rep * SROW, wbuf + SROW), vals)

    gt = gate_ref[...]
    for r in range(rep):
        rows = slice(r * SROW, (r + 1) * SROW)
        part_ref[rows, :] = (gt[:, 3 * r:3 * r + 1] * o_cmp[rows]
                             + gt[:, 3 * r + 2:3 * r + 3] * o_swa[rows])


def _nsa_sample_a(p_s, qrot_s, kc_all, swa3d, sig_s, past, n_blocks):
    dec_batch, nc, _ = kc_all.shape
    wbuf = swa3d.shape[1]
    gw = NSA_REP * HEAD_DIM
    nsp = -(-n_blocks // LANE) * LANE
    bg = lambda b, g: (b, g)
    return pl.pallas_call(
        functools.partial(_nsa_sample_a_kernel, past=past, n_blocks=n_blocks),
        grid=(dec_batch, NSA_KV_HEADS),
        in_specs=[
            pl.BlockSpec((SROW, gw), bg),
            pl.BlockSpec((SROW, gw), bg),
            pl.BlockSpec((None, nc, HEAD_DIM), lambda b, g: (b, 0, g)),
            pl.BlockSpec((None, nc, HEAD_DIM), lambda b, g: (b, 0, NSA_KV_HEADS + g)),
            pl.BlockSpec((None, wbuf, HEAD_DIM), lambda b, g: (b, 0, g)),
            pl.BlockSpec((None, wbuf, HEAD_DIM), lambda b, g: (b, 0, NSA_KV_HEADS + g)),
            pl.BlockSpec((SROW, HEAD_DIM), lambda b, g: (b, COL_KVW + g)),
            pl.BlockSpec((SROW, HEAD_DIM), lambda b, g: (b, COL_KVW + NSA_KV_HEADS + g)),
            pl.BlockSpec((SROW, LANE), bg),
        ],
        out_specs=[pl.BlockSpec((None, None, NSA_REP * SROW, HEAD_DIM), lambda b, g: (b, g, 0, 0)),
                   pl.BlockSpec((None, None, SROW, nsp), lambda b, g: (b, g, 0, 0))],
        out_shape=[jax.ShapeDtypeStruct((dec_batch, NSA_KV_HEADS, NSA_REP * SROW, HEAD_DIM), F32),
                   jax.ShapeDtypeStruct((dec_batch, NSA_KV_HEADS, SROW, nsp), F32)],
        compiler_params=_cparams(("arbitrary", "arbitrary")),
        name="nsa_sample_cmp_swa",
    )(p_s, qrot_s, kc_all, kc_all, swa3d, swa3d, p_s, p_s, sig_s)


def _nsa_sample_b_kernel(pt_ref, *refs, past, n_blocks):
    del pt_ref
    pages = refs[:PAGES_PER_STEP]
    (qrot_ref, sel_ref, knew_ref, vnew_ref, gate_ref, part_ref,
     o_ref, m_ref, l_ref, acc_ref) = refs[PAGES_PER_STEP:]
    c = pl.program_id(1)
    nchunk = pl.num_programs(1) - 1
    rep = NSA_REP
    page = pages[0].shape[0]
    keys_per_step = PAGES_PER_STEP * page
    nsp = sel_ref.shape[-1]
    gw = rep * HEAD_DIM
    kvw = NSA_KV_HEADS * HEAD_DIM

    @pl.when(c == 0)
    def _():
        m_ref[...] = jnp.full(m_ref.shape, NEG_INF, F32)
        l_ref[...] = jnp.zeros(l_ref.shape, F32)
        acc_ref[...] = jnp.zeros(acc_ref.shape, F32)

    def update(g, s, mask, v):
        carry = (m_ref[g], l_ref[g], acc_ref[g])
        m_new, l_new, acc_new = _online_update(carry, s, mask, v)
        m_ref[g] = m_new
        l_ref[g] = l_new
        acc_ref[g] = acc_new

    @pl.when(c < nchunk)
    def _():
        blk_row = lax.broadcasted_iota(jnp.int32, (nsp, keys_per_step), 0)
        k_lane = lax.broadcasted_iota(jnp.int32, (nsp, keys_per_step), 1)
        expand = jnp.where(blk_row == (c * keys_per_step + k_lane) // SLC_BLOCK, 1.0, 0.0).astype(BF)
        for g in range(NSA_KV_HEADS):
            q = _stack_heads(qrot_ref.at[:, g * gw:(g + 1) * gw], rep)
            k = jnp.concatenate([p[:, g * HEAD_DIM:(g + 1) * HEAD_DIM].astype(BF) for p in pages], 0)
            v = jnp.concatenate([p[:, kvw + g * HEAD_DIM:kvw + (g + 1) * HEAD_DIM].astype(BF)
                                 for p in pages], 0)
            s = (_dot_nt(q, k) * ATTN_SCALE).reshape(rep, SROW, keys_per_step)
            chosen = jnp.dot(sel_ref[g].astype(BF), expand, preferred_element_type=F32)
            update(g, s, (chosen > 0.5)[None], v)

    @pl.when(c == nchunk)
    def _():
        s_q = lax.broadcasted_iota(jnp.int32, (SROW, LANE), 0)
        s_k = lax.broadcasted_iota(jnp.int32, (SROW, LANE), 1)
        pad = jnp.zeros((LANE - SROW, HEAD_DIM), F32)
        gt = gate_ref[...]
        for g in range(NSA_KV_HEADS):
            q = _stack_heads(qrot_ref.at[:, g * gw:(g + 1) * gw], rep)
            k = jnp.concatenate([knew_ref[:, g * HEAD_DIM:(g + 1) * HEAD_DIM], pad], 0)
            v = jnp.concatenate([vnew_ref[:, g * HEAD_DIM:(g + 1) * HEAD_DIM], pad], 0)
            s = (_dot_nt(q, k) * ATTN_SCALE).reshape(rep, SROW, LANE)
            chosen = sel_ref[g][:, n_blocks - 1:n_blocks] > 0.5
            mask = chosen & (s_k <= s_q) & (s_k < SROW)
            update(g, s, mask[None], v)
            o_slc = acc_ref[g] / jnp.maximum(l_ref[g], 1e-30)
            for r in range(rep):
                col = 3 * rep * g + 3 * r + 1
                o_ref[:, (g * rep + r) * HEAD_DIM:(g * rep + r + 1) * HEAD_DIM] = (
                    part_ref[g, r * SROW:(r + 1) * SROW, :] + gt[:, col:col + 1] * o_slc[r])


def _nsa_sample_b(cache2d, page_table, p_s, qrot_s, sel, sig12, part, past, n_blocks):
    n_pool, page, kvw = cache2d.shape
    dec_batch, n_pages = page_table.shape
    nchunk = n_pages // PAGES_PER_STEP
    nsp = sel.shape[-1]
    qw = NSA_HEADS * HEAD_DIM
    half = NSA_KV_HEADS * HEAD_DIM
    row = lambda w, col: pl.BlockSpec((SROW, w), lambda b, c, pt: (b, col))
    return pl.pallas_call(
        functools.partial(_nsa_sample_b_kernel, past=past, n_blocks=n_blocks),
        grid_spec=pltpu.PrefetchScalarGridSpec(
            num_scalar_prefetch=1,
            grid=(dec_batch, nchunk + 1),
            in_specs=_page_specs(page, kvw, lambda c: jnp.minimum(c, nchunk - 1)) + [
                row(qw, 0),
                pl.BlockSpec((None, NSA_KV_HEADS, SROW, nsp), lambda b, c, pt: (b, 0, 0, 0)),
                row(half, COL_KVS * LANE // half),
                row(half, COL_KVS * LANE // half + 1),
                row(LANE, 0),
                pl.BlockSpec((None, NSA_KV_HEADS, NSA_REP * SROW, HEAD_DIM),
                             lambda b, c, pt: (b, 0, 0, 0)),
            ],
            out_specs=row(qw, 0),
            scratch_shapes=[pltpu.VMEM((NSA_KV_HEADS, NSA_REP, SROW, 1), F32),
                            pltpu.VMEM((NSA_KV_HEADS, NSA_REP, SROW, 1), F32),
                            pltpu.VMEM((NSA_KV_HEADS, NSA_REP, SROW, HEAD_DIM), F32)],
        ),
        out_shape=jax.ShapeDtypeStruct((dec_batch * SROW, qw), F32),
        compiler_params=_cparams(("arbitrary", "arbitrary")),
        name="nsa_sample_selected",
    )(page_table, *([cache2d] * PAGES_PER_STEP), qrot_s, sel, p_s, p_s, sig12, part)


def _fox_sample_kernel(pt_ref, *refs, s_len):
    del pt_ref
    pages = refs[:PAGES_PER_STEP]
    lpages = refs[PAGES_PER_STEP:2 * PAGES_PER_STEP]
    (q_ref, knew_ref, vnew_ref, lnew_ref, o_ref, m_ref, l_ref, acc_ref, carry_ref) = refs[2 * PAGES_PER_STEP:]
    c = pl.program_id(1)
    page = pages[0].shape[0]
    keys_per_step = PAGES_PER_STEP * page
    kvw = FOX_HEADS * HEAD_DIM

    def update(h, s, mask, v):
        carry = (m_ref[h], l_ref[h], acc_ref[h])
        m_new, l_new, acc_new = _online_update(carry, s, mask, v)
        m_ref[h] = m_new
        l_ref[h] = l_new
        acc_ref[h] = acc_new

    @pl.when(c == 0)
    def _():
        m_ref[...] = jnp.full(m_ref.shape, NEG_INF, F32)
        l_ref[...] = jnp.zeros(l_ref.shape, F32)
        acc_ref[...] = jnp.zeros(acc_ref.shape, F32)
        s_q = lax.broadcasted_iota(jnp.int32, (SROW, LANE), 0)
        s_k = lax.broadcasted_iota(jnp.int32, (SROW, LANE), 1)
        later = jnp.where((s_q > s_k) & (s_q < s_len), 1.0, 0.0)
        real = lax.broadcasted_iota(jnp.int32, (SROW, 1), 0) < s_len
        lnew = lnew_ref[...]
        pad = jnp.zeros((LANE - SROW, HEAD_DIM), F32)
        mask = (s_k <= s_q) & (s_k < s_len)
        totals = []
        for h in range(FOX_HEADS):
            lf = jnp.where(real, lnew[:, h:h + 1], 0.0)
            d_row = jnp.sum(lf * later, 0, keepdims=True)
            totals.append(jnp.sum(lf, 0, keepdims=True))
            k = jnp.concatenate([knew_ref[:, h * HEAD_DIM:(h + 1) * HEAD_DIM], pad], 0)
            v = jnp.concatenate([vnew_ref[:, h * HEAD_DIM:(h + 1) * HEAD_DIM], pad], 0)
            s = _dot_nt(q_ref[:, h * HEAD_DIM:(h + 1) * HEAD_DIM], k) * ATTN_SCALE + d_row
            update(h, s, mask, v)
        totals += [jnp.zeros((1, 1), F32)] * (SUBLANE - FOX_HEADS)
        carry_ref[...] = jnp.broadcast_to(jnp.concatenate(totals, 0), carry_ref.shape)

    @pl.when(c > 0)
    def _():
        lf = jnp.concatenate([p[...] for p in lpages], 1)
        incl = _suffix_sum_lanes(lf)
        carry = carry_ref[:, 0:1]
        decay = carry + (incl - lf)
        carry_ref[...] = jnp.broadcast_to(carry + incl[:, 0:1], carry_ref.shape)
        for h in range(FOX_HEADS):
            k = jnp.concatenate([p[:, h * HEAD_DIM:(h + 1) * HEAD_DIM].astype(BF) for p in pages], 0)
            v = jnp.concatenate([p[:, kvw + h * HEAD_DIM:kvw + (h + 1) * HEAD_DIM].astype(BF)
                                 for p in pages], 0)
            s = _dot_nt(q_ref[:, h * HEAD_DIM:(h + 1) * HEAD_DIM], k) * ATTN_SCALE + decay[h:h + 1, :]
            update(h, s, None, v)

    @pl.when(c == pl.num_programs(1) - 1)
    def _():
        for h in range(FOX_HEADS):
            o_ref[:, h * HEAD_DIM:(h + 1) * HEAD_DIM] = acc_ref[h] / jnp.maximum(l_ref[h], 1e-30)


def _fox_sample(cache2d, logf_t, page_table, p_s, lsig_s, s_len):
    n_pool, page, kvw2 = cache2d.shape
    dec_batch, n_pages = page_table.shape
    nchunk = n_pages // PAGES_PER_STEP
    fw = FOX_HEADS * HEAD_DIM
    chunk_of = lambda c: nchunk - jnp.maximum(c, 1)
    row = lambda w, col: pl.BlockSpec((SROW, w), lambda b, c, pt: (b, col))
    return pl.pallas_call(
        functools.partial(_fox_sample_kernel, s_len=s_len),
        grid_spec=pltpu.PrefetchScalarGridSpec(
            num_scalar_prefetch=1,
            grid=(dec_batch, nchunk + 1),
            in_specs=_page_specs(page, kvw2, chunk_of) + _page_specs(SUBLANE, page, chunk_of) + [
                row(fw, COL_FOX * LANE // fw),
                row(fw, COL_FOX * LANE // fw + 1),
                row(fw, COL_FOX * LANE // fw + 2),
                row(LANE, 0),
            ],
            out_specs=row(fw, 0),
            scratch_shapes=[pltpu.VMEM((FOX_HEADS, SROW, 1), F32),
                            pltpu.VMEM((FOX_HEADS, SROW, 1), F32),
                            pltpu.VMEM((FOX_HEADS, SROW, HEAD_DIM), F32),
                            pltpu.VMEM((SUBLANE, LANE), F32)],
        ),
        out_shape=jax.ShapeDtypeStruct((dec_batch * SROW, fw), F32),
        compiler_params=_cparams(("arbitrary", "arbitrary")),
        name="fox_attention_sample",
    )(page_table, *([cache2d] * PAGES_PER_STEP), *([logf_t] * PAGES_PER_STEP), p_s, p_s, p_s, lsig_s)


def _rope_tables(pos):
    half = ROT_DIM // 2
    inv = jnp.power(ROPE_THETA, -jnp.arange(half, dtype=F32) * (2.0 / ROT_DIM))
    ang = pos.astype(F32)[:, None] * inv[None, :]
    cos, sin = jnp.cos(ang), jnp.sin(ang)
    n = pos.shape[0]
    c = jnp.concatenate([cos, cos, jnp.ones((n, HEAD_DIM - ROT_DIM), F32)], 1)
    s = jnp.concatenate([-sin, sin, jnp.zeros((n, HEAD_DIM - ROT_DIM), F32)], 1)
    return c, s


def _layer_weights(w_in, b_gate, b_forget, cmp_w1, cmp_b1, cmp_w2, cmp_pos, w_mem_kv, w_out,
                   ln1_g, ln1_b, w_up, conv_w, conv_b, w_down, ln2_g, ln2_b):
    d = w_in.shape[0]
    nq = NSA_HEADS * HEAD_DIM
    kvw = 2 * NSA_KV_HEADS * HEAD_DIM
    o_g = nq + 3 * kvw
    o_fox = o_g + 3 * NSA_HEADS
    o_f = o_fox + 3 * FOX_HEADS * HEAD_DIM
    o_qm = o_f + FOX_HEADS
    w_main = jnp.concatenate([w_in[:, :o_g], w_in[:, o_fox:o_f], w_in[:, o_qm:]], 1).astype(BF)
    per_group = 3 * NSA_REP
    zpad = lambda n: jnp.zeros((d, n), w_in.dtype)
    w_small = jnp.concatenate([
        w_in[:, o_g:o_g + per_group], zpad(LANE - per_group),
        w_in[:, o_g + per_group:o_fox], zpad(LANE - per_group),
        w_in[:, o_f:o_qm], zpad(LANE - FOX_HEADS)], 1).astype(BF)
    bpad = lambda n: jnp.zeros((n,), F32)
    b_small = jnp.concatenate([
        b_gate[:per_group], bpad(LANE - per_group), b_gate[per_group:], bpad(LANE - per_group),
        b_forget, bpad(LANE - FOX_HEADS)])[None, :].astype(F32)
    kdim = CMP_STRIDE * HEAD_DIM
    w1cat = jnp.concatenate([cmp_w1[:, :CMP_STRIDE].reshape(2, kdim, CMP_HIDDEN),
                             cmp_w1[:, CMP_STRIDE:].reshape(2, kdim, CMP_HIDDEN)], 2).astype(BF)
    pe = jnp.concatenate([cmp_pos[:, :CMP_STRIDE].reshape(2, 1, kdim),
                          cmp_pos[:, CMP_STRIDE:].reshape(2, 1, kdim),
                          jnp.zeros((2, SUBLANE - 2, kdim), F32)], 1)
    cw = (w1cat, pe, cmp_b1[:, None, :], cmp_w2.astype(BF))
    conv_w8 = jnp.concatenate([conv_w, jnp.zeros((SUBLANE - CONV_W, conv_w.shape[1]), F32)], 0)
    fw = (w_up.astype(BF), conv_w8, conv_b[None, :], w_down.astype(BF), ln2_g[None, :], ln2_b[None, :])
    return dict(w_main=w_main, w_small=w_small, b_small=b_small, cw=cw,
                w_mem_kv=w_mem_kv.astype(BF), w_out=w_out.astype(BF),
                ln1_g=ln1_g[None, :], ln1_b=ln1_b[None, :], fw=fw)


def _prompt_layer(x, mem, lw, alpha):
    batch, seq, d = x.shape
    x2d = x.reshape(batch * seq, d)
    rc, rs = _rope_tables(jnp.tile(jnp.arange(seq, dtype=jnp.int32), batch))
    tm = min(512, seq)
    p_main, q_rot, sig, lsig = _input_projection(x2d, lw["w_main"], lw["w_small"], lw["b_small"],
                                                 rc, rs, tm)
    kc_all = _compress_prompt(p_main, batch, seq, lw["cw"])
    o_nsa = _nsa_prompt(p_main, q_rot, kc_all, sig, batch, seq)
    decay = _decay_prompt(lsig, batch, seq)
    o_fox = _fox_prompt(p_main, decay, batch, seq)
    mlen = mem.shape[1]
    mem_kv = _matmul(mem.reshape(batch * mlen, d), lw["w_mem_kv"], min(256, batch * mlen), 512)
    o_mem = _mem_attention(p_main, mem_kv, batch, seq, min(512, seq))
    h = _out_projection(o_nsa, o_fox, o_mem, x2d, lw["w_out"], lw["ln1_g"], lw["ln1_b"],
                        min(256, seq), alpha)
    y, tail = _ffn_prompt(h, lw["fw"], seq, tm, 512, alpha)
    kvw = 2 * NSA_KV_HEADS * HEAD_DIM
    cols = lambda c0, w: p_main[:, c0 * LANE:c0 * LANE + w].reshape(batch, seq, 2, -1, HEAD_DIM)
    n_win = min(WINDOW, seq)
    tiles = seq // tm
    conv_state = tail.reshape(batch, tiles, SUBLANE, -1)[:, -1, SUBLANE - (CONV_W - 1):]
    states = (cols(COL_KVC, kvw), cols(COL_KVS, kvw),
              cols(COL_FOX + FOX_HEADS, 2 * FOX_HEADS * HEAD_DIM),
              lsig[:, :FOX_HEADS].reshape(batch, seq, FOX_HEADS),
              cols(COL_KVW, kvw)[:, seq - n_win:],
              mem_kv.reshape(batch, mlen, 2, MEM_HEADS, HEAD_DIM), conv_state)
    return y.reshape(batch, seq, d), states


def _sample_layer(x, c_cmp, c_slc, c_fox, c_logf, c_swa, c_mem, s_conv, page_table, lw, alpha):
    dec_batch, s_len, d = x.shape
    n_pool, page = c_cmp.shape[0], c_cmp.shape[1]
    n_pages = page_table.shape[1]
    past = n_pages * page
    t_all = past + s_len
    assert s_len <= SROW and n_pages % PAGES_PER_STEP == 0
    assert (t_all // CMP_STRIDE) * CMP_STRIDE == past and past % SLC_BLOCK == 0
    n_blocks = -(-t_all // SLC_BLOCK)
    xp = jnp.pad(x, ((0, 0), (0, SROW - s_len), (0, 0))).reshape(dec_batch * SROW, d)
    pos = past + jnp.tile(jnp.arange(SROW, dtype=jnp.int32), dec_batch)
    rc, rs = _rope_tables(pos)
    p_s, qrot_s, sig_s, lsig_s = _input_projection(xp, lw["w_main"], lw["w_small"], lw["b_small"],
                                                   rc, rs, dec_batch * SROW)
    kvw = 2 * NSA_KV_HEADS * HEAD_DIM
    kc_all = _compress_paged(c_cmp.reshape(n_pool, page, kvw), page_table, lw["cw"])
    wbuf = c_swa.shape[1]
    swa3d = c_swa.reshape(dec_batch, wbuf, kvw)
    per_group = 3 * NSA_REP
    sig12 = jnp.concatenate([sig_s[:, :per_group], sig_s[:, LANE:LANE + per_group],
                             jnp.zeros((dec_batch * SROW, LANE - 2 * per_group), F32)], 1)
    part, sel = _nsa_sample_a(p_s, qrot_s, kc_all, swa3d, sig_s, past, n_blocks)
    o_nsa = _nsa_sample_b(c_slc.reshape(n_pool, page, kvw), page_table, p_s, qrot_s, sel, sig12,
                          part, past, n_blocks)
    logf_t = jnp.pad(jnp.swapaxes(c_logf, 1, 2), ((0, 0), (0, SUBLANE - FOX_HEADS), (0, 0)))
    o_fox = _fox_sample(c_fox.reshape(n_pool, page, 2 * FOX_HEADS * HEAD_DIM), logf_t, page_table,
                        p_s, lsig_s, s_len)
    mlen = c_mem.shape[1]
    o_mem = _mem_attention(p_s, c_mem.reshape(dec_batch * mlen, 2 * MEM_HEADS * HEAD_DIM),
                           dec_batch, SROW, SROW)
    h = _out_projection(o_nsa, o_fox, o_mem, xp, lw["w_out"], lw["ln1_g"], lw["ln1_b"],
                        dec_batch * SROW, alpha)
    dff = s_conv.shape[-1]
    zrow = jnp.zeros((dec_batch, 1, dff), F32)
    fill1 = jnp.concatenate([s_conv[:, 1:2]] + [zrow] * (SROW - 1), 1).reshape(dec_batch * SROW, dff)
    fill2 = jnp.concatenate([s_conv[:, 0:1], s_conv[:, 1:2]] + [zrow] * (SROW - 2), 1)
    y, a = _ffn_sample(h, lw["fw"], fill1, fill2.reshape(dec_batch * SROW, dff), 512, alpha)

    def rows(arr):
        return arr.reshape(dec_batch, SROW, -1)[:, :s_len]
    cols = lambda c0, w: rows(p_s[:, c0 * LANE:c0 * LANE + w]).reshape(dec_batch, s_len, 2, -1, HEAD_DIM)
    new_kv_swa = cols(COL_KVW, kvw)
    new_swa = jnp.concatenate([c_swa, new_kv_swa], 1)[:, s_len:]
    conv_state = jnp.concatenate([s_conv, rows(a)], 1)[:, -(CONV_W - 1):]
    states = (cols(COL_KVC, kvw), cols(COL_KVS, kvw),
              cols(COL_FOX + FOX_HEADS, 2 * FOX_HEADS * HEAD_DIM),
              rows(lsig_s[:, :FOX_HEADS]), new_swa, conv_state)
    return rows(y), states


def kernel(x_prompt, x_sample, mem_prompt, cache_nsa_cmp, cache_nsa_slc, cache_fox_kv, cache_fox_logf, cache_nsa_swa, cache_mem, state_conv, page_table, w_in, b_gate, b_forget, cmp_w1, cmp_b1, cmp_w2, cmp_pos, w_mem_kv, w_out, ln1_g, ln1_b, w_up, conv_w, conv_b, w_down, ln2_g, ln2_b):
    depth = w_in.shape[0]
    alpha = float((2 * depth) ** 0.25)
    yp, ys = x_prompt, x_sample
    acc_p = [[] for _ in range(7)]
    acc_s = [[] for _ in range(6)]
    for l in range(depth):
        lw = _layer_weights(w_in[l], b_gate[l], b_forget[l], cmp_w1[l], cmp_b1[l], cmp_w2[l],
                            cmp_pos[l], w_mem_kv[l], w_out[l], ln1_g[l], ln1_b[l], w_up[l],
                            conv_w[l], conv_b[l], w_down[l], ln2_g[l], ln2_b[l])
        yp, st_p = _prompt_layer(yp, mem_prompt, lw, alpha)
        ys, st_s = _sample_layer(ys, cache_nsa_cmp[l], cache_nsa_slc[l], cache_fox_kv[l],
                                 cache_fox_logf[l], cache_nsa_swa[l], cache_mem[l], state_conv[l],
                                 page_table, lw, alpha)
        for lst, a in zip(acc_p, st_p):
            lst.append(a)
        for lst, a in zip(acc_s, st_s):
            lst.append(a)
    outs_p = [jnp.stack(a, 0) for a in acc_p]
    outs_s = [jnp.stack(a, 0) for a in acc_s]
    return (yp, ys, *outs_p, *outs_s)
```

```python
import functools

import jax
import jax.numpy as jnp
import numpy as np
from jax import lax
from jax.experimental import pallas as pl
from jax.experimental.pallas import tpu as pltpu

HEAD_DIM = 128
NSA_KV_HEADS = 2
NSA_REP = 4
NSA_HEADS = NSA_KV_HEADS * NSA_REP
FOX_HEADS = 4
MEM_HEADS = 4
CMP_BLOCK = 32
CMP_STRIDE = 16
CMP_HIDDEN = 256
SLC_BLOCK = 64
SLC_TOPK = 16
WINDOW = 512
Q_BLOCK = 128
ROT_DIM = HEAD_DIM // 4
ROPE_THETA = 500000.0
CONV_W = 3
LN_EPS = 1e-5
ATTN_SCALE = HEAD_DIM ** -0.5
NEG_INF = -1e30
FORCE_BONUS = 1e3

LANE = 128
SUBLANE = 8
SROW = SUBLANE
PAGES_PER_STEP = 16
VMEM_LIMIT = 56 * 1024 * 1024

COL_Q = 0
COL_KVC = 8
COL_KVS = 12
COL_KVW = 16
COL_FOX = 20
COL_QM = 32
MAIN_W = 36 * LANE
IN_TILE = 512

BF = jnp.bfloat16
F32 = jnp.float32


def _dot(a, b):
    return jnp.dot(a.astype(BF), b.astype(BF), preferred_element_type=F32)


def _dot_nt(a, b):
    return lax.dot_general(a.astype(BF), b.astype(BF), (((1,), (1,)), ((), ())),
                           preferred_element_type=F32)


def _dot_exact01(x, m01):
    hi = x.astype(BF)
    r1 = x - hi.astype(F32)
    mid = r1.astype(BF)
    lo = (r1 - mid.astype(F32)).astype(BF)
    m = m01.astype(BF)
    d = lambda a: jnp.dot(a, m, preferred_element_type=F32)
    return d(hi) + d(mid) + d(lo)


def _cparams(sem):
    return pltpu.CompilerParams(dimension_semantics=sem, vmem_limit_bytes=VMEM_LIMIT)


def _masked_softmax(s, mask):
    s = jnp.where(mask, s, NEG_INF)
    m = jnp.max(s, -1, keepdims=True)
    e = jnp.where(mask, jnp.exp(s - m), 0.0)
    return e / jnp.maximum(jnp.sum(e, -1, keepdims=True), 1e-30)


def _online_update(carry, s, mask, v):
    m, l, acc = carry
    if mask is not None:
        s = jnp.where(mask, s, NEG_INF)
    m_new = jnp.maximum(m, jnp.max(s, -1, keepdims=True))
    p = jnp.exp(s - m_new)
    if mask is not None:
        p = jnp.where(mask, p, 0.0)
    alpha = jnp.exp(m - m_new)
    l_new = alpha * l + jnp.sum(p, -1, keepdims=True)
    lead = p.shape[:-1]
    pv = _dot(p.reshape((-1, p.shape[-1])), v).reshape(lead + (v.shape[-1],))
    return m_new, l_new, alpha * acc + pv


def _rope_tile(x, c, s):
    lane = lax.broadcasted_iota(jnp.int32, x.shape, 1)
    half = ROT_DIM // 2
    swapped = jnp.where(lane < half, pltpu.roll(x, LANE - half, 1), pltpu.roll(x, half, 1))
    return x * c + swapped * s


def _topk_select(score, n_sel, n_valid_cols):
    col = lax.broadcasted_iota(jnp.int32, score.shape, 1)
    rank = jnp.zeros(score.shape, F32)
    for i in range(n_valid_cols):
        ci = score[:, i:i + 1]
        rank = rank + jnp.where(col > i, jnp.where(ci >= score, 1.0, 0.0),
                                jnp.where(ci > score, 1.0, 0.0))
    return jnp.where(rank < n_sel, 1.0, 0.0)


def _inproj_kernel(x_ref, w_ref, ws_ref, bs_ref, c_ref, s_ref,
                   p_ref, qrot_ref, sig_ref, lsig_ref, xb_ref):
    j = pl.program_id(1)

    @pl.when(j == 0)
    def _():
        xb_ref[...] = x_ref[...].astype(BF)
        z = jnp.dot(xb_ref[...], ws_ref[...], preferred_element_type=F32) + bs_ref[...]
        sig_ref[...] = jax.nn.sigmoid(z[:, :2 * LANE])
        zf = z[:, 2 * LANE:]
        lsig_ref[...] = jnp.minimum(zf, 0.0) - jnp.log1p(jnp.exp(-jnp.abs(zf)))

    acc = jnp.dot(xb_ref[...], w_ref[...], preferred_element_type=F32)
    c = c_ref[...]
    s = s_ref[...]
    heads = IN_TILE // LANE

    @pl.when(j < 2)
    def _():
        p_ref[...] = acc
        for h in range(heads):
            qrot_ref[:, h * LANE:(h + 1) * LANE] = _rope_tile(acc[:, h * LANE:(h + 1) * LANE], c, s)

    @pl.when((j == COL_KVS // heads) | (j == COL_KVW // heads))
    def _():
        for h in range(heads):
            t = acc[:, h * LANE:(h + 1) * LANE]
            p_ref[:, h * LANE:(h + 1) * LANE] = _rope_tile(t, c, s) if h < NSA_KV_HEADS else t

    @pl.when((j >= 2) & (j != COL_KVS // heads) & (j != COL_KVW // heads))
    def _():
        p_ref[...] = acc


def _input_projection(x2d, w_main, w_small, b_small, rope_c, rope_s, tm):
    m, d = x2d.shape
    nj = MAIN_W // IN_TILE
    return pl.pallas_call(
        _inproj_kernel,
        grid=(m // tm, nj),
        in_specs=[
            pl.BlockSpec((tm, d), lambda i, j: (i, 0)),
            pl.BlockSpec((d, IN_TILE), lambda i, j: (0, j)),
            pl.BlockSpec((d, 3 * LANE), lambda i, j: (0, 0)),
            pl.BlockSpec((1, 3 * LANE), lambda i, j: (0, 0)),
            pl.BlockSpec((tm, LANE), lambda i, j: (i, 0)),
            pl.BlockSpec((tm, LANE), lambda i, j: (i, 0)),
        ],
        out_specs=[
            pl.BlockSpec((tm, IN_TILE), lambda i, j: (i, j)),
            pl.BlockSpec((tm, IN_TILE), lambda i, j: (i, jnp.minimum(j, 1))),
            pl.BlockSpec((tm, 2 * LANE), lambda i, j: (i, 0)),
            pl.BlockSpec((tm, LANE), lambda i, j: (i, 0)),
        ],
        out_shape=[
            jax.ShapeDtypeStruct((m, MAIN_W), F32),
            jax.ShapeDtypeStruct((m, NSA_HEADS * HEAD_DIM), F32),
            jax.ShapeDtypeStruct((m, 2 * LANE), F32),
            jax.ShapeDtypeStruct((m, LANE), F32),
        ],
        scratch_shapes=[pltpu.VMEM((tm, d), BF)],
        compiler_params=_cparams(("arbitrary", "arbitrary")),
        name="input_projection",
    )(x2d, w_main, w_small, b_small, rope_c, rope_s)


def _gelu_tanh(x):
    k = np.sqrt(2.0 / np.pi).astype(np.float32)
    return x * (0.5 * (1.0 + jnp.tanh(k * (x + 0.044715 * (x ** 3)))))


def _compress_body(load, nb, w1_ref, pe_ref, b1_ref, w2_ref, out_ref, carry_ref, first_step):
    @pl.when(first_step)
    def _():
        carry_ref[...] = jnp.zeros(carry_ref.shape, F32)

    row = lax.broadcasted_iota(jnp.int32, (nb, CMP_HIDDEN), 0)
    for kv in range(2):
        xs = []
        for g in range(NSA_KV_HEADS):
            per_l = [load(kv * NSA_KV_HEADS + g, l) for l in range(CMP_STRIDE)]
            xs.append(jnp.concatenate(per_l, 1).astype(BF))
        xs.append(pe_ref[kv].astype(BF))
        fs = jnp.dot(jnp.concatenate(xs, 0), w1_ref[kv], preferred_element_type=F32)
        pos = (fs[2 * nb:2 * nb + 1, :CMP_HIDDEN] + fs[2 * nb + 1:2 * nb + 2, CMP_HIDDEN:]
               + b1_ref[kv])
        for g in range(NSA_KV_HEADS):
            col = (kv * NSA_KV_HEADS + g) * HEAD_DIM
            first = fs[g * nb:(g + 1) * nb, :CMP_HIDDEN]
            second = fs[g * nb:(g + 1) * nb, CMP_HIDDEN:]
            slot = kv * NSA_KV_HEADS + g
            prev = jnp.where(row == 0, carry_ref[slot], pltpu.roll(first, 1, 0))
            carry_ref[slot] = first[nb - 1:nb, :]
            h = _gelu_tanh(prev + second + pos)
            out_ref[:, col:col + HEAD_DIM] = jnp.dot(h.astype(BF), w2_ref[kv],
                                                     preferred_element_type=F32)


N_KV_SLOTS = 2 * NSA_KV_HEADS


def _compress_prompt_kernel(*refs):
    slots = refs[:N_KV_SLOTS]
    w1_ref, pe_ref, b1_ref, w2_ref, out_ref, carry_ref = refs[N_KV_SLOTS:]
    nb = slots[0].shape[0] // CMP_STRIDE

    def load(slot, l):
        return slots[slot][pl.ds(l, nb, stride=CMP_STRIDE), :]

    _compress_body(load, nb, w1_ref, pe_ref, b1_ref, w2_ref, out_ref, carry_ref,
                   pl.program_id(1) == 0)


def _compress_paged_kernel(pt_ref, *refs):
    del pt_ref
    pages = refs[:PAGES_PER_STEP]
    w1_ref, pe_ref, b1_ref, w2_ref, out_ref, carry_ref = refs[PAGES_PER_STEP:]
    group = N_KV_SLOTS * CMP_STRIDE
    per_page = pages[0].shape[0] // group
    blocks = jnp.concatenate([p[...].reshape(per_page, group, HEAD_DIM) for p in pages], 0)
    by_row = jnp.swapaxes(blocks, 0, 1)

    def load(slot, l):
        return by_row[l * N_KV_SLOTS + slot]

    _compress_body(load, per_page * PAGES_PER_STEP, w1_ref, pe_ref, b1_ref, w2_ref, out_ref,
                   carry_ref, pl.program_id(1) == 0)


def _compress_weight_specs():
    const3 = (lambda *a: (0, 0, 0))
    return [
        pl.BlockSpec((2, CMP_STRIDE * HEAD_DIM, 2 * CMP_HIDDEN), const3),
        pl.BlockSpec((2, SUBLANE, CMP_STRIDE * HEAD_DIM), const3),
        pl.BlockSpec((2, 1, CMP_HIDDEN), const3),
        pl.BlockSpec((2, CMP_HIDDEN, HEAD_DIM), const3),
    ]


def _compress_prompt(p_main, batch, seq, cw):
    chunk = min(seq, 2048)
    nchunk = seq // chunk
    kvw = 2 * NSA_KV_HEADS * HEAD_DIM
    return pl.pallas_call(
        _compress_prompt_kernel,
        grid=(batch, nchunk),
        in_specs=[pl.BlockSpec((chunk, HEAD_DIM), (lambda b, c, s=s: (b * nchunk + c, COL_KVC + s)))
                  for s in range(N_KV_SLOTS)] + _compress_weight_specs(),
        out_specs=pl.BlockSpec((None, chunk // CMP_STRIDE, kvw), lambda b, c: (b, c, 0)),
        out_shape=jax.ShapeDtypeStruct((batch, seq // CMP_STRIDE, kvw), F32),
        scratch_shapes=[pltpu.VMEM((2 * NSA_KV_HEADS, 1, CMP_HIDDEN), F32)],
        compiler_params=_cparams(("arbitrary", "arbitrary")),
        name="nsa_compress_prompt",
    )(*([p_main] * N_KV_SLOTS), *cw)


def _page_specs(page_rows, width, chunk_of):
    def spec(i):
        return pl.BlockSpec((None, page_rows, width),
                            lambda b, c, pt: (pt[b, chunk_of(c) * PAGES_PER_STEP + i], 0, 0))
    return [spec(i) for i in range(PAGES_PER_STEP)]


def _compress_paged(cache_rows, page_table, cw):
    n_pool, page_rows, _ = cache_rows.shape
    page = page_rows // N_KV_SLOTS
    kvw = N_KV_SLOTS * HEAD_DIM
    dec_batch, n_pages = page_table.shape
    nchunk = n_pages // PAGES_PER_STEP
    rows = PAGES_PER_STEP * page // CMP_STRIDE
    return pl.pallas_call(
        _compress_paged_kernel,
        grid_spec=pltpu.PrefetchScalarGridSpec(
            num_scalar_prefetch=1,
            grid=(dec_batch, nchunk),
            in_specs=_page_specs(page_rows, HEAD_DIM, lambda c: c) + _compress_weight_specs(),
            out_specs=pl.BlockSpec((None, rows, kvw), lambda b, c, pt: (b, c, 0)),
            scratch_shapes=[pltpu.VMEM((2 * NSA_KV_HEADS, 1, CMP_HIDDEN), F32)],
        ),
        out_shape=jax.ShapeDtypeStruct((dec_batch, nchunk * rows, kvw), F32),
        compiler_params=_cparams(("arbitrary", "arbitrary")),
        name="nsa_compress_paged",
    )(page_table, *([cache_rows] * PAGES_PER_STEP), *cw)


def _overlap_matrix(n_rows, n_blocks):
    n = lax.broadcasted_iota(jnp.int32, (n_rows, n_blocks), 0)
    s = lax.broadcasted_iota(jnp.int32, (n_rows, n_blocks), 1)
    c0 = (n - 1) * CMP_STRIDE
    hit = (n >= 1) & (c0 < s * SLC_BLOCK + SLC_BLOCK) & (c0 + CMP_BLOCK > s * SLC_BLOCK)
    return jnp.where(hit, 1.0, 0.0)


def _block_scores(imp, q_pos, n_blocks):
    blk = lax.broadcasted_iota(jnp.int32, imp.shape, 1)
    cur = q_pos // SLC_BLOCK
    forced = (blk == 0) | (blk == cur) | (blk == cur - 1)
    valid = (blk * SLC_BLOCK <= q_pos) & (blk < n_blocks)
    return jnp.where(valid, imp + jnp.where(forced, FORCE_BONUS, 0.0), NEG_INF)


def _stack_heads(ref, n):
    return jnp.concatenate([ref[:, r * HEAD_DIM:(r + 1) * HEAD_DIM] for r in range(n)], 0)


def _nsa_prompt_kernel(qraw_ref, qrot_ref, kc_ref, vc_ref, ks_ref, vs_ref, kw_ref, vw_ref,
                       gate_ref, o_ref, *, seq, kchunk):
    i = pl.program_id(2)
    q0 = i * Q_BLOCK
    nc = kc_ref.shape[0]
    ns = seq // SLC_BLOCK
    rep = NSA_REP
    t_col = lax.broadcasted_iota(jnp.int32, (Q_BLOCK, 1), 0) + q0

    q_raw = _stack_heads(qraw_ref, rep)
    s_c = (_dot_nt(q_raw, kc_ref[...]) * ATTN_SCALE).reshape(rep, Q_BLOCK, nc)
    n_row = lax.broadcasted_iota(jnp.int32, (1, nc), 1)
    vis = (n_row >= 1) & (n_row * CMP_STRIDE + (CMP_BLOCK - CMP_STRIDE - 1) <= t_col)
    p_c = _masked_softmax(s_c, vis[None])
    o_cmp = _dot(p_c.reshape(rep * Q_BLOCK, nc), vc_ref[...])
    imp = _dot_exact01(jnp.sum(p_c, 0), _overlap_matrix(nc, ns))
    sel = _topk_select(_block_scores(imp, t_col, ns), min(SLC_TOPK, ns), ns)
    sel_b = sel.astype(BF)

    q_rot = _stack_heads(qrot_ref, rep).astype(BF)
    blk_row = lax.broadcasted_iota(jnp.int32, (ns, kchunk), 0)
    k_lane = lax.broadcasted_iota(jnp.int32, (ns, kchunk), 1)
    k_lane1 = lax.broadcasted_iota(jnp.int32, (1, kchunk), 1)

    def slc_step(c, carry):
        k0 = pl.multiple_of(c * kchunk, kchunk)
        s = (_dot_nt(q_rot, ks_ref[pl.ds(k0, kchunk), :]) * ATTN_SCALE).reshape(rep, Q_BLOCK, kchunk)
        expand = jnp.where(blk_row == (k0 + k_lane) // SLC_BLOCK, 1.0, 0.0).astype(BF)
        chosen = jnp.dot(sel_b, expand, preferred_element_type=F32)
        mask = (chosen > 0.5) & (k0 + k_lane1 <= t_col)
        return _online_update(carry, s, mask[None], vs_ref[pl.ds(k0, kchunk), :])

    init = (jnp.full((rep, Q_BLOCK, 1), NEG_INF, F32), jnp.zeros((rep, Q_BLOCK, 1), F32),
            jnp.zeros((rep, Q_BLOCK, HEAD_DIM), F32))
    n_steps = (q0 + Q_BLOCK + kchunk - 1) // kchunk
    _, l_s, acc_s = lax.fori_loop(0, n_steps, slc_step, init)
    o_slc = (acc_s / jnp.maximum(l_s, 1e-30)).reshape(rep * Q_BLOCK, HEAD_DIM)

    wk = WINDOW + Q_BLOCK
    w0 = pl.multiple_of(jnp.maximum(q0 - WINDOW, 0), Q_BLOCK)
    s_w = (_dot_nt(q_rot, kw_ref[pl.ds(w0, wk), :]) * ATTN_SCALE).reshape(rep, Q_BLOCK, wk)
    dist = t_col - (w0 + lax.broadcasted_iota(jnp.int32, (1, wk), 1))
    p_w = _masked_softmax(s_w, ((dist >= 0) & (dist < WINDOW))[None])
    o_swa = _dot(p_w.reshape(rep * Q_BLOCK, wk), vw_ref[pl.ds(w0, wk), :])

    gt = gate_ref[...]
    for r in range(rep):
        rows = slice(r * Q_BLOCK, (r + 1) * Q_BLOCK)
        o_ref[:, r * HEAD_DIM:(r + 1) * HEAD_DIM] = (
            gt[:, 3 * r:3 * r + 1] * o_cmp[rows] + gt[:, 3 * r + 1:3 * r + 2] * o_slc[rows]
            + gt[:, 3 * r + 2:3 * r + 3] * o_swa[rows])


def _nsa_prompt(p_main, q_rot, kc_all, sig, batch, seq):
    nq = seq // Q_BLOCK
    gw = NSA_REP * HEAD_DIM
    kchunk = min(512, seq)
    col = lambda base, kv: (lambda b, g, i: (b, base + kv * NSA_KV_HEADS + g))
    return pl.pallas_call(
        functools.partial(_nsa_prompt_kernel, seq=seq, kchunk=kchunk),
        grid=(batch, NSA_KV_HEADS, nq),
        in_specs=[
            pl.BlockSpec((Q_BLOCK, gw), lambda b, g, i: (b * nq + i, g)),
            pl.BlockSpec((Q_BLOCK, gw), lambda b, g, i: (b * nq + i, g)),
            pl.BlockSpec((None, seq // CMP_STRIDE, HEAD_DIM), lambda b, g, i: (b, 0, g)),
            pl.BlockSpec((None, seq // CMP_STRIDE, HEAD_DIM), lambda b, g, i: (b, 0, NSA_KV_HEADS + g)),
            pl.BlockSpec((seq, HEAD_DIM), col(COL_KVS, 0)),
            pl.BlockSpec((seq, HEAD_DIM), col(COL_KVS, 1)),
            pl.BlockSpec((seq, HEAD_DIM), col(COL_KVW, 0)),
            pl.BlockSpec((seq, HEAD_DIM), col(COL_KVW, 1)),
            pl.BlockSpec((Q_BLOCK, LANE), lambda b, g, i: (b * nq + i, g)),
        ],
        out_specs=pl.BlockSpec((Q_BLOCK, gw), lambda b, g, i: (b * nq + i, g)),
        out_shape=jax.ShapeDtypeStruct((batch * seq, NSA_HEADS * HEAD_DIM), F32),
        compiler_params=_cparams(("arbitrary", "arbitrary", "arbitrary")),
        name="nsa_attention_prompt",
    )(p_main, q_rot, kc_all, kc_all, p_main, p_main, p_main, p_main, sig)


def _suffix_sum_lanes(x):
    n = x.shape[-1]
    lane = lax.broadcasted_iota(jnp.int32, x.shape, x.ndim - 1)
    k = 1
    while k < n:
        x = x + jnp.where(lane < n - k, pltpu.roll(x, n - k, x.ndim - 1), 0.0)
        k *= 2
    return x


def _decay_prompt_kernel(lf_ref, d_ref):
    lt = lf_ref[...].T
    top = lt[:SUBLANE]
    d_ref[...] = _suffix_sum_lanes(top) - top


def _decay_prompt(lsig, batch, seq):
    return pl.pallas_call(
        _decay_prompt_kernel,
        grid=(batch,),
        in_specs=[pl.BlockSpec((seq, LANE), lambda b: (b, 0))],
        out_specs=pl.BlockSpec((None, SUBLANE, seq), lambda b: (b, 0, 0)),
        out_shape=jax.ShapeDtypeStruct((batch, SUBLANE, seq), F32),
        compiler_params=_cparams(("arbitrary",)),
        name="fox_decay_prompt",
    )(lsig)


def _fox_prompt_kernel(q_ref, k_ref, v_ref, d_ref, o_ref, *, tq, tk):
    h = pl.program_id(1)
    i = pl.program_id(2)
    q0 = i * tq
    q = q_ref[...].astype(BF)
    t_col = lax.broadcasted_iota(jnp.int32, (tq, 1), 0) + q0
    k_lane = lax.broadcasted_iota(jnp.int32, (1, tk), 1)

    def step(c, carry):
        k0 = pl.multiple_of(c * tk, tk)
        s = _dot_nt(q, k_ref[pl.ds(k0, tk), :]) * ATTN_SCALE + d_ref[pl.ds(h, 1), pl.ds(k0, tk)]
        return _online_update(carry, s, k0 + k_lane <= t_col, v_ref[pl.ds(k0, tk), :])

    init = (jnp.full((tq, 1), NEG_INF, F32), jnp.zeros((tq, 1), F32),
            jnp.zeros((tq, HEAD_DIM), F32))
    _, l, acc = lax.fori_loop(0, (q0 + tq + tk - 1) // tk, step, init)
    o_ref[...] = acc / jnp.maximum(l, 1e-30)


def _fox_prompt(p_main, decay, batch, seq):
    tq = min(512, seq)
    tk = min(512, seq)
    nq = seq // tq
    return pl.pallas_call(
        functools.partial(_fox_prompt_kernel, tq=tq, tk=tk),
        grid=(batch, FOX_HEADS, nq),
        in_specs=[
            pl.BlockSpec((tq, HEAD_DIM), lambda b, h, i: (b * nq + i, COL_FOX + h)),
            pl.BlockSpec((seq, HEAD_DIM), lambda b, h, i: (b, COL_FOX + FOX_HEADS + h)),
            pl.BlockSpec((seq, HEAD_DIM), lambda b, h, i: (b, COL_FOX + 2 * FOX_HEADS + h)),
            pl.BlockSpec((None, SUBLANE, seq), lambda b, h, i: (b, 0, 0)),
        ],
        out_specs=pl.BlockSpec((tq, HEAD_DIM), lambda b, h, i: (b * nq + i, h)),
        out_shape=jax.ShapeDtypeStruct((batch * seq, FOX_HEADS * HEAD_DIM), F32),
        compiler_params=_cparams(("arbitrary", "arbitrary", "arbitrary")),
        name="fox_attention_prompt",
    )(p_main, p_main, p_main, decay)


def _mem_attn_kernel(q_ref, k_ref, v_ref, o_ref):
    s = _dot_nt(q_ref[...], k_ref[...]) * ATTN_SCALE
    m = jnp.max(s, -1, keepdims=True)
    e = jnp.exp(s - m)
    p = e / jnp.sum(e, -1, keepdims=True)
    o_ref[...] = _dot(p, v_ref[...])


def _mem_attention(p_main, mem_kv2d, batch, rows_per_batch, tq):
    nq = rows_per_batch // tq
    mlen = mem_kv2d.shape[0] // batch
    return pl.pallas_call(
        _mem_attn_kernel,
        grid=(batch, MEM_HEADS, nq),
        in_specs=[
            pl.BlockSpec((tq, HEAD_DIM), lambda b, h, i: (b * nq + i, COL_QM + h)),
            pl.BlockSpec((mlen, HEAD_DIM), lambda b, h, i: (b, h)),
            pl.BlockSpec((mlen, HEAD_DIM), lambda b, h, i: (b, MEM_HEADS + h)),
        ],
        out_specs=pl.BlockSpec((tq, HEAD_DIM), lambda b, h, i: (b * nq + i, h)),
        out_shape=jax.ShapeDtypeStruct((batch * rows_per_batch, MEM_HEADS * HEAD_DIM), F32),
        compiler_params=_cparams(("arbitrary", "arbitrary", "arbitrary")),
        name="mem_attention",
    )(p_main, mem_kv2d, mem_kv2d)


def _mem_attn_rows_kernel(q_ref, kv_ref, o_ref):
    h = pl.program_id(1)
    slots = 2 * MEM_HEADS
    mlen = kv_ref.shape[0] // slots
    s = _dot_nt(q_ref[...], kv_ref[pl.ds(h, mlen, stride=slots), :]) * ATTN_SCALE
    m = jnp.max(s, -1, keepdims=True)
    e = jnp.exp(s - m)
    p = e / jnp.sum(e, -1, keepdims=True)
    o_ref[...] = _dot(p, kv_ref[pl.ds(MEM_HEADS + h, mlen, stride=slots), :])


def _mem_attention_rows(p_s, mem_rows):
    dec_batch, rows, _ = mem_rows.shape
    return pl.pallas_call(
        _mem_attn_rows_kernel,
        grid=(dec_batch, MEM_HEADS),
        in_specs=[pl.BlockSpec((SROW, HEAD_DIM), lambda b, h: (b, COL_QM + h)),
                  pl.BlockSpec((None, rows, HEAD_DIM), lambda b, h: (b, 0, 0))],
        out_specs=pl.BlockSpec((SROW, HEAD_DIM), lambda b, h: (b, h)),
        out_shape=jax.ShapeDtypeStruct((dec_batch * SROW, MEM_HEADS * HEAD_DIM), F32),
        compiler_params=_cparams(("arbitrary", "arbitrary")),
        name="mem_attention_sample",
    )(p_s, mem_rows)


def _matmul_kernel(x_ref, w_ref, o_ref):
    o_ref[...] = jnp.dot(x_ref[...].astype(BF), w_ref[...], preferred_element_type=F32)


def _matmul(x2d, w_bf, tm, tn):
    m, k = x2d.shape
    n = w_bf.shape[1]
    return pl.pallas_call(
        _matmul_kernel,
        grid=(m // tm, n // tn),
        in_specs=[pl.BlockSpec((tm, k), lambda i, j: (i, 0)),
                  pl.BlockSpec((k, tn), lambda i, j: (0, j))],
        out_specs=pl.BlockSpec((tm, tn), lambda i, j: (i, j)),
        out_shape=jax.ShapeDtypeStruct((m, n), F32),
        compiler_params=_cparams(("arbitrary", "arbitrary")),
        name="projection_matmul",
    )(x2d, w_bf)


def _layer_norm(z, g, b):
    zc = z - jnp.mean(z, -1, keepdims=True)
    var = jnp.mean(zc * zc, -1, keepdims=True)
    return zc * lax.rsqrt(var + LN_EPS) * g + b


def _outproj_kernel(on_ref, of_ref, om_ref, x_ref, w_ref, g_ref, b_ref, h_ref, *, alpha):
    mix = jnp.concatenate([on_ref[...].astype(BF), of_ref[...].astype(BF),
                           om_ref[...].astype(BF)], 1)
    y = jnp.dot(mix, w_ref[...], preferred_element_type=F32)
    h_ref[...] = _layer_norm(alpha * x_ref[...] + y, g_ref[...], b_ref[...])


def _out_projection(o_nsa, o_fox, o_mem, x2d, w_out, g, b, tm, alpha):
    m, d = x2d.shape
    row = lambda w: pl.BlockSpec((tm, w), lambda i: (i, 0))
    const = lambda shape: pl.BlockSpec(shape, lambda i: (0, 0))
    return pl.pallas_call(
        functools.partial(_outproj_kernel, alpha=alpha),
        grid=(m // tm,),
        in_specs=[row(o_nsa.shape[1]), row(o_fox.shape[1]), row(o_mem.shape[1]), row(d),
                  const(w_out.shape), const((1, d)), const((1, d))],
        out_specs=row(d),
        out_shape=jax.ShapeDtypeStruct((m, d), F32),
        compiler_params=_cparams(("arbitrary",)),
        name="out_projection_ln",
    )(o_nsa, o_fox, o_mem, x2d, w_out, g, b)


def _ffn_core(h_ref, wa_ref, wb_ref, cw_ref, cb_ref, wd_ref, g_ref, b_ref, y_ref, acc_ref,
              shifted, alpha):
    j = pl.program_id(1)
    hb = h_ref[...].astype(BF)
    a = jnp.dot(hb, wa_ref[...], preferred_element_type=F32)
    gate_in = jnp.dot(hb, wb_ref[...], preferred_element_type=F32)
    a1, a2 = shifted(a)
    cw = cw_ref[...]
    c = cb_ref[...] + a2 * cw[0:1] + a1 * cw[1:2] + a * cw[2:3]
    act = (c * jax.nn.sigmoid(c)) * gate_in
    part = jnp.dot(act.astype(BF), wd_ref[...], preferred_element_type=F32)

    @pl.when(j == 0)
    def _():
        acc_ref[...] = part

    @pl.when(j > 0)
    def _():
        acc_ref[...] += part

    @pl.when(j == pl.num_programs(1) - 1)
    def _():
        y_ref[...] = _layer_norm(alpha * h_ref[...] + acc_ref[...], g_ref[...], b_ref[...])
    return a


def _ffn_prompt_kernel(h_ref, wa_ref, wb_ref, cw_ref, cb_ref, wd_ref, g_ref, b_ref,
                       y_ref, tail_ref, acc_ref, halo_ref, *, tiles_per_seq, alpha):
    i = pl.program_id(0)
    j = pl.program_id(1)
    tm = h_ref.shape[0]
    seq_start = (i % tiles_per_seq) == 0

    @pl.when(seq_start)
    def _():
        halo_ref[j] = jnp.zeros(halo_ref.shape[1:], F32)

    def shifted(a):
        row = lax.broadcasted_iota(jnp.int32, a.shape, 0)
        halo = halo_ref[j]
        h1 = halo[SUBLANE - 1:SUBLANE]
        h2 = halo[SUBLANE - 2:SUBLANE - 1]
        a1 = jnp.where(row == 0, h1, pltpu.roll(a, 1, 0))
        a2 = jnp.where(row == 0, h2, jnp.where(row == 1, h1, pltpu.roll(a, 2, 0)))
        return a1, a2

    a = _ffn_core(h_ref, wa_ref, wb_ref, cw_ref, cb_ref, wd_ref, g_ref, b_ref, y_ref, acc_ref,
                  shifted, alpha)
    halo_ref[j] = a[tm - SUBLANE:]
    tail_ref[...] = a[tm - SUBLANE:]


def _ffn_sample_kernel(h_ref, wa_ref, wb_ref, cw_ref, cb_ref, wd_ref, g_ref, b_ref, f1_ref, f2_ref,
                       y_ref, a_ref, acc_ref, *, alpha):
    def shifted(a):
        s = lax.broadcasted_iota(jnp.int32, a.shape, 0) % SROW
        a1 = jnp.where(s >= 1, pltpu.roll(a, 1, 0), 0.0) + f1_ref[...]
        a2 = jnp.where(s >= 2, pltpu.roll(a, 2, 0), 0.0) + f2_ref[...]
        return a1, a2

    a_ref[...] = _ffn_core(h_ref, wa_ref, wb_ref, cw_ref, cb_ref, wd_ref, g_ref, b_ref, y_ref,
                           acc_ref, shifted, alpha)


def _ffn_specs(tm, d, tf, nf):
    return [
        pl.BlockSpec((tm, d), lambda i, j: (i, 0)),
        pl.BlockSpec((d, tf), lambda i, j: (0, j)),
        pl.BlockSpec((d, tf), lambda i, j: (0, nf + j)),
        pl.BlockSpec((SUBLANE, tf), lambda i, j: (0, j)),
        pl.BlockSpec((1, tf), lambda i, j: (0, j)),
        pl.BlockSpec((tf, d), lambda i, j: (j, 0)),
        pl.BlockSpec((1, d), lambda i, j: (0, 0)),
        pl.BlockSpec((1, d), lambda i, j: (0, 0)),
    ]


def _ffn_prompt(h2d, fw, seq, tm, tf, alpha):
    w_up, conv_w8, conv_b, w_down, g, b = fw
    m, d = h2d.shape
    dff = w_down.shape[0]
    nf = dff // tf
    return pl.pallas_call(
        functools.partial(_ffn_prompt_kernel, tiles_per_seq=seq // tm, alpha=alpha),
        grid=(m // tm, nf),
        in_specs=_ffn_specs(tm, d, tf, nf),
        out_specs=[pl.BlockSpec((tm, d), lambda i, j: (i, 0)),
                   pl.BlockSpec((None, SUBLANE, tf), lambda i, j: (i, 0, j))],
        out_shape=[jax.ShapeDtypeStruct((m, d), F32),
                   jax.ShapeDtypeStruct((m // tm, SUBLANE, dff), F32)],
        scratch_shapes=[pltpu.VMEM((tm, d), F32), pltpu.VMEM((nf, SUBLANE, tf), F32)],
        compiler_params=_cparams(("arbitrary", "arbitrary")),
        name="conv_ffn_prompt",
    )(h2d, w_up, w_up, conv_w8, conv_b, w_down, g, b)


def _ffn_sample(h2d, fw, fill1, fill2, tf, alpha):
    w_up, conv_w8, conv_b, w_down, g, b = fw
    m, d = h2d.shape
    dff = w_down.shape[0]
    nf = dff // tf
    return pl.pallas_call(
        functools.partial(_ffn_sample_kernel, alpha=alpha),
        grid=(1, nf),
        in_specs=_ffn_specs(m, d, tf, nf) + [pl.BlockSpec((m, tf), lambda i, j: (0, j)),
                                             pl.BlockSpec((m, tf), lambda i, j: (0, j))],
        out_specs=[pl.BlockSpec((m, d), lambda i, j: (0, 0)),
                   pl.BlockSpec((m, tf), lambda i, j: (0, j))],
        out_shape=[jax.ShapeDtypeStruct((m, d), F32), jax.ShapeDtypeStruct((m, dff), F32)],
        scratch_shapes=[pltpu.VMEM((m, d), F32)],
        compiler_params=_cparams(("arbitrary", "arbitrary")),
        name="conv_ffn_sample",
    )(h2d, w_up, w_up, conv_w8, conv_b, w_down, g, b, fill1, fill2)


def _nsa_sample_a_kernel(qraw_ref, qrot_ref, kc_ref, vc_ref, swa_ref, kwn_ref, vwn_ref,
                         gate_ref, part_ref, sel_ref, *, past, n_blocks):
    rep = NSA_REP
    g = pl.program_id(1)
    nc = kc_ref.shape[0]
    nsp = sel_ref.shape[-1]
    s_col = lax.broadcasted_iota(jnp.int32, (SROW, 1), 0) + past

    q_raw = _stack_heads(qraw_ref, rep)
    s_c = (_dot_nt(q_raw, kc_ref[...]) * ATTN_SCALE).reshape(rep, SROW, nc)
    n_row = lax.broadcasted_iota(jnp.int32, (1, nc), 1)
    vis = (n_row >= 1) & (n_row * CMP_STRIDE + (CMP_BLOCK - CMP_STRIDE - 1) <= s_col)
    p_c = _masked_softmax(s_c, vis[None])
    o_cmp = _dot(p_c.reshape(rep * SROW, nc), vc_ref[...])
    imp = _dot_exact01(jnp.sum(p_c, 0), _overlap_matrix(nc, nsp))
    sel_ref[...] = _topk_select(_block_scores(imp, s_col, n_blocks), min(SLC_TOPK, n_blocks), n_blocks)

    q_rot = _stack_heads(qrot_ref, rep)
    wbuf = swa_ref.shape[0] // N_KV_SLOTS
    keys = jnp.concatenate([swa_ref[pl.ds(g, wbuf, stride=N_KV_SLOTS), :], kwn_ref[...]], 0)
    vals = jnp.concatenate([swa_ref[pl.ds(NSA_KV_HEADS + g, wbuf, stride=N_KV_SLOTS), :],
                            vwn_ref[...]], 0)
    w_pos = past - wbuf + lax.broadcasted_iota(jnp.int32, (1, wbuf + SROW), 1)
    dist = s_col - w_pos
    mask = (dist >= 0) & (dist < WINDOW) & (w_pos >= 0)
    s_w = (_dot_nt(q_rot, keys) * ATTN_SCALE).reshape(rep, SROW, wbuf + SROW)
    p_w = _masked_softmax(s_w, mask[None])
    o_swa = _dot(p_w.reshape(rep * SROW, wbuf + SROW), vals)

    gt = gate_ref[...]
    for r in range(rep):
        rows = slice(r * SROW, (r + 1) * SROW)
        part_ref[rows, :] = (gt[:, 3 * r:3 * r + 1] * o_cmp[rows]
                             + gt[:, 3 * r + 2:3 * r + 3] * o_swa[rows])


def _nsa_sample_a(p_s, qrot_s, kc_all, swa_rows, sig_s, past, n_blocks):
    dec_batch, nc, _ = kc_all.shape
    gw = NSA_REP * HEAD_DIM
    nsp = -(-n_blocks // LANE) * LANE
    bg = lambda b, g: (b, g)
    return pl.pallas_call(
        functools.partial(_nsa_sample_a_kernel, past=past, n_blocks=n_blocks),
        grid=(dec_batch, NSA_KV_HEADS),
        in_specs=[
            pl.BlockSpec((SROW, gw), bg),
            pl.BlockSpec((SROW, gw), bg),
            pl.BlockSpec((None, nc, HEAD_DIM), lambda b, g: (b, 0, g)),
            pl.BlockSpec((None, nc, HEAD_DIM), lambda b, g: (b, 0, NSA_KV_HEADS + g)),
            pl.BlockSpec((None, swa_rows.shape[1], HEAD_DIM), lambda b, g: (b, 0, 0)),
            pl.BlockSpec((SROW, HEAD_DIM), lambda b, g: (b, COL_KVW + g)),
            pl.BlockSpec((SROW, HEAD_DIM), lambda b, g: (b, COL_KVW + NSA_KV_HEADS + g)),
            pl.BlockSpec((SROW, LANE), bg),
        ],
        out_specs=[pl.BlockSpec((None, None, NSA_REP * SROW, HEAD_DIM), lambda b, g: (b, g, 0, 0)),
                   pl.BlockSpec((None, None, SROW, nsp), lambda b, g: (b, g, 0, 0))],
        out_shape=[jax.ShapeDtypeStruct((dec_batch, NSA_KV_HEADS, NSA_REP * SROW, HEAD_DIM), F32),
                   jax.ShapeDtypeStruct((dec_batch, NSA_KV_HEADS, SROW, nsp), F32)],
        compiler_params=_cparams(("arbitrary", "arbitrary")),
        name="nsa_sample_cmp_swa",
    )(p_s, qrot_s, kc_all, kc_all, swa_rows, p_s, p_s, sig_s)


def _nsa_sample_b_kernel(pt_ref, *refs, past, n_blocks):
    del pt_ref
    pages = refs[:PAGES_PER_STEP]
    (qrot_ref, sel_ref, knew_ref, vnew_ref, gate_ref, part_ref,
     o_ref, m_ref, l_ref, acc_ref) = refs[PAGES_PER_STEP:]
    c = pl.program_id(1)
    nchunk = pl.num_programs(1) - 1
    rep = NSA_REP
    page = pages[0].shape[0] // N_KV_SLOTS
    keys_per_step = PAGES_PER_STEP * page
    blocks_per_step = keys_per_step // SLC_BLOCK
    nsp = sel_ref.shape[-1]
    gw = rep * HEAD_DIM

    def cache_rows(slot):
        return jnp.concatenate([p[pl.ds(slot, page, stride=N_KV_SLOTS), :].astype(BF)
                                for p in pages], 0)

    @pl.when(c == 0)
    def _():
        m_ref[...] = jnp.full(m_ref.shape, NEG_INF, F32)
        l_ref[...] = jnp.zeros(l_ref.shape, F32)
        acc_ref[...] = jnp.zeros(acc_ref.shape, F32)

    def update(g, s, mask, v):
        carry = (m_ref[g], l_ref[g], acc_ref[g])
        m_new, l_new, acc_new = _online_update(carry, s, mask, v)
        m_ref[g] = m_new
        l_ref[g] = l_new
        acc_ref[g] = acc_new

    @pl.when(c < nchunk)
    def _():
        blk = lax.broadcasted_iota(jnp.int32, (nsp, blocks_per_step), 0)
        col = lax.broadcasted_iota(jnp.int32, (nsp, blocks_per_step), 1)
        pick = jnp.where(blk == c * blocks_per_step + col, 1.0, 0.0).astype(BF)
        b_row = lax.broadcasted_iota(jnp.int32, (blocks_per_step, keys_per_step), 0)
        k_lane = lax.broadcasted_iota(jnp.int32, (blocks_per_step, keys_per_step), 1)
        expand = jnp.where(b_row == k_lane // SLC_BLOCK, 1.0, 0.0).astype(BF)
        for g in range(NSA_KV_HEADS):
            q = _stack_heads(qrot_ref.at[:, g * gw:(g + 1) * gw], rep)
            s = (_dot_nt(q, cache_rows(g)) * ATTN_SCALE).reshape(rep, SROW, keys_per_step)
            sel_c = jnp.dot(sel_ref[g].astype(BF), pick, preferred_element_type=F32)
            chosen = jnp.dot(sel_c.astype(BF), expand, preferred_element_type=F32)
            update(g, s, (chosen > 0.5)[None], cache_rows(NSA_KV_HEADS + g))

    @pl.when(c == nchunk)
    def _():
        s_q = lax.broadcasted_iota(jnp.int32, (SROW, LANE), 0)
        s_k = lax.broadcasted_iota(jnp.int32, (SROW, LANE), 1)
        pad = jnp.zeros((LANE - SROW, HEAD_DIM), F32)
        gt = gate_ref[...]
        for g in range(NSA_KV_HEADS):
            q = _stack_heads(qrot_ref.at[:, g * gw:(g + 1) * gw], rep)
            k = jnp.concatenate([knew_ref[:, g * HEAD_DIM:(g + 1) * HEAD_DIM], pad], 0)
            v = jnp.concatenate([vnew_ref[:, g * HEAD_DIM:(g + 1) * HEAD_DIM], pad], 0)
            s = (_dot_nt(q, k) * ATTN_SCALE).reshape(rep, SROW, LANE)
            chosen = sel_ref[g][:, n_blocks - 1:n_blocks] > 0.5
            mask = chosen & (s_k <= s_q) & (s_k < SROW)
            update(g, s, mask[None], v)
            o_slc = acc_ref[g] / jnp.maximum(l_ref[g], 1e-30)
            for r in range(rep):
                col = 3 * rep * g + 3 * r + 1
                o_ref[:, (g * rep + r) * HEAD_DIM:(g * rep + r + 1) * HEAD_DIM] = (
                    part_ref[g, r * SROW:(r + 1) * SROW, :] + gt[:, col:col + 1] * o_slc[r])


def _nsa_sample_b(cache_rows, page_table, p_s, qrot_s, sel, sig12, part, past, n_blocks):
    n_pool, page_rows, _ = cache_rows.shape
    dec_batch, n_pages = page_table.shape
    nchunk = n_pages // PAGES_PER_STEP
    nsp = sel.shape[-1]
    qw = NSA_HEADS * HEAD_DIM
    half = NSA_KV_HEADS * HEAD_DIM
    row = lambda w, col: pl.BlockSpec((SROW, w), lambda b, c, pt: (b, col))
    return pl.pallas_call(
        functools.partial(_nsa_sample_b_kernel, past=past, n_blocks=n_blocks),
        grid_spec=pltpu.PrefetchScalarGridSpec(
            num_scalar_prefetch=1,
            grid=(dec_batch, nchunk + 1),
            in_specs=_page_specs(page_rows, HEAD_DIM, lambda c: jnp.minimum(c, nchunk - 1)) + [
                row(qw, 0),
                pl.BlockSpec((None, NSA_KV_HEADS, SROW, nsp), lambda b, c, pt: (b, 0, 0, 0)),
                row(half, COL_KVS * LANE // half),
                row(half, COL_KVS * LANE // half + 1),
                row(LANE, 0),
                pl.BlockSpec((None, NSA_KV_HEADS, NSA_REP * SROW, HEAD_DIM),
                             lambda b, c, pt: (b, 0, 0, 0)),
            ],
            out_specs=row(qw, 0),
            scratch_shapes=[pltpu.VMEM((NSA_KV_HEADS, NSA_REP, SROW, 1), F32),
                            pltpu.VMEM((NSA_KV_HEADS, NSA_REP, SROW, 1), F32),
                            pltpu.VMEM((NSA_KV_HEADS, NSA_REP, SROW, HEAD_DIM), F32)],
        ),
        out_shape=jax.ShapeDtypeStruct((dec_batch * SROW, qw), F32),
        compiler_params=_cparams(("arbitrary", "arbitrary")),
        name="nsa_sample_selected",
    )(page_table, *([cache_rows] * PAGES_PER_STEP), qrot_s, sel, p_s, p_s, sig12, part)


def _fox_sample_kernel(pt_ref, *refs, s_len):
    del pt_ref
    pages = refs[:PAGES_PER_STEP]
    lpages = refs[PAGES_PER_STEP:2 * PAGES_PER_STEP]
    (q_ref, knew_ref, vnew_ref, lnew_ref, o_ref, m_ref, l_ref, acc_ref, carry_ref) = refs[2 * PAGES_PER_STEP:]
    c = pl.program_id(1)
    slots = 2 * FOX_HEADS
    page = pages[0].shape[0] // slots
    head = lambda ref, h: ref[:, h * HEAD_DIM:(h + 1) * HEAD_DIM]

    def cache_rows(slot):
        return jnp.concatenate([p[pl.ds(slot, page, stride=slots), :].astype(BF) for p in pages], 0)

    def update(scores, mask, values):
        s = jnp.concatenate(scores, 0)
        if mask is not None:
            s = jnp.where(mask, s, NEG_INF)
        m_old = m_ref[...]
        m_new = jnp.maximum(m_old, jnp.max(s, -1, keepdims=True))
        p = jnp.exp(s - m_new)
        if mask is not None:
            p = jnp.where(mask, p, 0.0)
        alpha = jnp.exp(m_old - m_new)
        pv = jnp.concatenate([_dot(p[h * SROW:(h + 1) * SROW], values[h])
                              for h in range(FOX_HEADS)], 0)
        m_ref[...] = m_new
        l_ref[...] = alpha * l_ref[...] + jnp.sum(p, -1, keepdims=True)
        acc_ref[...] = alpha * acc_ref[...] + pv

    @pl.when(c == 0)
    def _():
        m_ref[...] = jnp.full(m_ref.shape, NEG_INF, F32)
        l_ref[...] = jnp.zeros(l_ref.shape, F32)
        acc_ref[...] = jnp.zeros(acc_ref.shape, F32)
        s_q = lax.broadcasted_iota(jnp.int32, (SROW, LANE), 0)
        s_k = lax.broadcasted_iota(jnp.int32, (SROW, LANE), 1)
        later = jnp.where((s_q > s_k) & (s_q < s_len), 1.0, 0.0)
        real = lax.broadcasted_iota(jnp.int32, (SROW, 1), 0) < s_len
        lnew = lnew_ref[...]
        pad = jnp.zeros((LANE - SROW, HEAD_DIM), F32)
        rows_q = lax.broadcasted_iota(jnp.int32, (FOX_HEADS * SROW, LANE), 0) % SROW
        cols_k = lax.broadcasted_iota(jnp.int32, (FOX_HEADS * SROW, LANE), 1)
        mask = (cols_k <= rows_q) & (cols_k < s_len)
        totals, scores, values = [], [], []
        for h in range(FOX_HEADS):
            lf = jnp.where(real, lnew[:, h:h + 1], 0.0)
            d_row = jnp.sum(lf * later, 0, keepdims=True)
            totals.append(jnp.sum(lf, 0, keepdims=True))
            k = jnp.concatenate([head(knew_ref, h), pad], 0)
            values.append(jnp.concatenate([head(vnew_ref, h), pad], 0))
            scores.append(_dot_nt(head(q_ref, h), k) * ATTN_SCALE + d_row)
        update(scores, mask, values)
        totals += [jnp.zeros((1, 1), F32)] * (SUBLANE - FOX_HEADS)
        carry_ref[...] = jnp.broadcast_to(jnp.concatenate(totals, 0), carry_ref.shape)

    @pl.when(c > 0)
    def _():
        lf = jnp.concatenate([p[...] for p in lpages], 1)
        incl = _suffix_sum_lanes(lf)
        carry = carry_ref[:, 0:1]
        decay = carry + (incl - lf)
        carry_ref[...] = jnp.broadcast_to(carry + incl[:, 0:1], carry_ref.shape)
        scores = [_dot_nt(head(q_ref, h), cache_rows(h)) * ATTN_SCALE + decay[h:h + 1, :]
                  for h in range(FOX_HEADS)]
        update(scores, None, [cache_rows(FOX_HEADS + h) for h in range(FOX_HEADS)])

    @pl.when(c == pl.num_programs(1) - 1)
    def _():
        o = acc_ref[...] / jnp.maximum(l_ref[...], 1e-30)
        for h in range(FOX_HEADS):
            o_ref[:, h * HEAD_DIM:(h + 1) * HEAD_DIM] = o[h * SROW:(h + 1) * SROW]


def _fox_sample(cache_rows, logf_t, page_table, p_s, lsig_s, s_len):
    n_pool, page_rows, _ = cache_rows.shape
    page = logf_t.shape[-1]
    dec_batch, n_pages = page_table.shape
    nchunk = n_pages // PAGES_PER_STEP
    fw = FOX_HEADS * HEAD_DIM
    chunk_of = lambda c: nchunk - jnp.maximum(c, 1)
    row = lambda w, col: pl.BlockSpec((SROW, w), lambda b, c, pt: (b, col))
    return pl.pallas_call(
        functools.partial(_fox_sample_kernel, s_len=s_len),
        grid_spec=pltpu.PrefetchScalarGridSpec(
            num_scalar_prefetch=1,
            grid=(dec_batch, nchunk + 1),
            in_specs=_page_specs(page_rows, HEAD_DIM, chunk_of) + _page_specs(SUBLANE, page, chunk_of) + [
                row(fw, COL_FOX * LANE // fw),
                row(fw, COL_FOX * LANE // fw + 1),
                row(fw, COL_FOX * LANE // fw + 2),
                row(LANE, 0),
            ],
            out_specs=row(fw, 0),
            scratch_shapes=[pltpu.VMEM((FOX_HEADS * SROW, 1), F32),
                            pltpu.VMEM((FOX_HEADS * SROW, 1), F32),
                            pltpu.VMEM((FOX_HEADS * SROW, HEAD_DIM), F32),
                            pltpu.VMEM((SUBLANE, LANE), F32)],
        ),
        out_shape=jax.ShapeDtypeStruct((dec_batch * SROW, fw), F32),
        compiler_params=_cparams(("arbitrary", "arbitrary")),
        name="fox_attention_sample",
    )(page_table, *([cache_rows] * PAGES_PER_STEP), *([logf_t] * PAGES_PER_STEP), p_s, p_s, p_s, lsig_s)


def _rope_tables(pos):
    half = ROT_DIM // 2
    inv = jnp.power(ROPE_THETA, -jnp.arange(half, dtype=F32) * (2.0 / ROT_DIM))
    ang = pos.astype(F32)[:, None] * inv[None, :]
    cos, sin = jnp.cos(ang), jnp.sin(ang)
    n = pos.shape[0]
    c = jnp.concatenate([cos, cos, jnp.ones((n, HEAD_DIM - ROT_DIM), F32)], 1)
    s = jnp.concatenate([-sin, sin, jnp.zeros((n, HEAD_DIM - ROT_DIM), F32)], 1)
    return c, s


def _layer_weights(w_in, b_gate, b_forget, cmp_w1, cmp_b1, cmp_w2, cmp_pos, w_mem_kv, w_out,
                   ln1_g, ln1_b, w_up, conv_w, conv_b, w_down, ln2_g, ln2_b):
    d = w_in.shape[0]
    nq = NSA_HEADS * HEAD_DIM
    kvw = 2 * NSA_KV_HEADS * HEAD_DIM
    o_g = nq + 3 * kvw
    o_fox = o_g + 3 * NSA_HEADS
    o_f = o_fox + 3 * FOX_HEADS * HEAD_DIM
    o_qm = o_f + FOX_HEADS
    w_main = jnp.concatenate([w_in[:, :o_g], w_in[:, o_fox:o_f], w_in[:, o_qm:]], 1).astype(BF)
    per_group = 3 * NSA_REP
    zpad = lambda n: jnp.zeros((d, n), w_in.dtype)
    w_small = jnp.concatenate([
        w_in[:, o_g:o_g + per_group], zpad(LANE - per_group),
        w_in[:, o_g + per_group:o_fox], zpad(LANE - per_group),
        w_in[:, o_f:o_qm], zpad(LANE - FOX_HEADS)], 1).astype(BF)
    bpad = lambda n: jnp.zeros((n,), F32)
    b_small = jnp.concatenate([
        b_gate[:per_group], bpad(LANE - per_group), b_gate[per_group:], bpad(LANE - per_group),
        b_forget, bpad(LANE - FOX_HEADS)])[None, :].astype(F32)
    kdim = CMP_STRIDE * HEAD_DIM
    w1cat = jnp.concatenate([cmp_w1[:, :CMP_STRIDE].reshape(2, kdim, CMP_HIDDEN),
                             cmp_w1[:, CMP_STRIDE:].reshape(2, kdim, CMP_HIDDEN)], 2).astype(BF)
    pe = jnp.concatenate([cmp_pos[:, :CMP_STRIDE].reshape(2, 1, kdim),
                          cmp_pos[:, CMP_STRIDE:].reshape(2, 1, kdim),
                          jnp.zeros((2, SUBLANE - 2, kdim), F32)], 1)
    cw = (w1cat, pe, cmp_b1[:, None, :], cmp_w2.astype(BF))
    conv_w8 = jnp.concatenate([conv_w, jnp.zeros((SUBLANE - CONV_W, conv_w.shape[1]), F32)], 0)
    fw = (w_up.astype(BF), conv_w8, conv_b[None, :], w_down.astype(BF), ln2_g[None, :], ln2_b[None, :])
    return dict(w_main=w_main, w_small=w_small, b_small=b_small, cw=cw,
                w_mem_kv=w_mem_kv.astype(BF), w_out=w_out.astype(BF),
                ln1_g=ln1_g[None, :], ln1_b=ln1_b[None, :], fw=fw)


def _prompt_layer(x, mem, lw, alpha):
    batch, seq, d = x.shape
    x2d = x.reshape(batch * seq, d)
    rc, rs = _rope_tables(jnp.tile(jnp.arange(seq, dtype=jnp.int32), batch))
    tm = min(512, seq)
    p_main, q_rot, sig, lsig = _input_projection(x2d, lw["w_main"], lw["w_small"], lw["b_small"],
                                                 rc, rs, tm)
    kc_all = _compress_prompt(p_main, batch, seq, lw["cw"])
    o_nsa = _nsa_prompt(p_main, q_rot, kc_all, sig, batch, seq)
    decay = _decay_prompt(lsig, batch, seq)
    o_fox = _fox_prompt(p_main, decay, batch, seq)
    mlen = mem.shape[1]
    mem_kv = _matmul(mem.reshape(batch * mlen, d), lw["w_mem_kv"], min(256, batch * mlen), 512)
    o_mem = _mem_attention(p_main, mem_kv, batch, seq, min(512, seq))
    h = _out_projection(o_nsa, o_fox, o_mem, x2d, lw["w_out"], lw["ln1_g"], lw["ln1_b"],
                        min(256, seq), alpha)
    y, tail = _ffn_prompt(h, lw["fw"], seq, tm, 512, alpha)
    kvw = 2 * NSA_KV_HEADS * HEAD_DIM
    cols = lambda c0, w: p_main[:, c0 * LANE:c0 * LANE + w].reshape(batch, seq, 2, -1, HEAD_DIM)
    n_win = min(WINDOW, seq)
    tiles = seq // tm
    conv_state = tail.reshape(batch, tiles, SUBLANE, -1)[:, -1, SUBLANE - (CONV_W - 1):]
    states = (cols(COL_KVC, kvw), cols(COL_KVS, kvw),
              cols(COL_FOX + FOX_HEADS, 2 * FOX_HEADS * HEAD_DIM),
              lsig[:, :FOX_HEADS].reshape(batch, seq, FOX_HEADS),
              cols(COL_KVW, kvw)[:, seq - n_win:],
              mem_kv.reshape(batch, mlen, 2, MEM_HEADS, HEAD_DIM), conv_state)
    return y.reshape(batch, seq, d), states


def _sample_layer(x, c_cmp, c_slc, c_fox, c_logf, c_swa, c_mem, s_conv, page_table, lw, alpha):
    dec_batch, s_len, d = x.shape
    n_pool, page = c_cmp.shape[0], c_cmp.shape[1]
    n_pages = page_table.shape[1]
    past = n_pages * page
    t_all = past + s_len
    assert s_len <= SROW and n_pages % PAGES_PER_STEP == 0
    assert (t_all // CMP_STRIDE) * CMP_STRIDE == past and past % SLC_BLOCK == 0
    n_blocks = -(-t_all // SLC_BLOCK)
    xp = jnp.pad(x, ((0, 0), (0, SROW - s_len), (0, 0))).reshape(dec_batch * SROW, d)
    pos = past + jnp.tile(jnp.arange(SROW, dtype=jnp.int32), dec_batch)
    rc, rs = _rope_tables(pos)
    p_s, qrot_s, sig_s, lsig_s = _input_projection(xp, lw["w_main"], lw["w_small"], lw["b_small"],
                                                   rc, rs, dec_batch * SROW)
    kvw = 2 * NSA_KV_HEADS * HEAD_DIM
    as_rows = lambda a: a.reshape(a.shape[0], -1, HEAD_DIM)
    kc_all = _compress_paged(as_rows(c_cmp), page_table, lw["cw"])
    per_group = 3 * NSA_REP
    sig12 = jnp.concatenate([sig_s[:, :per_group], sig_s[:, LANE:LANE + per_group],
                             jnp.zeros((dec_batch * SROW, LANE - 2 * per_group), F32)], 1)
    part, sel = _nsa_sample_a(p_s, qrot_s, kc_all, as_rows(c_swa), sig_s, past, n_blocks)
    o_nsa = _nsa_sample_b(as_rows(c_slc), page_table, p_s, qrot_s, sel, sig12, part, past, n_blocks)
    logf_t = jnp.pad(jnp.swapaxes(c_logf, 1, 2), ((0, 0), (0, SUBLANE - FOX_HEADS), (0, 0)))
    o_fox = _fox_sample(as_rows(c_fox), logf_t, page_table, p_s, lsig_s, s_len)
    o_mem = _mem_attention_rows(p_s, as_rows(c_mem))
    h = _out_projection(o_nsa, o_fox, o_mem, xp, lw["w_out"], lw["ln1_g"], lw["ln1_b"],
                        dec_batch * SROW, alpha)
    dff = s_conv.shape[-1]
    zrow = jnp.zeros((dec_batch, 1, dff), F32)
    fill1 = jnp.concatenate([s_conv[:, 1:2]] + [zrow] * (SROW - 1), 1).reshape(dec_batch * SROW, dff)
    fill2 = jnp.concatenate([s_conv[:, 0:1], s_conv[:, 1:2]] + [zrow] * (SROW - 2), 1)
    y, a = _ffn_sample(h, lw["fw"], fill1, fill2.reshape(dec_batch * SROW, dff), 512, alpha)

    def rows(arr):
        return arr.reshape(dec_batch, SROW, -1)[:, :s_len]
    cols = lambda c0, w: rows(p_s[:, c0 * LANE:c0 * LANE + w]).reshape(dec_batch, s_len, 2, -1, HEAD_DIM)
    new_kv_swa = cols(COL_KVW, kvw)
    new_swa = jnp.concatenate([c_swa, new_kv_swa], 1)[:, s_len:]
    conv_state = jnp.concatenate([s_conv, rows(a)], 1)[:, -(CONV_W - 1):]
    states = (cols(COL_KVC, kvw), cols(COL_KVS, kvw),
              cols(COL_FOX + FOX_HEADS, 2 * FOX_HEADS * HEAD_DIM),
              rows(lsig_s[:, :FOX_HEADS]), new_swa, conv_state)
    return rows(y), states


def kernel(x_prompt, x_sample, mem_prompt, cache_nsa_cmp, cache_nsa_slc, cache_fox_kv, cache_fox_logf, cache_nsa_swa, cache_mem, state_conv, page_table, w_in, b_gate, b_forget, cmp_w1, cmp_b1, cmp_w2, cmp_pos, w_mem_kv, w_out, ln1_g, ln1_b, w_up, conv_w, conv_b, w_down, ln2_g, ln2_b):
    depth = w_in.shape[0]
    alpha = float((2 * depth) ** 0.25)
    yp, ys = x_prompt, x_sample
    acc_p = [[] for _ in range(7)]
    acc_s = [[] for _ in range(6)]
    for l in range(depth):
        lw = _layer_weights(w_in[l], b_gate[l], b_forget[l], cmp_w1[l], cmp_b1[l], cmp_w2[l],
                            cmp_pos[l], w_mem_kv[l], w_out[l], ln1_g[l], ln1_b[l], w_up[l],
                            conv_w[l], conv_b[l], w_down[l], ln2_g[l], ln2_b[l])
        yp, st_p = _prompt_layer(yp, mem_prompt, lw, alpha)
        ys, st_s = _sample_layer(ys, cache_nsa_cmp[l], cache_nsa_slc[l], cache_fox_kv[l],
                                 cache_fox_logf[l], cache_nsa_swa[l], cache_mem[l], state_conv[l],
                                 page_table, lw, alpha)
        for lst, a in zip(acc_p, st_p):
            lst.append(a)
        for lst, a in zip(acc_s, st_s):
            lst.append(a)
    outs_p = [jnp.stack(a, 0) for a in acc_p]
    outs_s = [jnp.stack(a, 0) for a in acc_s]
    return (yp, ys, *outs_p, *outs_s)
```

```python
import functools

import jax
import jax.numpy as jnp
import numpy as np
from jax import lax
from jax.experimental import pallas as pl
from jax.experimental.pallas import tpu as pltpu

HEAD_DIM = 128
NSA_KV_HEADS = 2
NSA_REP = 4
NSA_HEADS = NSA_KV_HEADS * NSA_REP
FOX_HEADS = 4
MEM_HEADS = 4
CMP_BLOCK = 32
CMP_STRIDE = 16
CMP_HIDDEN = 256
SLC_BLOCK = 64
SLC_TOPK = 16
WINDOW = 512
Q_BLOCK = 128
ROT_DIM = HEAD_DIM // 4
ROPE_THETA = 500000.0
CONV_W = 3
LN_EPS = 1e-5
ATTN_SCALE = HEAD_DIM ** -0.5
LOG2E = 1.4426950408889634
NEG_INF = -1e30
FORCE_BONUS = 1e3

LANE = 128
SUBLANE = 8
SROW = SUBLANE
PAGES_PER_STEP = 16
CMP_PAGES_PER_STEP = 32
VMEM_LIMIT = 56 * 1024 * 1024

COL_Q = 0
COL_KVC = 8
COL_KVS = 12
COL_KVW = 16
COL_FOX = 20
COL_QM = 32
MAIN_W = 36 * LANE
IN_TILE = 512

BF = jnp.bfloat16
F32 = jnp.float32


def _dot(a, b):
    return jnp.dot(a.astype(BF), b.astype(BF), preferred_element_type=F32)


def _dot_nt(a, b):
    return lax.dot_general(a.astype(BF), b.astype(BF), (((1,), (1,)), ((), ())),
                           preferred_element_type=F32)


def _dot_exact01(x, m01, m01_first=False):
    hi = x.astype(BF)
    r1 = x - hi.astype(F32)
    mid = r1.astype(BF)
    lo = (r1 - mid.astype(F32)).astype(BF)
    m = m01.astype(BF)
    if m01_first:
        d = lambda a: lax.dot_general(m, a, (((1,), (1,)), ((), ())), preferred_element_type=F32)
    else:
        d = lambda a: jnp.dot(a, m, preferred_element_type=F32)
    return d(hi) + d(mid) + d(lo)


def _cparams(sem):
    return pltpu.CompilerParams(dimension_semantics=sem, vmem_limit_bytes=VMEM_LIMIT)


def _masked_softmax(s, mask):
    s = jnp.where(mask, s, NEG_INF)
    m = jnp.max(s, -1, keepdims=True)
    e = jnp.where(mask, jnp.exp(s - m), 0.0)
    return e / jnp.maximum(jnp.sum(e, -1, keepdims=True), 1e-30)


def _online_update(carry, s, mask, v):
    m, l, acc = carry
    if mask is not None:
        s = jnp.where(mask, s, NEG_INF)
    m_new = jnp.maximum(m, jnp.max(s, -1, keepdims=True))
    p = jnp.exp(s - m_new)
    if mask is not None:
        p = jnp.where(mask, p, 0.0)
    alpha = jnp.exp(m - m_new)
    l_new = alpha * l + jnp.sum(p, -1, keepdims=True)
    lead = p.shape[:-1]
    pv = _dot(p.reshape((-1, p.shape[-1])), v).reshape(lead + (v.shape[-1],))
    return m_new, l_new, alpha * acc + pv


def _flash_update(carry, s2, v):
    m, l, acc = carry
    m_new = jnp.maximum(m, jnp.max(s2, -1, keepdims=True))
    p = jnp.exp2(s2 - m_new)
    alpha = jnp.exp2(m - m_new)
    return (m_new, alpha * l + jnp.sum(p, -1, keepdims=True),
            alpha * acc + jnp.dot(p.astype(BF), v, preferred_element_type=F32))


def _flash_init(m_ref, l_ref, acc_ref):
    m_ref[...] = jnp.full(m_ref.shape, NEG_INF, F32)
    l_ref[...] = jnp.zeros(l_ref.shape, F32)
    acc_ref[...] = jnp.zeros(acc_ref.shape, F32)


def _flash_step(m_ref, l_ref, acc_ref, rows, s2, v):
    m_old = m_ref[rows]
    m_new = jnp.maximum(m_old, jnp.max(s2, -1, keepdims=True))
    p = jnp.exp2(s2 - m_new)
    alpha = jnp.exp2(m_old - m_new)
    l_ref[rows] = alpha * l_ref[rows] + jnp.sum(p, -1, keepdims=True)
    acc_ref[rows] = alpha * acc_ref[rows] + jnp.dot(p.astype(BF), v, preferred_element_type=F32)
    m_ref[rows] = m_new


def _rope_tile(x, c, s):
    lane = lax.broadcasted_iota(jnp.int32, x.shape, 1)
    half = ROT_DIM // 2
    swapped = jnp.where(lane < half, pltpu.roll(x, LANE - half, 1), pltpu.roll(x, half, 1))
    return x * c + swapped * s


def _topk_select(score, n_sel, n_valid_cols):
    col = lax.broadcasted_iota(jnp.int32, score.shape, 1)
    rank = jnp.zeros(score.shape, F32)
    for i in range(n_valid_cols):
        ci = score[:, i:i + 1]
        rank = rank + jnp.where(col > i, jnp.where(ci >= score, 1.0, 0.0),
                                jnp.where(ci > score, 1.0, 0.0))
    return jnp.where(rank < n_sel, 1.0, 0.0)


def _topk_select_rows(score_t, n_sel):
    n, width = score_t.shape
    rank = jnp.zeros(score_t.shape, F32)
    for i in range(n):
        ci = jnp.broadcast_to(score_t[i:i + 1, :], (SUBLANE, width))
        parts = []
        for j0 in range(0, n, SUBLANE):
            sj = score_t[j0:j0 + SUBLANE, :]
            if j0 > i:
                ahead = ci >= sj
            elif j0 + SUBLANE - 1 <= i:
                ahead = ci > sj
            else:
                row = lax.broadcasted_iota(jnp.int32, (SUBLANE, width), 0) + j0
                ahead = jnp.where(row > i, jnp.where(ci >= sj, 1.0, 0.0),
                                  jnp.where(ci > sj, 1.0, 0.0)) > 0.5
            parts.append(jnp.where(ahead, 1.0, 0.0))
        rank = rank + jnp.concatenate(parts, 0)
    return jnp.where(rank < n_sel, 1.0, 0.0)


def _inproj_kernel(x_ref, w_ref, ws_ref, bs_ref, c_ref, s_ref,
                   p_ref, qrot_ref, sig_ref, lsig_ref, xb_ref):
    j = pl.program_id(1)

    @pl.when(j == 0)
    def _():
        xb_ref[...] = x_ref[...].astype(BF)
        z = jnp.dot(xb_ref[...], ws_ref[...], preferred_element_type=F32) + bs_ref[...]
        sig_ref[...] = jax.nn.sigmoid(z[:, :2 * LANE])
        zf = z[:, 2 * LANE:]
        lsig_ref[...] = jnp.minimum(zf, 0.0) - jnp.log1p(jnp.exp(-jnp.abs(zf)))

    acc = jnp.dot(xb_ref[...], w_ref[...], preferred_element_type=F32)
    c = c_ref[...]
    s = s_ref[...]
    heads = IN_TILE // LANE

    @pl.when(j < 2)
    def _():
        p_ref[...] = acc
        for h in range(heads):
            qrot_ref[:, h * LANE:(h + 1) * LANE] = _rope_tile(acc[:, h * LANE:(h + 1) * LANE], c, s)

    @pl.when((j == COL_KVS // heads) | (j == COL_KVW // heads))
    def _():
        for h in range(heads):
            t = acc[:, h * LANE:(h + 1) * LANE]
            p_ref[:, h * LANE:(h + 1) * LANE] = _rope_tile(t, c, s) if h < NSA_KV_HEADS else t

    @pl.when((j >= 2) & (j != COL_KVS // heads) & (j != COL_KVW // heads))
    def _():
        p_ref[...] = acc


def _input_projection(x2d, w_main, w_small, b_small, rope_c, rope_s, tm):
    m, d = x2d.shape
    nj = MAIN_W // IN_TILE
    return pl.pallas_call(
        _inproj_kernel,
        grid=(m // tm, nj),
        in_specs=[
            pl.BlockSpec((tm, d), lambda i, j: (i, 0)),
            pl.BlockSpec((d, IN_TILE), lambda i, j: (0, j)),
            pl.BlockSpec((d, 3 * LANE), lambda i, j: (0, 0)),
            pl.BlockSpec((1, 3 * LANE), lambda i, j: (0, 0)),
            pl.BlockSpec((tm, LANE), lambda i, j: (i, 0)),
            pl.BlockSpec((tm, LANE), lambda i, j: (i, 0)),
        ],
        out_specs=[
            pl.BlockSpec((tm, IN_TILE), lambda i, j: (i, j)),
            pl.BlockSpec((tm, IN_TILE), lambda i, j: (i, jnp.minimum(j, 1))),
            pl.BlockSpec((tm, 2 * LANE), lambda i, j: (i, 0)),
            pl.BlockSpec((tm, LANE), lambda i, j: (i, 0)),
        ],
        out_shape=[
            jax.ShapeDtypeStruct((m, MAIN_W), F32),
            jax.ShapeDtypeStruct((m, NSA_HEADS * HEAD_DIM), F32),
            jax.ShapeDtypeStruct((m, 2 * LANE), F32),
            jax.ShapeDtypeStruct((m, LANE), F32),
        ],
        scratch_shapes=[pltpu.VMEM((tm, d), BF)],
        compiler_params=_cparams(("arbitrary", "arbitrary")),
        name="input_projection",
    )(x2d, w_main, w_small, b_small, rope_c, rope_s)


def _gelu_tanh(x):
    k = np.sqrt(2.0 / np.pi).astype(np.float32)
    return x * (0.5 * (1.0 + jnp.tanh(k * (x + 0.044715 * (x ** 3)))))


def _compress_body(load, nb, w1_ref, pe_ref, b1_ref, w2_ref, out_ref, carry_ref, first_step):
    @pl.when(first_step)
    def _():
        carry_ref[...] = jnp.zeros(carry_ref.shape, F32)

    row = lax.broadcasted_iota(jnp.int32, (nb, CMP_HIDDEN), 0)
    for kv in range(2):
        xs = []
        for g in range(NSA_KV_HEADS):
            per_l = [load(kv * NSA_KV_HEADS + g, l) for l in range(CMP_STRIDE)]
            xs.append(jnp.concatenate(per_l, 1).astype(BF))
        xs.append(pe_ref[kv].astype(BF))
        fs = jnp.dot(jnp.concatenate(xs, 0), w1_ref[kv], preferred_element_type=F32)
        pos = (fs[2 * nb:2 * nb + 1, :CMP_HIDDEN] + fs[2 * nb + 1:2 * nb + 2, CMP_HIDDEN:]
               + b1_ref[kv])
        for g in range(NSA_KV_HEADS):
            col = (kv * NSA_KV_HEADS + g) * HEAD_DIM
            first = fs[g * nb:(g + 1) * nb, :CMP_HIDDEN]
            second = fs[g * nb:(g + 1) * nb, CMP_HIDDEN:]
            slot = kv * NSA_KV_HEADS + g
            prev = jnp.where(row == 0, carry_ref[slot], pltpu.roll(first, 1, 0))
            carry_ref[slot] = first[nb - 1:nb, :]
            h = _gelu_tanh(prev + second + pos)
            out_ref[:, col:col + HEAD_DIM] = jnp.dot(h.astype(BF), w2_ref[kv],
                                                     preferred_element_type=F32)


N_KV_SLOTS = 2 * NSA_KV_HEADS


def _compress_prompt_kernel(*refs):
    slots = refs[:N_KV_SLOTS]
    w1_ref, pe_ref, b1_ref, w2_ref, out_ref, carry_ref = refs[N_KV_SLOTS:]
    nb = slots[0].shape[0] // CMP_STRIDE

    def load(slot, l):
        return slots[slot][pl.ds(l, nb, stride=CMP_STRIDE), :]

    _compress_body(load, nb, w1_ref, pe_ref, b1_ref, w2_ref, out_ref, carry_ref,
                   pl.program_id(1) == 0)


def _compress_paged_kernel(pt_ref, *refs):
    del pt_ref
    pages = refs[:CMP_PAGES_PER_STEP]
    w1_ref, pe_ref, b1_ref, w2_ref, out_ref, carry_ref = refs[CMP_PAGES_PER_STEP:]
    group = N_KV_SLOTS * CMP_STRIDE
    per_page = pages[0].shape[0] // group
    blocks = jnp.concatenate([p[...].reshape(per_page, group, HEAD_DIM) for p in pages], 0)
    by_row = jnp.swapaxes(blocks, 0, 1)

    def load(slot, l):
        return by_row[l * N_KV_SLOTS + slot]

    _compress_body(load, per_page * CMP_PAGES_PER_STEP, w1_ref, pe_ref, b1_ref, w2_ref, out_ref,
                   carry_ref, pl.program_id(1) == 0)


def _compress_weight_specs():
    const3 = (lambda *a: (0, 0, 0))
    return [
        pl.BlockSpec((2, CMP_STRIDE * HEAD_DIM, 2 * CMP_HIDDEN), const3),
        pl.BlockSpec((2, SUBLANE, CMP_STRIDE * HEAD_DIM), const3),
        pl.BlockSpec((2, 1, CMP_HIDDEN), const3),
        pl.BlockSpec((2, CMP_HIDDEN, HEAD_DIM), const3),
    ]


def _compress_prompt(p_main, batch, seq, cw):
    chunk = min(seq, 2048)
    nchunk = seq // chunk
    kvw = 2 * NSA_KV_HEADS * HEAD_DIM
    return pl.pallas_call(
        _compress_prompt_kernel,
        grid=(batch, nchunk),
        in_specs=[pl.BlockSpec((chunk, HEAD_DIM), (lambda b, c, s=s: (b * nchunk + c, COL_KVC + s)))
                  for s in range(N_KV_SLOTS)] + _compress_weight_specs(),
        out_specs=pl.BlockSpec((None, chunk // CMP_STRIDE, kvw), lambda b, c: (b, c, 0)),
        out_shape=jax.ShapeDtypeStruct((batch, seq // CMP_STRIDE, kvw), F32),
        scratch_shapes=[pltpu.VMEM((2 * NSA_KV_HEADS, 1, CMP_HIDDEN), F32)],
        compiler_params=_cparams(("arbitrary", "arbitrary")),
        name="nsa_compress_prompt",
    )(*([p_main] * N_KV_SLOTS), *cw)


def _page_specs(page_rows, width, chunk_of, per_step=PAGES_PER_STEP):
    def spec(i):
        return pl.BlockSpec((None, page_rows, width),
                            lambda b, c, pt: (pt[b, chunk_of(c) * per_step + i], 0, 0))
    return [spec(i) for i in range(per_step)]


def _compress_paged(cache_rows, page_table, cw):
    n_pool, page_rows, _ = cache_rows.shape
    page = page_rows // N_KV_SLOTS
    kvw = N_KV_SLOTS * HEAD_DIM
    dec_batch, n_pages = page_table.shape
    nchunk = n_pages // CMP_PAGES_PER_STEP
    rows = CMP_PAGES_PER_STEP * page // CMP_STRIDE
    return pl.pallas_call(
        _compress_paged_kernel,
        grid_spec=pltpu.PrefetchScalarGridSpec(
            num_scalar_prefetch=1,
            grid=(dec_batch, nchunk),
            in_specs=(_page_specs(page_rows, HEAD_DIM, lambda c: c, CMP_PAGES_PER_STEP)
                      + _compress_weight_specs()),
            out_specs=pl.BlockSpec((None, rows, kvw), lambda b, c, pt: (b, c, 0)),
            scratch_shapes=[pltpu.VMEM((2 * NSA_KV_HEADS, 1, CMP_HIDDEN), F32)],
        ),
        out_shape=jax.ShapeDtypeStruct((dec_batch, nchunk * rows, kvw), F32),
        compiler_params=_cparams(("arbitrary", "arbitrary")),
        name="nsa_compress_paged",
    )(page_table, *([cache_rows] * CMP_PAGES_PER_STEP), *cw)


def _overlap_matrix(n_rows, n_blocks, blocks_first=False):
    shape = (n_blocks, n_rows) if blocks_first else (n_rows, n_blocks)
    n = lax.broadcasted_iota(jnp.int32, shape, 1 if blocks_first else 0)
    s = lax.broadcasted_iota(jnp.int32, shape, 0 if blocks_first else 1)
    c0 = (n - 1) * CMP_STRIDE
    hit = (n >= 1) & (c0 < s * SLC_BLOCK + SLC_BLOCK) & (c0 + CMP_BLOCK > s * SLC_BLOCK)
    return jnp.where(hit, 1.0, 0.0)


def _block_scores(imp, q_pos, n_blocks, block_axis=1):
    blk = lax.broadcasted_iota(jnp.int32, imp.shape, block_axis)
    cur = q_pos // SLC_BLOCK
    forced = (blk == 0) | (blk == cur) | (blk == cur - 1)
    valid = (blk * SLC_BLOCK <= q_pos) & (blk < n_blocks)
    return jnp.where(valid, imp + jnp.where(forced, FORCE_BONUS, 0.0), NEG_INF)


def _head_rows(tile_ref, h):
    tokens, heads, width = tile_ref.shape
    return tile_ref.reshape(tokens * heads, width)[pl.ds(h, tokens, stride=heads), :]


def _stack_heads(ref, n):
    return jnp.concatenate([ref[:, r * HEAD_DIM:(r + 1) * HEAD_DIM] for r in range(n)], 0)


def _nsa_prompt_kernel(qraw_ref, qrot_ref, kc_ref, vc_ref, ks_ref, vs_ref, kw_ref, vw_ref,
                       gate_ref, o_ref, ksb_ref, vsb_ref, kwb_ref, vwb_ref, *, seq, kchunk):
    i = pl.program_id(2)
    q0 = i * Q_BLOCK
    nc = kc_ref.shape[0]
    ns = seq // SLC_BLOCK
    rep = NSA_REP
    t_col = lax.broadcasted_iota(jnp.int32, (Q_BLOCK, 1), 0) + q0
    over_heads = lambda x: jnp.concatenate([x] * rep, 0)
    nt = (((1,), (1,)), ((), ()))

    @pl.when(i == 0)
    def _():
        ksb_ref[...] = ks_ref[...].astype(BF)
        vsb_ref[...] = vs_ref[...].astype(BF)
        kwb_ref[...] = kw_ref[...].astype(BF)
        vwb_ref[...] = vw_ref[...].astype(BF)

    q_raw = (_stack_heads(qraw_ref, rep) * (ATTN_SCALE * LOG2E)).astype(BF)
    n_row = lax.broadcasted_iota(jnp.int32, (1, nc), 1)
    vis = (n_row >= 1) & (n_row * CMP_STRIDE + (CMP_BLOCK - CMP_STRIDE - 1) <= t_col)
    s_c = (lax.dot_general(q_raw, kc_ref[...].astype(BF), nt, preferred_element_type=F32)
           + over_heads(jnp.where(vis, 0.0, NEG_INF)))
    e_c = jnp.exp2(s_c - jnp.max(s_c, -1, keepdims=True)) * over_heads(jnp.where(vis, 1.0, 0.0))
    p_c = e_c / jnp.maximum(jnp.sum(e_c, -1, keepdims=True), 1e-30)
    o_cmp = _dot(p_c, vc_ref[...])
    p_sum = p_c[0:Q_BLOCK]
    for r in range(1, rep):
        p_sum = p_sum + p_c[r * Q_BLOCK:(r + 1) * Q_BLOCK]
    imp_t = _dot_exact01(p_sum, _overlap_matrix(nc, ns, blocks_first=True), m01_first=True)
    t_row = lax.broadcasted_iota(jnp.int32, (1, Q_BLOCK), 1) + q0
    sel_t = _topk_select_rows(_block_scores(imp_t, t_row, ns, block_axis=0), min(SLC_TOPK, ns))
    sel_b = sel_t.T.astype(BF)

    q_rot = (_stack_heads(qrot_ref, rep) * (ATTN_SCALE * LOG2E)).astype(BF)
    per_chunk = kchunk // SLC_BLOCK
    blk = lax.broadcasted_iota(jnp.int32, (ns, per_chunk), 0)
    slot = lax.broadcasted_iota(jnp.int32, (ns, per_chunk), 1)
    b_row = lax.broadcasted_iota(jnp.int32, (per_chunk, kchunk), 0)
    b_lane = lax.broadcasted_iota(jnp.int32, (per_chunk, kchunk), 1)
    expand = jnp.where(b_row == b_lane // SLC_BLOCK, 1.0, 0.0).astype(BF)
    k_lane = lax.broadcasted_iota(jnp.int32, (1, kchunk), 1)

    def slc_chunk(c, carry, diagonal):
        k0 = pl.multiple_of(c * kchunk, kchunk)
        pick = jnp.where(blk == c * per_chunk + slot, 1.0, 0.0).astype(BF)
        sel_c = jnp.dot(sel_b, pick, preferred_element_type=F32).astype(BF)
        keep = jnp.dot(sel_c, expand, preferred_element_type=F32) > 0.5
        if diagonal:
            keep = keep & (k0 + k_lane <= t_col)
        s2 = (lax.dot_general(q_rot, ksb_ref[pl.ds(k0, kchunk), :], nt, preferred_element_type=F32)
              + over_heads(jnp.where(keep, 0.0, NEG_INF)))
        return _flash_update(carry, s2, vsb_ref[pl.ds(k0, kchunk), :])

    init = (jnp.full((rep * Q_BLOCK, 1), NEG_INF, F32), jnp.zeros((rep * Q_BLOCK, 1), F32),
            jnp.zeros((rep * Q_BLOCK, HEAD_DIM), F32))
    last = q0 // kchunk
    carry = lax.fori_loop(0, last, lambda c, carry: slc_chunk(c, carry, False), init)
    _, l_s, acc_s = slc_chunk(last, carry, True)
    o_slc = acc_s / jnp.maximum(l_s, 1e-30)

    wk = WINDOW + Q_BLOCK
    w0 = pl.multiple_of(jnp.maximum(q0 - WINDOW, 0), Q_BLOCK)
    dist = t_col - (w0 + lax.broadcasted_iota(jnp.int32, (1, wk), 1))
    s_w = (lax.dot_general(q_rot, kwb_ref[pl.ds(w0, wk), :], nt, preferred_element_type=F32)
           + over_heads(jnp.where((dist >= 0) & (dist < WINDOW), 0.0, NEG_INF)))
    e_w = jnp.exp2(s_w - jnp.max(s_w, -1, keepdims=True))
    p_w = e_w / jnp.sum(e_w, -1, keepdims=True)
    o_swa = jnp.dot(p_w.astype(BF), vwb_ref[pl.ds(w0, wk), :], preferred_element_type=F32)

    gt = gate_ref[...]
    for r in range(rep):
        rows = slice(r * Q_BLOCK, (r + 1) * Q_BLOCK)
        o_ref[:, r * HEAD_DIM:(r + 1) * HEAD_DIM] = (
            gt[:, 3 * r:3 * r + 1] * o_cmp[rows] + gt[:, 3 * r + 1:3 * r + 2] * o_slc[rows]
            + gt[:, 3 * r + 2:3 * r + 3] * o_swa[rows])


def _nsa_prompt(p_main, q_rot, kc_all, sig, batch, seq):
    nq = seq // Q_BLOCK
    gw = NSA_REP * HEAD_DIM
    kchunk = min(512, seq)
    col = lambda base, kv: (lambda b, g, i: (b, base + kv * NSA_KV_HEADS + g))
    return pl.pallas_call(
        functools.partial(_nsa_prompt_kernel, seq=seq, kchunk=kchunk),
        grid=(batch, NSA_KV_HEADS, nq),
        in_specs=[
            pl.BlockSpec((Q_BLOCK, gw), lambda b, g, i: (b * nq + i, g)),
            pl.BlockSpec((Q_BLOCK, gw), lambda b, g, i: (b * nq + i, g)),
            pl.BlockSpec((None, seq // CMP_STRIDE, HEAD_DIM), lambda b, g, i: (b, 0, g)),
            pl.BlockSpec((None, seq // CMP_STRIDE, HEAD_DIM), lambda b, g, i: (b, 0, NSA_KV_HEADS + g)),
            pl.BlockSpec((seq, HEAD_DIM), col(COL_KVS, 0)),
            pl.BlockSpec((seq, HEAD_DIM), col(COL_KVS, 1)),
            pl.BlockSpec((seq, HEAD_DIM), col(COL_KVW, 0)),
            pl.BlockSpec((seq, HEAD_DIM), col(COL_KVW, 1)),
            pl.BlockSpec((Q_BLOCK, LANE), lambda b, g, i: (b * nq + i, g)),
        ],
        out_specs=pl.BlockSpec((Q_BLOCK, gw), lambda b, g, i: (b * nq + i, g)),
        out_shape=jax.ShapeDtypeStruct((batch * seq, NSA_HEADS * HEAD_DIM), F32),
        scratch_shapes=[pltpu.VMEM((seq, HEAD_DIM), BF)] * 4,
        compiler_params=_cparams(("arbitrary", "arbitrary", "arbitrary")),
        name="nsa_attention_prompt",
    )(p_main, q_rot, kc_all, kc_all, p_main, p_main, p_main, p_main, sig)


def _suffix_sum_lanes(x):
    n = x.shape[-1]
    lane = lax.broadcasted_iota(jnp.int32, x.shape, x.ndim - 1)
    k = 1
    while k < n:
        x = x + jnp.where(lane < n - k, pltpu.roll(x, n - k, x.ndim - 1), 0.0)
        k *= 2
    return x


def _decay_prompt_kernel(lf_ref, d_ref):
    lt = lf_ref[...].T
    top = lt[:SUBLANE]
    d_ref[...] = (_suffix_sum_lanes(top) - top) * LOG2E


def _decay_prompt(lsig, batch, seq):
    return pl.pallas_call(
        _decay_prompt_kernel,
        grid=(batch,),
        in_specs=[pl.BlockSpec((seq, LANE), lambda b: (b, 0))],
        out_specs=pl.BlockSpec((None, SUBLANE, seq), lambda b: (b, 0, 0)),
        out_shape=jax.ShapeDtypeStruct((batch, SUBLANE, seq), F32),
        compiler_params=_cparams(("arbitrary",)),
        name="fox_decay_prompt",
    )(lsig)


def _fox_prompt_kernel(q_ref, k_ref, v_ref, d_ref, o_ref, kb_ref, vb_ref, m_ref, l_ref, acc_ref,
                       *, tile):
    h = pl.program_id(1)
    i = pl.program_id(2)

    @pl.when(i == 0)
    def _():
        kb_ref[...] = k_ref[...].astype(BF)
        vb_ref[...] = v_ref[...].astype(BF)

    q = (q_ref[...] * (ATTN_SCALE * LOG2E)).astype(BF)
    row = lax.broadcasted_iota(jnp.int32, (tile, tile), 0)
    col = lax.broadcasted_iota(jnp.int32, (tile, tile), 1)

    def chunk(c, carry, diagonal):
        k0 = pl.multiple_of(c * tile, tile)
        s2 = lax.dot_general(q, kb_ref[pl.ds(k0, tile), :], (((1,), (1,)), ((), ())),
                             preferred_element_type=F32) + d_ref[pl.ds(h, 1), pl.ds(k0, tile)]
        if diagonal:
            s2 = jnp.where(col <= row, s2, NEG_INF)
        return _flash_update(carry, s2, vb_ref[pl.ds(k0, tile), :])

    init = (jnp.full((tile, 1), NEG_INF, F32), jnp.zeros((tile, 1), F32),
            jnp.zeros((tile, HEAD_DIM), F32))
    carry = lax.fori_loop(0, i, lambda c, carry: chunk(c, carry, False), init)
    _, l, acc = chunk(i, carry, True)
    o_ref[...] = acc / jnp.maximum(l, 1e-30)


def _fox_prompt(p_main, decay, batch, seq):
    tq = min(512, seq)
    nq = seq // tq
    return pl.pallas_call(
        functools.partial(_fox_prompt_kernel, tile=tq),
        grid=(batch, FOX_HEADS, nq),
        scratch_shapes=[pltpu.VMEM((seq, HEAD_DIM), BF), pltpu.VMEM((seq, HEAD_DIM), BF),
                        pltpu.VMEM((tq, 1), F32), pltpu.VMEM((tq, 1), F32),
                        pltpu.VMEM((tq, HEAD_DIM), F32)],
        in_specs=[
            pl.BlockSpec((tq, HEAD_DIM), lambda b, h, i: (b * nq + i, COL_FOX + h)),
            pl.BlockSpec((seq, HEAD_DIM), lambda b, h, i: (b, COL_FOX + FOX_HEADS + h)),
            pl.BlockSpec((seq, HEAD_DIM), lambda b, h, i: (b, COL_FOX + 2 * FOX_HEADS + h)),
            pl.BlockSpec((None, SUBLANE, seq), lambda b, h, i: (b, 0, 0)),
        ],
        out_specs=pl.BlockSpec((tq, HEAD_DIM), lambda b, h, i: (b * nq + i, h)),
        out_shape=jax.ShapeDtypeStruct((batch * seq, FOX_HEADS * HEAD_DIM), F32),
        compiler_params=_cparams(("arbitrary", "arbitrary", "arbitrary")),
        name="fox_attention_prompt",
    )(p_main, p_main, p_main, decay)


def _mem_attn_kernel(q_ref, k_ref, v_ref, o_ref):
    s = _dot_nt(q_ref[...], k_ref[...]) * ATTN_SCALE
    m = jnp.max(s, -1, keepdims=True)
    e = jnp.exp(s - m)
    p = e / jnp.sum(e, -1, keepdims=True)
    o_ref[...] = _dot(p, v_ref[...])


def _mem_attention(p_main, mem_kv2d, batch, rows_per_batch, tq):
    nq = rows_per_batch // tq
    mlen = mem_kv2d.shape[0] // batch
    return pl.pallas_call(
        _mem_attn_kernel,
        grid=(batch, MEM_HEADS, nq),
        in_specs=[
            pl.BlockSpec((tq, HEAD_DIM), lambda b, h, i: (b * nq + i, COL_QM + h)),
            pl.BlockSpec((mlen, HEAD_DIM), lambda b, h, i: (b, h)),
            pl.BlockSpec((mlen, HEAD_DIM), lambda b, h, i: (b, MEM_HEADS + h)),
        ],
        out_specs=pl.BlockSpec((tq, HEAD_DIM), lambda b, h, i: (b * nq + i, h)),
        out_shape=jax.ShapeDtypeStruct((batch * rows_per_batch, MEM_HEADS * HEAD_DIM), F32),
        compiler_params=_cparams(("arbitrary", "arbitrary", "arbitrary")),
        name="mem_attention",
    )(p_main, mem_kv2d, mem_kv2d)


def _mem_attn_rows_kernel(q_ref, kv_ref, o_ref):
    h = pl.program_id(1)
    slots = 2 * MEM_HEADS
    mlen = kv_ref.shape[0] // slots
    s = _dot_nt(q_ref[...], kv_ref[pl.ds(h, mlen, stride=slots), :]) * ATTN_SCALE
    m = jnp.max(s, -1, keepdims=True)
    e = jnp.exp(s - m)
    p = e / jnp.sum(e, -1, keepdims=True)
    o_ref[...] = _dot(p, kv_ref[pl.ds(MEM_HEADS + h, mlen, stride=slots), :])


def _mem_attention_rows(p_s, mem_rows):
    dec_batch, rows, _ = mem_rows.shape
    return pl.pallas_call(
        _mem_attn_rows_kernel,
        grid=(dec_batch, MEM_HEADS),
        in_specs=[pl.BlockSpec((SROW, HEAD_DIM), lambda b, h: (b, COL_QM + h)),
                  pl.BlockSpec((None, rows, HEAD_DIM), lambda b, h: (b, 0, 0))],
        out_specs=pl.BlockSpec((SROW, HEAD_DIM), lambda b, h: (b, h)),
        out_shape=jax.ShapeDtypeStruct((dec_batch * SROW, MEM_HEADS * HEAD_DIM), F32),
        compiler_params=_cparams(("arbitrary", "arbitrary")),
        name="mem_attention_sample",
    )(p_s, mem_rows)


def _matmul_kernel(x_ref, w_ref, o_ref):
    o_ref[...] = jnp.dot(x_ref[...].astype(BF), w_ref[...], preferred_element_type=F32)


def _matmul(x2d, w_bf, tm, tn):
    m, k = x2d.shape
    n = w_bf.shape[1]
    return pl.pallas_call(
        _matmul_kernel,
        grid=(m // tm, n // tn),
        in_specs=[pl.BlockSpec((tm, k), lambda i, j: (i, 0)),
                  pl.BlockSpec((k, tn), lambda i, j: (0, j))],
        out_specs=pl.BlockSpec((tm, tn), lambda i, j: (i, j)),
        out_shape=jax.ShapeDtypeStruct((m, n), F32),
        compiler_params=_cparams(("arbitrary", "arbitrary")),
        name="projection_matmul",
    )(x2d, w_bf)


def _layer_norm(z, g, b):
    zc = z - jnp.mean(z, -1, keepdims=True)
    var = jnp.mean(zc * zc, -1, keepdims=True)
    return zc * lax.rsqrt(var + LN_EPS) * g + b


def _outproj_kernel(on_ref, of_ref, om_ref, x_ref, w_ref, g_ref, b_ref, h_ref, *, alpha):
    mix = jnp.concatenate([on_ref[...].astype(BF), of_ref[...].astype(BF),
                           om_ref[...].astype(BF)], 1)
    y = jnp.dot(mix, w_ref[...], preferred_element_type=F32)
    h_ref[...] = _layer_norm(alpha * x_ref[...] + y, g_ref[...], b_ref[...])


def _out_projection(o_nsa, o_fox, o_mem, x2d, w_out, g, b, tm, alpha):
    m, d = x2d.shape
    row = lambda w: pl.BlockSpec((tm, w), lambda i: (i, 0))
    const = lambda shape: pl.BlockSpec(shape, lambda i: (0, 0))
    return pl.pallas_call(
        functools.partial(_outproj_kernel, alpha=alpha),
        grid=(m // tm,),
        in_specs=[row(o_nsa.shape[1]), row(o_fox.shape[1]), row(o_mem.shape[1]), row(d),
                  const(w_out.shape), const((1, d)), const((1, d))],
        out_specs=row(d),
        out_shape=jax.ShapeDtypeStruct((m, d), F32),
        compiler_params=_cparams(("arbitrary",)),
        name="out_projection_ln",
    )(o_nsa, o_fox, o_mem, x2d, w_out, g, b)


def _ffn_core(h_ref, wa_ref, wb_ref, cw_ref, cb_ref, wd_ref, g_ref, b_ref, y_ref, acc_ref,
              shifted, alpha):
    j = pl.program_id(1)
    hb = h_ref[...].astype(BF)
    a = jnp.dot(hb, wa_ref[...], preferred_element_type=F32)
    gate_in = jnp.dot(hb, wb_ref[...], preferred_element_type=F32)
    a1, a2 = shifted(a)
    cw = cw_ref[...]
    c = cb_ref[...] + a2 * cw[0:1] + a1 * cw[1:2] + a * cw[2:3]
    act = (c * jax.nn.sigmoid(c)) * gate_in
    part = jnp.dot(act.astype(BF), wd_ref[...], preferred_element_type=F32)

    @pl.when(j == 0)
    def _():
        acc_ref[...] = part

    @pl.when(j > 0)
    def _():
        acc_ref[...] += part

    @pl.when(j == pl.num_programs(1) - 1)
    def _():
        y_ref[...] = _layer_norm(alpha * h_ref[...] + acc_ref[...], g_ref[...], b_ref[...])
    return a


def _ffn_prompt_kernel(h_ref, wa_ref, wb_ref, cw_ref, cb_ref, wd_ref, g_ref, b_ref,
                       y_ref, tail_ref, acc_ref, halo_ref, *, tiles_per_seq, alpha):
    i = pl.program_id(0)
    j = pl.program_id(1)
    tm = h_ref.shape[0]
    seq_start = (i % tiles_per_seq) == 0

    @pl.when(seq_start)
    def _():
        halo_ref[j] = jnp.zeros(halo_ref.shape[1:], F32)

    def shifted(a):
        row = lax.broadcasted_iota(jnp.int32, a.shape, 0)
        halo = halo_ref[j]
        h1 = halo[SUBLANE - 1:SUBLANE]
        h2 = halo[SUBLANE - 2:SUBLANE - 1]
        a1 = jnp.where(row == 0, h1, pltpu.roll(a, 1, 0))
        a2 = jnp.where(row == 0, h2, jnp.where(row == 1, h1, pltpu.roll(a, 2, 0)))
        return a1, a2

    a = _ffn_core(h_ref, wa_ref, wb_ref, cw_ref, cb_ref, wd_ref, g_ref, b_ref, y_ref, acc_ref,
                  shifted, alpha)
    halo_ref[j] = a[tm - SUBLANE:]
    tail_ref[...] = a[tm - SUBLANE:]


def _ffn_sample_kernel(h_ref, wa_ref, wb_ref, cw_ref, cb_ref, wd_ref, g_ref, b_ref, f1_ref, f2_ref,
                       y_ref, a_ref, acc_ref, *, alpha):
    def shifted(a):
        s = lax.broadcasted_iota(jnp.int32, a.shape, 0) % SROW
        a1 = jnp.where(s >= 1, pltpu.roll(a, 1, 0), 0.0) + f1_ref[...]
        a2 = jnp.where(s >= 2, pltpu.roll(a, 2, 0), 0.0) + f2_ref[...]
        return a1, a2

    a_ref[...] = _ffn_core(h_ref, wa_ref, wb_ref, cw_ref, cb_ref, wd_ref, g_ref, b_ref, y_ref,
                           acc_ref, shifted, alpha)


def _ffn_specs(tm, d, tf, nf):
    return [
        pl.BlockSpec((tm, d), lambda i, j: (i, 0)),
        pl.BlockSpec((d, tf), lambda i, j: (0, j)),
        pl.BlockSpec((d, tf), lambda i, j: (0, nf + j)),
        pl.BlockSpec((SUBLANE, tf), lambda i, j: (0, j)),
        pl.BlockSpec((1, tf), lambda i, j: (0, j)),
        pl.BlockSpec((tf, d), lambda i, j: (j, 0)),
        pl.BlockSpec((1, d), lambda i, j: (0, 0)),
        pl.BlockSpec((1, d), lambda i, j: (0, 0)),
    ]


def _ffn_prompt(h2d, fw, seq, tm, tf, alpha):
    w_up, conv_w8, conv_b, w_down, g, b = fw
    m, d = h2d.shape
    dff = w_down.shape[0]
    nf = dff // tf
    return pl.pallas_call(
        functools.partial(_ffn_prompt_kernel, tiles_per_seq=seq // tm, alpha=alpha),
        grid=(m // tm, nf),
        in_specs=_ffn_specs(tm, d, tf, nf),
        out_specs=[pl.BlockSpec((tm, d), lambda i, j: (i, 0)),
                   pl.BlockSpec((None, SUBLANE, tf), lambda i, j: (i, 0, j))],
        out_shape=[jax.ShapeDtypeStruct((m, d), F32),
                   jax.ShapeDtypeStruct((m // tm, SUBLANE, dff), F32)],
        scratch_shapes=[pltpu.VMEM((tm, d), F32), pltpu.VMEM((nf, SUBLANE, tf), F32)],
        compiler_params=_cparams(("arbitrary", "arbitrary")),
        name="conv_ffn_prompt",
    )(h2d, w_up, w_up, conv_w8, conv_b, w_down, g, b)


def _ffn_sample(h2d, fw, fill1, fill2, tf, alpha):
    w_up, conv_w8, conv_b, w_down, g, b = fw
    m, d = h2d.shape
    dff = w_down.shape[0]
    nf = dff // tf
    return pl.pallas_call(
        functools.partial(_ffn_sample_kernel, alpha=alpha),
        grid=(1, nf),
        in_specs=_ffn_specs(m, d, tf, nf) + [pl.BlockSpec((m, tf), lambda i, j: (0, j)),
                                             pl.BlockSpec((m, tf), lambda i, j: (0, j))],
        out_specs=[pl.BlockSpec((m, d), lambda i, j: (0, 0)),
                   pl.BlockSpec((m, tf), lambda i, j: (0, j))],
        out_shape=[jax.ShapeDtypeStruct((m, d), F32), jax.ShapeDtypeStruct((m, dff), F32)],
        scratch_shapes=[pltpu.VMEM((m, d), F32)],
        compiler_params=_cparams(("arbitrary", "arbitrary")),
        name="conv_ffn_sample",
    )(h2d, w_up, w_up, conv_w8, conv_b, w_down, g, b, fill1, fill2)


def _nsa_sample_a_kernel(qraw_ref, qrot_ref, kc_ref, vc_ref, swa_ref, kwn_ref, vwn_ref,
                         gate_ref, part_ref, sel_ref, *, past, n_blocks):
    rep = NSA_REP
    g = pl.program_id(1)
    nc = kc_ref.shape[0]
    nsp = sel_ref.shape[-1]
    s_col = lax.broadcasted_iota(jnp.int32, (SROW, 1), 0) + past

    q_raw = _stack_heads(qraw_ref, rep)
    s_c = (_dot_nt(q_raw, kc_ref[...]) * ATTN_SCALE).reshape(rep, SROW, nc)
    n_row = lax.broadcasted_iota(jnp.int32, (1, nc), 1)
    vis = (n_row >= 1) & (n_row * CMP_STRIDE + (CMP_BLOCK - CMP_STRIDE - 1) <= s_col)
    p_c = _masked_softmax(s_c, vis[None])
    o_cmp = _dot(p_c.reshape(rep * SROW, nc), vc_ref[...])
    imp = _dot_exact01(jnp.sum(p_c, 0), _overlap_matrix(nc, nsp))
    sel_ref[...] = _topk_select(_block_scores(imp, s_col, n_blocks), min(SLC_TOPK, n_blocks), n_blocks)

    q_rot = _stack_heads(qrot_ref, rep)
    wbuf = swa_ref.shape[0] // N_KV_SLOTS
    keys = jnp.concatenate([swa_ref[pl.ds(g, wbuf, stride=N_KV_SLOTS), :], kwn_ref[...]], 0)
    vals = jnp.concatenate([swa_ref[pl.ds(NSA_KV_HEADS + g, wbuf, stride=N_KV_SLOTS), :],
                            vwn_ref[...]], 0)
    w_pos = past - wbuf + lax.broadcasted_iota(jnp.int32, (1, wbuf + SROW), 1)
    dist = s_col - w_pos
    mask = (dist >= 0) & (dist < WINDOW) & (w_pos >= 0)
    s_w = (_dot_nt(q_rot, keys) * ATTN_SCALE).reshape(rep, SROW, wbuf + SROW)
    p_w = _masked_softmax(s_w, mask[None])
    o_swa = _dot(p_w.reshape(rep * SROW, wbuf + SROW), vals)

    gt = gate_ref[...]
    for r in range(rep):
        rows = slice(r * SROW, (r + 1) * SROW)
        part_ref[rows, :] = (gt[:, 3 * r:3 * r + 1] * o_cmp[rows]
                             + gt[:, 3 * r + 2:3 * r + 3] * o_swa[rows])


def _nsa_sample_a(p_s, qrot_s, kc_all, swa_rows, sig_s, past, n_blocks):
    dec_batch, nc, _ = kc_all.shape
    gw = NSA_REP * HEAD_DIM
    nsp = -(-n_blocks // LANE) * LANE
    bg = lambda b, g: (b, g)
    return pl.pallas_call(
        functools.partial(_nsa_sample_a_kernel, past=past, n_blocks=n_blocks),
        grid=(dec_batch, NSA_KV_HEADS),
        in_specs=[
            pl.BlockSpec((SROW, gw), bg),
            pl.BlockSpec((SROW, gw), bg),
            pl.BlockSpec((None, nc, HEAD_DIM), lambda b, g: (b, 0, g)),
            pl.BlockSpec((None, nc, HEAD_DIM), lambda b, g: (b, 0, NSA_KV_HEADS + g)),
            pl.BlockSpec((None, swa_rows.shape[1], HEAD_DIM), lambda b, g: (b, 0, 0)),
            pl.BlockSpec((SROW, HEAD_DIM), lambda b, g: (b, COL_KVW + g)),
            pl.BlockSpec((SROW, HEAD_DIM), lambda b, g: (b, COL_KVW + NSA_KV_HEADS + g)),
            pl.BlockSpec((SROW, LANE), bg),
        ],
        out_specs=[pl.BlockSpec((None, None, NSA_REP * SROW, HEAD_DIM), lambda b, g: (b, g, 0, 0)),
                   pl.BlockSpec((None, None, SROW, nsp), lambda b, g: (b, g, 0, 0))],
        out_shape=[jax.ShapeDtypeStruct((dec_batch, NSA_KV_HEADS, NSA_REP * SROW, HEAD_DIM), F32),
                   jax.ShapeDtypeStruct((dec_batch, NSA_KV_HEADS, SROW, nsp), F32)],
        compiler_params=_cparams(("arbitrary", "arbitrary")),
        name="nsa_sample_cmp_swa",
    )(p_s, qrot_s, kc_all, kc_all, swa_rows, p_s, p_s, sig_s)


def _nsa_sample_b_kernel(pt_ref, *refs, past, n_blocks):
    del pt_ref
    pages = refs[:PAGES_PER_STEP]
    (qrot_ref, sel_ref, knew_ref, vnew_ref, gate_ref, part_ref,
     o_ref, m_ref, l_ref, acc_ref) = refs[PAGES_PER_STEP:]
    c = pl.program_id(1)
    nchunk = pl.num_programs(1) - 1
    rep = NSA_REP
    page = pages[0].shape[0] // N_KV_SLOTS
    keys_per_step = PAGES_PER_STEP * page
    blocks_per_step = keys_per_step // SLC_BLOCK
    nsp = sel_ref.shape[-1]
    gw = rep * HEAD_DIM

    def cache_rows(slot):
        return jnp.concatenate([p[pl.ds(slot, page, stride=N_KV_SLOTS), :].astype(BF)
                                for p in pages], 0)

    @pl.when(c == 0)
    def _():
        _flash_init(m_ref, l_ref, acc_ref)

    groups = range(NSA_KV_HEADS)
    queries = [_stack_heads(qrot_ref.at[:, g * gw:(g + 1) * gw], rep) for g in groups]

    def update(scores, keeps, values):
        keep = jnp.concatenate([k for k in keeps for _ in range(rep)], 0) > 0.5
        s = jnp.where(keep, jnp.concatenate(scores, 0) * ATTN_SCALE, NEG_INF)
        m_old = m_ref[...]
        m_new = jnp.maximum(m_old, jnp.max(s, -1, keepdims=True))
        p = jnp.where(keep, jnp.exp(s - m_new), 0.0)
        alpha = jnp.exp(m_old - m_new)
        rows = rep * SROW
        pv = jnp.concatenate([_dot(p[g * rows:(g + 1) * rows], values[g]) for g in groups], 0)
        m_ref[...] = m_new
        l_ref[...] = alpha * l_ref[...] + jnp.sum(p, -1, keepdims=True)
        acc_ref[...] = alpha * acc_ref[...] + pv

    @pl.when(c < nchunk)
    def _():
        blk = lax.broadcasted_iota(jnp.int32, (nsp, blocks_per_step), 0)
        col = lax.broadcasted_iota(jnp.int32, (nsp, blocks_per_step), 1)
        pick = jnp.where(blk == c * blocks_per_step + col, 1.0, 0.0).astype(BF)
        b_row = lax.broadcasted_iota(jnp.int32, (blocks_per_step, keys_per_step), 0)
        k_lane = lax.broadcasted_iota(jnp.int32, (blocks_per_step, keys_per_step), 1)
        expand = jnp.where(b_row == k_lane // SLC_BLOCK, 1.0, 0.0).astype(BF)
        scores = [_dot_nt(queries[g], cache_rows(g)) for g in groups]
        keeps = []
        for g in groups:
            sel_c = jnp.dot(sel_ref[g].astype(BF), pick, preferred_element_type=F32)
            keeps.append(jnp.dot(sel_c.astype(BF), expand, preferred_element_type=F32))
        update(scores, keeps, [cache_rows(NSA_KV_HEADS + g) for g in groups])

    @pl.when(c == nchunk)
    def _():
        s_q = lax.broadcasted_iota(jnp.int32, (SROW, LANE), 0)
        s_k = lax.broadcasted_iota(jnp.int32, (SROW, LANE), 1)
        causal = jnp.where((s_k <= s_q) & (s_k < SROW), 1.0, 0.0)
        pad = jnp.zeros((LANE - SROW, HEAD_DIM), F32)
        new_rows = lambda ref, g: jnp.concatenate([ref[:, g * HEAD_DIM:(g + 1) * HEAD_DIM], pad], 0)
        scores = [_dot_nt(queries[g], new_rows(knew_ref, g)) for g in groups]
        keeps = [sel_ref[g][:, n_blocks - 1:n_blocks] * causal for g in groups]
        update(scores, keeps, [new_rows(vnew_ref, g) for g in groups])
        o_slc = acc_ref[...] / jnp.maximum(l_ref[...], 1e-30)
        gt = gate_ref[...]
        for g in groups:
            for r in range(rep):
                col = 3 * rep * g + 3 * r + 1
                rows = slice((g * rep + r) * SROW, (g * rep + r + 1) * SROW)
                o_ref[:, (g * rep + r) * HEAD_DIM:(g * rep + r + 1) * HEAD_DIM] = (
                    part_ref[g, r * SROW:(r + 1) * SROW, :] + gt[:, col:col + 1] * o_slc[rows])


def _nsa_sample_b(cache_rows, page_table, p_s, qrot_s, sel, sig12, part, past, n_blocks):
    n_pool, page_rows, _ = cache_rows.shape
    dec_batch, n_pages = page_table.shape
    nchunk = n_pages // PAGES_PER_STEP
    nsp = sel.shape[-1]
    qw = NSA_HEADS * HEAD_DIM
    half = NSA_KV_HEADS * HEAD_DIM
    row = lambda w, col: pl.BlockSpec((SROW, w), lambda b, c, pt: (b, col))
    return pl.pallas_call(
        functools.partial(_nsa_sample_b_kernel, past=past, n_blocks=n_blocks),
        grid_spec=pltpu.PrefetchScalarGridSpec(
            num_scalar_prefetch=1,
            grid=(dec_batch, nchunk + 1),
            in_specs=_page_specs(page_rows, HEAD_DIM, lambda c: jnp.minimum(c, nchunk - 1)) + [
                row(qw, 0),
                pl.BlockSpec((None, NSA_KV_HEADS, SROW, nsp), lambda b, c, pt: (b, 0, 0, 0)),
                row(half, COL_KVS * LANE // half),
                row(half, COL_KVS * LANE // half + 1),
                row(LANE, 0),
                pl.BlockSpec((None, NSA_KV_HEADS, NSA_REP * SROW, HEAD_DIM),
                             lambda b, c, pt: (b, 0, 0, 0)),
            ],
            out_specs=row(qw, 0),
            scratch_shapes=[pltpu.VMEM((NSA_HEADS * SROW, 1), F32),
                            pltpu.VMEM((NSA_HEADS * SROW, 1), F32),
                            pltpu.VMEM((NSA_HEADS * SROW, HEAD_DIM), F32)],
        ),
        out_shape=jax.ShapeDtypeStruct((dec_batch * SROW, qw), F32),
        compiler_params=_cparams(("arbitrary", "arbitrary")),
        name="nsa_sample_selected",
    )(page_table, *([cache_rows] * PAGES_PER_STEP), qrot_s, sel, p_s, p_s, sig12, part)


def _fox_sample_kernel(pt_ref, *refs, s_len):
    del pt_ref
    n = PAGES_PER_STEP
    kpages, vpages, lpages = refs[:n], refs[n:2 * n], refs[2 * n:3 * n]
    (q_ref, knew_ref, vnew_ref, lnew_ref, o_ref, m_ref, l_ref, acc_ref, carry_ref) = refs[3 * n:]
    c = pl.program_id(1)
    page = kpages[0].shape[0]
    head = lambda ref, h: ref[:, h * HEAD_DIM:(h + 1) * HEAD_DIM]

    def cache_rows(pages, h):
        return jnp.concatenate([_head_rows(p, h).astype(BF) for p in pages], 0)

    def update(scores, mask, values):
        s = jnp.concatenate(scores, 0)
        if mask is not None:
            s = jnp.where(mask, s, NEG_INF)
        m_old = m_ref[...]
        m_new = jnp.maximum(m_old, jnp.max(s, -1, keepdims=True))
        p = jnp.exp(s - m_new)
        if mask is not None:
            p = jnp.where(mask, p, 0.0)
        alpha = jnp.exp(m_old - m_new)
        pv = jnp.concatenate([_dot(p[h * SROW:(h + 1) * SROW], values[h])
                              for h in range(FOX_HEADS)], 0)
        m_ref[...] = m_new
        l_ref[...] = alpha * l_ref[...] + jnp.sum(p, -1, keepdims=True)
        acc_ref[...] = alpha * acc_ref[...] + pv

    @pl.when(c == 0)
    def _():
        m_ref[...] = jnp.full(m_ref.shape, NEG_INF, F32)
        l_ref[...] = jnp.zeros(l_ref.shape, F32)
        acc_ref[...] = jnp.zeros(acc_ref.shape, F32)
        s_q = lax.broadcasted_iota(jnp.int32, (SROW, LANE), 0)
        s_k = lax.broadcasted_iota(jnp.int32, (SROW, LANE), 1)
        later = jnp.where((s_q > s_k) & (s_q < s_len), 1.0, 0.0)
        real = lax.broadcasted_iota(jnp.int32, (SROW, 1), 0) < s_len
        lnew = lnew_ref[...]
        pad = jnp.zeros((LANE - SROW, HEAD_DIM), F32)
        rows_q = lax.broadcasted_iota(jnp.int32, (FOX_HEADS * SROW, LANE), 0) % SROW
        cols_k = lax.broadcasted_iota(jnp.int32, (FOX_HEADS * SROW, LANE), 1)
        mask = (cols_k <= rows_q) & (cols_k < s_len)
        totals, scores, values = [], [], []
        for h in range(FOX_HEADS):
            lf = jnp.where(real, lnew[:, h:h + 1], 0.0)
            d_row = jnp.sum(lf * later, 0, keepdims=True)
            totals.append(jnp.sum(lf, 0, keepdims=True))
            k = jnp.concatenate([head(knew_ref, h), pad], 0)
            values.append(jnp.concatenate([head(vnew_ref, h), pad], 0))
            scores.append(_dot_nt(head(q_ref, h), k) * ATTN_SCALE + d_row)
        update(scores, mask, values)
        totals += [jnp.zeros((1, 1), F32)] * (SUBLANE - FOX_HEADS)
        carry_ref[...] = jnp.broadcast_to(jnp.concatenate(totals, 0), carry_ref.shape)

    @pl.when(c > 0)
    def _():
        lf = jnp.concatenate([p[...] for p in lpages], 1)
        incl = _suffix_sum_lanes(lf)
        carry = carry_ref[:, 0:1]
        decay = carry + (incl - lf)
        carry_ref[...] = jnp.broadcast_to(carry + incl[:, 0:1], carry_ref.shape)
        scores = [_dot_nt(head(q_ref, h), cache_rows(kpages, h)) * ATTN_SCALE + decay[h:h + 1, :]
                  for h in range(FOX_HEADS)]
        update(scores, None, [cache_rows(vpages, h) for h in range(FOX_HEADS)])

    @pl.when(c == pl.num_programs(1) - 1)
    def _():
        o = acc_ref[...] / jnp.maximum(l_ref[...], 1e-30)
        for h in range(FOX_HEADS):
            o_ref[:, h * HEAD_DIM:(h + 1) * HEAD_DIM] = o[h * SROW:(h + 1) * SROW]


def _fox_sample(cache, logf_t, page_table, p_s, lsig_s, s_len):
    n_pool, page = cache.shape[:2]
    dec_batch, n_pages = page_table.shape
    nchunk = n_pages // PAGES_PER_STEP
    fw = FOX_HEADS * HEAD_DIM
    chunk_of = lambda c: nchunk - jnp.maximum(c, 1)
    row = lambda w, col: pl.BlockSpec((SROW, w), lambda b, c, pt: (b, col))

    def half_specs(kv):
        def spec(i):
            return pl.BlockSpec((None, page, None, FOX_HEADS, HEAD_DIM),
                                lambda b, c, pt: (pt[b, chunk_of(c) * PAGES_PER_STEP + i], 0, kv, 0, 0))
        return [spec(i) for i in range(PAGES_PER_STEP)]

    return pl.pallas_call(
        functools.partial(_fox_sample_kernel, s_len=s_len),
        grid_spec=pltpu.PrefetchScalarGridSpec(
            num_scalar_prefetch=1,
            grid=(dec_batch, nchunk + 1),
            in_specs=half_specs(0) + half_specs(1) + _page_specs(SUBLANE, page, chunk_of) + [
                row(fw, COL_FOX * LANE // fw),
                row(fw, COL_FOX * LANE // fw + 1),
                row(fw, COL_FOX * LANE // fw + 2),
                row(LANE, 0),
            ],
            out_specs=row(fw, 0),
            scratch_shapes=[pltpu.VMEM((FOX_HEADS * SROW, 1), F32),
                            pltpu.VMEM((FOX_HEADS * SROW, 1), F32),
                            pltpu.VMEM((FOX_HEADS * SROW, HEAD_DIM), F32),
                            pltpu.VMEM((SUBLANE, LANE), F32)],
        ),
        out_shape=jax.ShapeDtypeStruct((dec_batch * SROW, fw), F32),
        compiler_params=_cparams(("arbitrary", "arbitrary")),
        name="fox_attention_sample",
    )(page_table, *([cache] * (2 * PAGES_PER_STEP)), *([logf_t] * PAGES_PER_STEP), p_s, p_s, p_s, lsig_s)


def _rope_tables(pos):
    half = ROT_DIM // 2
    inv = jnp.power(ROPE_THETA, -jnp.arange(half, dtype=F32) * (2.0 / ROT_DIM))
    ang = pos.astype(F32)[:, None] * inv[None, :]
    cos, sin = jnp.cos(ang), jnp.sin(ang)
    n = pos.shape[0]
    c = jnp.concatenate([cos, cos, jnp.ones((n, HEAD_DIM - ROT_DIM), F32)], 1)
    s = jnp.concatenate([-sin, sin, jnp.zeros((n, HEAD_DIM - ROT_DIM), F32)], 1)
    return c, s


def _layer_weights(w_in, b_gate, b_forget, cmp_w1, cmp_b1, cmp_w2, cmp_pos, w_mem_kv, w_out,
                   ln1_g, ln1_b, w_up, conv_w, conv_b, w_down, ln2_g, ln2_b):
    d = w_in.shape[0]
    nq = NSA_HEADS * HEAD_DIM
    kvw = 2 * NSA_KV_HEADS * HEAD_DIM
    o_g = nq + 3 * kvw
    o_fox = o_g + 3 * NSA_HEADS
    o_f = o_fox + 3 * FOX_HEADS * HEAD_DIM
    o_qm = o_f + FOX_HEADS
    w_main = jnp.concatenate([w_in[:, :o_g], w_in[:, o_fox:o_f], w_in[:, o_qm:]], 1).astype(BF)
    per_group = 3 * NSA_REP
    zpad = lambda n: jnp.zeros((d, n), w_in.dtype)
    w_small = jnp.concatenate([
        w_in[:, o_g:o_g + per_group], zpad(LANE - per_group),
        w_in[:, o_g + per_group:o_fox], zpad(LANE - per_group),
        w_in[:, o_f:o_qm], zpad(LANE - FOX_HEADS)], 1).astype(BF)
    bpad = lambda n: jnp.zeros((n,), F32)
    b_small = jnp.concatenate([
        b_gate[:per_group], bpad(LANE - per_group), b_gate[per_group:], bpad(LANE - per_group),
        b_forget, bpad(LANE - FOX_HEADS)])[None, :].astype(F32)
    kdim = CMP_STRIDE * HEAD_DIM
    w1cat = jnp.concatenate([cmp_w1[:, :CMP_STRIDE].reshape(2, kdim, CMP_HIDDEN),
                             cmp_w1[:, CMP_STRIDE:].reshape(2, kdim, CMP_HIDDEN)], 2).astype(BF)
    pe = jnp.concatenate([cmp_pos[:, :CMP_STRIDE].reshape(2, 1, kdim),
                          cmp_pos[:, CMP_STRIDE:].reshape(2, 1, kdim),
                          jnp.zeros((2, SUBLANE - 2, kdim), F32)], 1)
    cw = (w1cat, pe, cmp_b1[:, None, :], cmp_w2.astype(BF))
    conv_w8 = jnp.concatenate([conv_w, jnp.zeros((SUBLANE - CONV_W, conv_w.shape[1]), F32)], 0)
    fw = (w_up.astype(BF), conv_w8, conv_b[None, :], w_down.astype(BF), ln2_g[None, :], ln2_b[None, :])
    return dict(w_main=w_main, w_small=w_small, b_small=b_small, cw=cw,
                w_mem_kv=w_mem_kv.astype(BF), w_out=w_out.astype(BF),
                ln1_g=ln1_g[None, :], ln1_b=ln1_b[None, :], fw=fw)


def _prompt_layer(x, mem, lw, alpha):
    batch, seq, d = x.shape
    x2d = x.reshape(batch * seq, d)
    rc, rs = _rope_tables(jnp.tile(jnp.arange(seq, dtype=jnp.int32), batch))
    tm = min(512, seq)
    p_main, q_rot, sig, lsig = _input_projection(x2d, lw["w_main"], lw["w_small"], lw["b_small"],
                                                 rc, rs, tm)
    kc_all = _compress_prompt(p_main, batch, seq, lw["cw"])
    o_nsa = _nsa_prompt(p_main, q_rot, kc_all, sig, batch, seq)
    decay = _decay_prompt(lsig, batch, seq)
    o_fox = _fox_prompt(p_main, decay, batch, seq)
    mlen = mem.shape[1]
    mem_kv = _matmul(mem.reshape(batch * mlen, d), lw["w_mem_kv"], min(256, batch * mlen), 512)
    o_mem = _mem_attention(p_main, mem_kv, batch, seq, min(512, seq))
    h = _out_projection(o_nsa, o_fox, o_mem, x2d, lw["w_out"], lw["ln1_g"], lw["ln1_b"],
                        min(256, seq), alpha)
    y, tail = _ffn_prompt(h, lw["fw"], seq, tm, 512, alpha)
    kvw = 2 * NSA_KV_HEADS * HEAD_DIM
    cols = lambda c0, w: p_main[:, c0 * LANE:c0 * LANE + w].reshape(batch, seq, 2, -1, HEAD_DIM)
    n_win = min(WINDOW, seq)
    tiles = seq // tm
    conv_state = tail.reshape(batch, tiles, SUBLANE, -1)[:, -1, SUBLANE - (CONV_W - 1):]
    states = (cols(COL_KVC, kvw), cols(COL_KVS, kvw),
              cols(COL_FOX + FOX_HEADS, 2 * FOX_HEADS * HEAD_DIM),
              lsig[:, :FOX_HEADS].reshape(batch, seq, FOX_HEADS),
              cols(COL_KVW, kvw)[:, seq - n_win:],
              mem_kv.reshape(batch, mlen, 2, MEM_HEADS, HEAD_DIM), conv_state)
    return y.reshape(batch, seq, d), states


def _sample_layer(x, c_cmp, c_slc, c_fox, c_logf, c_swa, c_mem, s_conv, page_table, lw, alpha):
    dec_batch, s_len, d = x.shape
    n_pool, page = c_cmp.shape[0], c_cmp.shape[1]
    n_pages = page_table.shape[1]
    past = n_pages * page
    t_all = past + s_len
    assert s_len <= SROW and n_pages % PAGES_PER_STEP == 0
    assert (t_all // CMP_STRIDE) * CMP_STRIDE == past and past % SLC_BLOCK == 0
    n_blocks = -(-t_all // SLC_BLOCK)
    xp = jnp.pad(x, ((0, 0), (0, SROW - s_len), (0, 0))).reshape(dec_batch * SROW, d)
    pos = past + jnp.tile(jnp.arange(SROW, dtype=jnp.int32), dec_batch)
    rc, rs = _rope_tables(pos)
    p_s, qrot_s, sig_s, lsig_s = _input_projection(xp, lw["w_main"], lw["w_small"], lw["b_small"],
                                                   rc, rs, dec_batch * SROW)
    kvw = 2 * NSA_KV_HEADS * HEAD_DIM
    as_rows = lambda a: a.reshape(a.shape[0], -1, HEAD_DIM)
    kc_all = _compress_paged(as_rows(c_cmp), page_table, lw["cw"])
    per_group = 3 * NSA_REP
    sig12 = jnp.concatenate([sig_s[:, :per_group], sig_s[:, LANE:LANE + per_group],
                             jnp.zeros((dec_batch * SROW, LANE - 2 * per_group), F32)], 1)
    part, sel = _nsa_sample_a(p_s, qrot_s, kc_all, as_rows(c_swa), sig_s, past, n_blocks)
    o_nsa = _nsa_sample_b(as_rows(c_slc), page_table, p_s, qrot_s, sel, sig12, part, past, n_blocks)
    logf_t = jnp.pad(jnp.swapaxes(c_logf, 1, 2), ((0, 0), (0, SUBLANE - FOX_HEADS), (0, 0)))
    o_fox = _fox_sample(c_fox, logf_t, page_table, p_s, lsig_s, s_len)
    o_mem = _mem_attention_rows(p_s, as_rows(c_mem))
    h = _out_projection(o_nsa, o_fox, o_mem, xp, lw["w_out"], lw["ln1_g"], lw["ln1_b"],
                        dec_batch * SROW, alpha)
    dff = s_conv.shape[-1]
    zrow = jnp.zeros((dec_batch, 1, dff), F32)
    fill1 = jnp.concatenate([s_conv[:, 1:2]] + [zrow] * (SROW - 1), 1).reshape(dec_batch * SROW, dff)
    fill2 = jnp.concatenate([s_conv[:, 0:1], s_conv[:, 1:2]] + [zrow] * (SROW - 2), 1)
    y, a = _ffn_sample(h, lw["fw"], fill1, fill2.reshape(dec_batch * SROW, dff), 512, alpha)

    def rows(arr):
        return arr.reshape(dec_batch, SROW, -1)[:, :s_len]
    cols = lambda c0, w: rows(p_s[:, c0 * LANE:c0 * LANE + w]).reshape(dec_batch, s_len, 2, -1, HEAD_DIM)
    new_kv_swa = cols(COL_KVW, kvw)
    new_swa = jnp.concatenate([c_swa, new_kv_swa], 1)[:, s_len:]
    conv_state = jnp.concatenate([s_conv, rows(a)], 1)[:, -(CONV_W - 1):]
    states = (cols(COL_KVC, kvw), cols(COL_KVS, kvw),
              cols(COL_FOX + FOX_HEADS, 2 * FOX_HEADS * HEAD_DIM),
              rows(lsig_s[:, :FOX_HEADS]), new_swa, conv_state)
    return rows(y), states


def kernel(x_prompt, x_sample, mem_prompt, cache_nsa_cmp, cache_nsa_slc, cache_fox_kv, cache_fox_logf, cache_nsa_swa, cache_mem, state_conv, page_table, w_in, b_gate, b_forget, cmp_w1, cmp_b1, cmp_w2, cmp_pos, w_mem_kv, w_out, ln1_g, ln1_b, w_up, conv_w, conv_b, w_down, ln2_g, ln2_b):
    depth = w_in.shape[0]
    alpha = float((2 * depth) ** 0.25)
    yp, ys = x_prompt, x_sample
    acc_p = [[] for _ in range(7)]
    acc_s = [[] for _ in range(6)]
    for l in range(depth):
        lw = _layer_weights(w_in[l], b_gate[l], b_forget[l], cmp_w1[l], cmp_b1[l], cmp_w2[l],
                            cmp_pos[l], w_mem_kv[l], w_out[l], ln1_g[l], ln1_b[l], w_up[l],
                            conv_w[l], conv_b[l], w_down[l], ln2_g[l], ln2_b[l])
        yp, st_p = _prompt_layer(yp, mem_prompt, lw, alpha)
        ys, st_s = _sample_layer(ys, cache_nsa_cmp[l], cache_nsa_slc[l], cache_fox_kv[l],
                                 cache_fox_logf[l], cache_nsa_swa[l], cache_mem[l], state_conv[l],
                                 page_table, lw, alpha)
        for lst, a in zip(acc_p, st_p):
            lst.append(a)
        for lst, a in zip(acc_s, st_s):
            lst.append(a)
    outs_p = [jnp.stack(a, 0) for a in acc_p]
    outs_s = [jnp.stack(a, 0) for a in acc_s]
    return (yp, ys, *outs_p, *outs_s)
```

```python
import functools

import jax
import jax.numpy as jnp
import numpy as np
from jax import lax
from jax.experimental import pallas as pl
from jax.experimental.pallas import tpu as pltpu

HEAD_DIM = 128
NSA_KV_HEADS = 2
NSA_REP = 4
NSA_HEADS = NSA_KV_HEADS * NSA_REP
FOX_HEADS = 4
MEM_HEADS = 4
CMP_BLOCK = 32
CMP_STRIDE = 16
CMP_HIDDEN = 256
SLC_BLOCK = 64
SLC_TOPK = 16
WINDOW = 512
Q_BLOCK = 128
ROT_DIM = HEAD_DIM // 4
ROPE_THETA = 500000.0
CONV_W = 3
LN_EPS = 1e-5
ATTN_SCALE = HEAD_DIM ** -0.5
LOG2E = 1.4426950408889634
NEG_INF = -1e30
FORCE_BONUS = 1e3

LANE = 128
SUBLANE = 8
SROW = SUBLANE
PAGES_PER_STEP = 16
CMP_PAGES_PER_STEP = 32
VMEM_LIMIT = 56 * 1024 * 1024

COL_Q = 0
COL_KVC = 8
COL_KVS = 12
COL_KVW = 16
COL_FOX = 20
COL_QM = 32
MAIN_W = 36 * LANE
IN_TILE = 512

BF = jnp.bfloat16
F32 = jnp.float32


def _dot(a, b):
    return jnp.dot(a.astype(BF), b.astype(BF), preferred_element_type=F32)


def _dot_nt(a, b):
    return lax.dot_general(a.astype(BF), b.astype(BF), (((1,), (1,)), ((), ())),
                           preferred_element_type=F32)


def _dot_exact01(x, m01, m01_first=False):
    hi = x.astype(BF)
    r1 = x - hi.astype(F32)
    mid = r1.astype(BF)
    lo = (r1 - mid.astype(F32)).astype(BF)
    m = m01.astype(BF)
    if m01_first:
        d = lambda a: lax.dot_general(m, a, (((1,), (1,)), ((), ())), preferred_element_type=F32)
    else:
        d = lambda a: jnp.dot(a, m, preferred_element_type=F32)
    return d(hi) + d(mid) + d(lo)


def _cparams(sem):
    return pltpu.CompilerParams(dimension_semantics=sem, vmem_limit_bytes=VMEM_LIMIT)


def _masked_softmax(s, mask):
    s = jnp.where(mask, s, NEG_INF)
    m = jnp.max(s, -1, keepdims=True)
    e = jnp.where(mask, jnp.exp(s - m), 0.0)
    return e / jnp.maximum(jnp.sum(e, -1, keepdims=True), 1e-30)


def _online_update(carry, s, mask, v):
    m, l, acc = carry
    if mask is not None:
        s = jnp.where(mask, s, NEG_INF)
    m_new = jnp.maximum(m, jnp.max(s, -1, keepdims=True))
    p = jnp.exp(s - m_new)
    if mask is not None:
        p = jnp.where(mask, p, 0.0)
    alpha = jnp.exp(m - m_new)
    l_new = alpha * l + jnp.sum(p, -1, keepdims=True)
    lead = p.shape[:-1]
    pv = _dot(p.reshape((-1, p.shape[-1])), v).reshape(lead + (v.shape[-1],))
    return m_new, l_new, alpha * acc + pv


def _flash_update(carry, s2, v):
    m, l, acc = carry
    m_new = jnp.maximum(m, jnp.max(s2, -1, keepdims=True))
    p = jnp.exp2(s2 - m_new)
    alpha = jnp.exp2(m - m_new)
    return (m_new, alpha * l + jnp.sum(p, -1, keepdims=True),
            alpha * acc + jnp.dot(p.astype(BF), v, preferred_element_type=F32))


def _flash_init(m_ref, l_ref, acc_ref):
    m_ref[...] = jnp.full(m_ref.shape, NEG_INF, F32)
    l_ref[...] = jnp.zeros(l_ref.shape, F32)
    acc_ref[...] = jnp.zeros(acc_ref.shape, F32)


def _flash_step(m_ref, l_ref, acc_ref, rows, s2, v):
    m_old = m_ref[rows]
    m_new = jnp.maximum(m_old, jnp.max(s2, -1, keepdims=True))
    p = jnp.exp2(s2 - m_new)
    alpha = jnp.exp2(m_old - m_new)
    l_ref[rows] = alpha * l_ref[rows] + jnp.sum(p, -1, keepdims=True)
    acc_ref[rows] = alpha * acc_ref[rows] + jnp.dot(p.astype(BF), v, preferred_element_type=F32)
    m_ref[rows] = m_new


def _rope_tile(x, c, s):
    lane = lax.broadcasted_iota(jnp.int32, x.shape, 1)
    half = ROT_DIM // 2
    swapped = jnp.where(lane < half, pltpu.roll(x, LANE - half, 1), pltpu.roll(x, half, 1))
    return x * c + swapped * s


def _topk_select(score, n_sel, n_valid_cols):
    col = lax.broadcasted_iota(jnp.int32, score.shape, 1)
    rank = jnp.zeros(score.shape, F32)
    for i in range(n_valid_cols):
        ci = score[:, i:i + 1]
        rank = rank + jnp.where(col > i, jnp.where(ci >= score, 1.0, 0.0),
                                jnp.where(ci > score, 1.0, 0.0))
    return jnp.where(rank < n_sel, 1.0, 0.0)


def _topk_select_rows(score_t, n_sel):
    n, width = score_t.shape
    rank = jnp.zeros(score_t.shape, F32)
    for i in range(n):
        ci = jnp.broadcast_to(score_t[i:i + 1, :], (SUBLANE, width))
        parts = []
        for j0 in range(0, n, SUBLANE):
            sj = score_t[j0:j0 + SUBLANE, :]
            if j0 > i:
                ahead = ci >= sj
            elif j0 + SUBLANE - 1 <= i:
                ahead = ci > sj
            else:
                row = lax.broadcasted_iota(jnp.int32, (SUBLANE, width), 0) + j0
                ahead = jnp.where(row > i, jnp.where(ci >= sj, 1.0, 0.0),
                                  jnp.where(ci > sj, 1.0, 0.0)) > 0.5
            parts.append(jnp.where(ahead, 1.0, 0.0))
        rank = rank + jnp.concatenate(parts, 0)
    return jnp.where(rank < n_sel, 1.0, 0.0)


def _inproj_kernel(x_ref, w_ref, ws_ref, bs_ref, c_ref, s_ref,
                   p_ref, qrot_ref, sig_ref, lsig_ref, stc_ref, sts_ref, stw_ref, stf_ref, xb_ref):
    j = pl.program_id(1)
    tm = x_ref.shape[0]

    def store_rows(ref, tile, first_slot, slots):
        for k in range(IN_TILE // LANE):
            ref[pl.ds(first_slot + k, tm, stride=slots), :] = tile[k]

    @pl.when(j == 0)
    def _():
        xb_ref[...] = x_ref[...].astype(BF)
        z = jnp.dot(xb_ref[...], ws_ref[...], preferred_element_type=F32) + bs_ref[...]
        sig_ref[...] = jax.nn.sigmoid(z[:, :2 * LANE])
        zf = z[:, 2 * LANE:]
        lsig_ref[...] = jnp.minimum(zf, 0.0) - jnp.log1p(jnp.exp(-jnp.abs(zf)))

    acc = jnp.dot(xb_ref[...], w_ref[...], preferred_element_type=F32)
    c = c_ref[...]
    s = s_ref[...]
    heads = IN_TILE // LANE

    @pl.when(j < 2)
    def _():
        p_ref[...] = acc
        for h in range(heads):
            qrot_ref[:, h * LANE:(h + 1) * LANE] = _rope_tile(acc[:, h * LANE:(h + 1) * LANE], c, s)

    lane_tiles = lambda a: [a[:, h * LANE:(h + 1) * LANE] for h in range(heads)]

    def roped_kv(st_ref):
        tiles = [_rope_tile(t, c, s) if h < NSA_KV_HEADS else t for h, t in enumerate(lane_tiles(acc))]
        for h, t in enumerate(tiles):
            p_ref[:, h * LANE:(h + 1) * LANE] = t
        store_rows(st_ref, tiles, 0, N_KV_SLOTS)

    @pl.when(j == COL_KVC // heads)
    def _():
        p_ref[...] = acc
        store_rows(stc_ref, lane_tiles(acc), 0, N_KV_SLOTS)

    @pl.when(j == COL_KVS // heads)
    def _():
        roped_kv(sts_ref)

    @pl.when(j == COL_KVW // heads)
    def _():
        roped_kv(stw_ref)

    fox_k = COL_FOX // heads + 1
    @pl.when((j == fox_k) | (j == fox_k + 1))
    def _():
        p_ref[...] = acc

    @pl.when(j == fox_k)
    def _():
        store_rows(stf_ref, lane_tiles(acc), 0, 2 * FOX_HEADS)

    @pl.when(j == fox_k + 1)
    def _():
        store_rows(stf_ref, lane_tiles(acc), FOX_HEADS, 2 * FOX_HEADS)

    @pl.when((j == COL_FOX // heads) | (j == COL_QM // heads))
    def _():
        p_ref[...] = acc


def _input_projection(x2d, w_main, w_small, b_small, rope_c, rope_s, tm):
    m, d = x2d.shape
    nj = MAIN_W // IN_TILE
    state_slots = (N_KV_SLOTS, N_KV_SLOTS, N_KV_SLOTS, 2 * FOX_HEADS)
    return pl.pallas_call(
        _inproj_kernel,
        grid=(m // tm, nj),
        in_specs=[
            pl.BlockSpec((tm, d), lambda i, j: (i, 0)),
            pl.BlockSpec((d, IN_TILE), lambda i, j: (0, j)),
            pl.BlockSpec((d, 3 * LANE), lambda i, j: (0, 0)),
            pl.BlockSpec((1, 3 * LANE), lambda i, j: (0, 0)),
            pl.BlockSpec((tm, LANE), lambda i, j: (i, 0)),
            pl.BlockSpec((tm, LANE), lambda i, j: (i, 0)),
        ],
        out_specs=[
            pl.BlockSpec((tm, IN_TILE), lambda i, j: (i, j)),
            pl.BlockSpec((tm, IN_TILE), lambda i, j: (i, jnp.minimum(j, 1))),
            pl.BlockSpec((tm, 2 * LANE), lambda i, j: (i, 0)),
            pl.BlockSpec((tm, LANE), lambda i, j: (i, 0)),
        ] + [pl.BlockSpec((tm * slots, LANE), lambda i, j: (i, 0)) for slots in state_slots],
        out_shape=[
            jax.ShapeDtypeStruct((m, MAIN_W), F32),
            jax.ShapeDtypeStruct((m, NSA_HEADS * HEAD_DIM), F32),
            jax.ShapeDtypeStruct((m, 2 * LANE), F32),
            jax.ShapeDtypeStruct((m, LANE), F32),
        ] + [jax.ShapeDtypeStruct((m * slots, LANE), F32) for slots in state_slots],
        scratch_shapes=[pltpu.VMEM((tm, d), BF)],
        compiler_params=_cparams(("arbitrary", "arbitrary")),
        name="input_projection",
    )(x2d, w_main, w_small, b_small, rope_c, rope_s)


def _gelu_tanh(x):
    k = np.sqrt(2.0 / np.pi).astype(np.float32)
    return x * (0.5 * (1.0 + jnp.tanh(k * (x + 0.044715 * (x ** 3)))))


def _compress_body(load, nb, w1_ref, pe_ref, b1_ref, w2_ref, out_ref, carry_ref, first_step):
    @pl.when(first_step)
    def _():
        carry_ref[...] = jnp.zeros(carry_ref.shape, F32)

    row = lax.broadcasted_iota(jnp.int32, (nb, CMP_HIDDEN), 0)
    for kv in range(2):
        xs = []
        for g in range(NSA_KV_HEADS):
            per_l = [load(kv * NSA_KV_HEADS + g, l) for l in range(CMP_STRIDE)]
            xs.append(jnp.concatenate(per_l, 1).astype(BF))
        xs.append(pe_ref[kv].astype(BF))
        fs = jnp.dot(jnp.concatenate(xs, 0), w1_ref[kv], preferred_element_type=F32)
        pos = (fs[2 * nb:2 * nb + 1, :CMP_HIDDEN] + fs[2 * nb + 1:2 * nb + 2, CMP_HIDDEN:]
               + b1_ref[kv])
        for g in range(NSA_KV_HEADS):
            col = (kv * NSA_KV_HEADS + g) * HEAD_DIM
            first = fs[g * nb:(g + 1) * nb, :CMP_HIDDEN]
            second = fs[g * nb:(g + 1) * nb, CMP_HIDDEN:]
            slot = kv * NSA_KV_HEADS + g
            prev = jnp.where(row == 0, carry_ref[slot], pltpu.roll(first, 1, 0))
            carry_ref[slot] = first[nb - 1:nb, :]
            h = _gelu_tanh(prev + second + pos)
            out_ref[:, col:col + HEAD_DIM] = jnp.dot(h.astype(BF), w2_ref[kv],
                                                     preferred_element_type=F32)


N_KV_SLOTS = 2 * NSA_KV_HEADS


def _compress_prompt_kernel(*refs):
    slots = refs[:N_KV_SLOTS]
    w1_ref, pe_ref, b1_ref, w2_ref, out_ref, carry_ref = refs[N_KV_SLOTS:]
    nb = slots[0].shape[0] // CMP_STRIDE

    def load(slot, l):
        return slots[slot][pl.ds(l, nb, stride=CMP_STRIDE), :]

    _compress_body(load, nb, w1_ref, pe_ref, b1_ref, w2_ref, out_ref, carry_ref,
                   pl.program_id(1) == 0)


def _compress_paged_kernel(pt_ref, *refs):
    del pt_ref
    pages = refs[:CMP_PAGES_PER_STEP]
    w1_ref, pe_ref, b1_ref, w2_ref, out_ref, carry_ref = refs[CMP_PAGES_PER_STEP:]
    group = N_KV_SLOTS * CMP_STRIDE
    per_page = pages[0].shape[0] // group
    blocks = jnp.concatenate([p[...].reshape(per_page, group, HEAD_DIM) for p in pages], 0)
    by_row = jnp.swapaxes(blocks, 0, 1)

    def load(slot, l):
        return by_row[l * N_KV_SLOTS + slot]

    _compress_body(load, per_page * CMP_PAGES_PER_STEP, w1_ref, pe_ref, b1_ref, w2_ref, out_ref,
                   carry_ref, pl.program_id(1) == 0)


def _compress_weight_specs():
    const3 = (lambda *a: (0, 0, 0))
    return [
        pl.BlockSpec((2, CMP_STRIDE * HEAD_DIM, 2 * CMP_HIDDEN), const3),
        pl.BlockSpec((2, SUBLANE, CMP_STRIDE * HEAD_DIM), const3),
        pl.BlockSpec((2, 1, CMP_HIDDEN), const3),
        pl.BlockSpec((2, CMP_HIDDEN, HEAD_DIM), const3),
    ]


def _compress_prompt(p_main, batch, seq, cw):
    chunk = min(seq, 2048)
    nchunk = seq // chunk
    kvw = 2 * NSA_KV_HEADS * HEAD_DIM
    return pl.pallas_call(
        _compress_prompt_kernel,
        grid=(batch, nchunk),
        in_specs=[pl.BlockSpec((chunk, HEAD_DIM), (lambda b, c, s=s: (b * nchunk + c, COL_KVC + s)))
                  for s in range(N_KV_SLOTS)] + _compress_weight_specs(),
        out_specs=pl.BlockSpec((None, chunk // CMP_STRIDE, kvw), lambda b, c: (b, c, 0)),
        out_shape=jax.ShapeDtypeStruct((batch, seq // CMP_STRIDE, kvw), F32),
        scratch_shapes=[pltpu.VMEM((2 * NSA_KV_HEADS, 1, CMP_HIDDEN), F32)],
        compiler_params=_cparams(("arbitrary", "arbitrary")),
        name="nsa_compress_prompt",
    )(*([p_main] * N_KV_SLOTS), *cw)


def _page_specs(page_rows, width, chunk_of, per_step=PAGES_PER_STEP):
    def spec(i):
        return pl.BlockSpec((None, page_rows, width),
                            lambda b, c, pt: (pt[b, chunk_of(c) * per_step + i], 0, 0))
    return [spec(i) for i in range(per_step)]


def _compress_paged(cache_rows, page_table, cw):
    n_pool, page_rows, _ = cache_rows.shape
    page = page_rows // N_KV_SLOTS
    kvw = N_KV_SLOTS * HEAD_DIM
    dec_batch, n_pages = page_table.shape
    nchunk = n_pages // CMP_PAGES_PER_STEP
    rows = CMP_PAGES_PER_STEP * page // CMP_STRIDE
    return pl.pallas_call(
        _compress_paged_kernel,
        grid_spec=pltpu.PrefetchScalarGridSpec(
            num_scalar_prefetch=1,
            grid=(dec_batch, nchunk),
            in_specs=(_page_specs(page_rows, HEAD_DIM, lambda c: c, CMP_PAGES_PER_STEP)
                      + _compress_weight_specs()),
            out_specs=pl.BlockSpec((None, rows, kvw), lambda b, c, pt: (b, c, 0)),
            scratch_shapes=[pltpu.VMEM((2 * NSA_KV_HEADS, 1, CMP_HIDDEN), F32)],
        ),
        out_shape=jax.ShapeDtypeStruct((dec_batch, nchunk * rows, kvw), F32),
        compiler_params=_cparams(("arbitrary", "arbitrary")),
        name="nsa_compress_paged",
    )(page_table, *([cache_rows] * CMP_PAGES_PER_STEP), *cw)


def _overlap_matrix(n_rows, n_blocks, blocks_first=False):
    shape = (n_blocks, n_rows) if blocks_first else (n_rows, n_blocks)
    n = lax.broadcasted_iota(jnp.int32, shape, 1 if blocks_first else 0)
    s = lax.broadcasted_iota(jnp.int32, shape, 0 if blocks_first else 1)
    c0 = (n - 1) * CMP_STRIDE
    hit = (n >= 1) & (c0 < s * SLC_BLOCK + SLC_BLOCK) & (c0 + CMP_BLOCK > s * SLC_BLOCK)
    return jnp.where(hit, 1.0, 0.0)


def _block_scores(imp, q_pos, n_blocks, block_axis=1):
    blk = lax.broadcasted_iota(jnp.int32, imp.shape, block_axis)
    cur = q_pos // SLC_BLOCK
    forced = (blk == 0) | (blk == cur) | (blk == cur - 1)
    valid = (blk * SLC_BLOCK <= q_pos) & (blk < n_blocks)
    return jnp.where(valid, imp + jnp.where(forced, FORCE_BONUS, 0.0), NEG_INF)


def _head_rows(tile_ref, h):
    tokens, heads, width = tile_ref.shape
    return tile_ref.reshape(tokens * heads, width)[pl.ds(h, tokens, stride=heads), :]


def _stack_heads(ref, n):
    return jnp.concatenate([ref[:, r * HEAD_DIM:(r + 1) * HEAD_DIM] for r in range(n)], 0)


def _nsa_prompt_kernel(qraw_ref, qrot_ref, kc_ref, vc_ref, ks_ref, vs_ref, kw_ref, vw_ref,
                       gate_ref, o_ref, ksb_ref, vsb_ref, kwb_ref, vwb_ref, *, seq, kchunk):
    i = pl.program_id(2)
    q0 = i * Q_BLOCK
    nc = kc_ref.shape[0]
    ns = seq // SLC_BLOCK
    rep = NSA_REP
    t_col = lax.broadcasted_iota(jnp.int32, (Q_BLOCK, 1), 0) + q0
    over_heads = lambda x: jnp.concatenate([x] * rep, 0)
    nt = (((1,), (1,)), ((), ()))

    @pl.when(i == 0)
    def _():
        ksb_ref[...] = ks_ref[...].astype(BF)
        vsb_ref[...] = vs_ref[...].astype(BF)
        kwb_ref[...] = kw_ref[...].astype(BF)
        vwb_ref[...] = vw_ref[...].astype(BF)

    q_raw = (_stack_heads(qraw_ref, rep) * (ATTN_SCALE * LOG2E)).astype(BF)
    n_row = lax.broadcasted_iota(jnp.int32, (1, nc), 1)
    vis = (n_row >= 1) & (n_row * CMP_STRIDE + (CMP_BLOCK - CMP_STRIDE - 1) <= t_col)
    s_c = (lax.dot_general(q_raw, kc_ref[...].astype(BF), nt, preferred_element_type=F32)
           + over_heads(jnp.where(vis, 0.0, NEG_INF)))
    e_c = jnp.exp2(s_c - jnp.max(s_c, -1, keepdims=True)) * over_heads(jnp.where(vis, 1.0, 0.0))
    p_c = e_c / jnp.maximum(jnp.sum(e_c, -1, keepdims=True), 1e-30)
    o_cmp = _dot(p_c, vc_ref[...])
    p_sum = p_c[0:Q_BLOCK]
    for r in range(1, rep):
        p_sum = p_sum + p_c[r * Q_BLOCK:(r + 1) * Q_BLOCK]
    imp_t = _dot_exact01(p_sum, _overlap_matrix(nc, ns, blocks_first=True), m01_first=True)
    t_row = lax.broadcasted_iota(jnp.int32, (1, Q_BLOCK), 1) + q0
    sel_t = _topk_select_rows(_block_scores(imp_t, t_row, ns, block_axis=0), min(SLC_TOPK, ns))
    sel_b = sel_t.T.astype(BF)

    q_rot = (_stack_heads(qrot_ref, rep) * (ATTN_SCALE * LOG2E)).astype(BF)
    per_chunk = kchunk // SLC_BLOCK
    blk = lax.broadcasted_iota(jnp.int32, (ns, per_chunk), 0)
    slot = lax.broadcasted_iota(jnp.int32, (ns, per_chunk), 1)
    b_row = lax.broadcasted_iota(jnp.int32, (per_chunk, kchunk), 0)
    b_lane = lax.broadcasted_iota(jnp.int32, (per_chunk, kchunk), 1)
    expand = jnp.where(b_row == b_lane // SLC_BLOCK, 1.0, 0.0).astype(BF)
    k_lane = lax.broadcasted_iota(jnp.int32, (1, kchunk), 1)

    def slc_chunk(c, carry, diagonal):
        k0 = pl.multiple_of(c * kchunk, kchunk)
        pick = jnp.where(blk == c * per_chunk + slot, 1.0, 0.0).astype(BF)
        sel_c = jnp.dot(sel_b, pick, preferred_element_type=F32).astype(BF)
        keep = jnp.dot(sel_c, expand, preferred_element_type=F32) > 0.5
        if diagonal:
            keep = keep & (k0 + k_lane <= t_col)
        s2 = (lax.dot_general(q_rot, ksb_ref[pl.ds(k0, kchunk), :], nt, preferred_element_type=F32)
              + over_heads(jnp.where(keep, 0.0, NEG_INF)))
        return _flash_update(carry, s2, vsb_ref[pl.ds(k0, kchunk), :])

    init = (jnp.full((rep * Q_BLOCK, 1), NEG_INF, F32), jnp.zeros((rep * Q_BLOCK, 1), F32),
            jnp.zeros((rep * Q_BLOCK, HEAD_DIM), F32))
    last = q0 // kchunk
    carry = lax.fori_loop(0, last, lambda c, carry: slc_chunk(c, carry, False), init)
    _, l_s, acc_s = slc_chunk(last, carry, True)
    o_slc = acc_s / jnp.maximum(l_s, 1e-30)

    wk = WINDOW + Q_BLOCK
    w0 = pl.multiple_of(jnp.maximum(q0 - WINDOW, 0), Q_BLOCK)
    dist = t_col - (w0 + lax.broadcasted_iota(jnp.int32, (1, wk), 1))
    s_w = (lax.dot_general(q_rot, kwb_ref[pl.ds(w0, wk), :], nt, preferred_element_type=F32)
           + over_heads(jnp.where((dist >= 0) & (dist < WINDOW), 0.0, NEG_INF)))
    e_w = jnp.exp2(s_w - jnp.max(s_w, -1, keepdims=True))
    p_w = e_w / jnp.sum(e_w, -1, keepdims=True)
    o_swa = jnp.dot(p_w.astype(BF), vwb_ref[pl.ds(w0, wk), :], preferred_element_type=F32)

    gt = gate_ref[...]
    for r in range(rep):
        rows = slice(r * Q_BLOCK, (r + 1) * Q_BLOCK)
        o_ref[:, r * HEAD_DIM:(r + 1) * HEAD_DIM] = (
            gt[:, 3 * r:3 * r + 1] * o_cmp[rows] + gt[:, 3 * r + 1:3 * r + 2] * o_slc[rows]
            + gt[:, 3 * r + 2:3 * r + 3] * o_swa[rows])


def _nsa_prompt(p_main, q_rot, kc_all, sig, batch, seq):
    nq = seq // Q_BLOCK
    gw = NSA_REP * HEAD_DIM
    kchunk = min(512, seq)
    col = lambda base, kv: (lambda b, g, i: (b, base + kv * NSA_KV_HEADS + g))
    return pl.pallas_call(
        functools.partial(_nsa_prompt_kernel, seq=seq, kchunk=kchunk),
        grid=(batch, NSA_KV_HEADS, nq),
        in_specs=[
            pl.BlockSpec((Q_BLOCK, gw), lambda b, g, i: (b * nq + i, g)),
            pl.BlockSpec((Q_BLOCK, gw), lambda b, g, i: (b * nq + i, g)),
            pl.BlockSpec((None, seq // CMP_STRIDE, HEAD_DIM), lambda b, g, i: (b, 0, g)),
            pl.BlockSpec((None, seq // CMP_STRIDE, HEAD_DIM), lambda b, g, i: (b, 0, NSA_KV_HEADS + g)),
            pl.BlockSpec((seq, HEAD_DIM), col(COL_KVS, 0)),
            pl.BlockSpec((seq, HEAD_DIM), col(COL_KVS, 1)),
            pl.BlockSpec((seq, HEAD_DIM), col(COL_KVW, 0)),
            pl.BlockSpec((seq, HEAD_DIM), col(COL_KVW, 1)),
            pl.BlockSpec((Q_BLOCK, LANE), lambda b, g, i: (b * nq + i, g)),
        ],
        out_specs=pl.BlockSpec((Q_BLOCK, gw), lambda b, g, i: (b * nq + i, g)),
        out_shape=jax.ShapeDtypeStruct((batch * seq, NSA_HEADS * HEAD_DIM), F32),
        scratch_shapes=[pltpu.VMEM((seq, HEAD_DIM), BF)] * 4,
        compiler_params=_cparams(("arbitrary", "arbitrary", "arbitrary")),
        name="nsa_attention_prompt",
    )(p_main, q_rot, kc_all, kc_all, p_main, p_main, p_main, p_main, sig)


def _suffix_sum_lanes(x):
    n = x.shape[-1]
    lane = lax.broadcasted_iota(jnp.int32, x.shape, x.ndim - 1)
    k = 1
    while k < n:
        x = x + jnp.where(lane < n - k, pltpu.roll(x, n - k, x.ndim - 1), 0.0)
        k *= 2
    return x


def _decay_prompt_kernel(lf_ref, d_ref):
    lt = lf_ref[...].T
    top = lt[:SUBLANE]
    d_ref[...] = (_suffix_sum_lanes(top) - top) * LOG2E


def _decay_prompt(lsig, batch, seq):
    return pl.pallas_call(
        _decay_prompt_kernel,
        grid=(batch,),
        in_specs=[pl.BlockSpec((seq, LANE), lambda b: (b, 0))],
        out_specs=pl.BlockSpec((None, SUBLANE, seq), lambda b: (b, 0, 0)),
        out_shape=jax.ShapeDtypeStruct((batch, SUBLANE, seq), F32),
        compiler_params=_cparams(("arbitrary",)),
        name="fox_decay_prompt",
    )(lsig)


def _fox_prompt_kernel(q_ref, k_ref, v_ref, d_ref, o_ref, kb_ref, vb_ref, m_ref, l_ref, acc_ref,
                       *, tile):
    h = pl.program_id(1)
    i = pl.program_id(2)

    @pl.when(i == 0)
    def _():
        kb_ref[...] = k_ref[...].astype(BF)
        vb_ref[...] = v_ref[...].astype(BF)

    q = (q_ref[...] * (ATTN_SCALE * LOG2E)).astype(BF)
    row = lax.broadcasted_iota(jnp.int32, (tile, tile), 0)
    col = lax.broadcasted_iota(jnp.int32, (tile, tile), 1)

    def chunk(c, carry, diagonal):
        k0 = pl.multiple_of(c * tile, tile)
        s2 = lax.dot_general(q, kb_ref[pl.ds(k0, tile), :], (((1,), (1,)), ((), ())),
                             preferred_element_type=F32) + d_ref[pl.ds(h, 1), pl.ds(k0, tile)]
        if diagonal:
            s2 = jnp.where(col <= row, s2, NEG_INF)
        return _flash_update(carry, s2, vb_ref[pl.ds(k0, tile), :])

    init = (jnp.full((tile, 1), NEG_INF, F32), jnp.zeros((tile, 1), F32),
            jnp.zeros((tile, HEAD_DIM), F32))
    carry = lax.fori_loop(0, i, lambda c, carry: chunk(c, carry, False), init)
    _, l, acc = chunk(i, carry, True)
    o_ref[...] = acc / jnp.maximum(l, 1e-30)


def _fox_prompt(p_main, decay, batch, seq):
    tq = min(512, seq)
    nq = seq // tq
    return pl.pallas_call(
        functools.partial(_fox_prompt_kernel, tile=tq),
        grid=(batch, FOX_HEADS, nq),
        scratch_shapes=[pltpu.VMEM((seq, HEAD_DIM), BF), pltpu.VMEM((seq, HEAD_DIM), BF),
                        pltpu.VMEM((tq, 1), F32), pltpu.VMEM((tq, 1), F32),
                        pltpu.VMEM((tq, HEAD_DIM), F32)],
        in_specs=[
            pl.BlockSpec((tq, HEAD_DIM), lambda b, h, i: (b * nq + i, COL_FOX + h)),
            pl.BlockSpec((seq, HEAD_DIM), lambda b, h, i: (b, COL_FOX + FOX_HEADS + h)),
            pl.BlockSpec((seq, HEAD_DIM), lambda b, h, i: (b, COL_FOX + 2 * FOX_HEADS + h)),
            pl.BlockSpec((None, SUBLANE, seq), lambda b, h, i: (b, 0, 0)),
        ],
        out_specs=pl.BlockSpec((tq, HEAD_DIM), lambda b, h, i: (b * nq + i, h)),
        out_shape=jax.ShapeDtypeStruct((batch * seq, FOX_HEADS * HEAD_DIM), F32),
        compiler_params=_cparams(("arbitrary", "arbitrary", "arbitrary")),
        name="fox_attention_prompt",
    )(p_main, p_main, p_main, decay)


def _mem_attn_kernel(q_ref, k_ref, v_ref, o_ref):
    s = _dot_nt(q_ref[...], k_ref[...]) * ATTN_SCALE
    m = jnp.max(s, -1, keepdims=True)
    e = jnp.exp(s - m)
    p = e / jnp.sum(e, -1, keepdims=True)
    o_ref[...] = _dot(p, v_ref[...])


def _mem_attention(p_main, mem_kv2d, batch, rows_per_batch, tq):
    nq = rows_per_batch // tq
    mlen = mem_kv2d.shape[0] // batch
    return pl.pallas_call(
        _mem_attn_kernel,
        grid=(batch, MEM_HEADS, nq),
        in_specs=[
            pl.BlockSpec((tq, HEAD_DIM), lambda b, h, i: (b * nq + i, COL_QM + h)),
            pl.BlockSpec((mlen, HEAD_DIM), lambda b, h, i: (b, h)),
            pl.BlockSpec((mlen, HEAD_DIM), lambda b, h, i: (b, MEM_HEADS + h)),
        ],
        out_specs=pl.BlockSpec((tq, HEAD_DIM), lambda b, h, i: (b * nq + i, h)),
        out_shape=jax.ShapeDtypeStruct((batch * rows_per_batch, MEM_HEADS * HEAD_DIM), F32),
        compiler_params=_cparams(("arbitrary", "arbitrary", "arbitrary")),
        name="mem_attention",
    )(p_main, mem_kv2d, mem_kv2d)


def _mem_attn_rows_kernel(q_ref, kv_ref, o_ref):
    slots = 2 * MEM_HEADS
    mlen = kv_ref.shape[0] // slots
    scores = [_dot_nt(q_ref[:, h * HEAD_DIM:(h + 1) * HEAD_DIM],
                      kv_ref[pl.ds(h, mlen, stride=slots), :]) for h in range(MEM_HEADS)]
    s = jnp.concatenate(scores, 0) * ATTN_SCALE
    m = jnp.max(s, -1, keepdims=True)
    e = jnp.exp(s - m)
    p = e / jnp.sum(e, -1, keepdims=True)
    for h in range(MEM_HEADS):
        o_ref[:, h * HEAD_DIM:(h + 1) * HEAD_DIM] = _dot(
            p[h * SROW:(h + 1) * SROW], kv_ref[pl.ds(MEM_HEADS + h, mlen, stride=slots), :])


def _mem_attention_rows(p_s, mem_rows):
    dec_batch, rows, _ = mem_rows.shape
    mw = MEM_HEADS * HEAD_DIM
    return pl.pallas_call(
        _mem_attn_rows_kernel,
        grid=(dec_batch,),
        in_specs=[pl.BlockSpec((SROW, mw), lambda b: (b, COL_QM * LANE // mw)),
                  pl.BlockSpec((None, rows, HEAD_DIM), lambda b: (b, 0, 0))],
        out_specs=pl.BlockSpec((SROW, mw), lambda b: (b, 0)),
        out_shape=jax.ShapeDtypeStruct((dec_batch * SROW, mw), F32),
        compiler_params=_cparams(("arbitrary",)),
        name="mem_attention_sample",
    )(p_s, mem_rows)


def _matmul_kernel(x_ref, w_ref, o_ref):
    o_ref[...] = jnp.dot(x_ref[...].astype(BF), w_ref[...], preferred_element_type=F32)


def _matmul(x2d, w_bf, tm, tn):
    m, k = x2d.shape
    n = w_bf.shape[1]
    return pl.pallas_call(
        _matmul_kernel,
        grid=(m // tm, n // tn),
        in_specs=[pl.BlockSpec((tm, k), lambda i, j: (i, 0)),
                  pl.BlockSpec((k, tn), lambda i, j: (0, j))],
        out_specs=pl.BlockSpec((tm, tn), lambda i, j: (i, j)),
        out_shape=jax.ShapeDtypeStruct((m, n), F32),
        compiler_params=_cparams(("arbitrary", "arbitrary")),
        name="projection_matmul",
    )(x2d, w_bf)


def _layer_norm(z, g, b):
    zc = z - jnp.mean(z, -1, keepdims=True)
    var = jnp.mean(zc * zc, -1, keepdims=True)
    return zc * lax.rsqrt(var + LN_EPS) * g + b


def _outproj_kernel(on_ref, of_ref, om_ref, x_ref, w_ref, g_ref, b_ref, h_ref, *, alpha):
    mix = jnp.concatenate([on_ref[...].astype(BF), of_ref[...].astype(BF),
                           om_ref[...].astype(BF)], 1)
    y = jnp.dot(mix, w_ref[...], preferred_element_type=F32)
    h_ref[...] = _layer_norm(alpha * x_ref[...] + y, g_ref[...], b_ref[...])


def _out_projection(o_nsa, o_fox, o_mem, x2d, w_out, g, b, tm, alpha):
    m, d = x2d.shape
    row = lambda w: pl.BlockSpec((tm, w), lambda i: (i, 0))
    const = lambda shape: pl.BlockSpec(shape, lambda i: (0, 0))
    return pl.pallas_call(
        functools.partial(_outproj_kernel, alpha=alpha),
        grid=(m // tm,),
        in_specs=[row(o_nsa.shape[1]), row(o_fox.shape[1]), row(o_mem.shape[1]), row(d),
                  const(w_out.shape), const((1, d)), const((1, d))],
        out_specs=row(d),
        out_shape=jax.ShapeDtypeStruct((m, d), F32),
        compiler_params=_cparams(("arbitrary",)),
        name="out_projection_ln",
    )(o_nsa, o_fox, o_mem, x2d, w_out, g, b)


def _ffn_core(h_ref, wa_ref, wb_ref, cw_ref, cb_ref, wd_ref, g_ref, b_ref, y_ref, acc_ref,
              shifted, alpha):
    j = pl.program_id(1)
    hb = h_ref[...].astype(BF)
    a = jnp.dot(hb, wa_ref[...], preferred_element_type=F32)
    gate_in = jnp.dot(hb, wb_ref[...], preferred_element_type=F32)
    a1, a2 = shifted(a)
    cw = cw_ref[...]
    c = cb_ref[...] + a2 * cw[0:1] + a1 * cw[1:2] + a * cw[2:3]
    act = (c * jax.nn.sigmoid(c)) * gate_in
    part = jnp.dot(act.astype(BF), wd_ref[...], preferred_element_type=F32)

    @pl.when(j == 0)
    def _():
        acc_ref[...] = part

    @pl.when(j > 0)
    def _():
        acc_ref[...] += part

    @pl.when(j == pl.num_programs(1) - 1)
    def _():
        y_ref[...] = _layer_norm(alpha * h_ref[...] + acc_ref[...], g_ref[...], b_ref[...])
    return a


def _ffn_prompt_kernel(h_ref, wa_ref, wb_ref, cw_ref, cb_ref, wd_ref, g_ref, b_ref,
                       y_ref, tail_ref, acc_ref, halo_ref, *, tiles_per_seq, alpha):
    i = pl.program_id(0)
    j = pl.program_id(1)
    tm = h_ref.shape[0]
    seq_start = (i % tiles_per_seq) == 0

    @pl.when(seq_start)
    def _():
        halo_ref[j] = jnp.zeros(halo_ref.shape[1:], F32)

    def shifted(a):
        row = lax.broadcasted_iota(jnp.int32, a.shape, 0)
        halo = halo_ref[j]
        h1 = halo[SUBLANE - 1:SUBLANE]
        h2 = halo[SUBLANE - 2:SUBLANE - 1]
        a1 = jnp.where(row == 0, h1, pltpu.roll(a, 1, 0))
        a2 = jnp.where(row == 0, h2, jnp.where(row == 1, h1, pltpu.roll(a, 2, 0)))
        return a1, a2

    a = _ffn_core(h_ref, wa_ref, wb_ref, cw_ref, cb_ref, wd_ref, g_ref, b_ref, y_ref, acc_ref,
                  shifted, alpha)
    halo_ref[j] = a[tm - SUBLANE:]
    tail_ref[...] = a[tm - SUBLANE:]


def _ffn_sample_kernel(h_ref, wa_ref, wb_ref, cw_ref, cb_ref, wd_ref, g_ref, b_ref, f1_ref, f2_ref,
                       y_ref, a_ref, acc_ref, *, alpha):
    def shifted(a):
        s = lax.broadcasted_iota(jnp.int32, a.shape, 0) % SROW
        a1 = jnp.where(s >= 1, pltpu.roll(a, 1, 0), 0.0) + f1_ref[...]
        a2 = jnp.where(s >= 2, pltpu.roll(a, 2, 0), 0.0) + f2_ref[...]
        return a1, a2

    a_ref[...] = _ffn_core(h_ref, wa_ref, wb_ref, cw_ref, cb_ref, wd_ref, g_ref, b_ref, y_ref,
                           acc_ref, shifted, alpha)


def _ffn_specs(tm, d, tf, nf):
    return [
        pl.BlockSpec((tm, d), lambda i, j: (i, 0)),
        pl.BlockSpec((d, tf), lambda i, j: (0, j)),
        pl.BlockSpec((d, tf), lambda i, j: (0, nf + j)),
        pl.BlockSpec((SUBLANE, tf), lambda i, j: (0, j)),
        pl.BlockSpec((1, tf), lambda i, j: (0, j)),
        pl.BlockSpec((tf, d), lambda i, j: (j, 0)),
        pl.BlockSpec((1, d), lambda i, j: (0, 0)),
        pl.BlockSpec((1, d), lambda i, j: (0, 0)),
    ]


def _ffn_prompt(h2d, fw, seq, tm, tf, alpha):
    w_up, conv_w8, conv_b, w_down, g, b = fw
    m, d = h2d.shape
    dff = w_down.shape[0]
    nf = dff // tf
    return pl.pallas_call(
        functools.partial(_ffn_prompt_kernel, tiles_per_seq=seq // tm, alpha=alpha),
        grid=(m // tm, nf),
        in_specs=_ffn_specs(tm, d, tf, nf),
        out_specs=[pl.BlockSpec((tm, d), lambda i, j: (i, 0)),
                   pl.BlockSpec((None, SUBLANE, tf), lambda i, j: (i, 0, j))],
        out_shape=[jax.ShapeDtypeStruct((m, d), F32),
                   jax.ShapeDtypeStruct((m // tm, SUBLANE, dff), F32)],
        scratch_shapes=[pltpu.VMEM((tm, d), F32), pltpu.VMEM((nf, SUBLANE, tf), F32)],
        compiler_params=_cparams(("arbitrary", "arbitrary")),
        name="conv_ffn_prompt",
    )(h2d, w_up, w_up, conv_w8, conv_b, w_down, g, b)


def _ffn_sample(h2d, fw, fill1, fill2, tf, alpha):
    w_up, conv_w8, conv_b, w_down, g, b = fw
    m, d = h2d.shape
    dff = w_down.shape[0]
    nf = dff // tf
    return pl.pallas_call(
        functools.partial(_ffn_sample_kernel, alpha=alpha),
        grid=(1, nf),
        in_specs=_ffn_specs(m, d, tf, nf) + [pl.BlockSpec((m, tf), lambda i, j: (0, j)),
                                             pl.BlockSpec((m, tf), lambda i, j: (0, j))],
        out_specs=[pl.BlockSpec((m, d), lambda i, j: (0, 0)),
                   pl.BlockSpec((m, tf), lambda i, j: (0, j))],
        out_shape=[jax.ShapeDtypeStruct((m, d), F32), jax.ShapeDtypeStruct((m, dff), F32)],
        scratch_shapes=[pltpu.VMEM((m, d), F32)],
        compiler_params=_cparams(("arbitrary", "arbitrary")),
        name="conv_ffn_sample",
    )(h2d, w_up, w_up, conv_w8, conv_b, w_down, g, b, fill1, fill2)


def _nsa_sample_a_kernel(qraw_ref, qrot_ref, kc_ref, vc_ref, swa_ref, kwn_ref, vwn_ref,
                         gate_ref, part_ref, sel_ref, *, past, n_blocks):
    rep = NSA_REP
    g = pl.program_id(1)
    nc = kc_ref.shape[0]
    nsp = sel_ref.shape[-1]
    s_col = lax.broadcasted_iota(jnp.int32, (SROW, 1), 0) + past

    q_raw = _stack_heads(qraw_ref, rep)
    s_c = (_dot_nt(q_raw, kc_ref[...]) * ATTN_SCALE).reshape(rep, SROW, nc)
    n_row = lax.broadcasted_iota(jnp.int32, (1, nc), 1)
    vis = (n_row >= 1) & (n_row * CMP_STRIDE + (CMP_BLOCK - CMP_STRIDE - 1) <= s_col)
    p_c = _masked_softmax(s_c, vis[None])
    o_cmp = _dot(p_c.reshape(rep * SROW, nc), vc_ref[...])
    imp = _dot_exact01(jnp.sum(p_c, 0), _overlap_matrix(nc, nsp))
    sel_ref[...] = _topk_select(_block_scores(imp, s_col, n_blocks), min(SLC_TOPK, n_blocks), n_blocks)

    q_rot = _stack_heads(qrot_ref, rep)
    wbuf = swa_ref.shape[0] // N_KV_SLOTS
    keys = jnp.concatenate([swa_ref[pl.ds(g, wbuf, stride=N_KV_SLOTS), :], kwn_ref[...]], 0)
    vals = jnp.concatenate([swa_ref[pl.ds(NSA_KV_HEADS + g, wbuf, stride=N_KV_SLOTS), :],
                            vwn_ref[...]], 0)
    w_pos = past - wbuf + lax.broadcasted_iota(jnp.int32, (1, wbuf + SROW), 1)
    dist = s_col - w_pos
    mask = (dist >= 0) & (dist < WINDOW) & (w_pos >= 0)
    s_w = (_dot_nt(q_rot, keys) * ATTN_SCALE).reshape(rep, SROW, wbuf + SROW)
    p_w = _masked_softmax(s_w, mask[None])
    o_swa = _dot(p_w.reshape(rep * SROW, wbuf + SROW), vals)

    gt = gate_ref[...]
    for r in range(rep):
        rows = slice(r * SROW, (r + 1) * SROW)
        part_ref[rows, :] = (gt[:, 3 * r:3 * r + 1] * o_cmp[rows]
                             + gt[:, 3 * r + 2:3 * r + 3] * o_swa[rows])


def _nsa_sample_a(p_s, qrot_s, kc_all, swa_rows, sig_s, past, n_blocks):
    dec_batch, nc, _ = kc_all.shape
    gw = NSA_REP * HEAD_DIM
    nsp = -(-n_blocks // LANE) * LANE
    bg = lambda b, g: (b, g)
    return pl.pallas_call(
        functools.partial(_nsa_sample_a_kernel, past=past, n_blocks=n_blocks),
        grid=(dec_batch, NSA_KV_HEADS),
        in_specs=[
            pl.BlockSpec((SROW, gw), bg),
            pl.BlockSpec((SROW, gw), bg),
            pl.BlockSpec((None, nc, HEAD_DIM), lambda b, g: (b, 0, g)),
            pl.BlockSpec((None, nc, HEAD_DIM), lambda b, g: (b, 0, NSA_KV_HEADS + g)),
            pl.BlockSpec((None, swa_rows.shape[1], HEAD_DIM), lambda b, g: (b, 0, 0)),
            pl.BlockSpec((SROW, HEAD_DIM), lambda b, g: (b, COL_KVW + g)),
            pl.BlockSpec((SROW, HEAD_DIM), lambda b, g: (b, COL_KVW + NSA_KV_HEADS + g)),
            pl.BlockSpec((SROW, LANE), bg),
        ],
        out_specs=[pl.BlockSpec((None, None, NSA_REP * SROW, HEAD_DIM), lambda b, g: (b, g, 0, 0)),
                   pl.BlockSpec((None, None, SROW, nsp), lambda b, g: (b, g, 0, 0))],
        out_shape=[jax.ShapeDtypeStruct((dec_batch, NSA_KV_HEADS, NSA_REP * SROW, HEAD_DIM), F32),
                   jax.ShapeDtypeStruct((dec_batch, NSA_KV_HEADS, SROW, nsp), F32)],
        compiler_params=_cparams(("arbitrary", "arbitrary")),
        name="nsa_sample_cmp_swa",
    )(p_s, qrot_s, kc_all, kc_all, swa_rows, p_s, p_s, sig_s)


def _nsa_sample_b_kernel(pt_ref, *refs, past, n_blocks):
    del pt_ref
    pages = refs[:PAGES_PER_STEP]
    (qrot_ref, sel_ref, knew_ref, vnew_ref, gate_ref, part_ref,
     o_ref, m_ref, l_ref, acc_ref) = refs[PAGES_PER_STEP:]
    c = pl.program_id(1)
    nchunk = pl.num_programs(1) - 1
    rep = NSA_REP
    page = pages[0].shape[0] // N_KV_SLOTS
    keys_per_step = PAGES_PER_STEP * page
    blocks_per_step = keys_per_step // SLC_BLOCK
    nsp = sel_ref.shape[-1]
    gw = rep * HEAD_DIM

    def cache_rows(slot):
        return jnp.concatenate([p[pl.ds(slot, page, stride=N_KV_SLOTS), :].astype(BF)
                                for p in pages], 0)

    @pl.when(c == 0)
    def _():
        _flash_init(m_ref, l_ref, acc_ref)

    groups = range(NSA_KV_HEADS)
    queries = [_stack_heads(qrot_ref.at[:, g * gw:(g + 1) * gw], rep) for g in groups]

    def update(scores, keeps, values):
        keep = jnp.concatenate([k for k in keeps for _ in range(rep)], 0) > 0.5
        s = jnp.where(keep, jnp.concatenate(scores, 0) * ATTN_SCALE, NEG_INF)
        m_old = m_ref[...]
        m_new = jnp.maximum(m_old, jnp.max(s, -1, keepdims=True))
        p = jnp.where(keep, jnp.exp(s - m_new), 0.0)
        alpha = jnp.exp(m_old - m_new)
        rows = rep * SROW
        pv = jnp.concatenate([_dot(p[g * rows:(g + 1) * rows], values[g]) for g in groups], 0)
        m_ref[...] = m_new
        l_ref[...] = alpha * l_ref[...] + jnp.sum(p, -1, keepdims=True)
        acc_ref[...] = alpha * acc_ref[...] + pv

    @pl.when(c < nchunk)
    def _():
        blk = lax.broadcasted_iota(jnp.int32, (nsp, blocks_per_step), 0)
        col = lax.broadcasted_iota(jnp.int32, (nsp, blocks_per_step), 1)
        pick = jnp.where(blk == c * blocks_per_step + col, 1.0, 0.0).astype(BF)
        b_row = lax.broadcasted_iota(jnp.int32, (blocks_per_step, keys_per_step), 0)
        k_lane = lax.broadcasted_iota(jnp.int32, (blocks_per_step, keys_per_step), 1)
        expand = jnp.where(b_row == k_lane // SLC_BLOCK, 1.0, 0.0).astype(BF)
        scores = [_dot_nt(queries[g], cache_rows(g)) for g in groups]
        keeps = []
        for g in groups:
            sel_c = jnp.dot(sel_ref[g].astype(BF), pick, preferred_element_type=F32)
            keeps.append(jnp.dot(sel_c.astype(BF), expand, preferred_element_type=F32))
        update(scores, keeps, [cache_rows(NSA_KV_HEADS + g) for g in groups])

    @pl.when(c == nchunk)
    def _():
        s_q = lax.broadcasted_iota(jnp.int32, (SROW, LANE), 0)
        s_k = lax.broadcasted_iota(jnp.int32, (SROW, LANE), 1)
        causal = jnp.where((s_k <= s_q) & (s_k < SROW), 1.0, 0.0)
        pad = jnp.zeros((LANE - SROW, HEAD_DIM), F32)
        new_rows = lambda ref, g: jnp.concatenate([ref[:, g * HEAD_DIM:(g + 1) * HEAD_DIM], pad], 0)
        scores = [_dot_nt(queries[g], new_rows(knew_ref, g)) for g in groups]
        keeps = [sel_ref[g][:, n_blocks - 1:n_blocks] * causal for g in groups]
        update(scores, keeps, [new_rows(vnew_ref, g) for g in groups])
        o_slc = acc_ref[...] / jnp.maximum(l_ref[...], 1e-30)
        gt = gate_ref[...]
        for g in groups:
            for r in range(rep):
                col = 3 * rep * g + 3 * r + 1
                rows = slice((g * rep + r) * SROW, (g * rep + r + 1) * SROW)
                o_ref[:, (g * rep + r) * HEAD_DIM:(g * rep + r + 1) * HEAD_DIM] = (
                    part_ref[g, r * SROW:(r + 1) * SROW, :] + gt[:, col:col + 1] * o_slc[rows])


def _nsa_sample_b(cache_rows, page_table, p_s, qrot_s, sel, sig12, part, past, n_blocks):
    n_pool, page_rows, _ = cache_rows.shape
    dec_batch, n_pages = page_table.shape
    nchunk = n_pages // PAGES_PER_STEP
    nsp = sel.shape[-1]
    qw = NSA_HEADS * HEAD_DIM
    half = NSA_KV_HEADS * HEAD_DIM
    row = lambda w, col: pl.BlockSpec((SROW, w), lambda b, c, pt: (b, col))
    return pl.pallas_call(
        functools.partial(_nsa_sample_b_kernel, past=past, n_blocks=n_blocks),
        grid_spec=pltpu.PrefetchScalarGridSpec(
            num_scalar_prefetch=1,
            grid=(dec_batch, nchunk + 1),
            in_specs=_page_specs(page_rows, HEAD_DIM, lambda c: jnp.minimum(c, nchunk - 1)) + [
                row(qw, 0),
                pl.BlockSpec((None, NSA_KV_HEADS, SROW, nsp), lambda b, c, pt: (b, 0, 0, 0)),
                row(half, COL_KVS * LANE // half),
                row(half, COL_KVS * LANE // half + 1),
                row(LANE, 0),
                pl.BlockSpec((None, NSA_KV_HEADS, NSA_REP * SROW, HEAD_DIM),
                             lambda b, c, pt: (b, 0, 0, 0)),
            ],
            out_specs=row(qw, 0),
            scratch_shapes=[pltpu.VMEM((NSA_HEADS * SROW, 1), F32),
                            pltpu.VMEM((NSA_HEADS * SROW, 1), F32),
                            pltpu.VMEM((NSA_HEADS * SROW, HEAD_DIM), F32)],
        ),
        out_shape=jax.ShapeDtypeStruct((dec_batch * SROW, qw), F32),
        compiler_params=_cparams(("arbitrary", "arbitrary")),
        name="nsa_sample_selected",
    )(page_table, *([cache_rows] * PAGES_PER_STEP), qrot_s, sel, p_s, p_s, sig12, part)


def _fox_sample_kernel(pt_ref, *refs, s_len):
    n = PAGES_PER_STEP
    kpages, vpages = refs[:n], refs[n:2 * n]
    (logf_ref, q_ref, knew_ref, vnew_ref, lnew_ref,
     o_ref, m_ref, l_ref, acc_ref, carry_ref) = refs[2 * n:]
    b = pl.program_id(0)
    c = pl.program_id(1)
    page = kpages[0].shape[0]
    head = lambda ref, h: ref[:, h * HEAD_DIM:(h + 1) * HEAD_DIM]

    def cache_rows(pages, h):
        return jnp.concatenate([_head_rows(p, h).astype(BF) for p in pages], 0)

    def update(scores, mask, values):
        s = jnp.concatenate(scores, 0)
        if mask is not None:
            s = jnp.where(mask, s, NEG_INF)
        m_old = m_ref[...]
        m_new = jnp.maximum(m_old, jnp.max(s, -1, keepdims=True))
        p = jnp.exp(s - m_new)
        if mask is not None:
            p = jnp.where(mask, p, 0.0)
        alpha = jnp.exp(m_old - m_new)
        pv = jnp.concatenate([_dot(p[h * SROW:(h + 1) * SROW], values[h])
                              for h in range(FOX_HEADS)], 0)
        m_ref[...] = m_new
        l_ref[...] = alpha * l_ref[...] + jnp.sum(p, -1, keepdims=True)
        acc_ref[...] = alpha * acc_ref[...] + pv

    @pl.when(c == 0)
    def _():
        m_ref[...] = jnp.full(m_ref.shape, NEG_INF, F32)
        l_ref[...] = jnp.zeros(l_ref.shape, F32)
        acc_ref[...] = jnp.zeros(acc_ref.shape, F32)
        s_q = lax.broadcasted_iota(jnp.int32, (SROW, LANE), 0)
        s_k = lax.broadcasted_iota(jnp.int32, (SROW, LANE), 1)
        later = jnp.where((s_q > s_k) & (s_q < s_len), 1.0, 0.0)
        real = lax.broadcasted_iota(jnp.int32, (SROW, 1), 0) < s_len
        lnew = lnew_ref[...]
        pad = jnp.zeros((LANE - SROW, HEAD_DIM), F32)
        rows_q = lax.broadcasted_iota(jnp.int32, (FOX_HEADS * SROW, LANE), 0) % SROW
        cols_k = lax.broadcasted_iota(jnp.int32, (FOX_HEADS * SROW, LANE), 1)
        mask = (cols_k <= rows_q) & (cols_k < s_len)
        totals, scores, values = [], [], []
        for h in range(FOX_HEADS):
            lf = jnp.where(real, lnew[:, h:h + 1], 0.0)
            d_row = jnp.sum(lf * later, 0, keepdims=True)
            totals.append(jnp.sum(lf, 0, keepdims=True))
            k = jnp.concatenate([head(knew_ref, h), pad], 0)
            values.append(jnp.concatenate([head(vnew_ref, h), pad], 0))
            scores.append(_dot_nt(head(q_ref, h), k) * ATTN_SCALE + d_row)
        update(scores, mask, values)
        totals += [jnp.zeros((1, 1), F32)] * (SUBLANE - FOX_HEADS)
        carry_ref[...] = jnp.broadcast_to(jnp.concatenate(totals, 0), carry_ref.shape)

    @pl.when(c > 0)
    def _():
        first = (pl.num_programs(1) - 1 - c) * n
        ids = [pt_ref[b, first + i] for i in range(n)]
        lf = jnp.concatenate(
            [jnp.concatenate([logf_ref[h, pl.ds(pg, 1), :] for pg in ids], 1) for h in range(FOX_HEADS)]
            + [jnp.zeros((SUBLANE - FOX_HEADS, n * page), F32)], 0)
        incl = _suffix_sum_lanes(lf)
        carry = carry_ref[:, 0:1]
        decay = carry + (incl - lf)
        carry_ref[...] = jnp.broadcast_to(carry + incl[:, 0:1], carry_ref.shape)
        scores = [_dot_nt(head(q_ref, h), cache_rows(kpages, h)) * ATTN_SCALE + decay[h:h + 1, :]
                  for h in range(FOX_HEADS)]
        update(scores, None, [cache_rows(vpages, h) for h in range(FOX_HEADS)])

    @pl.when(c == pl.num_programs(1) - 1)
    def _():
        o = acc_ref[...] / jnp.maximum(l_ref[...], 1e-30)
        for h in range(FOX_HEADS):
            o_ref[:, h * HEAD_DIM:(h + 1) * HEAD_DIM] = o[h * SROW:(h + 1) * SROW]


def _fox_sample(cache, logf_t, page_table, p_s, lsig_s, s_len):
    n_pool, page = cache.shape[:2]
    dec_batch, n_pages = page_table.shape
    nchunk = n_pages // PAGES_PER_STEP
    fw = FOX_HEADS * HEAD_DIM
    chunk_of = lambda c: nchunk - jnp.maximum(c, 1)
    row = lambda w, col: pl.BlockSpec((SROW, w), lambda b, c, pt: (b, col))

    def half_specs(kv):
        def spec(i):
            return pl.BlockSpec((None, page, None, FOX_HEADS, HEAD_DIM),
                                lambda b, c, pt: (pt[b, chunk_of(c) * PAGES_PER_STEP + i], 0, kv, 0, 0))
        return [spec(i) for i in range(PAGES_PER_STEP)]

    return pl.pallas_call(
        functools.partial(_fox_sample_kernel, s_len=s_len),
        grid_spec=pltpu.PrefetchScalarGridSpec(
            num_scalar_prefetch=1,
            grid=(dec_batch, nchunk + 1),
            in_specs=half_specs(0) + half_specs(1) + [
                pl.BlockSpec(logf_t.shape, lambda b, c, pt: (0, 0, 0), pipeline_mode=pl.Buffered(1)),
                row(fw, COL_FOX * LANE // fw),
                row(fw, COL_FOX * LANE // fw + 1),
                row(fw, COL_FOX * LANE // fw + 2),
                row(LANE, 0),
            ],
            out_specs=row(fw, 0),
            scratch_shapes=[pltpu.VMEM((FOX_HEADS * SROW, 1), F32),
                            pltpu.VMEM((FOX_HEADS * SROW, 1), F32),
                            pltpu.VMEM((FOX_HEADS * SROW, HEAD_DIM), F32),
                            pltpu.VMEM((SUBLANE, LANE), F32)],
        ),
        out_shape=jax.ShapeDtypeStruct((dec_batch * SROW, fw), F32),
        compiler_params=_cparams(("arbitrary", "arbitrary")),
        name="fox_attention_sample",
    )(page_table, *([cache] * (2 * PAGES_PER_STEP)), logf_t, p_s, p_s, p_s, lsig_s)


def _rope_tables(pos):
    half = ROT_DIM // 2
    inv = jnp.power(ROPE_THETA, -jnp.arange(half, dtype=F32) * (2.0 / ROT_DIM))
    ang = pos.astype(F32)[:, None] * inv[None, :]
    cos, sin = jnp.cos(ang), jnp.sin(ang)
    n = pos.shape[0]
    c = jnp.concatenate([cos, cos, jnp.ones((n, HEAD_DIM - ROT_DIM), F32)], 1)
    s = jnp.concatenate([-sin, sin, jnp.zeros((n, HEAD_DIM - ROT_DIM), F32)], 1)
    return c, s


def _layer_weights(w_in, b_gate, b_forget, cmp_w1, cmp_b1, cmp_w2, cmp_pos, w_mem_kv, w_out,
                   ln1_g, ln1_b, w_up, conv_w, conv_b, w_down, ln2_g, ln2_b):
    d = w_in.shape[0]
    nq = NSA_HEADS * HEAD_DIM
    kvw = 2 * NSA_KV_HEADS * HEAD_DIM
    o_g = nq + 3 * kvw
    o_fox = o_g + 3 * NSA_HEADS
    o_f = o_fox + 3 * FOX_HEADS * HEAD_DIM
    o_qm = o_f + FOX_HEADS
    w_main = jnp.concatenate([w_in[:, :o_g], w_in[:, o_fox:o_f], w_in[:, o_qm:]], 1).astype(BF)
    per_group = 3 * NSA_REP
    zpad = lambda n: jnp.zeros((d, n), w_in.dtype)
    w_small = jnp.concatenate([
        w_in[:, o_g:o_g + per_group], zpad(LANE - per_group),
        w_in[:, o_g + per_group:o_fox], zpad(LANE - per_group),
        w_in[:, o_f:o_qm], zpad(LANE - FOX_HEADS)], 1).astype(BF)
    bpad = lambda n: jnp.zeros((n,), F32)
    b_small = jnp.concatenate([
        b_gate[:per_group], bpad(LANE - per_group), b_gate[per_group:], bpad(LANE - per_group),
        b_forget, bpad(LANE - FOX_HEADS)])[None, :].astype(F32)
    kdim = CMP_STRIDE * HEAD_DIM
    w1cat = jnp.concatenate([cmp_w1[:, :CMP_STRIDE].reshape(2, kdim, CMP_HIDDEN),
                             cmp_w1[:, CMP_STRIDE:].reshape(2, kdim, CMP_HIDDEN)], 2).astype(BF)
    pe = jnp.concatenate([cmp_pos[:, :CMP_STRIDE].reshape(2, 1, kdim),
                          cmp_pos[:, CMP_STRIDE:].reshape(2, 1, kdim),
                          jnp.zeros((2, SUBLANE - 2, kdim), F32)], 1)
    cw = (w1cat, pe, cmp_b1[:, None, :], cmp_w2.astype(BF))
    conv_w8 = jnp.concatenate([conv_w, jnp.zeros((SUBLANE - CONV_W, conv_w.shape[1]), F32)], 0)
    fw = (w_up.astype(BF), conv_w8, conv_b[None, :], w_down.astype(BF), ln2_g[None, :], ln2_b[None, :])
    return dict(w_main=w_main, w_small=w_small, b_small=b_small, cw=cw,
                w_mem_kv=w_mem_kv.astype(BF), w_out=w_out.astype(BF),
                ln1_g=ln1_g[None, :], ln1_b=ln1_b[None, :], fw=fw)


def _prompt_layer(x, mem, lw, alpha):
    batch, seq, d = x.shape
    x2d = x.reshape(batch * seq, d)
    rc, rs = _rope_tables(jnp.tile(jnp.arange(seq, dtype=jnp.int32), batch))
    tm = min(512, seq)
    p_main, q_rot, sig, lsig, st_c, st_s, st_w, st_f = _input_projection(
        x2d, lw["w_main"], lw["w_small"], lw["b_small"], rc, rs, tm)
    kc_all = _compress_prompt(p_main, batch, seq, lw["cw"])
    o_nsa = _nsa_prompt(p_main, q_rot, kc_all, sig, batch, seq)
    decay = _decay_prompt(lsig, batch, seq)
    o_fox = _fox_prompt(p_main, decay, batch, seq)
    mlen = mem.shape[1]
    mem_kv = _matmul(mem.reshape(batch * mlen, d), lw["w_mem_kv"], min(256, batch * mlen), 512)
    o_mem = _mem_attention(p_main, mem_kv, batch, seq, min(512, seq))
    h = _out_projection(o_nsa, o_fox, o_mem, x2d, lw["w_out"], lw["ln1_g"], lw["ln1_b"],
                        min(256, seq), alpha)
    y, tail = _ffn_prompt(h, lw["fw"], seq, tm, 512, alpha)
    kv_state = lambda st: st.reshape(batch, seq, 2, -1, HEAD_DIM)
    n_win = min(WINDOW, seq)
    tiles = seq // tm
    conv_state = tail.reshape(batch, tiles, SUBLANE, -1)[:, -1, SUBLANE - (CONV_W - 1):]
    states = (kv_state(st_c), kv_state(st_s), kv_state(st_f),
              lsig[:, :FOX_HEADS].reshape(batch, seq, FOX_HEADS),
              kv_state(st_w)[:, seq - n_win:],
              mem_kv.reshape(batch, mlen, 2, MEM_HEADS, HEAD_DIM), conv_state)
    return y.reshape(batch, seq, d), states


def _sample_layer(x, c_cmp, c_slc, c_fox, c_logf, c_swa, c_mem, s_conv, page_table, lw, alpha):
    dec_batch, s_len, d = x.shape
    n_pool, page = c_cmp.shape[0], c_cmp.shape[1]
    n_pages = page_table.shape[1]
    past = n_pages * page
    t_all = past + s_len
    assert s_len <= SROW and n_pages % PAGES_PER_STEP == 0
    assert (t_all // CMP_STRIDE) * CMP_STRIDE == past and past % SLC_BLOCK == 0
    n_blocks = -(-t_all // SLC_BLOCK)
    xp = jnp.pad(x, ((0, 0), (0, SROW - s_len), (0, 0))).reshape(dec_batch * SROW, d)
    pos = past + jnp.tile(jnp.arange(SROW, dtype=jnp.int32), dec_batch)
    rc, rs = _rope_tables(pos)
    p_s, qrot_s, sig_s, lsig_s = _input_projection(xp, lw["w_main"], lw["w_small"], lw["b_small"],
                                                   rc, rs, dec_batch * SROW)[:4]
    kvw = 2 * NSA_KV_HEADS * HEAD_DIM
    as_rows = lambda a: a.reshape(a.shape[0], -1, HEAD_DIM)
    kc_all = _compress_paged(as_rows(c_cmp), page_table, lw["cw"])
    per_group = 3 * NSA_REP
    sig12 = jnp.concatenate([sig_s[:, :per_group], sig_s[:, LANE:LANE + per_group],
                             jnp.zeros((dec_batch * SROW, LANE - 2 * per_group), F32)], 1)
    part, sel = _nsa_sample_a(p_s, qrot_s, kc_all, as_rows(c_swa), sig_s, past, n_blocks)
    o_nsa = _nsa_sample_b(as_rows(c_slc), page_table, p_s, qrot_s, sel, sig12, part, past, n_blocks)
    logf_t = jnp.transpose(c_logf, (2, 0, 1))
    o_fox = _fox_sample(c_fox, logf_t, page_table, p_s, lsig_s, s_len)
    o_mem = _mem_attention_rows(p_s, as_rows(c_mem))
    h = _out_projection(o_nsa, o_fox, o_mem, xp, lw["w_out"], lw["ln1_g"], lw["ln1_b"],
                        dec_batch * SROW, alpha)
    dff = s_conv.shape[-1]
    zrow = jnp.zeros((dec_batch, 1, dff), F32)
    fill1 = jnp.concatenate([s_conv[:, 1:2]] + [zrow] * (SROW - 1), 1).reshape(dec_batch * SROW, dff)
    fill2 = jnp.concatenate([s_conv[:, 0:1], s_conv[:, 1:2]] + [zrow] * (SROW - 2), 1)
    y, a = _ffn_sample(h, lw["fw"], fill1, fill2.reshape(dec_batch * SROW, dff), 512, alpha)

    def rows(arr):
        return arr.reshape(dec_batch, SROW, -1)[:, :s_len]
    cols = lambda c0, w: rows(p_s[:, c0 * LANE:c0 * LANE + w]).reshape(dec_batch, s_len, 2, -1, HEAD_DIM)
    new_kv_swa = cols(COL_KVW, kvw)
    new_swa = jnp.concatenate([c_swa, new_kv_swa], 1)[:, s_len:]
    conv_state = jnp.concatenate([s_conv, rows(a)], 1)[:, -(CONV_W - 1):]
    states = (cols(COL_KVC, kvw), cols(COL_KVS, kvw),
              cols(COL_FOX + FOX_HEADS, 2 * FOX_HEADS * HEAD_DIM),
              rows(lsig_s[:, :FOX_HEADS]), new_swa, conv_state)
    return rows(y), states


def kernel(x_prompt, x_sample, mem_prompt, cache_nsa_cmp, cache_nsa_slc, cache_fox_kv, cache_fox_logf, cache_nsa_swa, cache_mem, state_conv, page_table, w_in, b_gate, b_forget, cmp_w1, cmp_b1, cmp_w2, cmp_pos, w_mem_kv, w_out, ln1_g, ln1_b, w_up, conv_w, conv_b, w_down, ln2_g, ln2_b):
    depth = w_in.shape[0]
    alpha = float((2 * depth) ** 0.25)
    yp, ys = x_prompt, x_sample
    acc_p = [[] for _ in range(7)]
    acc_s = [[] for _ in range(6)]
    for l in range(depth):
        lw = _layer_weights(w_in[l], b_gate[l], b_forget[l], cmp_w1[l], cmp_b1[l], cmp_w2[l],
                            cmp_pos[l], w_mem_kv[l], w_out[l], ln1_g[l], ln1_b[l], w_up[l],
                            conv_w[l], conv_b[l], w_down[l], ln2_g[l], ln2_b[l])
        yp, st_p = _prompt_layer(yp, mem_prompt, lw, alpha)
        ys, st_s = _sample_layer(ys, cache_nsa_cmp[l], cache_nsa_slc[l], cache_fox_kv[l],
                                 cache_fox_logf[l], cache_nsa_swa[l], cache_mem[l], state_conv[l],
                                 page_table, lw, alpha)
        for lst, a in zip(acc_p, st_p):
            lst.append(a)
        for lst, a in zip(acc_s, st_s):
            lst.append(a)
    outs_p = [jnp.stack(a, 0) for a in acc_p]
    outs_s = [jnp.stack(a, 0) for a in acc_s]
    return (yp, ys, *outs_p, *outs_s)
```

```python
import functools

import jax
import jax.numpy as jnp
import numpy as np
from jax import lax
from jax.experimental import pallas as pl
from jax.experimental.pallas import tpu as pltpu

HEAD_DIM = 128
NSA_KV_HEADS = 2
NSA_REP = 4
NSA_HEADS = NSA_KV_HEADS * NSA_REP
FOX_HEADS = 4
MEM_HEADS = 4
CMP_BLOCK = 32
CMP_STRIDE = 16
CMP_HIDDEN = 256
SLC_BLOCK = 64
SLC_TOPK = 16
WINDOW = 512
Q_BLOCK = 128
ROT_DIM = HEAD_DIM // 4
ROPE_THETA = 500000.0
CONV_W = 3
LN_EPS = 1e-5
ATTN_SCALE = HEAD_DIM ** -0.5
LOG2E = 1.4426950408889634
NEG_INF = -1e30
FORCE_BONUS = 1e3

LANE = 128
SUBLANE = 8
SROW = SUBLANE
PAGES_PER_STEP = 16
CMP_PAGES_PER_STEP = 32
VMEM_LIMIT = 56 * 1024 * 1024

COL_Q = 0
COL_KVC = 8
COL_KVS = 12
COL_KVW = 16
COL_FOX = 20
COL_QM = 32
MAIN_W = 36 * LANE
IN_TILE = 512

BF = jnp.bfloat16
F32 = jnp.float32


def _dot(a, b):
    return jnp.dot(a.astype(BF), b.astype(BF), preferred_element_type=F32)


def _dot_nt(a, b):
    return lax.dot_general(a.astype(BF), b.astype(BF), (((1,), (1,)), ((), ())),
                           preferred_element_type=F32)


def _dot_exact01(x, m01, m01_first=False):
    hi = x.astype(BF)
    r1 = x - hi.astype(F32)
    mid = r1.astype(BF)
    lo = (r1 - mid.astype(F32)).astype(BF)
    m = m01.astype(BF)
    if m01_first:
        d = lambda a: lax.dot_general(m, a, (((1,), (1,)), ((), ())), preferred_element_type=F32)
    else:
        d = lambda a: jnp.dot(a, m, preferred_element_type=F32)
    return d(hi) + d(mid) + d(lo)


def _cparams(sem):
    return pltpu.CompilerParams(dimension_semantics=sem, vmem_limit_bytes=VMEM_LIMIT)


def _masked_softmax(s, mask):
    s = jnp.where(mask, s, NEG_INF)
    m = jnp.max(s, -1, keepdims=True)
    e = jnp.where(mask, jnp.exp(s - m), 0.0)
    return e / jnp.maximum(jnp.sum(e, -1, keepdims=True), 1e-30)


def _flash_update(carry, s2, v):
    m, l, acc = carry
    m_new = jnp.maximum(m, jnp.max(s2, -1, keepdims=True))
    p = jnp.exp2(s2 - m_new)
    alpha = jnp.exp2(m - m_new)
    return (m_new, alpha * l + jnp.sum(p, -1, keepdims=True),
            alpha * acc + jnp.dot(p.astype(BF), v, preferred_element_type=F32))


def _rope_tile(x, c, s):
    lane = lax.broadcasted_iota(jnp.int32, x.shape, 1)
    half = ROT_DIM // 2
    swapped = jnp.where(lane < half, pltpu.roll(x, LANE - half, 1), pltpu.roll(x, half, 1))
    return x * c + swapped * s


def _topk_rank(score, n_valid_cols):
    col = lax.broadcasted_iota(jnp.int32, score.shape, 1)
    rank = jnp.zeros(score.shape, F32)
    for i in range(n_valid_cols):
        ci = score[:, i:i + 1]
        rank = rank + jnp.where(col > i, jnp.where(ci >= score, 1.0, 0.0),
                                jnp.where(ci > score, 1.0, 0.0))
    return rank


def _topk_select_rows(score_t, n_sel):
    n, width = score_t.shape
    rank = jnp.zeros(score_t.shape, F32)
    for i in range(n):
        ci = jnp.broadcast_to(score_t[i:i + 1, :], (SUBLANE, width))
        parts = []
        for j0 in range(0, n, SUBLANE):
            sj = score_t[j0:j0 + SUBLANE, :]
            if j0 > i:
                ahead = ci >= sj
            elif j0 + SUBLANE - 1 <= i:
                ahead = ci > sj
            else:
                row = lax.broadcasted_iota(jnp.int32, (SUBLANE, width), 0) + j0
                ahead = jnp.where(row > i, jnp.where(ci >= sj, 1.0, 0.0),
                                  jnp.where(ci > sj, 1.0, 0.0)) > 0.5
            parts.append(jnp.where(ahead, 1.0, 0.0))
        rank = rank + jnp.concatenate(parts, 0)
    return jnp.where(rank < n_sel, 1.0, 0.0)


def _inproj_kernel(x_ref, w_ref, ws_ref, bs_ref, c_ref, s_ref,
                   p_ref, qrot_ref, sig_ref, lsig_ref, stc_ref, sts_ref, stw_ref, stf_ref, xb_ref):
    j = pl.program_id(1)
    tm = x_ref.shape[0]

    def store_rows(ref, tile, first_slot, slots):
        for k in range(IN_TILE // LANE):
            ref[pl.ds(first_slot + k, tm, stride=slots), :] = tile[k]

    @pl.when(j == 0)
    def _():
        xb_ref[...] = x_ref[...].astype(BF)
        z = jnp.dot(xb_ref[...], ws_ref[...], preferred_element_type=F32) + bs_ref[...]
        sig_ref[...] = jax.nn.sigmoid(z[:, :2 * LANE])
        zf = z[:, 2 * LANE:]
        lsig_ref[...] = jnp.minimum(zf, 0.0) - jnp.log1p(jnp.exp(-jnp.abs(zf)))

    acc = jnp.dot(xb_ref[...], w_ref[...], preferred_element_type=F32)
    c = c_ref[...]
    s = s_ref[...]
    heads = IN_TILE // LANE

    @pl.when(j < 2)
    def _():
        p_ref[...] = acc
        for h in range(heads):
            qrot_ref[:, h * LANE:(h + 1) * LANE] = _rope_tile(acc[:, h * LANE:(h + 1) * LANE], c, s)

    lane_tiles = lambda a: [a[:, h * LANE:(h + 1) * LANE] for h in range(heads)]

    def roped_kv(st_ref):
        tiles = [_rope_tile(t, c, s) if h < NSA_KV_HEADS else t for h, t in enumerate(lane_tiles(acc))]
        for h, t in enumerate(tiles):
            p_ref[:, h * LANE:(h + 1) * LANE] = t
        store_rows(st_ref, tiles, 0, N_KV_SLOTS)

    @pl.when(j == COL_KVC // heads)
    def _():
        p_ref[...] = acc
        store_rows(stc_ref, lane_tiles(acc), 0, N_KV_SLOTS)

    @pl.when(j == COL_KVS // heads)
    def _():
        roped_kv(sts_ref)

    @pl.when(j == COL_KVW // heads)
    def _():
        roped_kv(stw_ref)

    fox_k = COL_FOX // heads + 1
    @pl.when((j == fox_k) | (j == fox_k + 1))
    def _():
        p_ref[...] = acc

    @pl.when(j == fox_k)
    def _():
        store_rows(stf_ref, lane_tiles(acc), 0, 2 * FOX_HEADS)

    @pl.when(j == fox_k + 1)
    def _():
        store_rows(stf_ref, lane_tiles(acc), FOX_HEADS, 2 * FOX_HEADS)

    @pl.when((j == COL_FOX // heads) | (j == COL_QM // heads))
    def _():
        p_ref[...] = acc


def _input_projection(x2d, w_main, w_small, b_small, rope_c, rope_s, tm):
    m, d = x2d.shape
    nj = MAIN_W // IN_TILE
    state_slots = (N_KV_SLOTS, N_KV_SLOTS, N_KV_SLOTS, 2 * FOX_HEADS)
    return pl.pallas_call(
        _inproj_kernel,
        grid=(m // tm, nj),
        in_specs=[
            pl.BlockSpec((tm, d), lambda i, j: (i, 0)),
            pl.BlockSpec((d, IN_TILE), lambda i, j: (0, j)),
            pl.BlockSpec((d, 3 * LANE), lambda i, j: (0, 0)),
            pl.BlockSpec((1, 3 * LANE), lambda i, j: (0, 0)),
            pl.BlockSpec((tm, LANE), lambda i, j: (i, 0)),
            pl.BlockSpec((tm, LANE), lambda i, j: (i, 0)),
        ],
        out_specs=[
            pl.BlockSpec((tm, IN_TILE), lambda i, j: (i, j)),
            pl.BlockSpec((tm, IN_TILE), lambda i, j: (i, jnp.minimum(j, 1))),
            pl.BlockSpec((tm, 2 * LANE), lambda i, j: (i, 0)),
            pl.BlockSpec((tm, LANE), lambda i, j: (i, 0)),
        ] + [pl.BlockSpec((tm * slots, LANE), lambda i, j: (i, 0)) for slots in state_slots],
        out_shape=[
            jax.ShapeDtypeStruct((m, MAIN_W), F32),
            jax.ShapeDtypeStruct((m, NSA_HEADS * HEAD_DIM), F32),
            jax.ShapeDtypeStruct((m, 2 * LANE), F32),
            jax.ShapeDtypeStruct((m, LANE), F32),
        ] + [jax.ShapeDtypeStruct((m * slots, LANE), F32) for slots in state_slots],
        scratch_shapes=[pltpu.VMEM((tm, d), BF)],
        compiler_params=_cparams(("arbitrary", "arbitrary")),
        name="input_projection",
    )(x2d, w_main, w_small, b_small, rope_c, rope_s)


def _gelu_tanh(x):
    k = np.sqrt(2.0 / np.pi).astype(np.float32)
    return x * (0.5 * (1.0 + jnp.tanh(k * (x + 0.044715 * (x ** 3)))))


def _compress_body(load, nb, w1_ref, pe_ref, b1_ref, w2_ref, out_ref, carry_ref, first_step):
    @pl.when(first_step)
    def _():
        carry_ref[...] = jnp.zeros(carry_ref.shape, F32)

    row = lax.broadcasted_iota(jnp.int32, (nb, CMP_HIDDEN), 0)
    for kv in range(2):
        xs = []
        for g in range(NSA_KV_HEADS):
            per_l = [load(kv * NSA_KV_HEADS + g, l) for l in range(CMP_STRIDE)]
            xs.append(jnp.concatenate(per_l, 1).astype(BF))
        xs.append(pe_ref[kv].astype(BF))
        fs = jnp.dot(jnp.concatenate(xs, 0), w1_ref[kv], preferred_element_type=F32)
        pos = (fs[2 * nb:2 * nb + 1, :CMP_HIDDEN] + fs[2 * nb + 1:2 * nb + 2, CMP_HIDDEN:]
               + b1_ref[kv])
        for g in range(NSA_KV_HEADS):
            col = (kv * NSA_KV_HEADS + g) * HEAD_DIM
            first = fs[g * nb:(g + 1) * nb, :CMP_HIDDEN]
            second = fs[g * nb:(g + 1) * nb, CMP_HIDDEN:]
            slot = kv * NSA_KV_HEADS + g
            prev = jnp.where(row == 0, carry_ref[slot], pltpu.roll(first, 1, 0))
            carry_ref[slot] = first[nb - 1:nb, :]
            h = _gelu_tanh(prev + second + pos)
            out_ref[:, col:col + HEAD_DIM] = jnp.dot(h.astype(BF), w2_ref[kv],
                                                     preferred_element_type=F32)


N_KV_SLOTS = 2 * NSA_KV_HEADS


def _compress_prompt_kernel(*refs):
    slots = refs[:N_KV_SLOTS]
    w1_ref, pe_ref, b1_ref, w2_ref, out_ref, carry_ref = refs[N_KV_SLOTS:]
    nb = slots[0].shape[0] // CMP_STRIDE

    def load(slot, l):
        return slots[slot][pl.ds(l, nb, stride=CMP_STRIDE), :]

    _compress_body(load, nb, w1_ref, pe_ref, b1_ref, w2_ref, out_ref, carry_ref,
                   pl.program_id(1) == 0)


def _compress_paged_kernel(pt_ref, *refs):
    del pt_ref
    pages = refs[:CMP_PAGES_PER_STEP]
    w1_ref, pe_ref, b1_ref, w2_ref, out_ref, carry_ref = refs[CMP_PAGES_PER_STEP:]
    group = N_KV_SLOTS * CMP_STRIDE
    per_page = pages[0].shape[0] // group
    blocks = jnp.concatenate([p[...].reshape(per_page, group, HEAD_DIM) for p in pages], 0)
    by_row = jnp.swapaxes(blocks, 0, 1)

    def load(slot, l):
        return by_row[l * N_KV_SLOTS + slot]

    _compress_body(load, per_page * CMP_PAGES_PER_STEP, w1_ref, pe_ref, b1_ref, w2_ref, out_ref,
                   carry_ref, pl.program_id(1) == 0)


def _compress_weight_specs():
    const3 = (lambda *a: (0, 0, 0))
    return [
        pl.BlockSpec((2, CMP_STRIDE * HEAD_DIM, 2 * CMP_HIDDEN), const3),
        pl.BlockSpec((2, SUBLANE, CMP_STRIDE * HEAD_DIM), const3),
        pl.BlockSpec((2, 1, CMP_HIDDEN), const3),
        pl.BlockSpec((2, CMP_HIDDEN, HEAD_DIM), const3),
    ]


def _compress_prompt(p_main, batch, seq, cw):
    chunk = min(seq, 2048)
    nchunk = seq // chunk
    kvw = 2 * NSA_KV_HEADS * HEAD_DIM
    return pl.pallas_call(
        _compress_prompt_kernel,
        grid=(batch, nchunk),
        in_specs=[pl.BlockSpec((chunk, HEAD_DIM), (lambda b, c, s=s: (b * nchunk + c, COL_KVC + s)))
                  for s in range(N_KV_SLOTS)] + _compress_weight_specs(),
        out_specs=pl.BlockSpec((None, chunk // CMP_STRIDE, kvw), lambda b, c: (b, c, 0)),
        out_shape=jax.ShapeDtypeStruct((batch, seq // CMP_STRIDE, kvw), F32),
        scratch_shapes=[pltpu.VMEM((2 * NSA_KV_HEADS, 1, CMP_HIDDEN), F32)],
        compiler_params=_cparams(("arbitrary", "arbitrary")),
        name="nsa_compress_prompt",
    )(*([p_main] * N_KV_SLOTS), *cw)


def _page_specs(page_rows, width, chunk_of, per_step=PAGES_PER_STEP):
    def spec(i):
        return pl.BlockSpec((None, page_rows, width),
                            lambda b, c, pt: (pt[b, chunk_of(c) * per_step + i], 0, 0))
    return [spec(i) for i in range(per_step)]


def _compress_paged(cache_rows, page_table, cw):
    n_pool, page_rows, _ = cache_rows.shape
    page = page_rows // N_KV_SLOTS
    kvw = N_KV_SLOTS * HEAD_DIM
    dec_batch, n_pages = page_table.shape
    nchunk = n_pages // CMP_PAGES_PER_STEP
    rows = CMP_PAGES_PER_STEP * page // CMP_STRIDE
    return pl.pallas_call(
        _compress_paged_kernel,
        grid_spec=pltpu.PrefetchScalarGridSpec(
            num_scalar_prefetch=1,
            grid=(dec_batch, nchunk),
            in_specs=(_page_specs(page_rows, HEAD_DIM, lambda c: c, CMP_PAGES_PER_STEP)
                      + _compress_weight_specs()),
            out_specs=pl.BlockSpec((None, rows, kvw), lambda b, c, pt: (b, c, 0)),
            scratch_shapes=[pltpu.VMEM((2 * NSA_KV_HEADS, 1, CMP_HIDDEN), F32)],
        ),
        out_shape=jax.ShapeDtypeStruct((dec_batch, nchunk * rows, kvw), F32),
        compiler_params=_cparams(("arbitrary", "arbitrary")),
        name="nsa_compress_paged",
    )(page_table, *([cache_rows] * CMP_PAGES_PER_STEP), *cw)


def _overlap_matrix(n_rows, n_blocks, blocks_first=False):
    shape = (n_blocks, n_rows) if blocks_first else (n_rows, n_blocks)
    n = lax.broadcasted_iota(jnp.int32, shape, 1 if blocks_first else 0)
    s = lax.broadcasted_iota(jnp.int32, shape, 0 if blocks_first else 1)
    c0 = (n - 1) * CMP_STRIDE
    hit = (n >= 1) & (c0 < s * SLC_BLOCK + SLC_BLOCK) & (c0 + CMP_BLOCK > s * SLC_BLOCK)
    return jnp.where(hit, 1.0, 0.0)


def _block_scores(imp, q_pos, n_blocks, block_axis=1):
    blk = lax.broadcasted_iota(jnp.int32, imp.shape, block_axis)
    cur = q_pos // SLC_BLOCK
    forced = (blk == 0) | (blk == cur) | (blk == cur - 1)
    valid = (blk * SLC_BLOCK <= q_pos) & (blk < n_blocks)
    return jnp.where(valid, imp + jnp.where(forced, FORCE_BONUS, 0.0), NEG_INF)


def _head_rows(tile_ref, h):
    tokens, heads, width = tile_ref.shape
    return tile_ref.reshape(tokens * heads, width)[pl.ds(h, tokens, stride=heads), :]


def _stack_heads(ref, n):
    return jnp.concatenate([ref[:, r * HEAD_DIM:(r + 1) * HEAD_DIM] for r in range(n)], 0)


def _nsa_prompt_kernel(qraw_ref, qrot_ref, kc_ref, vc_ref, ks_ref, vs_ref, kw_ref, vw_ref,
                       gate_ref, o_ref, ksb_ref, vsb_ref, kwb_ref, vwb_ref, *, seq, kchunk):
    i = pl.program_id(2)
    q0 = i * Q_BLOCK
    nc = kc_ref.shape[0]
    ns = seq // SLC_BLOCK
    rep = NSA_REP
    t_col = lax.broadcasted_iota(jnp.int32, (Q_BLOCK, 1), 0) + q0
    over_heads = lambda x: jnp.concatenate([x] * rep, 0)
    nt = (((1,), (1,)), ((), ()))

    @pl.when(i == 0)
    def _():
        ksb_ref[...] = ks_ref[...].astype(BF)
        vsb_ref[...] = vs_ref[...].astype(BF)
        kwb_ref[...] = kw_ref[...].astype(BF)
        vwb_ref[...] = vw_ref[...].astype(BF)

    q_raw = (_stack_heads(qraw_ref, rep) * (ATTN_SCALE * LOG2E)).astype(BF)
    n_row = lax.broadcasted_iota(jnp.int32, (1, nc), 1)
    vis = (n_row >= 1) & (n_row * CMP_STRIDE + (CMP_BLOCK - CMP_STRIDE - 1) <= t_col)
    s_c = (lax.dot_general(q_raw, kc_ref[...].astype(BF), nt, preferred_element_type=F32)
           + over_heads(jnp.where(vis, 0.0, NEG_INF)))
    e_c = jnp.exp2(s_c - jnp.max(s_c, -1, keepdims=True)) * over_heads(jnp.where(vis, 1.0, 0.0))
    p_c = e_c / jnp.maximum(jnp.sum(e_c, -1, keepdims=True), 1e-30)
    o_cmp = _dot(p_c, vc_ref[...])
    p_sum = p_c[0:Q_BLOCK]
    for r in range(1, rep):
        p_sum = p_sum + p_c[r * Q_BLOCK:(r + 1) * Q_BLOCK]
    imp_t = _dot_exact01(p_sum, _overlap_matrix(nc, ns, blocks_first=True), m01_first=True)
    t_row = lax.broadcasted_iota(jnp.int32, (1, Q_BLOCK), 1) + q0
    sel_t = _topk_select_rows(_block_scores(imp_t, t_row, ns, block_axis=0), min(SLC_TOPK, ns))
    sel_b = sel_t.T.astype(BF)

    q_rot = (_stack_heads(qrot_ref, rep) * (ATTN_SCALE * LOG2E)).astype(BF)
    per_chunk = kchunk // SLC_BLOCK
    blk = lax.broadcasted_iota(jnp.int32, (ns, per_chunk), 0)
    slot = lax.broadcasted_iota(jnp.int32, (ns, per_chunk), 1)
    b_row = lax.broadcasted_iota(jnp.int32, (per_chunk, kchunk), 0)
    b_lane = lax.broadcasted_iota(jnp.int32, (per_chunk, kchunk), 1)
    expand = jnp.where(b_row == b_lane // SLC_BLOCK, 1.0, 0.0).astype(BF)
    k_lane = lax.broadcasted_iota(jnp.int32, (1, kchunk), 1)

    def slc_chunk(c, carry, diagonal):
        k0 = pl.multiple_of(c * kchunk, kchunk)
        pick = jnp.where(blk == c * per_chunk + slot, 1.0, 0.0).astype(BF)
        sel_c = jnp.dot(sel_b, pick, preferred_element_type=F32).astype(BF)
        keep = jnp.dot(sel_c, expand, preferred_element_type=F32) > 0.5
        if diagonal:
            keep = keep & (k0 + k_lane <= t_col)
        s2 = (lax.dot_general(q_rot, ksb_ref[pl.ds(k0, kchunk), :], nt, preferred_element_type=F32)
              + over_heads(jnp.where(keep, 0.0, NEG_INF)))
        return _flash_update(carry, s2, vsb_ref[pl.ds(k0, kchunk), :])

    init = (jnp.full((rep * Q_BLOCK, 1), NEG_INF, F32), jnp.zeros((rep * Q_BLOCK, 1), F32),
            jnp.zeros((rep * Q_BLOCK, HEAD_DIM), F32))
    last = q0 // kchunk
    carry = lax.fori_loop(0, last, lambda c, carry: slc_chunk(c, carry, False), init)
    _, l_s, acc_s = slc_chunk(last, carry, True)
    o_slc = acc_s / jnp.maximum(l_s, 1e-30)

    wk = WINDOW + Q_BLOCK
    w0 = pl.multiple_of(jnp.maximum(q0 - WINDOW, 0), Q_BLOCK)
    dist = t_col - (w0 + lax.broadcasted_iota(jnp.int32, (1, wk), 1))
    s_w = (lax.dot_general(q_rot, kwb_ref[pl.ds(w0, wk), :], nt, preferred_element_type=F32)
           + over_heads(jnp.where((dist >= 0) & (dist < WINDOW), 0.0, NEG_INF)))
    e_w = jnp.exp2(s_w - jnp.max(s_w, -1, keepdims=True))
    p_w = e_w / jnp.sum(e_w, -1, keepdims=True)
    o_swa = jnp.dot(p_w.astype(BF), vwb_ref[pl.ds(w0, wk), :], preferred_element_type=F32)

    gt = gate_ref[...]
    for r in range(rep):
        rows = slice(r * Q_BLOCK, (r + 1) * Q_BLOCK)
        o_ref[:, r * HEAD_DIM:(r + 1) * HEAD_DIM] = (
            gt[:, 3 * r:3 * r + 1] * o_cmp[rows] + gt[:, 3 * r + 1:3 * r + 2] * o_slc[rows]
            + gt[:, 3 * r + 2:3 * r + 3] * o_swa[rows])


def _nsa_prompt(p_main, q_rot, kc_all, sig, batch, seq):
    nq = seq // Q_BLOCK
    gw = NSA_REP * HEAD_DIM
    kchunk = min(512, seq)
    col = lambda base, kv: (lambda b, g, i: (b, base + kv * NSA_KV_HEADS + g))
    return pl.pallas_call(
        functools.partial(_nsa_prompt_kernel, seq=seq, kchunk=kchunk),
        grid=(batch, NSA_KV_HEADS, nq),
        in_specs=[
            pl.BlockSpec((Q_BLOCK, gw), lambda b, g, i: (b * nq + i, g)),
            pl.BlockSpec((Q_BLOCK, gw), lambda b, g, i: (b * nq + i, g)),
            pl.BlockSpec((None, seq // CMP_STRIDE, HEAD_DIM), lambda b, g, i: (b, 0, g)),
            pl.BlockSpec((None, seq // CMP_STRIDE, HEAD_DIM), lambda b, g, i: (b, 0, NSA_KV_HEADS + g)),
            pl.BlockSpec((seq, HEAD_DIM), col(COL_KVS, 0)),
            pl.BlockSpec((seq, HEAD_DIM), col(COL_KVS, 1)),
            pl.BlockSpec((seq, HEAD_DIM), col(COL_KVW, 0)),
            pl.BlockSpec((seq, HEAD_DIM), col(COL_KVW, 1)),
            pl.BlockSpec((Q_BLOCK, LANE), lambda b, g, i: (b * nq + i, g)),
        ],
        out_specs=pl.BlockSpec((Q_BLOCK, gw), lambda b, g, i: (b * nq + i, g)),
        out_shape=jax.ShapeDtypeStruct((batch * seq, NSA_HEADS * HEAD_DIM), F32),
        scratch_shapes=[pltpu.VMEM((seq, HEAD_DIM), BF)] * 4,
        compiler_params=_cparams(("arbitrary", "arbitrary", "arbitrary")),
        name="nsa_attention_prompt",
    )(p_main, q_rot, kc_all, kc_all, p_main, p_main, p_main, p_main, sig)


def _suffix_sum_lanes(x):
    n = x.shape[-1]
    lane = lax.broadcasted_iota(jnp.int32, x.shape, x.ndim - 1)
    k = 1
    while k < n:
        x = x + jnp.where(lane < n - k, pltpu.roll(x, n - k, x.ndim - 1), 0.0)
        k *= 2
    return x


def _decay_prompt_kernel(lf_ref, d_ref):
    lt = lf_ref[...].T
    top = lt[:SUBLANE]
    d_ref[...] = (_suffix_sum_lanes(top) - top) * LOG2E


def _decay_prompt(lsig, batch, seq):
    return pl.pallas_call(
        _decay_prompt_kernel,
        grid=(batch,),
        in_specs=[pl.BlockSpec((seq, LANE), lambda b: (b, 0))],
        out_specs=pl.BlockSpec((None, SUBLANE, seq), lambda b: (b, 0, 0)),
        out_shape=jax.ShapeDtypeStruct((batch, SUBLANE, seq), F32),
        compiler_params=_cparams(("arbitrary",)),
        name="fox_decay_prompt",
    )(lsig)


def _fox_prompt_kernel(q_ref, k_ref, v_ref, d_ref, o_ref, kb_ref, vb_ref, m_ref, l_ref, acc_ref,
                       *, tile):
    h = pl.program_id(1)
    i = pl.program_id(2)

    @pl.when(i == 0)
    def _():
        kb_ref[...] = k_ref[...].astype(BF)
        vb_ref[...] = v_ref[...].astype(BF)

    q = (q_ref[...] * (ATTN_SCALE * LOG2E)).astype(BF)
    row = lax.broadcasted_iota(jnp.int32, (tile, tile), 0)
    col = lax.broadcasted_iota(jnp.int32, (tile, tile), 1)

    def chunk(c, carry, diagonal):
        k0 = pl.multiple_of(c * tile, tile)
        s2 = lax.dot_general(q, kb_ref[pl.ds(k0, tile), :], (((1,), (1,)), ((), ())),
                             preferred_element_type=F32) + d_ref[pl.ds(h, 1), pl.ds(k0, tile)]
        if diagonal:
            s2 = jnp.where(col <= row, s2, NEG_INF)
        return _flash_update(carry, s2, vb_ref[pl.ds(k0, tile), :])

    init = (jnp.full((tile, 1), NEG_INF, F32), jnp.zeros((tile, 1), F32),
            jnp.zeros((tile, HEAD_DIM), F32))
    carry = lax.fori_loop(0, i, lambda c, carry: chunk(c, carry, False), init)
    _, l, acc = chunk(i, carry, True)
    o_ref[...] = acc / jnp.maximum(l, 1e-30)


def _fox_prompt(p_main, decay, batch, seq):
    tq = min(512, seq)
    nq = seq // tq
    return pl.pallas_call(
        functools.partial(_fox_prompt_kernel, tile=tq),
        grid=(batch, FOX_HEADS, nq),
        scratch_shapes=[pltpu.VMEM((seq, HEAD_DIM), BF), pltpu.VMEM((seq, HEAD_DIM), BF),
                        pltpu.VMEM((tq, 1), F32), pltpu.VMEM((tq, 1), F32),
                        pltpu.VMEM((tq, HEAD_DIM), F32)],
        in_specs=[
            pl.BlockSpec((tq, HEAD_DIM), lambda b, h, i: (b * nq + i, COL_FOX + h)),
            pl.BlockSpec((seq, HEAD_DIM), lambda b, h, i: (b, COL_FOX + FOX_HEADS + h)),
            pl.BlockSpec((seq, HEAD_DIM), lambda b, h, i: (b, COL_FOX + 2 * FOX_HEADS + h)),
            pl.BlockSpec((None, SUBLANE, seq), lambda b, h, i: (b, 0, 0)),
        ],
        out_specs=pl.BlockSpec((tq, HEAD_DIM), lambda b, h, i: (b * nq + i, h)),
        out_shape=jax.ShapeDtypeStruct((batch * seq, FOX_HEADS * HEAD_DIM), F32),
        compiler_params=_cparams(("arbitrary", "arbitrary", "arbitrary")),
        name="fox_attention_prompt",
    )(p_main, p_main, p_main, decay)


def _mem_attn_kernel(q_ref, k_ref, v_ref, o_ref):
    s = _dot_nt(q_ref[...], k_ref[...]) * ATTN_SCALE
    m = jnp.max(s, -1, keepdims=True)
    e = jnp.exp(s - m)
    p = e / jnp.sum(e, -1, keepdims=True)
    o_ref[...] = _dot(p, v_ref[...])


def _mem_attention(p_main, mem_kv2d, batch, rows_per_batch, tq):
    nq = rows_per_batch // tq
    mlen = mem_kv2d.shape[0] // batch
    return pl.pallas_call(
        _mem_attn_kernel,
        grid=(batch, MEM_HEADS, nq),
        in_specs=[
            pl.BlockSpec((tq, HEAD_DIM), lambda b, h, i: (b * nq + i, COL_QM + h)),
            pl.BlockSpec((mlen, HEAD_DIM), lambda b, h, i: (b, h)),
            pl.BlockSpec((mlen, HEAD_DIM), lambda b, h, i: (b, MEM_HEADS + h)),
        ],
        out_specs=pl.BlockSpec((tq, HEAD_DIM), lambda b, h, i: (b * nq + i, h)),
        out_shape=jax.ShapeDtypeStruct((batch * rows_per_batch, MEM_HEADS * HEAD_DIM), F32),
        compiler_params=_cparams(("arbitrary", "arbitrary", "arbitrary")),
        name="mem_attention",
    )(p_main, mem_kv2d, mem_kv2d)


def _mem_attn_rows_kernel(q_ref, kv_ref, o_ref):
    slots = 2 * MEM_HEADS
    mlen = kv_ref.shape[0] // slots
    scores = [_dot_nt(q_ref[:, h * HEAD_DIM:(h + 1) * HEAD_DIM],
                      kv_ref[pl.ds(h, mlen, stride=slots), :]) for h in range(MEM_HEADS)]
    s = jnp.concatenate(scores, 0) * ATTN_SCALE
    m = jnp.max(s, -1, keepdims=True)
    e = jnp.exp(s - m)
    p = e / jnp.sum(e, -1, keepdims=True)
    for h in range(MEM_HEADS):
        o_ref[:, h * HEAD_DIM:(h + 1) * HEAD_DIM] = _dot(
            p[h * SROW:(h + 1) * SROW], kv_ref[pl.ds(MEM_HEADS + h, mlen, stride=slots), :])


def _mem_attention_rows(p_s, mem_rows):
    dec_batch, rows, _ = mem_rows.shape
    mw = MEM_HEADS * HEAD_DIM
    return pl.pallas_call(
        _mem_attn_rows_kernel,
        grid=(dec_batch,),
        in_specs=[pl.BlockSpec((SROW, mw), lambda b: (b, COL_QM * LANE // mw)),
                  pl.BlockSpec((None, rows, HEAD_DIM), lambda b: (b, 0, 0))],
        out_specs=pl.BlockSpec((SROW, mw), lambda b: (b, 0)),
        out_shape=jax.ShapeDtypeStruct((dec_batch * SROW, mw), F32),
        compiler_params=_cparams(("arbitrary",)),
        name="mem_attention_sample",
    )(p_s, mem_rows)


def _matmul_kernel(x_ref, w_ref, o_ref):
    o_ref[...] = jnp.dot(x_ref[...].astype(BF), w_ref[...], preferred_element_type=F32)


def _matmul(x2d, w_bf, tm, tn):
    m, k = x2d.shape
    n = w_bf.shape[1]
    return pl.pallas_call(
        _matmul_kernel,
        grid=(m // tm, n // tn),
        in_specs=[pl.BlockSpec((tm, k), lambda i, j: (i, 0)),
                  pl.BlockSpec((k, tn), lambda i, j: (0, j))],
        out_specs=pl.BlockSpec((tm, tn), lambda i, j: (i, j)),
        out_shape=jax.ShapeDtypeStruct((m, n), F32),
        compiler_params=_cparams(("arbitrary", "arbitrary")),
        name="projection_matmul",
    )(x2d, w_bf)


def _layer_norm(z, g, b):
    zc = z - jnp.mean(z, -1, keepdims=True)
    var = jnp.mean(zc * zc, -1, keepdims=True)
    return zc * lax.rsqrt(var + LN_EPS) * g + b


def _outproj_kernel(on_ref, of_ref, om_ref, x_ref, w_ref, g_ref, b_ref, h_ref, *, alpha):
    mix = jnp.concatenate([on_ref[...].astype(BF), of_ref[...].astype(BF),
                           om_ref[...].astype(BF)], 1)
    y = jnp.dot(mix, w_ref[...], preferred_element_type=F32)
    h_ref[...] = _layer_norm(alpha * x_ref[...] + y, g_ref[...], b_ref[...])


def _out_projection(o_nsa, o_fox, o_mem, x2d, w_out, g, b, tm, alpha):
    m, d = x2d.shape
    row = lambda w: pl.BlockSpec((tm, w), lambda i: (i, 0))
    const = lambda shape: pl.BlockSpec(shape, lambda i: (0, 0))
    return pl.pallas_call(
        functools.partial(_outproj_kernel, alpha=alpha),
        grid=(m // tm,),
        in_specs=[row(o_nsa.shape[1]), row(o_fox.shape[1]), row(o_mem.shape[1]), row(d),
                  const(w_out.shape), const((1, d)), const((1, d))],
        out_specs=row(d),
        out_shape=jax.ShapeDtypeStruct((m, d), F32),
        compiler_params=_cparams(("arbitrary",)),
        name="out_projection_ln",
    )(o_nsa, o_fox, o_mem, x2d, w_out, g, b)


def _ffn_core(h_ref, wa_ref, wb_ref, cw_ref, cb_ref, wd_ref, g_ref, b_ref, y_ref, acc_ref, hb_ref,
              shifted, alpha):
    j = pl.program_id(1)

    @pl.when(j == 0)
    def _():
        hb_ref[...] = h_ref[...].astype(BF)

    hb = hb_ref[...]
    a = jnp.dot(hb, wa_ref[...], preferred_element_type=F32)
    gate_in = jnp.dot(hb, wb_ref[...], preferred_element_type=F32)
    a1, a2 = shifted(a)
    cw = cw_ref[...]
    c = cb_ref[...] + a2 * cw[0:1] + a1 * cw[1:2] + a * cw[2:3]
    act = (c * jax.nn.sigmoid(c)) * gate_in
    part = jnp.dot(act.astype(BF), wd_ref[...], preferred_element_type=F32)

    @pl.when(j == 0)
    def _():
        acc_ref[...] = part

    @pl.when(j > 0)
    def _():
        acc_ref[...] += part

    @pl.when(j == pl.num_programs(1) - 1)
    def _():
        y_ref[...] = _layer_norm(alpha * h_ref[...] + acc_ref[...], g_ref[...], b_ref[...])
    return a


def _ffn_prompt_kernel(h_ref, wa_ref, wb_ref, cw_ref, cb_ref, wd_ref, g_ref, b_ref,
                       y_ref, tail_ref, acc_ref, hb_ref, halo_ref, *, tiles_per_seq, alpha):
    i = pl.program_id(0)
    j = pl.program_id(1)
    tm = h_ref.shape[0]
    seq_start = (i % tiles_per_seq) == 0

    @pl.when(seq_start)
    def _():
        halo_ref[j] = jnp.zeros(halo_ref.shape[1:], F32)

    def shifted(a):
        row = lax.broadcasted_iota(jnp.int32, a.shape, 0)
        halo = halo_ref[j]
        h1 = halo[SUBLANE - 1:SUBLANE]
        h2 = halo[SUBLANE - 2:SUBLANE - 1]
        a1 = jnp.where(row == 0, h1, pltpu.roll(a, 1, 0))
        a2 = jnp.where(row == 0, h2, jnp.where(row == 1, h1, pltpu.roll(a, 2, 0)))
        return a1, a2

    a = _ffn_core(h_ref, wa_ref, wb_ref, cw_ref, cb_ref, wd_ref, g_ref, b_ref, y_ref, acc_ref,
                  hb_ref, shifted, alpha)
    halo_ref[j] = a[tm - SUBLANE:]
    tail_ref[...] = a[tm - SUBLANE:]


def _ffn_sample_kernel(h_ref, wa_ref, wb_ref, cw_ref, cb_ref, wd_ref, g_ref, b_ref, f1_ref, f2_ref,
                       y_ref, a_ref, acc_ref, hb_ref, *, alpha):
    def shifted(a):
        s = lax.broadcasted_iota(jnp.int32, a.shape, 0) % SROW
        a1 = jnp.where(s >= 1, pltpu.roll(a, 1, 0), 0.0) + f1_ref[...]
        a2 = jnp.where(s >= 2, pltpu.roll(a, 2, 0), 0.0) + f2_ref[...]
        return a1, a2

    a_ref[...] = _ffn_core(h_ref, wa_ref, wb_ref, cw_ref, cb_ref, wd_ref, g_ref, b_ref, y_ref,
                           acc_ref, hb_ref, shifted, alpha)


def _ffn_specs(tm, d, tf, nf):
    return [
        pl.BlockSpec((tm, d), lambda i, j: (i, 0)),
        pl.BlockSpec((d, tf), lambda i, j: (0, j)),
        pl.BlockSpec((d, tf), lambda i, j: (0, nf + j)),
        pl.BlockSpec((SUBLANE, tf), lambda i, j: (0, j)),
        pl.BlockSpec((1, tf), lambda i, j: (0, j)),
        pl.BlockSpec((tf, d), lambda i, j: (j, 0)),
        pl.BlockSpec((1, d), lambda i, j: (0, 0)),
        pl.BlockSpec((1, d), lambda i, j: (0, 0)),
    ]


def _ffn_prompt(h2d, fw, seq, tm, tf, alpha):
    w_up, conv_w8, conv_b, w_down, g, b = fw
    m, d = h2d.shape
    dff = w_down.shape[0]
    nf = dff // tf
    return pl.pallas_call(
        functools.partial(_ffn_prompt_kernel, tiles_per_seq=seq // tm, alpha=alpha),
        grid=(m // tm, nf),
        in_specs=_ffn_specs(tm, d, tf, nf),
        out_specs=[pl.BlockSpec((tm, d), lambda i, j: (i, 0)),
                   pl.BlockSpec((None, SUBLANE, tf), lambda i, j: (i, 0, j))],
        out_shape=[jax.ShapeDtypeStruct((m, d), F32),
                   jax.ShapeDtypeStruct((m // tm, SUBLANE, dff), F32)],
        scratch_shapes=[pltpu.VMEM((tm, d), F32), pltpu.VMEM((tm, d), BF),
                        pltpu.VMEM((nf, SUBLANE, tf), F32)],
        compiler_params=_cparams(("arbitrary", "arbitrary")),
        name="conv_ffn_prompt",
    )(h2d, w_up, w_up, conv_w8, conv_b, w_down, g, b)


def _ffn_sample(h2d, fw, fill1, fill2, tf, alpha):
    w_up, conv_w8, conv_b, w_down, g, b = fw
    m, d = h2d.shape
    dff = w_down.shape[0]
    nf = dff // tf
    return pl.pallas_call(
        functools.partial(_ffn_sample_kernel, alpha=alpha),
        grid=(1, nf),
        in_specs=_ffn_specs(m, d, tf, nf) + [pl.BlockSpec((m, tf), lambda i, j: (0, j)),
                                             pl.BlockSpec((m, tf), lambda i, j: (0, j))],
        out_specs=[pl.BlockSpec((m, d), lambda i, j: (0, 0)),
                   pl.BlockSpec((m, tf), lambda i, j: (0, j))],
        out_shape=[jax.ShapeDtypeStruct((m, d), F32), jax.ShapeDtypeStruct((m, dff), F32)],
        scratch_shapes=[pltpu.VMEM((m, d), F32), pltpu.VMEM((m, d), BF)],
        compiler_params=_cparams(("arbitrary", "arbitrary")),
        name="conv_ffn_sample",
    )(h2d, w_up, w_up, conv_w8, conv_b, w_down, g, b, fill1, fill2)


def _nsa_sample_a_kernel(qraw_ref, qrot_ref, kc_ref, vc_ref, swa_ref, kwn_ref, vwn_ref,
                         gate_ref, part_ref, sel_ref, *, past, n_blocks):
    rep = NSA_REP
    g = pl.program_id(1)
    nc = kc_ref.shape[0]
    nsp = -(-n_blocks // LANE) * LANE
    s_col = lax.broadcasted_iota(jnp.int32, (SROW, 1), 0) + past

    q_raw = _stack_heads(qraw_ref, rep)
    s_c = (_dot_nt(q_raw, kc_ref[...]) * ATTN_SCALE).reshape(rep, SROW, nc)
    n_row = lax.broadcasted_iota(jnp.int32, (1, nc), 1)
    vis = (n_row >= 1) & (n_row * CMP_STRIDE + (CMP_BLOCK - CMP_STRIDE - 1) <= s_col)
    p_c = _masked_softmax(s_c, vis[None])
    o_cmp = _dot(p_c.reshape(rep * SROW, nc), vc_ref[...])
    imp = _dot_exact01(jnp.sum(p_c, 0), _overlap_matrix(nc, nsp))
    score = _block_scores(imp, s_col, n_blocks)
    rank = _topk_rank(score, n_blocks)
    col1 = lax.broadcasted_iota(jnp.int32, score.shape, 1).astype(F32) + 1.0
    lane = lax.broadcasted_iota(jnp.int32, (SROW, LANE), 1)
    ids = jnp.full((SROW, LANE), -1.0, F32)
    for r in range(min(SLC_TOPK, n_blocks)):
        hit = jnp.where(rank == r, jnp.where(score > 0.5 * NEG_INF, col1, 0.0), 0.0)
        ids = jnp.where(lane == r, jnp.sum(hit, -1, keepdims=True) - 1.0, ids)
    sel_ref[...] = ids.astype(jnp.int32)

    q_rot = _stack_heads(qrot_ref, rep)
    wbuf = swa_ref.shape[0] // N_KV_SLOTS
    keys = jnp.concatenate([swa_ref[pl.ds(g, wbuf, stride=N_KV_SLOTS), :], kwn_ref[...]], 0)
    vals = jnp.concatenate([swa_ref[pl.ds(NSA_KV_HEADS + g, wbuf, stride=N_KV_SLOTS), :],
                            vwn_ref[...]], 0)
    w_pos = past - wbuf + lax.broadcasted_iota(jnp.int32, (1, wbuf + SROW), 1)
    dist = s_col - w_pos
    mask = (dist >= 0) & (dist < WINDOW) & (w_pos >= 0)
    s_w = (_dot_nt(q_rot, keys) * ATTN_SCALE).reshape(rep, SROW, wbuf + SROW)
    p_w = _masked_softmax(s_w, mask[None])
    o_swa = _dot(p_w.reshape(rep * SROW, wbuf + SROW), vals)

    gt = gate_ref[...]
    for r in range(rep):
        rows = slice(r * SROW, (r + 1) * SROW)
        part_ref[rows, :] = (gt[:, 3 * r:3 * r + 1] * o_cmp[rows]
                             + gt[:, 3 * r + 2:3 * r + 3] * o_swa[rows])


def _nsa_sample_a(p_s, qrot_s, kc_all, swa_rows, sig_s, past, n_blocks):
    dec_batch, nc, _ = kc_all.shape
    gw = NSA_REP * HEAD_DIM
    bg = lambda b, g: (b, g)
    return pl.pallas_call(
        functools.partial(_nsa_sample_a_kernel, past=past, n_blocks=n_blocks),
        grid=(dec_batch, NSA_KV_HEADS),
        in_specs=[
            pl.BlockSpec((SROW, gw), bg),
            pl.BlockSpec((SROW, gw), bg),
            pl.BlockSpec((None, nc, HEAD_DIM), lambda b, g: (b, 0, g)),
            pl.BlockSpec((None, nc, HEAD_DIM), lambda b, g: (b, 0, NSA_KV_HEADS + g)),
            pl.BlockSpec((None, swa_rows.shape[1], HEAD_DIM), lambda b, g: (b, 0, 0)),
            pl.BlockSpec((SROW, HEAD_DIM), lambda b, g: (b, COL_KVW + g)),
            pl.BlockSpec((SROW, HEAD_DIM), lambda b, g: (b, COL_KVW + NSA_KV_HEADS + g)),
            pl.BlockSpec((SROW, LANE), bg),
        ],
        out_specs=[pl.BlockSpec((None, None, NSA_REP * SROW, HEAD_DIM), lambda b, g: (b, g, 0, 0)),
                   pl.BlockSpec((None, None, SROW, LANE), lambda b, g: (b, g, 0, 0))],
        out_shape=[jax.ShapeDtypeStruct((dec_batch, NSA_KV_HEADS, NSA_REP * SROW, HEAD_DIM), F32),
                   jax.ShapeDtypeStruct((dec_batch, NSA_KV_HEADS, SROW, LANE), jnp.int32)],
        compiler_params=_cparams(("arbitrary", "arbitrary")),
        name="nsa_sample_cmp_swa",
    )(p_s, qrot_s, kc_all, kc_all, swa_rows, p_s, p_s, sig_s)


def _nsa_sample_gather_kernel(pt_ref, ix_ref, *refs, n_blocks, s_len):
    del pt_ref, ix_ref
    n_sel = min(SLC_TOPK, n_blocks)
    n_fetch = s_len * n_sel
    blocks = refs[:n_fetch]
    (qrot_ref, ids_ref, knew_ref, vnew_ref, gate_ref, part_ref, o_ref) = refs[n_fetch:]
    g = pl.program_id(1)
    rep = NSA_REP
    n_keys = n_fetch * SLC_BLOCK

    def block_rows(slot):
        return jnp.concatenate([r[pl.ds(slot, SLC_BLOCK, stride=N_KV_SLOTS), :].astype(BF)
                                for r in blocks], 0)

    q = _stack_heads(qrot_ref, rep)
    s_cache = _dot_nt(q, block_rows(g)) * ATTN_SCALE

    ids = ids_ref[...]
    cached = jnp.where((ids >= 0) & (ids < n_blocks - 1), 1.0, 0.0).astype(BF)
    rank_row = lax.broadcasted_iota(jnp.int32, (LANE, n_keys), 0)
    key_col = lax.broadcasted_iota(jnp.int32, (LANE, n_keys), 1)
    by_rank = jnp.where(rank_row == (key_col // SLC_BLOCK) % n_sel, 1.0, 0.0).astype(BF)
    tok = lax.broadcasted_iota(jnp.int32, (SROW, n_keys), 0)
    owner = lax.broadcasted_iota(jnp.int32, (SROW, n_keys), 1) // (SLC_BLOCK * n_sel)
    keep_cache = jnp.where(owner == tok, jnp.dot(cached, by_rank, preferred_element_type=F32), 0.0)

    s_q = lax.broadcasted_iota(jnp.int32, (SROW, LANE), 0)
    s_k = lax.broadcasted_iota(jnp.int32, (SROW, LANE), 1)
    newest = jnp.sum(jnp.where(ids == n_blocks - 1, 1.0, 0.0), -1, keepdims=True)
    keep_new = jnp.where((s_k <= s_q) & (s_k < SROW), newest, 0.0)
    pad = jnp.zeros((LANE - SROW, HEAD_DIM), F32)
    s_new = _dot_nt(q, jnp.concatenate([knew_ref[...], pad], 0)) * ATTN_SCALE

    over_heads = lambda x: jnp.concatenate([x] * rep, 0) > 0.5
    mask_c, mask_n = over_heads(keep_cache), over_heads(keep_new)
    s_cache = jnp.where(mask_c, s_cache, NEG_INF)
    s_new = jnp.where(mask_n, s_new, NEG_INF)
    m = jnp.maximum(jnp.max(s_cache, -1, keepdims=True), jnp.max(s_new, -1, keepdims=True))
    e_c = jnp.where(mask_c, jnp.exp(s_cache - m), 0.0)
    e_n = jnp.where(mask_n, jnp.exp(s_new - m), 0.0)
    denom = jnp.maximum(jnp.sum(e_c, -1, keepdims=True) + jnp.sum(e_n, -1, keepdims=True), 1e-30)
    o_slc = (_dot(e_c, block_rows(NSA_KV_HEADS + g))
             + _dot(e_n, jnp.concatenate([vnew_ref[...], pad], 0))) / denom
    gt = gate_ref[...]
    for r in range(rep):
        rows = slice(r * SROW, (r + 1) * SROW)
        o_ref[:, r * HEAD_DIM:(r + 1) * HEAD_DIM] = (
            part_ref[rows, :] + gt[:, 3 * r + 1:3 * r + 2] * o_slc[rows])


def _nsa_sample_gather(cache_rows, page_table, ids, p_s, qrot_s, sig_s, part, n_blocks, s_len):
    n_pool, page_rows, _ = cache_rows.shape
    dec_batch, n_pages = page_table.shape
    n_sel = min(SLC_TOPK, n_blocks)
    per_page = page_rows // (N_KV_SLOTS * SLC_BLOCK)
    half_rows = N_KV_SLOTS * SLC_BLOCK
    gw = NSA_REP * HEAD_DIM
    flat_ids = ids[:, :, :s_len, :n_sel].reshape(dec_batch, NSA_KV_HEADS * s_len * n_sel)
    pool_page = jnp.take_along_axis(page_table, jnp.clip(flat_ids // per_page, 0, n_pages - 1), 1)
    in_page = jnp.maximum(flat_ids, 0) % per_page

    def fetch_spec(n):
        def index(b, g, pp, ip):
            return (pp[b, g * (s_len * n_sel) + n], ip[b, g * (s_len * n_sel) + n], 0)
        return pl.BlockSpec((None, half_rows, HEAD_DIM), index)

    bg = lambda b, g, pt, ix: (b, g)
    return pl.pallas_call(
        functools.partial(_nsa_sample_gather_kernel, n_blocks=n_blocks, s_len=s_len),
        grid_spec=pltpu.PrefetchScalarGridSpec(
            num_scalar_prefetch=2,
            grid=(dec_batch, NSA_KV_HEADS),
            in_specs=[fetch_spec(n) for n in range(s_len * n_sel)] + [
                pl.BlockSpec((SROW, gw), bg),
                pl.BlockSpec((None, None, SROW, LANE), lambda b, g, pt, ix: (b, g, 0, 0)),
                pl.BlockSpec((SROW, HEAD_DIM), lambda b, g, pt, ix: (b, COL_KVS + g)),
                pl.BlockSpec((SROW, HEAD_DIM), lambda b, g, pt, ix: (b, COL_KVS + NSA_KV_HEADS + g)),
                pl.BlockSpec((SROW, LANE), bg),
                pl.BlockSpec((None, None, NSA_REP * SROW, HEAD_DIM), lambda b, g, pt, ix: (b, g, 0, 0)),
            ],
            out_specs=pl.BlockSpec((SROW, gw), bg),
        ),
        out_shape=jax.ShapeDtypeStruct((dec_batch * SROW, NSA_HEADS * HEAD_DIM), F32),
        compiler_params=_cparams(("arbitrary", "arbitrary")),
        name="nsa_sample_selected",
    )(pool_page, in_page, *([cache_rows] * (s_len * n_sel)), qrot_s, ids, p_s, p_s, sig_s, part)


def _fox_sample_kernel(pt_ref, *refs, s_len):
    n = PAGES_PER_STEP
    kpages, vpages = refs[:n], refs[n:2 * n]
    (logf_ref, q_ref, knew_ref, vnew_ref, lnew_ref,
     o_ref, m_ref, l_ref, acc_ref, carry_ref) = refs[2 * n:]
    b = pl.program_id(0)
    c = pl.program_id(1)
    page = kpages[0].shape[0]
    head = lambda ref, h: ref[:, h * HEAD_DIM:(h + 1) * HEAD_DIM]

    def cache_rows(pages, h):
        return jnp.concatenate([_head_rows(p, h).astype(BF) for p in pages], 0)

    def update(scores, mask, values):
        s = jnp.concatenate(scores, 0)
        if mask is not None:
            s = jnp.where(mask, s, NEG_INF)
        m_old = m_ref[...]
        m_new = jnp.maximum(m_old, jnp.max(s, -1, keepdims=True))
        p = jnp.exp(s - m_new)
        if mask is not None:
            p = jnp.where(mask, p, 0.0)
        alpha = jnp.exp(m_old - m_new)
        pv = jnp.concatenate([_dot(p[h * SROW:(h + 1) * SROW], values[h])
                              for h in range(FOX_HEADS)], 0)
        m_ref[...] = m_new
        l_ref[...] = alpha * l_ref[...] + jnp.sum(p, -1, keepdims=True)
        acc_ref[...] = alpha * acc_ref[...] + pv

    @pl.when(c == 0)
    def _():
        m_ref[...] = jnp.full(m_ref.shape, NEG_INF, F32)
        l_ref[...] = jnp.zeros(l_ref.shape, F32)
        acc_ref[...] = jnp.zeros(acc_ref.shape, F32)
        s_q = lax.broadcasted_iota(jnp.int32, (SROW, LANE), 0)
        s_k = lax.broadcasted_iota(jnp.int32, (SROW, LANE), 1)
        later = jnp.where((s_q > s_k) & (s_q < s_len), 1.0, 0.0)
        real = lax.broadcasted_iota(jnp.int32, (SROW, 1), 0) < s_len
        lnew = lnew_ref[...]
        pad = jnp.zeros((LANE - SROW, HEAD_DIM), F32)
        rows_q = lax.broadcasted_iota(jnp.int32, (FOX_HEADS * SROW, LANE), 0) % SROW
        cols_k = lax.broadcasted_iota(jnp.int32, (FOX_HEADS * SROW, LANE), 1)
        mask = (cols_k <= rows_q) & (cols_k < s_len)
        totals, scores, values = [], [], []
        for h in range(FOX_HEADS):
            lf = jnp.where(real, lnew[:, h:h + 1], 0.0)
            d_row = jnp.sum(lf * later, 0, keepdims=True)
            totals.append(jnp.sum(lf, 0, keepdims=True))
            k = jnp.concatenate([head(knew_ref, h), pad], 0)
            values.append(jnp.concatenate([head(vnew_ref, h), pad], 0))
            scores.append(_dot_nt(head(q_ref, h), k) * ATTN_SCALE + d_row)
        update(scores, mask, values)
        totals += [jnp.zeros((1, 1), F32)] * (SUBLANE - FOX_HEADS)
        carry_ref[...] = jnp.broadcast_to(jnp.concatenate(totals, 0), carry_ref.shape)

    @pl.when(c > 0)
    def _():
        ids = [pt_ref[b, c * n + i] for i in range(n)]
        lf = jnp.concatenate(
            [jnp.concatenate([logf_ref[h, pl.ds(pg, 1), :] for pg in ids], 1) for h in range(FOX_HEADS)]
            + [jnp.zeros((SUBLANE - FOX_HEADS, n * page), F32)], 0)
        incl = _suffix_sum_lanes(lf)
        carry = carry_ref[:, 0:1]
        decay = carry + (incl - lf)
        carry_ref[...] = jnp.broadcast_to(carry + incl[:, 0:1], carry_ref.shape)
        scores = [_dot_nt(head(q_ref, h), cache_rows(kpages, h)) * ATTN_SCALE + decay[h:h + 1, :]
                  for h in range(FOX_HEADS)]
        update(scores, None, [cache_rows(vpages, h) for h in range(FOX_HEADS)])

    @pl.when(c == pl.num_programs(1) - 1)
    def _():
        o = acc_ref[...] / jnp.maximum(l_ref[...], 1e-30)
        for h in range(FOX_HEADS):
            o_ref[:, h * HEAD_DIM:(h + 1) * HEAD_DIM] = o[h * SROW:(h + 1) * SROW]


def _fox_sample(cache, logf_t, page_table, p_s, lsig_s, s_len):
    n_pool, page = cache.shape[:2]
    dec_batch, n_pages = page_table.shape
    nchunk = n_pages // PAGES_PER_STEP
    fw = FOX_HEADS * HEAD_DIM
    row = lambda w, col: pl.BlockSpec((SROW, w), lambda b, c, pt: (b, col))
    chunks = page_table.reshape(dec_batch, nchunk, PAGES_PER_STEP)[:, ::-1]
    steps = jnp.concatenate([chunks[:, :1], chunks], 1).reshape(dec_batch, (nchunk + 1) * PAGES_PER_STEP)

    def half_specs(kv):
        def spec(i):
            return pl.BlockSpec((None, page, None, FOX_HEADS, HEAD_DIM),
                                lambda b, c, pt: (pt[b, c * PAGES_PER_STEP + i], 0, kv, 0, 0))
        return [spec(i) for i in range(PAGES_PER_STEP)]

    return pl.pallas_call(
        functools.partial(_fox_sample_kernel, s_len=s_len),
        grid_spec=pltpu.PrefetchScalarGridSpec(
            num_scalar_prefetch=1,
            grid=(dec_batch, nchunk + 1),
            in_specs=half_specs(0) + half_specs(1) + [
                pl.BlockSpec(logf_t.shape, lambda b, c, pt: (0, 0, 0), pipeline_mode=pl.Buffered(1)),
                row(fw, COL_FOX * LANE // fw),
                row(fw, COL_FOX * LANE // fw + 1),
                row(fw, COL_FOX * LANE // fw + 2),
                row(LANE, 0),
            ],
            out_specs=row(fw, 0),
            scratch_shapes=[pltpu.VMEM((FOX_HEADS * SROW, 1), F32),
                            pltpu.VMEM((FOX_HEADS * SROW, 1), F32),
                            pltpu.VMEM((FOX_HEADS * SROW, HEAD_DIM), F32),
                            pltpu.VMEM((SUBLANE, LANE), F32)],
        ),
        out_shape=jax.ShapeDtypeStruct((dec_batch * SROW, fw), F32),
        compiler_params=_cparams(("arbitrary", "arbitrary")),
        name="fox_attention_sample",
    )(steps, *([cache] * (2 * PAGES_PER_STEP)), logf_t, p_s, p_s, p_s, lsig_s)


def _rope_tables(pos):
    half = ROT_DIM // 2
    inv = jnp.power(ROPE_THETA, -jnp.arange(half, dtype=F32) * (2.0 / ROT_DIM))
    ang = pos.astype(F32)[:, None] * inv[None, :]
    cos, sin = jnp.cos(ang), jnp.sin(ang)
    n = pos.shape[0]
    c = jnp.concatenate([cos, cos, jnp.ones((n, HEAD_DIM - ROT_DIM), F32)], 1)
    s = jnp.concatenate([-sin, sin, jnp.zeros((n, HEAD_DIM - ROT_DIM), F32)], 1)
    return c, s


def _layer_weights(w_in, b_gate, b_forget, cmp_w1, cmp_b1, cmp_w2, cmp_pos, w_mem_kv, w_out,
                   ln1_g, ln1_b, w_up, conv_w, conv_b, w_down, ln2_g, ln2_b):
    d = w_in.shape[0]
    nq = NSA_HEADS * HEAD_DIM
    kvw = 2 * NSA_KV_HEADS * HEAD_DIM
    o_g = nq + 3 * kvw
    o_fox = o_g + 3 * NSA_HEADS
    o_f = o_fox + 3 * FOX_HEADS * HEAD_DIM
    o_qm = o_f + FOX_HEADS
    w_main = jnp.concatenate([w_in[:, :o_g], w_in[:, o_fox:o_f], w_in[:, o_qm:]], 1).astype(BF)
    per_group = 3 * NSA_REP
    zpad = lambda n: jnp.zeros((d, n), w_in.dtype)
    w_small = jnp.concatenate([
        w_in[:, o_g:o_g + per_group], zpad(LANE - per_group),
        w_in[:, o_g + per_group:o_fox], zpad(LANE - per_group),
        w_in[:, o_f:o_qm], zpad(LANE - FOX_HEADS)], 1).astype(BF)
    bpad = lambda n: jnp.zeros((n,), F32)
    b_small = jnp.concatenate([
        b_gate[:per_group], bpad(LANE - per_group), b_gate[per_group:], bpad(LANE - per_group),
        b_forget, bpad(LANE - FOX_HEADS)])[None, :].astype(F32)
    kdim = CMP_STRIDE * HEAD_DIM
    w1cat = jnp.concatenate([cmp_w1[:, :CMP_STRIDE].reshape(2, kdim, CMP_HIDDEN),
                             cmp_w1[:, CMP_STRIDE:].reshape(2, kdim, CMP_HIDDEN)], 2).astype(BF)
    pe = jnp.concatenate([cmp_pos[:, :CMP_STRIDE].reshape(2, 1, kdim),
                          cmp_pos[:, CMP_STRIDE:].reshape(2, 1, kdim),
                          jnp.zeros((2, SUBLANE - 2, kdim), F32)], 1)
    cw = (w1cat, pe, cmp_b1[:, None, :], cmp_w2.astype(BF))
    conv_w8 = jnp.concatenate([conv_w, jnp.zeros((SUBLANE - CONV_W, conv_w.shape[1]), F32)], 0)
    fw = (w_up.astype(BF), conv_w8, conv_b[None, :], w_down.astype(BF), ln2_g[None, :], ln2_b[None, :])
    return dict(w_main=w_main, w_small=w_small, b_small=b_small, cw=cw,
                w_mem_kv=w_mem_kv.astype(BF), w_out=w_out.astype(BF),
                ln1_g=ln1_g[None, :], ln1_b=ln1_b[None, :], fw=fw)


def _prompt_layer(x, mem, lw, alpha):
    batch, seq, d = x.shape
    x2d = x.reshape(batch * seq, d)
    rc, rs = _rope_tables(jnp.tile(jnp.arange(seq, dtype=jnp.int32), batch))
    tm = min(512, seq)
    p_main, q_rot, sig, lsig, st_c, st_s, st_w, st_f = _input_projection(
        x2d, lw["w_main"], lw["w_small"], lw["b_small"], rc, rs, tm)
    kc_all = _compress_prompt(p_main, batch, seq, lw["cw"])
    o_nsa = _nsa_prompt(p_main, q_rot, kc_all, sig, batch, seq)
    decay = _decay_prompt(lsig, batch, seq)
    o_fox = _fox_prompt(p_main, decay, batch, seq)
    mlen = mem.shape[1]
    mem_kv = _matmul(mem.reshape(batch * mlen, d), lw["w_mem_kv"], min(256, batch * mlen), 512)
    o_mem = _mem_attention(p_main, mem_kv, batch, seq, min(512, seq))
    h = _out_projection(o_nsa, o_fox, o_mem, x2d, lw["w_out"], lw["ln1_g"], lw["ln1_b"],
                        min(256, seq), alpha)
    y, tail = _ffn_prompt(h, lw["fw"], seq, tm, 512, alpha)
    kv_state = lambda st: st.reshape(batch, seq, 2, -1, HEAD_DIM)
    n_win = min(WINDOW, seq)
    tiles = seq // tm
    conv_state = tail.reshape(batch, tiles, SUBLANE, -1)[:, -1, SUBLANE - (CONV_W - 1):]
    states = (kv_state(st_c), kv_state(st_s), kv_state(st_f),
              lsig[:, :FOX_HEADS].reshape(batch, seq, FOX_HEADS),
              kv_state(st_w)[:, seq - n_win:],
              mem_kv.reshape(batch, mlen, 2, MEM_HEADS, HEAD_DIM), conv_state)
    return y.reshape(batch, seq, d), states


def _sample_layer(x, c_cmp, c_slc, c_fox, c_logf, c_swa, c_mem, s_conv, page_table, lw, alpha):
    dec_batch, s_len, d = x.shape
    n_pool, page = c_cmp.shape[0], c_cmp.shape[1]
    n_pages = page_table.shape[1]
    past = n_pages * page
    t_all = past + s_len
    assert s_len <= SROW and n_pages % PAGES_PER_STEP == 0
    assert (t_all // CMP_STRIDE) * CMP_STRIDE == past and past % SLC_BLOCK == 0
    n_blocks = -(-t_all // SLC_BLOCK)
    xp = jnp.pad(x, ((0, 0), (0, SROW - s_len), (0, 0))).reshape(dec_batch * SROW, d)
    pos = past + jnp.tile(jnp.arange(SROW, dtype=jnp.int32), dec_batch)
    rc, rs = _rope_tables(pos)
    p_s, qrot_s, sig_s, lsig_s = _input_projection(xp, lw["w_main"], lw["w_small"], lw["b_small"],
                                                   rc, rs, dec_batch * SROW)[:4]
    kvw = 2 * NSA_KV_HEADS * HEAD_DIM
    as_rows = lambda a: a.reshape(a.shape[0], -1, HEAD_DIM)
    kc_all = _compress_paged(as_rows(c_cmp), page_table, lw["cw"])
    part, ids = _nsa_sample_a(p_s, qrot_s, kc_all, as_rows(c_swa), sig_s, past, n_blocks)
    o_nsa = _nsa_sample_gather(as_rows(c_slc), page_table, ids, p_s, qrot_s, sig_s, part,
                               n_blocks, s_len)
    logf_t = jnp.transpose(c_logf, (2, 0, 1))
    o_fox = _fox_sample(c_fox, logf_t, page_table, p_s, lsig_s, s_len)
    o_mem = _mem_attention_rows(p_s, as_rows(c_mem))
    h = _out_projection(o_nsa, o_fox, o_mem, xp, lw["w_out"], lw["ln1_g"], lw["ln1_b"],
                        dec_batch * SROW, alpha)
    dff = s_conv.shape[-1]
    zrow = jnp.zeros((dec_batch, 1, dff), F32)
    fill1 = jnp.concatenate([s_conv[:, 1:2]] + [zrow] * (SROW - 1), 1).reshape(dec_batch * SROW, dff)
    fill2 = jnp.concatenate([s_conv[:, 0:1], s_conv[:, 1:2]] + [zrow] * (SROW - 2), 1)
    y, a = _ffn_sample(h, lw["fw"], fill1, fill2.reshape(dec_batch * SROW, dff), 512, alpha)

    def rows(arr):
        return arr.reshape(dec_batch, SROW, -1)[:, :s_len]
    cols = lambda c0, w: rows(p_s[:, c0 * LANE:c0 * LANE + w]).reshape(dec_batch, s_len, 2, -1, HEAD_DIM)
    new_kv_swa = cols(COL_KVW, kvw)
    new_swa = jnp.concatenate([c_swa, new_kv_swa], 1)[:, s_len:]
    conv_state = jnp.concatenate([s_conv, rows(a)], 1)[:, -(CONV_W - 1):]
    states = (cols(COL_KVC, kvw), cols(COL_KVS, kvw),
              cols(COL_FOX + FOX_HEADS, 2 * FOX_HEADS * HEAD_DIM),
              rows(lsig_s[:, :FOX_HEADS]), new_swa, conv_state)
    return rows(y), states


def kernel(x_prompt, x_sample, mem_prompt, cache_nsa_cmp, cache_nsa_slc, cache_fox_kv, cache_fox_logf, cache_nsa_swa, cache_mem, state_conv, page_table, w_in, b_gate, b_forget, cmp_w1, cmp_b1, cmp_w2, cmp_pos, w_mem_kv, w_out, ln1_g, ln1_b, w_up, conv_w, conv_b, w_down, ln2_g, ln2_b):
    depth = w_in.shape[0]
    alpha = float((2 * depth) ** 0.25)
    yp, ys = x_prompt, x_sample
    acc_p = [[] for _ in range(7)]
    acc_s = [[] for _ in range(6)]
    for l in range(depth):
        lw = _layer_weights(w_in[l], b_gate[l], b_forget[l], cmp_w1[l], cmp_b1[l], cmp_w2[l],
                            cmp_pos[l], w_mem_kv[l], w_out[l], ln1_g[l], ln1_b[l], w_up[l],
                            conv_w[l], conv_b[l], w_down[l], ln2_g[l], ln2_b[l])
        yp, st_p = _prompt_layer(yp, mem_prompt, lw, alpha)
        ys, st_s = _sample_layer(ys, cache_nsa_cmp[l], cache_nsa_slc[l], cache_fox_kv[l],
                                 cache_fox_logf[l], cache_nsa_swa[l], cache_mem[l], state_conv[l],
                                 page_table, lw, alpha)
        for lst, a in zip(acc_p, st_p):
            lst.append(a)
        for lst, a in zip(acc_s, st_s):
            lst.append(a)
    outs_p = [jnp.stack(a, 0) for a in acc_p]
    outs_s = [jnp.stack(a, 0) for a in acc_s]
    return (yp, ys, *outs_p, *outs_s)
```

```python
import functools

import jax
import jax.numpy as jnp
import numpy as np
from jax import lax
from jax.experimental import pallas as pl
from jax.experimental.pallas import tpu as pltpu

HEAD_DIM = 128
NSA_KV_HEADS = 2
NSA_REP = 4
NSA_HEADS = NSA_KV_HEADS * NSA_REP
FOX_HEADS = 4
MEM_HEADS = 4
CMP_BLOCK = 32
CMP_STRIDE = 16
CMP_HIDDEN = 256
SLC_BLOCK = 64
SLC_TOPK = 16
WINDOW = 512
Q_BLOCK = 128
ROT_DIM = HEAD_DIM // 4
ROPE_THETA = 500000.0
CONV_W = 3
LN_EPS = 1e-5
ATTN_SCALE = HEAD_DIM ** -0.5
LOG2E = 1.4426950408889634
NEG_INF = -1e30
FORCE_BONUS = 1e3

LANE = 128
SUBLANE = 8
SROW = SUBLANE
PAGES_PER_STEP = 32
CMP_PAGES_PER_STEP = 32
VMEM_LIMIT = 56 * 1024 * 1024

COL_Q = 0
COL_KVC = 8
COL_KVS = 12
COL_KVW = 16
COL_FOX = 20
COL_QM = 32
MAIN_W = 36 * LANE
IN_TILE = 512

BF = jnp.bfloat16
F32 = jnp.float32


def _dot(a, b):
    return jnp.dot(a.astype(BF), b.astype(BF), preferred_element_type=F32)


def _dot_nt(a, b):
    return lax.dot_general(a.astype(BF), b.astype(BF), (((1,), (1,)), ((), ())),
                           preferred_element_type=F32)


def _dot_exact01(x, m01, m01_first=False):
    hi = x.astype(BF)
    r1 = x - hi.astype(F32)
    mid = r1.astype(BF)
    lo = (r1 - mid.astype(F32)).astype(BF)
    m = m01.astype(BF)
    if m01_first:
        d = lambda a: lax.dot_general(m, a, (((1,), (1,)), ((), ())), preferred_element_type=F32)
    else:
        d = lambda a: jnp.dot(a, m, preferred_element_type=F32)
    return d(hi) + d(mid) + d(lo)


def _cparams(sem):
    return pltpu.CompilerParams(dimension_semantics=sem, vmem_limit_bytes=VMEM_LIMIT)


def _masked_softmax(s, mask):
    s = jnp.where(mask, s, NEG_INF)
    m = jnp.max(s, -1, keepdims=True)
    e = jnp.where(mask, jnp.exp(s - m), 0.0)
    return e / jnp.maximum(jnp.sum(e, -1, keepdims=True), 1e-30)


def _flash_update(carry, s2, v):
    m, l, acc = carry
    m_new = jnp.maximum(m, jnp.max(s2, -1, keepdims=True))
    p = jnp.exp2(s2 - m_new)
    alpha = jnp.exp2(m - m_new)
    return (m_new, alpha * l + jnp.sum(p, -1, keepdims=True),
            alpha * acc + jnp.dot(p.astype(BF), v, preferred_element_type=F32))


def _rope_tile(x, c, s):
    lane = lax.broadcasted_iota(jnp.int32, x.shape, 1)
    half = ROT_DIM // 2
    swapped = jnp.where(lane < half, pltpu.roll(x, LANE - half, 1), pltpu.roll(x, half, 1))
    return x * c + swapped * s


def _topk_rank(score, n_valid_cols):
    col = lax.broadcasted_iota(jnp.int32, score.shape, 1)
    rank = jnp.zeros(score.shape, F32)
    for i in range(n_valid_cols):
        ci = score[:, i:i + 1]
        rank = rank + jnp.where(col > i, jnp.where(ci >= score, 1.0, 0.0),
                                jnp.where(ci > score, 1.0, 0.0))
    return rank


def _topk_select_rows(score_t, n_sel):
    n, width = score_t.shape
    rank = jnp.zeros(score_t.shape, F32)
    for i in range(n):
        ci = jnp.broadcast_to(score_t[i:i + 1, :], (SUBLANE, width))
        parts = []
        for j0 in range(0, n, SUBLANE):
            sj = score_t[j0:j0 + SUBLANE, :]
            if j0 > i:
                ahead = ci >= sj
            elif j0 + SUBLANE - 1 <= i:
                ahead = ci > sj
            else:
                row = lax.broadcasted_iota(jnp.int32, (SUBLANE, width), 0) + j0
                ahead = jnp.where(row > i, jnp.where(ci >= sj, 1.0, 0.0),
                                  jnp.where(ci > sj, 1.0, 0.0)) > 0.5
            parts.append(jnp.where(ahead, 1.0, 0.0))
        rank = rank + jnp.concatenate(parts, 0)
    return jnp.where(rank < n_sel, 1.0, 0.0)


def _inproj_kernel(x_ref, w_ref, ws_ref, bs_ref, c_ref, s_ref,
                   p_ref, qrot_ref, sig_ref, lsig_ref, stc_ref, sts_ref, stw_ref, stf_ref, xb_ref):
    j = pl.program_id(1)
    tm = x_ref.shape[0]

    def store_rows(ref, tile, first_slot, slots):
        for k in range(IN_TILE // LANE):
            ref[pl.ds(first_slot + k, tm, stride=slots), :] = tile[k]

    @pl.when(j == 0)
    def _():
        xb_ref[...] = x_ref[...].astype(BF)
        z = jnp.dot(xb_ref[...], ws_ref[...], preferred_element_type=F32) + bs_ref[...]
        sig_ref[...] = jax.nn.sigmoid(z[:, :2 * LANE])
        zf = z[:, 2 * LANE:]
        lsig_ref[...] = jnp.minimum(zf, 0.0) - jnp.log1p(jnp.exp(-jnp.abs(zf)))

    acc = jnp.dot(xb_ref[...], w_ref[...], preferred_element_type=F32)
    c = c_ref[...]
    s = s_ref[...]
    heads = IN_TILE // LANE

    @pl.when(j < 2)
    def _():
        p_ref[...] = acc
        for h in range(heads):
            qrot_ref[:, h * LANE:(h + 1) * LANE] = _rope_tile(acc[:, h * LANE:(h + 1) * LANE], c, s)

    lane_tiles = lambda a: [a[:, h * LANE:(h + 1) * LANE] for h in range(heads)]

    def roped_kv(st_ref):
        tiles = [_rope_tile(t, c, s) if h < NSA_KV_HEADS else t for h, t in enumerate(lane_tiles(acc))]
        for h, t in enumerate(tiles):
            p_ref[:, h * LANE:(h + 1) * LANE] = t
        store_rows(st_ref, tiles, 0, N_KV_SLOTS)

    @pl.when(j == COL_KVC // heads)
    def _():
        p_ref[...] = acc
        store_rows(stc_ref, lane_tiles(acc), 0, N_KV_SLOTS)

    @pl.when(j == COL_KVS // heads)
    def _():
        roped_kv(sts_ref)

    @pl.when(j == COL_KVW // heads)
    def _():
        roped_kv(stw_ref)

    fox_k = COL_FOX // heads + 1
    @pl.when((j == fox_k) | (j == fox_k + 1))
    def _():
        p_ref[...] = acc

    @pl.when(j == fox_k)
    def _():
        store_rows(stf_ref, lane_tiles(acc), 0, 2 * FOX_HEADS)

    @pl.when(j == fox_k + 1)
    def _():
        store_rows(stf_ref, lane_tiles(acc), FOX_HEADS, 2 * FOX_HEADS)

    @pl.when((j == COL_FOX // heads) | (j == COL_QM // heads))
    def _():
        p_ref[...] = acc


def _input_projection(x2d, w_main, w_small, b_small, rope_c, rope_s, tm):
    m, d = x2d.shape
    nj = MAIN_W // IN_TILE
    state_slots = (N_KV_SLOTS, N_KV_SLOTS, N_KV_SLOTS, 2 * FOX_HEADS)
    return pl.pallas_call(
        _inproj_kernel,
        grid=(m // tm, nj),
        in_specs=[
            pl.BlockSpec((tm, d), lambda i, j: (i, 0)),
            pl.BlockSpec((d, IN_TILE), lambda i, j: (0, j)),
            pl.BlockSpec((d, 3 * LANE), lambda i, j: (0, 0)),
            pl.BlockSpec((1, 3 * LANE), lambda i, j: (0, 0)),
            pl.BlockSpec((tm, LANE), lambda i, j: (i, 0)),
            pl.BlockSpec((tm, LANE), lambda i, j: (i, 0)),
        ],
        out_specs=[
            pl.BlockSpec((tm, IN_TILE), lambda i, j: (i, j)),
            pl.BlockSpec((tm, IN_TILE), lambda i, j: (i, jnp.minimum(j, 1))),
            pl.BlockSpec((tm, 2 * LANE), lambda i, j: (i, 0)),
            pl.BlockSpec((tm, LANE), lambda i, j: (i, 0)),
        ] + [pl.BlockSpec((tm * slots, LANE), lambda i, j: (i, 0)) for slots in state_slots],
        out_shape=[
            jax.ShapeDtypeStruct((m, MAIN_W), F32),
            jax.ShapeDtypeStruct((m, NSA_HEADS * HEAD_DIM), F32),
            jax.ShapeDtypeStruct((m, 2 * LANE), F32),
            jax.ShapeDtypeStruct((m, LANE), F32),
        ] + [jax.ShapeDtypeStruct((m * slots, LANE), F32) for slots in state_slots],
        scratch_shapes=[pltpu.VMEM((tm, d), BF)],
        compiler_params=_cparams(("arbitrary", "arbitrary")),
        name="input_projection",
    )(x2d, w_main, w_small, b_small, rope_c, rope_s)


def _gelu_tanh(x):
    k = np.sqrt(2.0 / np.pi).astype(np.float32)
    return x * (0.5 * (1.0 + jnp.tanh(k * (x + 0.044715 * (x ** 3)))))


def _compress_body(load, nb, w1_ref, pe_ref, b1_ref, w2_ref, out_ref, carry_ref, first_step):
    @pl.when(first_step)
    def _():
        carry_ref[...] = jnp.zeros(carry_ref.shape, F32)

    row = lax.broadcasted_iota(jnp.int32, (nb, CMP_HIDDEN), 0)
    for kv in range(2):
        xs = []
        for g in range(NSA_KV_HEADS):
            per_l = [load(kv * NSA_KV_HEADS + g, l) for l in range(CMP_STRIDE)]
            xs.append(jnp.concatenate(per_l, 1).astype(BF))
        xs.append(pe_ref[kv].astype(BF))
        fs = jnp.dot(jnp.concatenate(xs, 0), w1_ref[kv], preferred_element_type=F32)
        pos = (fs[2 * nb:2 * nb + 1, :CMP_HIDDEN] + fs[2 * nb + 1:2 * nb + 2, CMP_HIDDEN:]
               + b1_ref[kv])
        for g in range(NSA_KV_HEADS):
            col = (kv * NSA_KV_HEADS + g) * HEAD_DIM
            first = fs[g * nb:(g + 1) * nb, :CMP_HIDDEN]
            second = fs[g * nb:(g + 1) * nb, CMP_HIDDEN:]
            slot = kv * NSA_KV_HEADS + g
            prev = jnp.where(row == 0, carry_ref[slot], pltpu.roll(first, 1, 0))
            carry_ref[slot] = first[nb - 1:nb, :]
            h = _gelu_tanh(prev + second + pos)
            out_ref[:, col:col + HEAD_DIM] = jnp.dot(h.astype(BF), w2_ref[kv],
                                                     preferred_element_type=F32)


N_KV_SLOTS = 2 * NSA_KV_HEADS


def _compress_prompt_kernel(*refs):
    slots = refs[:N_KV_SLOTS]
    w1_ref, pe_ref, b1_ref, w2_ref, out_ref, carry_ref = refs[N_KV_SLOTS:]
    nb = slots[0].shape[0] // CMP_STRIDE

    def load(slot, l):
        return slots[slot][pl.ds(l, nb, stride=CMP_STRIDE), :]

    _compress_body(load, nb, w1_ref, pe_ref, b1_ref, w2_ref, out_ref, carry_ref,
                   pl.program_id(1) == 0)


def _compress_paged_kernel(pt_ref, *refs):
    del pt_ref
    pages = refs[:CMP_PAGES_PER_STEP]
    w1_ref, pe_ref, b1_ref, w2_ref, out_ref, carry_ref = refs[CMP_PAGES_PER_STEP:]
    group = N_KV_SLOTS * CMP_STRIDE
    per_page = pages[0].shape[0] // group
    blocks = jnp.concatenate([p[...].reshape(per_page, group, HEAD_DIM) for p in pages], 0)
    by_row = jnp.swapaxes(blocks, 0, 1)

    def load(slot, l):
        return by_row[l * N_KV_SLOTS + slot]

    _compress_body(load, per_page * CMP_PAGES_PER_STEP, w1_ref, pe_ref, b1_ref, w2_ref, out_ref,
                   carry_ref, pl.program_id(1) == 0)


def _compress_weight_specs():
    const3 = (lambda *a: (0, 0, 0))
    return [
        pl.BlockSpec((2, CMP_STRIDE * HEAD_DIM, 2 * CMP_HIDDEN), const3),
        pl.BlockSpec((2, SUBLANE, CMP_STRIDE * HEAD_DIM), const3),
        pl.BlockSpec((2, 1, CMP_HIDDEN), const3),
        pl.BlockSpec((2, CMP_HIDDEN, HEAD_DIM), const3),
    ]


def _compress_prompt(p_main, batch, seq, cw):
    chunk = min(seq, 2048)
    nchunk = seq // chunk
    kvw = 2 * NSA_KV_HEADS * HEAD_DIM
    return pl.pallas_call(
        _compress_prompt_kernel,
        grid=(batch, nchunk),
        in_specs=[pl.BlockSpec((chunk, HEAD_DIM), (lambda b, c, s=s: (b * nchunk + c, COL_KVC + s)))
                  for s in range(N_KV_SLOTS)] + _compress_weight_specs(),
        out_specs=pl.BlockSpec((None, chunk // CMP_STRIDE, kvw), lambda b, c: (b, c, 0)),
        out_shape=jax.ShapeDtypeStruct((batch, seq // CMP_STRIDE, kvw), F32),
        scratch_shapes=[pltpu.VMEM((2 * NSA_KV_HEADS, 1, CMP_HIDDEN), F32)],
        compiler_params=_cparams(("arbitrary", "arbitrary")),
        name="nsa_compress_prompt",
    )(*([p_main] * N_KV_SLOTS), *cw)


def _page_specs(page_rows, width, chunk_of, per_step):
    def spec(i):
        return pl.BlockSpec((None, page_rows, width),
                            lambda b, c, pt: (pt[b, chunk_of(c) * per_step + i], 0, 0))
    return [spec(i) for i in range(per_step)]


def _compress_paged(cache_rows, page_table, cw):
    n_pool, page_rows, _ = cache_rows.shape
    page = page_rows // N_KV_SLOTS
    kvw = N_KV_SLOTS * HEAD_DIM
    dec_batch, n_pages = page_table.shape
    nchunk = n_pages // CMP_PAGES_PER_STEP
    rows = CMP_PAGES_PER_STEP * page // CMP_STRIDE
    return pl.pallas_call(
        _compress_paged_kernel,
        grid_spec=pltpu.PrefetchScalarGridSpec(
            num_scalar_prefetch=1,
            grid=(dec_batch, nchunk),
            in_specs=(_page_specs(page_rows, HEAD_DIM, lambda c: c, CMP_PAGES_PER_STEP)
                      + _compress_weight_specs()),
            out_specs=pl.BlockSpec((None, rows, kvw), lambda b, c, pt: (b, c, 0)),
            scratch_shapes=[pltpu.VMEM((2 * NSA_KV_HEADS, 1, CMP_HIDDEN), F32)],
        ),
        out_shape=jax.ShapeDtypeStruct((dec_batch, nchunk * rows, kvw), F32),
        compiler_params=_cparams(("arbitrary", "arbitrary")),
        name="nsa_compress_paged",
    )(page_table, *([cache_rows] * CMP_PAGES_PER_STEP), *cw)


def _overlap_matrix(n_rows, n_blocks, blocks_first=False):
    shape = (n_blocks, n_rows) if blocks_first else (n_rows, n_blocks)
    n = lax.broadcasted_iota(jnp.int32, shape, 1 if blocks_first else 0)
    s = lax.broadcasted_iota(jnp.int32, shape, 0 if blocks_first else 1)
    c0 = (n - 1) * CMP_STRIDE
    hit = (n >= 1) & (c0 < s * SLC_BLOCK + SLC_BLOCK) & (c0 + CMP_BLOCK > s * SLC_BLOCK)
    return jnp.where(hit, 1.0, 0.0)


def _block_scores(imp, q_pos, n_blocks, block_axis=1):
    blk = lax.broadcasted_iota(jnp.int32, imp.shape, block_axis)
    cur = q_pos // SLC_BLOCK
    forced = (blk == 0) | (blk == cur) | (blk == cur - 1)
    valid = (blk * SLC_BLOCK <= q_pos) & (blk < n_blocks)
    return jnp.where(valid, imp + jnp.where(forced, FORCE_BONUS, 0.0), NEG_INF)


def _head_rows(tile_ref, h):
    tokens, heads, width = tile_ref.shape
    return tile_ref.reshape(tokens * heads, width)[pl.ds(h, tokens, stride=heads), :]


def _stack_heads(ref, n):
    return jnp.concatenate([ref[:, r * HEAD_DIM:(r + 1) * HEAD_DIM] for r in range(n)], 0)


def _nsa_prompt_kernel(qraw_ref, qrot_ref, kc_ref, vc_ref, ks_ref, vs_ref, kw_ref, vw_ref,
                       gate_ref, o_ref, ksb_ref, vsb_ref, kwb_ref, vwb_ref, *, seq, kchunk):
    i = pl.program_id(2)
    q0 = i * Q_BLOCK
    nc = kc_ref.shape[0]
    ns = seq // SLC_BLOCK
    rep = NSA_REP
    t_col = lax.broadcasted_iota(jnp.int32, (Q_BLOCK, 1), 0) + q0
    over_heads = lambda x: jnp.concatenate([x] * rep, 0)
    nt = (((1,), (1,)), ((), ()))

    @pl.when(i == 0)
    def _():
        ksb_ref[...] = ks_ref[...].astype(BF)
        vsb_ref[...] = vs_ref[...].astype(BF)
        kwb_ref[...] = kw_ref[...].astype(BF)
        vwb_ref[...] = vw_ref[...].astype(BF)

    q_raw = (_stack_heads(qraw_ref, rep) * (ATTN_SCALE * LOG2E)).astype(BF)
    n_row = lax.broadcasted_iota(jnp.int32, (1, nc), 1)
    vis = (n_row >= 1) & (n_row * CMP_STRIDE + (CMP_BLOCK - CMP_STRIDE - 1) <= t_col)
    s_c = (lax.dot_general(q_raw, kc_ref[...].astype(BF), nt, preferred_element_type=F32)
           + over_heads(jnp.where(vis, 0.0, NEG_INF)))
    e_c = jnp.exp2(s_c - jnp.max(s_c, -1, keepdims=True)) * over_heads(jnp.where(vis, 1.0, 0.0))
    p_c = e_c / jnp.maximum(jnp.sum(e_c, -1, keepdims=True), 1e-30)
    o_cmp = _dot(p_c, vc_ref[...])
    p_sum = p_c[0:Q_BLOCK]
    for r in range(1, rep):
        p_sum = p_sum + p_c[r * Q_BLOCK:(r + 1) * Q_BLOCK]
    q_rot = (_stack_heads(qrot_ref, rep) * (ATTN_SCALE * LOG2E)).astype(BF)
    wk = WINDOW + Q_BLOCK
    w0 = pl.multiple_of(jnp.maximum(q0 - WINDOW, 0), Q_BLOCK)
    dist = t_col - (w0 + lax.broadcasted_iota(jnp.int32, (1, wk), 1))
    s_w = (lax.dot_general(q_rot, kwb_ref[pl.ds(w0, wk), :], nt, preferred_element_type=F32)
           + over_heads(jnp.where((dist >= 0) & (dist < WINDOW), 0.0, NEG_INF)))
    e_w = jnp.exp2(s_w - jnp.max(s_w, -1, keepdims=True))
    p_w = e_w / jnp.sum(e_w, -1, keepdims=True)
    o_swa = jnp.dot(p_w.astype(BF), vwb_ref[pl.ds(w0, wk), :], preferred_element_type=F32)

    imp_t = _dot_exact01(p_sum, _overlap_matrix(nc, ns, blocks_first=True), m01_first=True)
    t_row = lax.broadcasted_iota(jnp.int32, (1, Q_BLOCK), 1) + q0
    sel_t = _topk_select_rows(_block_scores(imp_t, t_row, ns, block_axis=0), min(SLC_TOPK, ns))
    sel_b = sel_t.T.astype(BF)

    per_chunk = kchunk // SLC_BLOCK
    blk = lax.broadcasted_iota(jnp.int32, (ns, per_chunk), 0)
    slot = lax.broadcasted_iota(jnp.int32, (ns, per_chunk), 1)
    b_row = lax.broadcasted_iota(jnp.int32, (per_chunk, kchunk), 0)
    b_lane = lax.broadcasted_iota(jnp.int32, (per_chunk, kchunk), 1)
    expand = jnp.where(b_row == b_lane // SLC_BLOCK, 1.0, 0.0).astype(BF)
    k_lane = lax.broadcasted_iota(jnp.int32, (1, kchunk), 1)

    def slc_chunk(c, carry, diagonal):
        k0 = pl.multiple_of(c * kchunk, kchunk)
        pick = jnp.where(blk == c * per_chunk + slot, 1.0, 0.0).astype(BF)
        sel_c = jnp.dot(sel_b, pick, preferred_element_type=F32).astype(BF)
        keep = jnp.dot(sel_c, expand, preferred_element_type=F32) > 0.5
        if diagonal:
            keep = keep & (k0 + k_lane <= t_col)
        s2 = (lax.dot_general(q_rot, ksb_ref[pl.ds(k0, kchunk), :], nt, preferred_element_type=F32)
              + over_heads(jnp.where(keep, 0.0, NEG_INF)))
        return _flash_update(carry, s2, vsb_ref[pl.ds(k0, kchunk), :])

    init = (jnp.full((rep * Q_BLOCK, 1), NEG_INF, F32), jnp.zeros((rep * Q_BLOCK, 1), F32),
            jnp.zeros((rep * Q_BLOCK, HEAD_DIM), F32))
    last = q0 // kchunk
    carry = lax.fori_loop(0, last, lambda c, carry: slc_chunk(c, carry, False), init)
    _, l_s, acc_s = slc_chunk(last, carry, True)
    o_slc = acc_s / jnp.maximum(l_s, 1e-30)

    gt = gate_ref[...]
    for r in range(rep):
        rows = slice(r * Q_BLOCK, (r + 1) * Q_BLOCK)
        o_ref[:, r * HEAD_DIM:(r + 1) * HEAD_DIM] = (
            gt[:, 3 * r:3 * r + 1] * o_cmp[rows] + gt[:, 3 * r + 1:3 * r + 2] * o_slc[rows]
            + gt[:, 3 * r + 2:3 * r + 3] * o_swa[rows])


def _nsa_prompt(p_main, q_rot, kc_all, sig, batch, seq):
    nq = seq // Q_BLOCK
    gw = NSA_REP * HEAD_DIM
    kchunk = min(512, seq)
    col = lambda base, kv: (lambda b, g, i: (b, base + kv * NSA_KV_HEADS + g))
    return pl.pallas_call(
        functools.partial(_nsa_prompt_kernel, seq=seq, kchunk=kchunk),
        grid=(batch, NSA_KV_HEADS, nq),
        in_specs=[
            pl.BlockSpec((Q_BLOCK, gw), lambda b, g, i: (b * nq + i, g)),
            pl.BlockSpec((Q_BLOCK, gw), lambda b, g, i: (b * nq + i, g)),
            pl.BlockSpec((None, seq // CMP_STRIDE, HEAD_DIM), lambda b, g, i: (b, 0, g)),
            pl.BlockSpec((None, seq // CMP_STRIDE, HEAD_DIM), lambda b, g, i: (b, 0, NSA_KV_HEADS + g)),
            pl.BlockSpec((seq, HEAD_DIM), col(COL_KVS, 0)),
            pl.BlockSpec((seq, HEAD_DIM), col(COL_KVS, 1)),
            pl.BlockSpec((seq, HEAD_DIM), col(COL_KVW, 0)),
            pl.BlockSpec((seq, HEAD_DIM), col(COL_KVW, 1)),
            pl.BlockSpec((Q_BLOCK, LANE), lambda b, g, i: (b * nq + i, g)),
        ],
        out_specs=pl.BlockSpec((Q_BLOCK, gw), lambda b, g, i: (b * nq + i, g)),
        out_shape=jax.ShapeDtypeStruct((batch * seq, NSA_HEADS * HEAD_DIM), F32),
        scratch_shapes=[pltpu.VMEM((seq, HEAD_DIM), BF)] * 4,
        compiler_params=_cparams(("arbitrary", "arbitrary", "arbitrary")),
        name="nsa_attention_prompt",
    )(p_main, q_rot, kc_all, kc_all, p_main, p_main, p_main, p_main, sig)


def _suffix_sum_lanes(x):
    n = x.shape[-1]
    lane = lax.broadcasted_iota(jnp.int32, x.shape, x.ndim - 1)
    k = 1
    while k < n:
        x = x + jnp.where(lane < n - k, pltpu.roll(x, n - k, x.ndim - 1), 0.0)
        k *= 2
    return x


def _decay_prompt_kernel(lf_ref, d_ref):
    lt = lf_ref[...].T
    top = lt[:SUBLANE]
    d_ref[...] = (_suffix_sum_lanes(top) - top) * LOG2E


def _decay_prompt(lsig, batch, seq):
    return pl.pallas_call(
        _decay_prompt_kernel,
        grid=(batch,),
        in_specs=[pl.BlockSpec((seq, LANE), lambda b: (b, 0))],
        out_specs=pl.BlockSpec((None, SUBLANE, seq), lambda b: (b, 0, 0)),
        out_shape=jax.ShapeDtypeStruct((batch, SUBLANE, seq), F32),
        compiler_params=_cparams(("arbitrary",)),
        name="fox_decay_prompt",
    )(lsig)


def _fox_prompt_kernel(q_ref, k_ref, v_ref, d_ref, o_ref, kb_ref, vb_ref, m_ref, l_ref, acc_ref,
                       *, tile):
    h = pl.program_id(1)
    i = pl.program_id(2)

    @pl.when(i == 0)
    def _():
        kb_ref[...] = k_ref[...].astype(BF)
        vb_ref[...] = v_ref[...].astype(BF)

    q = (q_ref[...] * (ATTN_SCALE * LOG2E)).astype(BF)
    row = lax.broadcasted_iota(jnp.int32, (tile, tile), 0)
    col = lax.broadcasted_iota(jnp.int32, (tile, tile), 1)

    def chunk(c, carry, diagonal):
        k0 = pl.multiple_of(c * tile, tile)
        s2 = lax.dot_general(q, kb_ref[pl.ds(k0, tile), :], (((1,), (1,)), ((), ())),
                             preferred_element_type=F32) + d_ref[pl.ds(h, 1), pl.ds(k0, tile)]
        if diagonal:
            s2 = jnp.where(col <= row, s2, NEG_INF)
        return _flash_update(carry, s2, vb_ref[pl.ds(k0, tile), :])

    init = (jnp.full((tile, 1), NEG_INF, F32), jnp.zeros((tile, 1), F32),
            jnp.zeros((tile, HEAD_DIM), F32))
    carry = lax.fori_loop(0, i, lambda c, carry: chunk(c, carry, False), init)
    _, l, acc = chunk(i, carry, True)
    o_ref[...] = acc / jnp.maximum(l, 1e-30)


def _fox_prompt(p_main, decay, batch, seq):
    tq = min(512, seq)
    nq = seq // tq
    return pl.pallas_call(
        functools.partial(_fox_prompt_kernel, tile=tq),
        grid=(batch, FOX_HEADS, nq),
        scratch_shapes=[pltpu.VMEM((seq, HEAD_DIM), BF), pltpu.VMEM((seq, HEAD_DIM), BF),
                        pltpu.VMEM((tq, 1), F32), pltpu.VMEM((tq, 1), F32),
                        pltpu.VMEM((tq, HEAD_DIM), F32)],
        in_specs=[
            pl.BlockSpec((tq, HEAD_DIM), lambda b, h, i: (b * nq + i, COL_FOX + h)),
            pl.BlockSpec((seq, HEAD_DIM), lambda b, h, i: (b, COL_FOX + FOX_HEADS + h)),
            pl.BlockSpec((seq, HEAD_DIM), lambda b, h, i: (b, COL_FOX + 2 * FOX_HEADS + h)),
            pl.BlockSpec((None, SUBLANE, seq), lambda b, h, i: (b, 0, 0)),
        ],
        out_specs=pl.BlockSpec((tq, HEAD_DIM), lambda b, h, i: (b * nq + i, h)),
        out_shape=jax.ShapeDtypeStruct((batch * seq, FOX_HEADS * HEAD_DIM), F32),
        compiler_params=_cparams(("arbitrary", "arbitrary", "arbitrary")),
        name="fox_attention_prompt",
    )(p_main, p_main, p_main, decay)


def _mem_attn_kernel(q_ref, k_ref, v_ref, o_ref):
    s = _dot_nt(q_ref[...], k_ref[...]) * ATTN_SCALE
    m = jnp.max(s, -1, keepdims=True)
    e = jnp.exp(s - m)
    p = e / jnp.sum(e, -1, keepdims=True)
    o_ref[...] = _dot(p, v_ref[...])


def _mem_attention(p_main, mem_kv2d, batch, rows_per_batch, tq):
    nq = rows_per_batch // tq
    mlen = mem_kv2d.shape[0] // batch
    return pl.pallas_call(
        _mem_attn_kernel,
        grid=(batch, MEM_HEADS, nq),
        in_specs=[
            pl.BlockSpec((tq, HEAD_DIM), lambda b, h, i: (b * nq + i, COL_QM + h)),
            pl.BlockSpec((mlen, HEAD_DIM), lambda b, h, i: (b, h)),
            pl.BlockSpec((mlen, HEAD_DIM), lambda b, h, i: (b, MEM_HEADS + h)),
        ],
        out_specs=pl.BlockSpec((tq, HEAD_DIM), lambda b, h, i: (b * nq + i, h)),
        out_shape=jax.ShapeDtypeStruct((batch * rows_per_batch, MEM_HEADS * HEAD_DIM), F32),
        compiler_params=_cparams(("arbitrary", "arbitrary", "arbitrary")),
        name="mem_attention",
    )(p_main, mem_kv2d, mem_kv2d)


def _mem_attn_rows_kernel(q_ref, kv_ref, o_ref):
    slots = 2 * MEM_HEADS
    mlen = kv_ref.shape[0] // slots
    scores = [_dot_nt(q_ref[:, h * HEAD_DIM:(h + 1) * HEAD_DIM],
                      kv_ref[pl.ds(h, mlen, stride=slots), :]) for h in range(MEM_HEADS)]
    s = jnp.concatenate(scores, 0) * ATTN_SCALE
    m = jnp.max(s, -1, keepdims=True)
    e = jnp.exp(s - m)
    p = e / jnp.sum(e, -1, keepdims=True)
    for h in range(MEM_HEADS):
        o_ref[:, h * HEAD_DIM:(h + 1) * HEAD_DIM] = _dot(
            p[h * SROW:(h + 1) * SROW], kv_ref[pl.ds(MEM_HEADS + h, mlen, stride=slots), :])


def _mem_attention_rows(p_s, mem_rows):
    dec_batch, rows, _ = mem_rows.shape
    mw = MEM_HEADS * HEAD_DIM
    return pl.pallas_call(
        _mem_attn_rows_kernel,
        grid=(dec_batch,),
        in_specs=[pl.BlockSpec((SROW, mw), lambda b: (b, COL_QM * LANE // mw)),
                  pl.BlockSpec((None, rows, HEAD_DIM), lambda b: (b, 0, 0))],
        out_specs=pl.BlockSpec((SROW, mw), lambda b: (b, 0)),
        out_shape=jax.ShapeDtypeStruct((dec_batch * SROW, mw), F32),
        compiler_params=_cparams(("arbitrary",)),
        name="mem_attention_sample",
    )(p_s, mem_rows)


def _matmul_kernel(x_ref, w_ref, o_ref):
    o_ref[...] = jnp.dot(x_ref[...].astype(BF), w_ref[...], preferred_element_type=F32)


def _matmul(x2d, w_bf, tm, tn):
    m, k = x2d.shape
    n = w_bf.shape[1]
    return pl.pallas_call(
        _matmul_kernel,
        grid=(m // tm, n // tn),
        in_specs=[pl.BlockSpec((tm, k), lambda i, j: (i, 0)),
                  pl.BlockSpec((k, tn), lambda i, j: (0, j))],
        out_specs=pl.BlockSpec((tm, tn), lambda i, j: (i, j)),
        out_shape=jax.ShapeDtypeStruct((m, n), F32),
        compiler_params=_cparams(("arbitrary", "arbitrary")),
        name="projection_matmul",
    )(x2d, w_bf)


def _layer_norm(z, g, b):
    zc = z - jnp.mean(z, -1, keepdims=True)
    var = jnp.mean(zc * zc, -1, keepdims=True)
    return zc * lax.rsqrt(var + LN_EPS) * g + b


def _outproj_kernel(on_ref, of_ref, om_ref, x_ref, w_ref, g_ref, b_ref, h_ref, *, alpha):
    mix = jnp.concatenate([on_ref[...].astype(BF), of_ref[...].astype(BF),
                           om_ref[...].astype(BF)], 1)
    y = jnp.dot(mix, w_ref[...], preferred_element_type=F32)
    h_ref[...] = _layer_norm(alpha * x_ref[...] + y, g_ref[...], b_ref[...])


def _out_projection(o_nsa, o_fox, o_mem, x2d, w_out, g, b, tm, alpha):
    m, d = x2d.shape
    row = lambda w: pl.BlockSpec((tm, w), lambda i: (i, 0))
    const = lambda shape: pl.BlockSpec(shape, lambda i: (0, 0))
    return pl.pallas_call(
        functools.partial(_outproj_kernel, alpha=alpha),
        grid=(m // tm,),
        in_specs=[row(o_nsa.shape[1]), row(o_fox.shape[1]), row(o_mem.shape[1]), row(d),
                  const(w_out.shape), const((1, d)), const((1, d))],
        out_specs=row(d),
        out_shape=jax.ShapeDtypeStruct((m, d), F32),
        compiler_params=_cparams(("arbitrary",)),
        name="out_projection_ln",
    )(o_nsa, o_fox, o_mem, x2d, w_out, g, b)


def _ffn_core(h_ref, wa_ref, wb_ref, cw_ref, cb_ref, wd_ref, g_ref, b_ref, y_ref, acc_ref, hb_ref,
              shifted, alpha):
    j = pl.program_id(1)

    @pl.when(j == 0)
    def _():
        hb_ref[...] = h_ref[...].astype(BF)

    hb = hb_ref[...]
    a = jnp.dot(hb, wa_ref[...], preferred_element_type=F32)
    gate_in = jnp.dot(hb, wb_ref[...], preferred_element_type=F32)
    a1, a2 = shifted(a)
    cw = cw_ref[...]
    c = cb_ref[...] + a2 * cw[0:1] + a1 * cw[1:2] + a * cw[2:3]
    act = (c * jax.nn.sigmoid(c)) * gate_in
    part = jnp.dot(act.astype(BF), wd_ref[...], preferred_element_type=F32)

    @pl.when(j == 0)
    def _():
        acc_ref[...] = part

    @pl.when(j > 0)
    def _():
        acc_ref[...] += part

    @pl.when(j == pl.num_programs(1) - 1)
    def _():
        y_ref[...] = _layer_norm(alpha * h_ref[...] + acc_ref[...], g_ref[...], b_ref[...])
    return a


def _ffn_prompt_kernel(h_ref, wa_ref, wb_ref, cw_ref, cb_ref, wd_ref, g_ref, b_ref,
                       y_ref, tail_ref, acc_ref, hb_ref, halo_ref, *, tiles_per_seq, alpha):
    i = pl.program_id(0)
    j = pl.program_id(1)
    tm = h_ref.shape[0]
    seq_start = (i % tiles_per_seq) == 0

    @pl.when(seq_start)
    def _():
        halo_ref[j] = jnp.zeros(halo_ref.shape[1:], F32)

    def shifted(a):
        row = lax.broadcasted_iota(jnp.int32, a.shape, 0)
        halo = halo_ref[j]
        h1 = halo[SUBLANE - 1:SUBLANE]
        h2 = halo[SUBLANE - 2:SUBLANE - 1]
        a1 = jnp.where(row == 0, h1, pltpu.roll(a, 1, 0))
        a2 = jnp.where(row == 0, h2, jnp.where(row == 1, h1, pltpu.roll(a, 2, 0)))
        return a1, a2

    a = _ffn_core(h_ref, wa_ref, wb_ref, cw_ref, cb_ref, wd_ref, g_ref, b_ref, y_ref, acc_ref,
                  hb_ref, shifted, alpha)
    halo_ref[j] = a[tm - SUBLANE:]
    tail_ref[...] = a[tm - SUBLANE:]


def _ffn_sample_kernel(h_ref, wa_ref, wb_ref, cw_ref, cb_ref, wd_ref, g_ref, b_ref, f1_ref, f2_ref,
                       y_ref, a_ref, acc_ref, hb_ref, *, alpha):
    def shifted(a):
        s = lax.broadcasted_iota(jnp.int32, a.shape, 0) % SROW
        a1 = jnp.where(s >= 1, pltpu.roll(a, 1, 0), 0.0) + f1_ref[...]
        a2 = jnp.where(s >= 2, pltpu.roll(a, 2, 0), 0.0) + f2_ref[...]
        return a1, a2

    a_ref[...] = _ffn_core(h_ref, wa_ref, wb_ref, cw_ref, cb_ref, wd_ref, g_ref, b_ref, y_ref,
                           acc_ref, hb_ref, shifted, alpha)


def _ffn_specs(tm, d, tf, nf):
    return [
        pl.BlockSpec((tm, d), lambda i, j: (i, 0)),
        pl.BlockSpec((d, tf), lambda i, j: (0, j)),
        pl.BlockSpec((d, tf), lambda i, j: (0, nf + j)),
        pl.BlockSpec((SUBLANE, tf), lambda i, j: (0, j)),
        pl.BlockSpec((1, tf), lambda i, j: (0, j)),
        pl.BlockSpec((tf, d), lambda i, j: (j, 0)),
        pl.BlockSpec((1, d), lambda i, j: (0, 0)),
        pl.BlockSpec((1, d), lambda i, j: (0, 0)),
    ]


def _ffn_prompt(h2d, fw, seq, tm, tf, alpha):
    w_up, conv_w8, conv_b, w_down, g, b = fw
    m, d = h2d.shape
    dff = w_down.shape[0]
    nf = dff // tf
    return pl.pallas_call(
        functools.partial(_ffn_prompt_kernel, tiles_per_seq=seq // tm, alpha=alpha),
        grid=(m // tm, nf),
        in_specs=_ffn_specs(tm, d, tf, nf),
        out_specs=[pl.BlockSpec((tm, d), lambda i, j: (i, 0)),
                   pl.BlockSpec((None, SUBLANE, tf), lambda i, j: (i, 0, j))],
        out_shape=[jax.ShapeDtypeStruct((m, d), F32),
                   jax.ShapeDtypeStruct((m // tm, SUBLANE, dff), F32)],
        scratch_shapes=[pltpu.VMEM((tm, d), F32), pltpu.VMEM((tm, d), BF),
                        pltpu.VMEM((nf, SUBLANE, tf), F32)],
        compiler_params=_cparams(("arbitrary", "arbitrary")),
        name="conv_ffn_prompt",
    )(h2d, w_up, w_up, conv_w8, conv_b, w_down, g, b)


def _ffn_sample(h2d, fw, fill1, fill2, tf, alpha):
    w_up, conv_w8, conv_b, w_down, g, b = fw
    m, d = h2d.shape
    dff = w_down.shape[0]
    nf = dff // tf
    return pl.pallas_call(
        functools.partial(_ffn_sample_kernel, alpha=alpha),
        grid=(1, nf),
        in_specs=_ffn_specs(m, d, tf, nf) + [pl.BlockSpec((m, tf), lambda i, j: (0, j)),
                                             pl.BlockSpec((m, tf), lambda i, j: (0, j))],
        out_specs=[pl.BlockSpec((m, d), lambda i, j: (0, 0)),
                   pl.BlockSpec((m, tf), lambda i, j: (0, j))],
        out_shape=[jax.ShapeDtypeStruct((m, d), F32), jax.ShapeDtypeStruct((m, dff), F32)],
        scratch_shapes=[pltpu.VMEM((m, d), F32), pltpu.VMEM((m, d), BF)],
        compiler_params=_cparams(("arbitrary", "arbitrary")),
        name="conv_ffn_sample",
    )(h2d, w_up, w_up, conv_w8, conv_b, w_down, g, b, fill1, fill2)


def _nsa_sample_a_kernel(qraw_ref, qrot_ref, kc_ref, vc_ref, swa_ref, kwn_ref, vwn_ref,
                         gate_ref, part_ref, sel_ref, *, past, n_blocks):
    rep = NSA_REP
    g = pl.program_id(1)
    nc = kc_ref.shape[0]
    nsp = -(-n_blocks // LANE) * LANE
    s_col = lax.broadcasted_iota(jnp.int32, (SROW, 1), 0) + past

    q_raw = _stack_heads(qraw_ref, rep)
    s_c = (_dot_nt(q_raw, kc_ref[...]) * ATTN_SCALE).reshape(rep, SROW, nc)
    n_row = lax.broadcasted_iota(jnp.int32, (1, nc), 1)
    vis = (n_row >= 1) & (n_row * CMP_STRIDE + (CMP_BLOCK - CMP_STRIDE - 1) <= s_col)
    p_c = _masked_softmax(s_c, vis[None])
    o_cmp = _dot(p_c.reshape(rep * SROW, nc), vc_ref[...])
    imp = _dot_exact01(jnp.sum(p_c, 0), _overlap_matrix(nc, nsp))
    score = _block_scores(imp, s_col, n_blocks)
    rank = _topk_rank(score, n_blocks)
    col1 = lax.broadcasted_iota(jnp.int32, score.shape, 1).astype(F32) + 1.0
    lane = lax.broadcasted_iota(jnp.int32, (SROW, LANE), 1)
    ids = jnp.full((SROW, LANE), -1.0, F32)
    for r in range(min(SLC_TOPK, n_blocks)):
        hit = jnp.where(rank == r, jnp.where(score > 0.5 * NEG_INF, col1, 0.0), 0.0)
        ids = jnp.where(lane == r, jnp.sum(hit, -1, keepdims=True) - 1.0, ids)
    sel_ref[...] = ids.astype(jnp.int32)

    q_rot = _stack_heads(qrot_ref, rep)
    wbuf = swa_ref.shape[0] // N_KV_SLOTS
    keys = jnp.concatenate([swa_ref[pl.ds(g, wbuf, stride=N_KV_SLOTS), :], kwn_ref[...]], 0)
    vals = jnp.concatenate([swa_ref[pl.ds(NSA_KV_HEADS + g, wbuf, stride=N_KV_SLOTS), :],
                            vwn_ref[...]], 0)
    w_pos = past - wbuf + lax.broadcasted_iota(jnp.int32, (1, wbuf + SROW), 1)
    dist = s_col - w_pos
    mask = (dist >= 0) & (dist < WINDOW) & (w_pos >= 0)
    s_w = (_dot_nt(q_rot, keys) * ATTN_SCALE).reshape(rep, SROW, wbuf + SROW)
    p_w = _masked_softmax(s_w, mask[None])
    o_swa = _dot(p_w.reshape(rep * SROW, wbuf + SROW), vals)

    gt = gate_ref[...]
    for r in range(rep):
        rows = slice(r * SROW, (r + 1) * SROW)
        part_ref[rows, :] = (gt[:, 3 * r:3 * r + 1] * o_cmp[rows]
                             + gt[:, 3 * r + 2:3 * r + 3] * o_swa[rows])


def _nsa_sample_a(p_s, qrot_s, kc_all, swa_rows, sig_s, past, n_blocks):
    dec_batch, nc, _ = kc_all.shape
    gw = NSA_REP * HEAD_DIM
    bg = lambda b, g: (b, g)
    return pl.pallas_call(
        functools.partial(_nsa_sample_a_kernel, past=past, n_blocks=n_blocks),
        grid=(dec_batch, NSA_KV_HEADS),
        in_specs=[
            pl.BlockSpec((SROW, gw), bg),
            pl.BlockSpec((SROW, gw), bg),
            pl.BlockSpec((None, nc, HEAD_DIM), lambda b, g: (b, 0, g)),
            pl.BlockSpec((None, nc, HEAD_DIM), lambda b, g: (b, 0, NSA_KV_HEADS + g)),
            pl.BlockSpec((None, swa_rows.shape[1], HEAD_DIM), lambda b, g: (b, 0, 0)),
            pl.BlockSpec((SROW, HEAD_DIM), lambda b, g: (b, COL_KVW + g)),
            pl.BlockSpec((SROW, HEAD_DIM), lambda b, g: (b, COL_KVW + NSA_KV_HEADS + g)),
            pl.BlockSpec((SROW, LANE), bg),
        ],
        out_specs=[pl.BlockSpec((None, None, NSA_REP * SROW, HEAD_DIM), lambda b, g: (b, g, 0, 0)),
                   pl.BlockSpec((None, None, SROW, LANE), lambda b, g: (b, g, 0, 0))],
        out_shape=[jax.ShapeDtypeStruct((dec_batch, NSA_KV_HEADS, NSA_REP * SROW, HEAD_DIM), F32),
                   jax.ShapeDtypeStruct((dec_batch, NSA_KV_HEADS, SROW, LANE), jnp.int32)],
        compiler_params=_cparams(("arbitrary", "arbitrary")),
        name="nsa_sample_cmp_swa",
    )(p_s, qrot_s, kc_all, kc_all, swa_rows, p_s, p_s, sig_s)


def _nsa_sample_gather_kernel(pt_ref, ix_ref, *refs, n_blocks, s_len):
    del pt_ref, ix_ref
    n_sel = min(SLC_TOPK, n_blocks)
    n_fetch = s_len * n_sel
    blocks = refs[:n_fetch]
    (qrot_ref, ids_ref, knew_ref, vnew_ref, gate_ref, part_ref, o_ref) = refs[n_fetch:]
    g = pl.program_id(1)
    rep = NSA_REP
    n_keys = n_fetch * SLC_BLOCK

    def block_rows(slot):
        return jnp.concatenate([r[pl.ds(slot, SLC_BLOCK, stride=N_KV_SLOTS), :].astype(BF)
                                for r in blocks], 0)

    q = _stack_heads(qrot_ref, rep)
    s_cache = _dot_nt(q, block_rows(g)) * ATTN_SCALE

    ids = ids_ref[...]
    cached = jnp.where((ids >= 0) & (ids < n_blocks - 1), 1.0, 0.0).astype(BF)
    rank_row = lax.broadcasted_iota(jnp.int32, (LANE, n_keys), 0)
    key_col = lax.broadcasted_iota(jnp.int32, (LANE, n_keys), 1)
    by_rank = jnp.where(rank_row == (key_col // SLC_BLOCK) % n_sel, 1.0, 0.0).astype(BF)
    tok = lax.broadcasted_iota(jnp.int32, (SROW, n_keys), 0)
    owner = lax.broadcasted_iota(jnp.int32, (SROW, n_keys), 1) // (SLC_BLOCK * n_sel)
    keep_cache = jnp.where(owner == tok, jnp.dot(cached, by_rank, preferred_element_type=F32), 0.0)

    s_q = lax.broadcasted_iota(jnp.int32, (SROW, LANE), 0)
    s_k = lax.broadcasted_iota(jnp.int32, (SROW, LANE), 1)
    newest = jnp.sum(jnp.where(ids == n_blocks - 1, 1.0, 0.0), -1, keepdims=True)
    keep_new = jnp.where((s_k <= s_q) & (s_k < SROW), newest, 0.0)
    pad = jnp.zeros((LANE - SROW, HEAD_DIM), F32)
    s_new = _dot_nt(q, jnp.concatenate([knew_ref[...], pad], 0)) * ATTN_SCALE

    over_heads = lambda x: jnp.concatenate([x] * rep, 0) > 0.5
    mask_c, mask_n = over_heads(keep_cache), over_heads(keep_new)
    s_cache = jnp.where(mask_c, s_cache, NEG_INF)
    s_new = jnp.where(mask_n, s_new, NEG_INF)
    m = jnp.maximum(jnp.max(s_cache, -1, keepdims=True), jnp.max(s_new, -1, keepdims=True))
    e_c = jnp.where(mask_c, jnp.exp(s_cache - m), 0.0)
    e_n = jnp.where(mask_n, jnp.exp(s_new - m), 0.0)
    denom = jnp.maximum(jnp.sum(e_c, -1, keepdims=True) + jnp.sum(e_n, -1, keepdims=True), 1e-30)
    o_slc = (_dot(e_c, block_rows(NSA_KV_HEADS + g))
             + _dot(e_n, jnp.concatenate([vnew_ref[...], pad], 0))) / denom
    gt = gate_ref[...]
    for r in range(rep):
        rows = slice(r * SROW, (r + 1) * SROW)
        o_ref[:, r * HEAD_DIM:(r + 1) * HEAD_DIM] = (
            part_ref[rows, :] + gt[:, 3 * r + 1:3 * r + 2] * o_slc[rows])


def _nsa_sample_gather(cache_rows, page_table, ids, p_s, qrot_s, sig_s, part, n_blocks, s_len):
    n_pool, page_rows, _ = cache_rows.shape
    dec_batch, n_pages = page_table.shape
    n_sel = min(SLC_TOPK, n_blocks)
    per_page = page_rows // (N_KV_SLOTS * SLC_BLOCK)
    half_rows = N_KV_SLOTS * SLC_BLOCK
    gw = NSA_REP * HEAD_DIM
    flat_ids = ids[:, :, :s_len, :n_sel].reshape(dec_batch, NSA_KV_HEADS * s_len * n_sel)
    pool_page = jnp.take_along_axis(page_table, jnp.clip(flat_ids // per_page, 0, n_pages - 1), 1)
    in_page = jnp.maximum(flat_ids, 0) % per_page

    def fetch_spec(n):
        def index(b, g, pp, ip):
            return (pp[b, g * (s_len * n_sel) + n], ip[b, g * (s_len * n_sel) + n], 0)
        return pl.BlockSpec((None, half_rows, HEAD_DIM), index)

    bg = lambda b, g, pt, ix: (b, g)
    return pl.pallas_call(
        functools.partial(_nsa_sample_gather_kernel, n_blocks=n_blocks, s_len=s_len),
        grid_spec=pltpu.PrefetchScalarGridSpec(
            num_scalar_prefetch=2,
            grid=(dec_batch, NSA_KV_HEADS),
            in_specs=[fetch_spec(n) for n in range(s_len * n_sel)] + [
                pl.BlockSpec((SROW, gw), bg),
                pl.BlockSpec((None, None, SROW, LANE), lambda b, g, pt, ix: (b, g, 0, 0)),
                pl.BlockSpec((SROW, HEAD_DIM), lambda b, g, pt, ix: (b, COL_KVS + g)),
                pl.BlockSpec((SROW, HEAD_DIM), lambda b, g, pt, ix: (b, COL_KVS + NSA_KV_HEADS + g)),
                pl.BlockSpec((SROW, LANE), bg),
                pl.BlockSpec((None, None, NSA_REP * SROW, HEAD_DIM), lambda b, g, pt, ix: (b, g, 0, 0)),
            ],
            out_specs=pl.BlockSpec((SROW, gw), bg),
        ),
        out_shape=jax.ShapeDtypeStruct((dec_batch * SROW, NSA_HEADS * HEAD_DIM), F32),
        compiler_params=_cparams(("arbitrary", "arbitrary")),
        name="nsa_sample_selected",
    )(pool_page, in_page, *([cache_rows] * (s_len * n_sel)), qrot_s, ids, p_s, p_s, sig_s, part)


def _fox_sample_kernel(pt_ref, *refs, s_len):
    n = PAGES_PER_STEP
    kpages, vpages = refs[:n], refs[n:2 * n]
    (logf_ref, q_ref, knew_ref, vnew_ref, lnew_ref,
     o_ref, m_ref, l_ref, acc_ref, carry_ref) = refs[2 * n:]
    b = pl.program_id(0)
    c = pl.program_id(1)
    page = kpages[0].shape[0]
    head = lambda ref, h: ref[:, h * HEAD_DIM:(h + 1) * HEAD_DIM]

    def cache_rows(pages, h):
        return jnp.concatenate([_head_rows(p, h).astype(BF) for p in pages], 0)

    def update(scores, mask, values):
        s = jnp.concatenate(scores, 0)
        if mask is not None:
            s = jnp.where(mask, s, NEG_INF)
        m_old = m_ref[...]
        m_new = jnp.maximum(m_old, jnp.max(s, -1, keepdims=True))
        p = jnp.exp(s - m_new)
        if mask is not None:
            p = jnp.where(mask, p, 0.0)
        alpha = jnp.exp(m_old - m_new)
        pv = jnp.concatenate([_dot(p[h * SROW:(h + 1) * SROW], values[h])
                              for h in range(FOX_HEADS)], 0)
        m_ref[...] = m_new
        l_ref[...] = alpha * l_ref[...] + jnp.sum(p, -1, keepdims=True)
        acc_ref[...] = alpha * acc_ref[...] + pv

    @pl.when(c == 0)
    def _():
        m_ref[...] = jnp.full(m_ref.shape, NEG_INF, F32)
        l_ref[...] = jnp.zeros(l_ref.shape, F32)
        acc_ref[...] = jnp.zeros(acc_ref.shape, F32)
        s_q = lax.broadcasted_iota(jnp.int32, (SROW, LANE), 0)
        s_k = lax.broadcasted_iota(jnp.int32, (SROW, LANE), 1)
        later = jnp.where((s_q > s_k) & (s_q < s_len), 1.0, 0.0)
        real = lax.broadcasted_iota(jnp.int32, (SROW, 1), 0) < s_len
        lnew = lnew_ref[...]
        pad = jnp.zeros((LANE - SROW, HEAD_DIM), F32)
        rows_q = lax.broadcasted_iota(jnp.int32, (FOX_HEADS * SROW, LANE), 0) % SROW
        cols_k = lax.broadcasted_iota(jnp.int32, (FOX_HEADS * SROW, LANE), 1)
        mask = (cols_k <= rows_q) & (cols_k < s_len)
        totals, scores, values = [], [], []
        for h in range(FOX_HEADS):
            lf = jnp.where(real, lnew[:, h:h + 1], 0.0)
            d_row = jnp.sum(lf * later, 0, keepdims=True)
            totals.append(jnp.sum(lf, 0, keepdims=True))
            k = jnp.concatenate([head(knew_ref, h), pad], 0)
            values.append(jnp.concatenate([head(vnew_ref, h), pad], 0))
            scores.append(_dot_nt(head(q_ref, h), k) * ATTN_SCALE + d_row)
        update(scores, mask, values)
        totals += [jnp.zeros((1, 1), F32)] * (SUBLANE - FOX_HEADS)
        carry_ref[...] = jnp.broadcast_to(jnp.concatenate(totals, 0), carry_ref.shape)

    @pl.when(c > 0)
    def _():
        ids = [pt_ref[b, c * n + i] for i in range(n)]
        lf = jnp.concatenate(
            [jnp.concatenate([logf_ref[h, pl.ds(pg, 1), :] for pg in ids], 1) for h in range(FOX_HEADS)]
            + [jnp.zeros((SUBLANE - FOX_HEADS, n * page), F32)], 0)
        incl = _suffix_sum_lanes(lf)
        carry = carry_ref[:, 0:1]
        decay = carry + (incl - lf)
        carry_ref[...] = jnp.broadcast_to(carry + incl[:, 0:1], carry_ref.shape)
        scores = [_dot_nt(head(q_ref, h), cache_rows(kpages, h)) * ATTN_SCALE + decay[h:h + 1, :]
                  for h in range(FOX_HEADS)]
        update(scores, None, [cache_rows(vpages, h) for h in range(FOX_HEADS)])

    @pl.when(c == pl.num_programs(1) - 1)
    def _():
        o = acc_ref[...] / jnp.maximum(l_ref[...], 1e-30)
        for h in range(FOX_HEADS):
            o_ref[:, h * HEAD_DIM:(h + 1) * HEAD_DIM] = o[h * SROW:(h + 1) * SROW]


def _fox_sample(cache, logf_t, page_table, p_s, lsig_s, s_len):
    n_pool, page = cache.shape[:2]
    dec_batch, n_pages = page_table.shape
    nchunk = n_pages // PAGES_PER_STEP
    fw = FOX_HEADS * HEAD_DIM
    row = lambda w, col: pl.BlockSpec((SROW, w), lambda b, c, pt: (b, col))
    chunks = page_table.reshape(dec_batch, nchunk, PAGES_PER_STEP)[:, ::-1]
    steps = jnp.concatenate([chunks[:, :1], chunks], 1).reshape(dec_batch, (nchunk + 1) * PAGES_PER_STEP)

    def half_specs(kv):
        def spec(i):
            return pl.BlockSpec((None, page, None, FOX_HEADS, HEAD_DIM),
                                lambda b, c, pt: (pt[b, c * PAGES_PER_STEP + i], 0, kv, 0, 0))
        return [spec(i) for i in range(PAGES_PER_STEP)]

    return pl.pallas_call(
        functools.partial(_fox_sample_kernel, s_len=s_len),
        grid_spec=pltpu.PrefetchScalarGridSpec(
            num_scalar_prefetch=1,
            grid=(dec_batch, nchunk + 1),
            in_specs=half_specs(0) + half_specs(1) + [
                pl.BlockSpec(logf_t.shape, lambda b, c, pt: (0, 0, 0), pipeline_mode=pl.Buffered(1)),
                row(fw, COL_FOX * LANE // fw),
                row(fw, COL_FOX * LANE // fw + 1),
                row(fw, COL_FOX * LANE // fw + 2),
                row(LANE, 0),
            ],
            out_specs=row(fw, 0),
            scratch_shapes=[pltpu.VMEM((FOX_HEADS * SROW, 1), F32),
                            pltpu.VMEM((FOX_HEADS * SROW, 1), F32),
                            pltpu.VMEM((FOX_HEADS * SROW, HEAD_DIM), F32),
                            pltpu.VMEM((SUBLANE, LANE), F32)],
        ),
        out_shape=jax.ShapeDtypeStruct((dec_batch * SROW, fw), F32),
        compiler_params=_cparams(("arbitrary", "arbitrary")),
        name="fox_attention_sample",
    )(steps, *([cache] * (2 * PAGES_PER_STEP)), logf_t, p_s, p_s, p_s, lsig_s)


def _rope_tables(pos):
    half = ROT_DIM // 2
    inv = jnp.power(ROPE_THETA, -jnp.arange(half, dtype=F32) * (2.0 / ROT_DIM))
    ang = pos.astype(F32)[:, None] * inv[None, :]
    cos, sin = jnp.cos(ang), jnp.sin(ang)
    n = pos.shape[0]
    c = jnp.concatenate([cos, cos, jnp.ones((n, HEAD_DIM - ROT_DIM), F32)], 1)
    s = jnp.concatenate([-sin, sin, jnp.zeros((n, HEAD_DIM - ROT_DIM), F32)], 1)
    return c, s


def _layer_weights(w_in, b_gate, b_forget, cmp_w1, cmp_b1, cmp_w2, cmp_pos, w_mem_kv, w_out,
                   ln1_g, ln1_b, w_up, conv_w, conv_b, w_down, ln2_g, ln2_b):
    d = w_in.shape[0]
    nq = NSA_HEADS * HEAD_DIM
    kvw = 2 * NSA_KV_HEADS * HEAD_DIM
    o_g = nq + 3 * kvw
    o_fox = o_g + 3 * NSA_HEADS
    o_f = o_fox + 3 * FOX_HEADS * HEAD_DIM
    o_qm = o_f + FOX_HEADS
    w_main = jnp.concatenate([w_in[:, :o_g], w_in[:, o_fox:o_f], w_in[:, o_qm:]], 1).astype(BF)
    per_group = 3 * NSA_REP
    zpad = lambda n: jnp.zeros((d, n), w_in.dtype)
    w_small = jnp.concatenate([
        w_in[:, o_g:o_g + per_group], zpad(LANE - per_group),
        w_in[:, o_g + per_group:o_fox], zpad(LANE - per_group),
        w_in[:, o_f:o_qm], zpad(LANE - FOX_HEADS)], 1).astype(BF)
    bpad = lambda n: jnp.zeros((n,), F32)
    b_small = jnp.concatenate([
        b_gate[:per_group], bpad(LANE - per_group), b_gate[per_group:], bpad(LANE - per_group),
        b_forget, bpad(LANE - FOX_HEADS)])[None, :].astype(F32)
    kdim = CMP_STRIDE * HEAD_DIM
    w1cat = jnp.concatenate([cmp_w1[:, :CMP_STRIDE].reshape(2, kdim, CMP_HIDDEN),
                             cmp_w1[:, CMP_STRIDE:].reshape(2, kdim, CMP_HIDDEN)], 2).astype(BF)
    pe = jnp.concatenate([cmp_pos[:, :CMP_STRIDE].reshape(2, 1, kdim),
                          cmp_pos[:, CMP_STRIDE:].reshape(2, 1, kdim),
                          jnp.zeros((2, SUBLANE - 2, kdim), F32)], 1)
    cw = (w1cat, pe, cmp_b1[:, None, :], cmp_w2.astype(BF))
    conv_w8 = jnp.concatenate([conv_w, jnp.zeros((SUBLANE - CONV_W, conv_w.shape[1]), F32)], 0)
    fw = (w_up.astype(BF), conv_w8, conv_b[None, :], w_down.astype(BF), ln2_g[None, :], ln2_b[None, :])
    return dict(w_main=w_main, w_small=w_small, b_small=b_small, cw=cw,
                w_mem_kv=w_mem_kv.astype(BF), w_out=w_out.astype(BF),
                ln1_g=ln1_g[None, :], ln1_b=ln1_b[None, :], fw=fw)


def _prompt_layer(x, mem, lw, alpha):
    batch, seq, d = x.shape
    x2d = x.reshape(batch * seq, d)
    rc, rs = _rope_tables(jnp.tile(jnp.arange(seq, dtype=jnp.int32), batch))
    tm = min(512, seq)
    p_main, q_rot, sig, lsig, st_c, st_s, st_w, st_f = _input_projection(
        x2d, lw["w_main"], lw["w_small"], lw["b_small"], rc, rs, tm)
    kc_all = _compress_prompt(p_main, batch, seq, lw["cw"])
    o_nsa = _nsa_prompt(p_main, q_rot, kc_all, sig, batch, seq)
    decay = _decay_prompt(lsig, batch, seq)
    o_fox = _fox_prompt(p_main, decay, batch, seq)
    mlen = mem.shape[1]
    mem_kv = _matmul(mem.reshape(batch * mlen, d), lw["w_mem_kv"], min(256, batch * mlen), 512)
    o_mem = _mem_attention(p_main, mem_kv, batch, seq, min(512, seq))
    h = _out_projection(o_nsa, o_fox, o_mem, x2d, lw["w_out"], lw["ln1_g"], lw["ln1_b"],
                        min(256, seq), alpha)
    y, tail = _ffn_prompt(h, lw["fw"], seq, tm, 512, alpha)
    kv_state = lambda st: st.reshape(batch, seq, 2, -1, HEAD_DIM)
    n_win = min(WINDOW, seq)
    tiles = seq // tm
    conv_state = tail.reshape(batch, tiles, SUBLANE, -1)[:, -1, SUBLANE - (CONV_W - 1):]
    states = (kv_state(st_c), kv_state(st_s), kv_state(st_f),
              lsig[:, :FOX_HEADS].reshape(batch, seq, FOX_HEADS),
              kv_state(st_w)[:, seq - n_win:],
              mem_kv.reshape(batch, mlen, 2, MEM_HEADS, HEAD_DIM), conv_state)
    return y.reshape(batch, seq, d), states


def _sample_layer(x, c_cmp, c_slc, c_fox, c_logf, c_swa, c_mem, s_conv, page_table, lw, alpha):
    dec_batch, s_len, d = x.shape
    n_pool, page = c_cmp.shape[0], c_cmp.shape[1]
    n_pages = page_table.shape[1]
    past = n_pages * page
    t_all = past + s_len
    assert s_len <= SROW and n_pages % PAGES_PER_STEP == 0
    assert (t_all // CMP_STRIDE) * CMP_STRIDE == past and past % SLC_BLOCK == 0
    n_blocks = -(-t_all // SLC_BLOCK)
    xp = jnp.pad(x, ((0, 0), (0, SROW - s_len), (0, 0))).reshape(dec_batch * SROW, d)
    pos = past + jnp.tile(jnp.arange(SROW, dtype=jnp.int32), dec_batch)
    rc, rs = _rope_tables(pos)
    p_s, qrot_s, sig_s, lsig_s = _input_projection(xp, lw["w_main"], lw["w_small"], lw["b_small"],
                                                   rc, rs, dec_batch * SROW)[:4]
    kvw = 2 * NSA_KV_HEADS * HEAD_DIM
    as_rows = lambda a: a.reshape(a.shape[0], -1, HEAD_DIM)
    kc_all = _compress_paged(as_rows(c_cmp), page_table, lw["cw"])
    part, ids = _nsa_sample_a(p_s, qrot_s, kc_all, as_rows(c_swa), sig_s, past, n_blocks)
    o_nsa = _nsa_sample_gather(as_rows(c_slc), page_table, ids, p_s, qrot_s, sig_s, part,
                               n_blocks, s_len)
    logf_t = jnp.transpose(c_logf, (2, 0, 1))
    o_fox = _fox_sample(c_fox, logf_t, page_table, p_s, lsig_s, s_len)
    o_mem = _mem_attention_rows(p_s, as_rows(c_mem))
    h = _out_projection(o_nsa, o_fox, o_mem, xp, lw["w_out"], lw["ln1_g"], lw["ln1_b"],
                        dec_batch * SROW, alpha)
    dff = s_conv.shape[-1]
    zrow = jnp.zeros((dec_batch, 1, dff), F32)
    fill1 = jnp.concatenate([s_conv[:, 1:2]] + [zrow] * (SROW - 1), 1).reshape(dec_batch * SROW, dff)
    fill2 = jnp.concatenate([s_conv[:, 0:1], s_conv[:, 1:2]] + [zrow] * (SROW - 2), 1)
    y, a = _ffn_sample(h, lw["fw"], fill1, fill2.reshape(dec_batch * SROW, dff), 512, alpha)

    def rows(arr):
        return arr.reshape(dec_batch, SROW, -1)[:, :s_len]
    cols = lambda c0, w: rows(p_s[:, c0 * LANE:c0 * LANE + w]).reshape(dec_batch, s_len, 2, -1, HEAD_DIM)
    new_kv_swa = cols(COL_KVW, kvw)
    new_swa = jnp.concatenate([c_swa, new_kv_swa], 1)[:, s_len:]
    conv_state = jnp.concatenate([s_conv, rows(a)], 1)[:, -(CONV_W - 1):]
    states = (cols(COL_KVC, kvw), cols(COL_KVS, kvw),
              cols(COL_FOX + FOX_HEADS, 2 * FOX_HEADS * HEAD_DIM),
              rows(lsig_s[:, :FOX_HEADS]), new_swa, conv_state)
    return rows(y), states


def kernel(x_prompt, x_sample, mem_prompt, cache_nsa_cmp, cache_nsa_slc, cache_fox_kv, cache_fox_logf, cache_nsa_swa, cache_mem, state_conv, page_table, w_in, b_gate, b_forget, cmp_w1, cmp_b1, cmp_w2, cmp_pos, w_mem_kv, w_out, ln1_g, ln1_b, w_up, conv_w, conv_b, w_down, ln2_g, ln2_b):
    depth = w_in.shape[0]
    alpha = float((2 * depth) ** 0.25)
    yp, ys = x_prompt, x_sample
    acc_p = [[] for _ in range(7)]
    acc_s = [[] for _ in range(6)]
    for l in range(depth):
        lw = _layer_weights(w_in[l], b_gate[l], b_forget[l], cmp_w1[l], cmp_b1[l], cmp_w2[l],
                            cmp_pos[l], w_mem_kv[l], w_out[l], ln1_g[l], ln1_b[l], w_up[l],
                            conv_w[l], conv_b[l], w_down[l], ln2_g[l], ln2_b[l])
        yp, st_p = _prompt_layer(yp, mem_prompt, lw, alpha)
        ys, st_s = _sample_layer(ys, cache_nsa_cmp[l], cache_nsa_slc[l], cache_fox_kv[l],
                                 cache_fox_logf[l], cache_nsa_swa[l], cache_mem[l], state_conv[l],
                                 page_table, lw, alpha)
        for lst, a in zip(acc_p, st_p):
            lst.append(a)
        for lst, a in zip(acc_s, st_s):
            lst.append(a)
    outs_p = [jnp.stack(a, 0) for a in acc_p]
    outs_s = [jnp.stack(a, 0) for a in acc_s]
    return (yp, ys, *outs_p, *outs_s)
```

```python
import functools

import jax
import jax.numpy as jnp
import numpy as np
from jax import lax
from jax.experimental import pallas as pl
from jax.experimental.pallas import tpu as pltpu

HEAD_DIM = 128
NSA_KV_HEADS = 2
NSA_REP = 4
NSA_HEADS = NSA_KV_HEADS * NSA_REP
FOX_HEADS = 4
MEM_HEADS = 4
CMP_BLOCK = 32
CMP_STRIDE = 16
CMP_HIDDEN = 256
SLC_BLOCK = 64
SLC_TOPK = 16
WINDOW = 512
Q_BLOCK = 128
ROT_DIM = HEAD_DIM // 4
ROPE_THETA = 500000.0
CONV_W = 3
LN_EPS = 1e-5
ATTN_SCALE = HEAD_DIM ** -0.5
LOG2E = 1.4426950408889634
NEG_INF = -1e30
FORCE_BONUS = 1e3

LANE = 128
SUBLANE = 8
SROW = SUBLANE
PAGES_PER_STEP = 32
CMP_PAGES_PER_STEP = 32
VMEM_LIMIT = 56 * 1024 * 1024

COL_Q = 0
COL_KVC = 8
COL_KVS = 12
COL_KVW = 16
COL_FOX = 20
COL_QM = 32
MAIN_W = 36 * LANE
IN_TILE = 512

BF = jnp.bfloat16
F32 = jnp.float32


def _dot(a, b):
    return jnp.dot(a.astype(BF), b.astype(BF), preferred_element_type=F32)


def _dot_nt(a, b):
    return lax.dot_general(a.astype(BF), b.astype(BF), (((1,), (1,)), ((), ())),
                           preferred_element_type=F32)


def _dot_exact01(x, m01, m01_first=False):
    hi = x.astype(BF)
    r1 = x - hi.astype(F32)
    mid = r1.astype(BF)
    lo = (r1 - mid.astype(F32)).astype(BF)
    m = m01.astype(BF)
    if m01_first:
        d = lambda a: lax.dot_general(m, a, (((1,), (1,)), ((), ())), preferred_element_type=F32)
    else:
        d = lambda a: jnp.dot(a, m, preferred_element_type=F32)
    return d(hi) + d(mid) + d(lo)


def _cparams(sem):
    return pltpu.CompilerParams(dimension_semantics=sem, vmem_limit_bytes=VMEM_LIMIT)


def _masked_softmax(s, mask):
    s = jnp.where(mask, s, NEG_INF)
    m = jnp.max(s, -1, keepdims=True)
    e = jnp.where(mask, jnp.exp(s - m), 0.0)
    return e / jnp.maximum(jnp.sum(e, -1, keepdims=True), 1e-30)


def _flash_update(carry, s2, v):
    m, l, acc = carry
    m_new = jnp.maximum(m, jnp.max(s2, -1, keepdims=True))
    p = jnp.exp2(s2 - m_new)
    alpha = jnp.exp2(m - m_new)
    return (m_new, alpha * l + jnp.sum(p, -1, keepdims=True),
            alpha * acc + jnp.dot(p.astype(BF), v, preferred_element_type=F32))


def _rope_tile(x, c, s):
    lane = lax.broadcasted_iota(jnp.int32, x.shape, 1)
    half = ROT_DIM // 2
    swapped = jnp.where(lane < half, pltpu.roll(x, LANE - half, 1), pltpu.roll(x, half, 1))
    return x * c + swapped * s


def _topk_rank(score, n_valid_cols):
    col = lax.broadcasted_iota(jnp.int32, score.shape, 1)
    rank = jnp.zeros(score.shape, F32)
    for i in range(n_valid_cols):
        ci = score[:, i:i + 1]
        rank = rank + jnp.where(col > i, jnp.where(ci >= score, 1.0, 0.0),
                                jnp.where(ci > score, 1.0, 0.0))
    return rank


def _topk_select_rows(score_t, n_sel):
    n, width = score_t.shape
    rank = jnp.zeros(score_t.shape, F32)
    for i in range(n):
        ci = jnp.broadcast_to(score_t[i:i + 1, :], (SUBLANE, width))
        parts = []
        for j0 in range(0, n, SUBLANE):
            sj = score_t[j0:j0 + SUBLANE, :]
            if j0 > i:
                ahead = ci >= sj
            elif j0 + SUBLANE - 1 <= i:
                ahead = ci > sj
            else:
                row = lax.broadcasted_iota(jnp.int32, (SUBLANE, width), 0) + j0
                ahead = jnp.where(row > i, jnp.where(ci >= sj, 1.0, 0.0),
                                  jnp.where(ci > sj, 1.0, 0.0)) > 0.5
            parts.append(jnp.where(ahead, 1.0, 0.0))
        rank = rank + jnp.concatenate(parts, 0)
    return jnp.where(rank < n_sel, 1.0, 0.0)


def _inproj_kernel(x_ref, w_ref, ws_ref, bs_ref, c_ref, s_ref,
                   p_ref, qrot_ref, sig_ref, lsig_ref, stc_ref, sts_ref, stw_ref, stf_ref, xb_ref):
    j = pl.program_id(1)
    tm = x_ref.shape[0]

    def store_rows(ref, tile, first_slot, slots):
        for k in range(IN_TILE // LANE):
            ref[pl.ds(first_slot + k, tm, stride=slots), :] = tile[k]

    @pl.when(j == 0)
    def _():
        xb_ref[...] = x_ref[...].astype(BF)
        z = jnp.dot(xb_ref[...], ws_ref[...], preferred_element_type=F32) + bs_ref[...]
        sig_ref[...] = jax.nn.sigmoid(z[:, :2 * LANE])
        zf = z[:, 2 * LANE:]
        lsig_ref[...] = jnp.minimum(zf, 0.0) - jnp.log1p(jnp.exp(-jnp.abs(zf)))

    acc = jnp.dot(xb_ref[...], w_ref[...], preferred_element_type=F32)
    c = c_ref[...]
    s = s_ref[...]
    heads = IN_TILE // LANE

    @pl.when(j < 2)
    def _():
        p_ref[...] = acc
        for h in range(heads):
            qrot_ref[:, h * LANE:(h + 1) * LANE] = _rope_tile(acc[:, h * LANE:(h + 1) * LANE], c, s)

    lane_tiles = lambda a: [a[:, h * LANE:(h + 1) * LANE] for h in range(heads)]

    def roped_kv(st_ref):
        tiles = [_rope_tile(t, c, s) if h < NSA_KV_HEADS else t for h, t in enumerate(lane_tiles(acc))]
        for h, t in enumerate(tiles):
            p_ref[:, h * LANE:(h + 1) * LANE] = t
        store_rows(st_ref, tiles, 0, N_KV_SLOTS)

    @pl.when(j == COL_KVC // heads)
    def _():
        p_ref[...] = acc
        store_rows(stc_ref, lane_tiles(acc), 0, N_KV_SLOTS)

    @pl.when(j == COL_KVS // heads)
    def _():
        roped_kv(sts_ref)

    @pl.when(j == COL_KVW // heads)
    def _():
        roped_kv(stw_ref)

    fox_k = COL_FOX // heads + 1
    @pl.when((j == fox_k) | (j == fox_k + 1))
    def _():
        p_ref[...] = acc

    @pl.when(j == fox_k)
    def _():
        store_rows(stf_ref, lane_tiles(acc), 0, 2 * FOX_HEADS)

    @pl.when(j == fox_k + 1)
    def _():
        store_rows(stf_ref, lane_tiles(acc), FOX_HEADS, 2 * FOX_HEADS)

    @pl.when((j == COL_FOX // heads) | (j == COL_QM // heads))
    def _():
        p_ref[...] = acc


def _input_projection(x2d, w_main, w_small, b_small, rope_c, rope_s, tm):
    m, d = x2d.shape
    nj = MAIN_W // IN_TILE
    state_slots = (N_KV_SLOTS, N_KV_SLOTS, N_KV_SLOTS, 2 * FOX_HEADS)
    return pl.pallas_call(
        _inproj_kernel,
        grid=(m // tm, nj),
        in_specs=[
            pl.BlockSpec((tm, d), lambda i, j: (i, 0)),
            pl.BlockSpec((d, IN_TILE), lambda i, j: (0, j)),
            pl.BlockSpec((d, 3 * LANE), lambda i, j: (0, 0)),
            pl.BlockSpec((1, 3 * LANE), lambda i, j: (0, 0)),
            pl.BlockSpec((tm, LANE), lambda i, j: (i, 0)),
            pl.BlockSpec((tm, LANE), lambda i, j: (i, 0)),
        ],
        out_specs=[
            pl.BlockSpec((tm, IN_TILE), lambda i, j: (i, j)),
            pl.BlockSpec((tm, IN_TILE), lambda i, j: (i, jnp.minimum(j, 1))),
            pl.BlockSpec((tm, 2 * LANE), lambda i, j: (i, 0)),
            pl.BlockSpec((tm, LANE), lambda i, j: (i, 0)),
        ] + [pl.BlockSpec((tm * slots, LANE), lambda i, j: (i, 0)) for slots in state_slots],
        out_shape=[
            jax.ShapeDtypeStruct((m, MAIN_W), F32),
            jax.ShapeDtypeStruct((m, NSA_HEADS * HEAD_DIM), F32),
            jax.ShapeDtypeStruct((m, 2 * LANE), F32),
            jax.ShapeDtypeStruct((m, LANE), F32),
        ] + [jax.ShapeDtypeStruct((m * slots, LANE), F32) for slots in state_slots],
        scratch_shapes=[pltpu.VMEM((tm, d), BF)],
        compiler_params=_cparams(("arbitrary", "arbitrary")),
        name="input_projection",
    )(x2d, w_main, w_small, b_small, rope_c, rope_s)


def _gelu_tanh(x):
    k = np.sqrt(2.0 / np.pi).astype(np.float32)
    return x * (0.5 * (1.0 + jnp.tanh(k * (x + 0.044715 * (x ** 3)))))


def _compress_body(load, nb, w1_ref, pe_ref, b1_ref, w2_ref, out_ref, carry_ref, first_step):
    @pl.when(first_step)
    def _():
        carry_ref[...] = jnp.zeros(carry_ref.shape, F32)

    row = lax.broadcasted_iota(jnp.int32, (nb, CMP_HIDDEN), 0)
    for kv in range(2):
        xs = []
        for g in range(NSA_KV_HEADS):
            per_l = [load(kv * NSA_KV_HEADS + g, l) for l in range(CMP_STRIDE)]
            xs.append(jnp.concatenate(per_l, 1).astype(BF))
        xs.append(pe_ref[kv].astype(BF))
        fs = jnp.dot(jnp.concatenate(xs, 0), w1_ref[kv], preferred_element_type=F32)
        pos = (fs[2 * nb:2 * nb + 1, :CMP_HIDDEN] + fs[2 * nb + 1:2 * nb + 2, CMP_HIDDEN:]
               + b1_ref[kv])
        for g in range(NSA_KV_HEADS):
            col = (kv * NSA_KV_HEADS + g) * HEAD_DIM
            first = fs[g * nb:(g + 1) * nb, :CMP_HIDDEN]
            second = fs[g * nb:(g + 1) * nb, CMP_HIDDEN:]
            slot = kv * NSA_KV_HEADS + g
            prev = jnp.where(row == 0, carry_ref[slot], pltpu.roll(first, 1, 0))
            carry_ref[slot] = first[nb - 1:nb, :]
            h = _gelu_tanh(prev + second + pos)
            out_ref[:, col:col + HEAD_DIM] = jnp.dot(h.astype(BF), w2_ref[kv],
                                                     preferred_element_type=F32)


N_KV_SLOTS = 2 * NSA_KV_HEADS


def _compress_prompt_kernel(*refs):
    slots = refs[:N_KV_SLOTS]
    w1_ref, pe_ref, b1_ref, w2_ref, out_ref, carry_ref = refs[N_KV_SLOTS:]
    nb = slots[0].shape[0] // CMP_STRIDE

    def load(slot, l):
        return slots[slot][pl.ds(l, nb, stride=CMP_STRIDE), :]

    _compress_body(load, nb, w1_ref, pe_ref, b1_ref, w2_ref, out_ref, carry_ref,
                   pl.program_id(1) == 0)


def _compress_paged_kernel(pt_ref, *refs):
    del pt_ref
    pages = refs[:CMP_PAGES_PER_STEP]
    w1_ref, pe_ref, b1_ref, w2_ref, out_ref, carry_ref = refs[CMP_PAGES_PER_STEP:]
    group = N_KV_SLOTS * CMP_STRIDE
    per_page = pages[0].shape[0] // group
    blocks = jnp.concatenate([p[...].reshape(per_page, group, HEAD_DIM) for p in pages], 0)
    by_row = jnp.swapaxes(blocks, 0, 1)

    def load(slot, l):
        return by_row[l * N_KV_SLOTS + slot]

    _compress_body(load, per_page * CMP_PAGES_PER_STEP, w1_ref, pe_ref, b1_ref, w2_ref, out_ref,
                   carry_ref, pl.program_id(1) == 0)


def _compress_weight_specs():
    const3 = (lambda *a: (0, 0, 0))
    return [
        pl.BlockSpec((2, CMP_STRIDE * HEAD_DIM, 2 * CMP_HIDDEN), const3),
        pl.BlockSpec((2, SUBLANE, CMP_STRIDE * HEAD_DIM), const3),
        pl.BlockSpec((2, 1, CMP_HIDDEN), const3),
        pl.BlockSpec((2, CMP_HIDDEN, HEAD_DIM), const3),
    ]


def _compress_prompt(p_main, batch, seq, cw):
    chunk = min(seq, 2048)
    nchunk = seq // chunk
    kvw = 2 * NSA_KV_HEADS * HEAD_DIM
    return pl.pallas_call(
        _compress_prompt_kernel,
        grid=(batch, nchunk),
        in_specs=[pl.BlockSpec((chunk, HEAD_DIM), (lambda b, c, s=s: (b * nchunk + c, COL_KVC + s)))
                  for s in range(N_KV_SLOTS)] + _compress_weight_specs(),
        out_specs=pl.BlockSpec((None, chunk // CMP_STRIDE, kvw), lambda b, c: (b, c, 0)),
        out_shape=jax.ShapeDtypeStruct((batch, seq // CMP_STRIDE, kvw), F32),
        scratch_shapes=[pltpu.VMEM((2 * NSA_KV_HEADS, 1, CMP_HIDDEN), F32)],
        compiler_params=_cparams(("arbitrary", "arbitrary")),
        name="nsa_compress_prompt",
    )(*([p_main] * N_KV_SLOTS), *cw)


def _page_specs(page_rows, width, chunk_of, per_step):
    def spec(i):
        return pl.BlockSpec((None, page_rows, width),
                            lambda b, c, pt: (pt[b, chunk_of(c) * per_step + i], 0, 0))
    return [spec(i) for i in range(per_step)]


def _compress_paged(cache_rows, page_table, cw):
    n_pool, page_rows, _ = cache_rows.shape
    page = page_rows // N_KV_SLOTS
    kvw = N_KV_SLOTS * HEAD_DIM
    dec_batch, n_pages = page_table.shape
    nchunk = n_pages // CMP_PAGES_PER_STEP
    rows = CMP_PAGES_PER_STEP * page // CMP_STRIDE
    return pl.pallas_call(
        _compress_paged_kernel,
        grid_spec=pltpu.PrefetchScalarGridSpec(
            num_scalar_prefetch=1,
            grid=(dec_batch, nchunk),
            in_specs=(_page_specs(page_rows, HEAD_DIM, lambda c: c, CMP_PAGES_PER_STEP)
                      + _compress_weight_specs()),
            out_specs=pl.BlockSpec((None, rows, kvw), lambda b, c, pt: (b, c, 0)),
            scratch_shapes=[pltpu.VMEM((2 * NSA_KV_HEADS, 1, CMP_HIDDEN), F32)],
        ),
        out_shape=jax.ShapeDtypeStruct((dec_batch, nchunk * rows, kvw), F32),
        compiler_params=_cparams(("arbitrary", "arbitrary")),
        name="nsa_compress_paged",
    )(page_table, *([cache_rows] * CMP_PAGES_PER_STEP), *cw)


def _overlap_matrix(n_rows, n_blocks, blocks_first=False):
    shape = (n_blocks, n_rows) if blocks_first else (n_rows, n_blocks)
    n = lax.broadcasted_iota(jnp.int32, shape, 1 if blocks_first else 0)
    s = lax.broadcasted_iota(jnp.int32, shape, 0 if blocks_first else 1)
    c0 = (n - 1) * CMP_STRIDE
    hit = (n >= 1) & (c0 < s * SLC_BLOCK + SLC_BLOCK) & (c0 + CMP_BLOCK > s * SLC_BLOCK)
    return jnp.where(hit, 1.0, 0.0)


def _block_scores(imp, q_pos, n_blocks, block_axis=1):
    blk = lax.broadcasted_iota(jnp.int32, imp.shape, block_axis)
    cur = q_pos // SLC_BLOCK
    forced = (blk == 0) | (blk == cur) | (blk == cur - 1)
    valid = (blk * SLC_BLOCK <= q_pos) & (blk < n_blocks)
    return jnp.where(valid, imp + jnp.where(forced, FORCE_BONUS, 0.0), NEG_INF)


def _head_rows(tile_ref, h):
    tokens, heads, width = tile_ref.shape
    return tile_ref.reshape(tokens * heads, width)[pl.ds(h, tokens, stride=heads), :]


def _stack_heads(ref, n):
    return jnp.concatenate([ref[:, r * HEAD_DIM:(r + 1) * HEAD_DIM] for r in range(n)], 0)


def _nsa_prompt_kernel(qraw_ref, qrot_ref, kc_ref, vc_ref, ks_ref, vs_ref, kw_ref, vw_ref,
                       gate_ref, o_ref, ksb_ref, vsb_ref, kwb_ref, vwb_ref, *, seq, kchunk):
    i = pl.program_id(2)
    q0 = i * Q_BLOCK
    nc = kc_ref.shape[0]
    ns = seq // SLC_BLOCK
    rep = NSA_REP
    t_col = lax.broadcasted_iota(jnp.int32, (Q_BLOCK, 1), 0) + q0
    over_heads = lambda x: jnp.concatenate([x] * rep, 0)
    nt = (((1,), (1,)), ((), ()))

    @pl.when(i == 0)
    def _():
        ksb_ref[...] = ks_ref[...].astype(BF)
        vsb_ref[...] = vs_ref[...].astype(BF)
        kwb_ref[...] = kw_ref[...].astype(BF)
        vwb_ref[...] = vw_ref[...].astype(BF)

    q_raw = (_stack_heads(qraw_ref, rep) * (ATTN_SCALE * LOG2E)).astype(BF)
    n_row = lax.broadcasted_iota(jnp.int32, (1, nc), 1)
    vis = (n_row >= 1) & (n_row * CMP_STRIDE + (CMP_BLOCK - CMP_STRIDE - 1) <= t_col)
    s_c = (lax.dot_general(q_raw, kc_ref[...].astype(BF), nt, preferred_element_type=F32)
           + over_heads(jnp.where(vis, 0.0, NEG_INF)))
    e_c = jnp.exp2(s_c - jnp.max(s_c, -1, keepdims=True)) * over_heads(jnp.where(vis, 1.0, 0.0))
    p_c = e_c / jnp.maximum(jnp.sum(e_c, -1, keepdims=True), 1e-30)
    o_cmp = _dot(p_c, vc_ref[...])
    p_sum = p_c[0:Q_BLOCK]
    for r in range(1, rep):
        p_sum = p_sum + p_c[r * Q_BLOCK:(r + 1) * Q_BLOCK]
    q_rot = (_stack_heads(qrot_ref, rep) * (ATTN_SCALE * LOG2E)).astype(BF)
    wk = WINDOW + Q_BLOCK
    w0 = pl.multiple_of(jnp.maximum(q0 - WINDOW, 0), Q_BLOCK)
    dist = t_col - (w0 + lax.broadcasted_iota(jnp.int32, (1, wk), 1))
    s_w = (lax.dot_general(q_rot, kwb_ref[pl.ds(w0, wk), :], nt, preferred_element_type=F32)
           + over_heads(jnp.where((dist >= 0) & (dist < WINDOW), 0.0, NEG_INF)))
    e_w = jnp.exp2(s_w - jnp.max(s_w, -1, keepdims=True))
    p_w = e_w / jnp.sum(e_w, -1, keepdims=True)
    o_swa = jnp.dot(p_w.astype(BF), vwb_ref[pl.ds(w0, wk), :], preferred_element_type=F32)

    imp_t = _dot_exact01(p_sum, _overlap_matrix(nc, ns, blocks_first=True), m01_first=True)
    t_row = lax.broadcasted_iota(jnp.int32, (1, Q_BLOCK), 1) + q0
    sel_t = _topk_select_rows(_block_scores(imp_t, t_row, ns, block_axis=0), min(SLC_TOPK, ns))
    sel_b = sel_t.T.astype(BF)

    per_chunk = kchunk // SLC_BLOCK
    blk = lax.broadcasted_iota(jnp.int32, (ns, per_chunk), 0)
    slot = lax.broadcasted_iota(jnp.int32, (ns, per_chunk), 1)
    b_row = lax.broadcasted_iota(jnp.int32, (per_chunk, kchunk), 0)
    b_lane = lax.broadcasted_iota(jnp.int32, (per_chunk, kchunk), 1)
    expand = jnp.where(b_row == b_lane // SLC_BLOCK, 1.0, 0.0).astype(BF)
    k_lane = lax.broadcasted_iota(jnp.int32, (1, kchunk), 1)

    def slc_chunk(c, carry, diagonal):
        k0 = pl.multiple_of(c * kchunk, kchunk)
        pick = jnp.where(blk == c * per_chunk + slot, 1.0, 0.0).astype(BF)
        sel_c = jnp.dot(sel_b, pick, preferred_element_type=F32).astype(BF)
        keep = jnp.dot(sel_c, expand, preferred_element_type=F32) > 0.5
        if diagonal:
            keep = keep & (k0 + k_lane <= t_col)
        s2 = (lax.dot_general(q_rot, ksb_ref[pl.ds(k0, kchunk), :], nt, preferred_element_type=F32)
              + over_heads(jnp.where(keep, 0.0, NEG_INF)))
        return _flash_update(carry, s2, vsb_ref[pl.ds(k0, kchunk), :])

    init = (jnp.full((rep * Q_BLOCK, 1), NEG_INF, F32), jnp.zeros((rep * Q_BLOCK, 1), F32),
            jnp.zeros((rep * Q_BLOCK, HEAD_DIM), F32))
    last = q0 // kchunk
    carry = lax.fori_loop(0, last, lambda c, carry: slc_chunk(c, carry, False), init)
    _, l_s, acc_s = slc_chunk(last, carry, True)
    o_slc = acc_s / jnp.maximum(l_s, 1e-30)

    gt = gate_ref[...]
    for r in range(rep):
        rows = slice(r * Q_BLOCK, (r + 1) * Q_BLOCK)
        o_ref[:, r * HEAD_DIM:(r + 1) * HEAD_DIM] = (
            gt[:, 3 * r:3 * r + 1] * o_cmp[rows] + gt[:, 3 * r + 1:3 * r + 2] * o_slc[rows]
            + gt[:, 3 * r + 2:3 * r + 3] * o_swa[rows])


def _nsa_prompt(p_main, q_rot, kc_all, sig, batch, seq):
    nq = seq // Q_BLOCK
    gw = NSA_REP * HEAD_DIM
    kchunk = min(1024, seq)
    col = lambda base, kv: (lambda b, g, i: (b, base + kv * NSA_KV_HEADS + g))
    return pl.pallas_call(
        functools.partial(_nsa_prompt_kernel, seq=seq, kchunk=kchunk),
        grid=(batch, NSA_KV_HEADS, nq),
        in_specs=[
            pl.BlockSpec((Q_BLOCK, gw), lambda b, g, i: (b * nq + i, g)),
            pl.BlockSpec((Q_BLOCK, gw), lambda b, g, i: (b * nq + i, g)),
            pl.BlockSpec((None, seq // CMP_STRIDE, HEAD_DIM), lambda b, g, i: (b, 0, g)),
            pl.BlockSpec((None, seq // CMP_STRIDE, HEAD_DIM), lambda b, g, i: (b, 0, NSA_KV_HEADS + g)),
            pl.BlockSpec((seq, HEAD_DIM), col(COL_KVS, 0)),
            pl.BlockSpec((seq, HEAD_DIM), col(COL_KVS, 1)),
            pl.BlockSpec((seq, HEAD_DIM), col(COL_KVW, 0)),
            pl.BlockSpec((seq, HEAD_DIM), col(COL_KVW, 1)),
            pl.BlockSpec((Q_BLOCK, LANE), lambda b, g, i: (b * nq + i, g)),
        ],
        out_specs=pl.BlockSpec((Q_BLOCK, gw), lambda b, g, i: (b * nq + i, g)),
        out_shape=jax.ShapeDtypeStruct((batch * seq, NSA_HEADS * HEAD_DIM), F32),
        scratch_shapes=[pltpu.VMEM((seq, HEAD_DIM), BF)] * 4,
        compiler_params=_cparams(("arbitrary", "arbitrary", "arbitrary")),
        name="nsa_attention_prompt",
    )(p_main, q_rot, kc_all, kc_all, p_main, p_main, p_main, p_main, sig)


def _suffix_sum_lanes(x):
    n = x.shape[-1]
    lane = lax.broadcasted_iota(jnp.int32, x.shape, x.ndim - 1)
    k = 1
    while k < n:
        x = x + jnp.where(lane < n - k, pltpu.roll(x, n - k, x.ndim - 1), 0.0)
        k *= 2
    return x


def _decay_prompt_kernel(lf_ref, d_ref):
    lt = lf_ref[...].T
    top = lt[:SUBLANE]
    d_ref[...] = (_suffix_sum_lanes(top) - top) * LOG2E


def _decay_prompt(lsig, batch, seq):
    return pl.pallas_call(
        _decay_prompt_kernel,
        grid=(batch,),
        in_specs=[pl.BlockSpec((seq, LANE), lambda b: (b, 0))],
        out_specs=pl.BlockSpec((None, SUBLANE, seq), lambda b: (b, 0, 0)),
        out_shape=jax.ShapeDtypeStruct((batch, SUBLANE, seq), F32),
        compiler_params=_cparams(("arbitrary",)),
        name="fox_decay_prompt",
    )(lsig)


def _fox_prompt_kernel(q_ref, k_ref, v_ref, d_ref, o_ref, kb_ref, vb_ref, m_ref, l_ref, acc_ref,
                       *, tile):
    h = pl.program_id(1)
    i = pl.program_id(2)

    @pl.when(i == 0)
    def _():
        kb_ref[...] = k_ref[...].astype(BF)
        vb_ref[...] = v_ref[...].astype(BF)

    q = (q_ref[...] * (ATTN_SCALE * LOG2E)).astype(BF)
    row = lax.broadcasted_iota(jnp.int32, (tile, tile), 0)
    col = lax.broadcasted_iota(jnp.int32, (tile, tile), 1)

    def chunk(c, carry, diagonal):
        k0 = pl.multiple_of(c * tile, tile)
        s2 = lax.dot_general(q, kb_ref[pl.ds(k0, tile), :], (((1,), (1,)), ((), ())),
                             preferred_element_type=F32) + d_ref[pl.ds(h, 1), pl.ds(k0, tile)]
        if diagonal:
            s2 = jnp.where(col <= row, s2, NEG_INF)
        return _flash_update(carry, s2, vb_ref[pl.ds(k0, tile), :])

    init = (jnp.full((tile, 1), NEG_INF, F32), jnp.zeros((tile, 1), F32),
            jnp.zeros((tile, HEAD_DIM), F32))
    carry = lax.fori_loop(0, i, lambda c, carry: chunk(c, carry, False), init)
    _, l, acc = chunk(i, carry, True)
    o_ref[...] = acc / jnp.maximum(l, 1e-30)


def _fox_prompt(p_main, decay, batch, seq):
    tq = min(1024, seq)
    nq = seq // tq
    return pl.pallas_call(
        functools.partial(_fox_prompt_kernel, tile=tq),
        grid=(batch, FOX_HEADS, nq),
        scratch_shapes=[pltpu.VMEM((seq, HEAD_DIM), BF), pltpu.VMEM((seq, HEAD_DIM), BF),
                        pltpu.VMEM((tq, 1), F32), pltpu.VMEM((tq, 1), F32),
                        pltpu.VMEM((tq, HEAD_DIM), F32)],
        in_specs=[
            pl.BlockSpec((tq, HEAD_DIM), lambda b, h, i: (b * nq + i, COL_FOX + h)),
            pl.BlockSpec((seq, HEAD_DIM), lambda b, h, i: (b, COL_FOX + FOX_HEADS + h)),
            pl.BlockSpec((seq, HEAD_DIM), lambda b, h, i: (b, COL_FOX + 2 * FOX_HEADS + h)),
            pl.BlockSpec((None, SUBLANE, seq), lambda b, h, i: (b, 0, 0)),
        ],
        out_specs=pl.BlockSpec((tq, HEAD_DIM), lambda b, h, i: (b * nq + i, h)),
        out_shape=jax.ShapeDtypeStruct((batch * seq, FOX_HEADS * HEAD_DIM), F32),
        compiler_params=_cparams(("arbitrary", "arbitrary", "arbitrary")),
        name="fox_attention_prompt",
    )(p_main, p_main, p_main, decay)


def _mem_attn_kernel(q_ref, k_ref, v_ref, o_ref):
    s = _dot_nt(q_ref[...], k_ref[...]) * ATTN_SCALE
    m = jnp.max(s, -1, keepdims=True)
    e = jnp.exp(s - m)
    p = e / jnp.sum(e, -1, keepdims=True)
    o_ref[...] = _dot(p, v_ref[...])


def _mem_attention(p_main, mem_kv2d, batch, rows_per_batch, tq):
    nq = rows_per_batch // tq
    mlen = mem_kv2d.shape[0] // batch
    return pl.pallas_call(
        _mem_attn_kernel,
        grid=(batch, MEM_HEADS, nq),
        in_specs=[
            pl.BlockSpec((tq, HEAD_DIM), lambda b, h, i: (b * nq + i, COL_QM + h)),
            pl.BlockSpec((mlen, HEAD_DIM), lambda b, h, i: (b, h)),
            pl.BlockSpec((mlen, HEAD_DIM), lambda b, h, i: (b, MEM_HEADS + h)),
        ],
        out_specs=pl.BlockSpec((tq, HEAD_DIM), lambda b, h, i: (b * nq + i, h)),
        out_shape=jax.ShapeDtypeStruct((batch * rows_per_batch, MEM_HEADS * HEAD_DIM), F32),
        compiler_params=_cparams(("arbitrary", "arbitrary", "arbitrary")),
        name="mem_attention",
    )(p_main, mem_kv2d, mem_kv2d)


def _mem_attn_rows_kernel(q_ref, kv_ref, o_ref):
    slots = 2 * MEM_HEADS
    mlen = kv_ref.shape[0] // slots
    scores = [_dot_nt(q_ref[:, h * HEAD_DIM:(h + 1) * HEAD_DIM],
                      kv_ref[pl.ds(h, mlen, stride=slots), :]) for h in range(MEM_HEADS)]
    s = jnp.concatenate(scores, 0) * ATTN_SCALE
    m = jnp.max(s, -1, keepdims=True)
    e = jnp.exp(s - m)
    p = e / jnp.sum(e, -1, keepdims=True)
    for h in range(MEM_HEADS):
        o_ref[:, h * HEAD_DIM:(h + 1) * HEAD_DIM] = _dot(
            p[h * SROW:(h + 1) * SROW], kv_ref[pl.ds(MEM_HEADS + h, mlen, stride=slots), :])


def _mem_attention_rows(p_s, mem_rows):
    dec_batch, rows, _ = mem_rows.shape
    mw = MEM_HEADS * HEAD_DIM
    return pl.pallas_call(
        _mem_attn_rows_kernel,
        grid=(dec_batch,),
        in_specs=[pl.BlockSpec((SROW, mw), lambda b: (b, COL_QM * LANE // mw)),
                  pl.BlockSpec((None, rows, HEAD_DIM), lambda b: (b, 0, 0))],
        out_specs=pl.BlockSpec((SROW, mw), lambda b: (b, 0)),
        out_shape=jax.ShapeDtypeStruct((dec_batch * SROW, mw), F32),
        compiler_params=_cparams(("arbitrary",)),
        name="mem_attention_sample",
    )(p_s, mem_rows)


def _matmul_kernel(x_ref, w_ref, o_ref):
    o_ref[...] = jnp.dot(x_ref[...].astype(BF), w_ref[...], preferred_element_type=F32)


def _matmul(x2d, w_bf, tm, tn):
    m, k = x2d.shape
    n = w_bf.shape[1]
    return pl.pallas_call(
        _matmul_kernel,
        grid=(m // tm, n // tn),
        in_specs=[pl.BlockSpec((tm, k), lambda i, j: (i, 0)),
                  pl.BlockSpec((k, tn), lambda i, j: (0, j))],
        out_specs=pl.BlockSpec((tm, tn), lambda i, j: (i, j)),
        out_shape=jax.ShapeDtypeStruct((m, n), F32),
        compiler_params=_cparams(("arbitrary", "arbitrary")),
        name="projection_matmul",
    )(x2d, w_bf)


def _layer_norm(z, g, b):
    zc = z - jnp.mean(z, -1, keepdims=True)
    var = jnp.mean(zc * zc, -1, keepdims=True)
    return zc * lax.rsqrt(var + LN_EPS) * g + b


def _outproj_kernel(on_ref, of_ref, om_ref, x_ref, w_ref, g_ref, b_ref, h_ref, *, alpha):
    mix = jnp.concatenate([on_ref[...].astype(BF), of_ref[...].astype(BF),
                           om_ref[...].astype(BF)], 1)
    y = jnp.dot(mix, w_ref[...], preferred_element_type=F32)
    h_ref[...] = _layer_norm(alpha * x_ref[...] + y, g_ref[...], b_ref[...])


def _out_projection(o_nsa, o_fox, o_mem, x2d, w_out, g, b, tm, alpha):
    m, d = x2d.shape
    row = lambda w: pl.BlockSpec((tm, w), lambda i: (i, 0))
    const = lambda shape: pl.BlockSpec(shape, lambda i: (0, 0))
    return pl.pallas_call(
        functools.partial(_outproj_kernel, alpha=alpha),
        grid=(m // tm,),
        in_specs=[row(o_nsa.shape[1]), row(o_fox.shape[1]), row(o_mem.shape[1]), row(d),
                  const(w_out.shape), const((1, d)), const((1, d))],
        out_specs=row(d),
        out_shape=jax.ShapeDtypeStruct((m, d), F32),
        compiler_params=_cparams(("arbitrary",)),
        name="out_projection_ln",
    )(o_nsa, o_fox, o_mem, x2d, w_out, g, b)


def _ffn_core(h_ref, wa_ref, wb_ref, cw_ref, cb_ref, wd_ref, g_ref, b_ref, y_ref, acc_ref, hb_ref,
              shifted, alpha):
    j = pl.program_id(1)

    @pl.when(j == 0)
    def _():
        hb_ref[...] = h_ref[...].astype(BF)

    hb = hb_ref[...]
    a = jnp.dot(hb, wa_ref[...], preferred_element_type=F32)
    gate_in = jnp.dot(hb, wb_ref[...], preferred_element_type=F32)
    a1, a2 = shifted(a)
    cw = cw_ref[...]
    c = cb_ref[...] + a2 * cw[0:1] + a1 * cw[1:2] + a * cw[2:3]
    act = (c * jax.nn.sigmoid(c)) * gate_in
    part = jnp.dot(act.astype(BF), wd_ref[...], preferred_element_type=F32)

    @pl.when(j == 0)
    def _():
        acc_ref[...] = part

    @pl.when(j > 0)
    def _():
        acc_ref[...] += part

    @pl.when(j == pl.num_programs(1) - 1)
    def _():
        y_ref[...] = _layer_norm(alpha * h_ref[...] + acc_ref[...], g_ref[...], b_ref[...])
    return a


def _ffn_prompt_kernel(h_ref, wa_ref, wb_ref, cw_ref, cb_ref, wd_ref, g_ref, b_ref,
                       y_ref, tail_ref, acc_ref, hb_ref, halo_ref, *, tiles_per_seq, alpha):
    i = pl.program_id(0)
    j = pl.program_id(1)
    tm = h_ref.shape[0]
    seq_start = (i % tiles_per_seq) == 0

    @pl.when(seq_start)
    def _():
        halo_ref[j] = jnp.zeros(halo_ref.shape[1:], F32)

    def shifted(a):
        row = lax.broadcasted_iota(jnp.int32, a.shape, 0)
        halo = halo_ref[j]
        h1 = halo[SUBLANE - 1:SUBLANE]
        h2 = halo[SUBLANE - 2:SUBLANE - 1]
        a1 = jnp.where(row == 0, h1, pltpu.roll(a, 1, 0))
        a2 = jnp.where(row == 0, h2, jnp.where(row == 1, h1, pltpu.roll(a, 2, 0)))
        return a1, a2

    a = _ffn_core(h_ref, wa_ref, wb_ref, cw_ref, cb_ref, wd_ref, g_ref, b_ref, y_ref, acc_ref,
                  hb_ref, shifted, alpha)
    halo_ref[j] = a[tm - SUBLANE:]
    tail_ref[...] = a[tm - SUBLANE:]


def _ffn_sample_kernel(h_ref, wa_ref, wb_ref, cw_ref, cb_ref, wd_ref, g_ref, b_ref, f1_ref, f2_ref,
                       y_ref, a_ref, acc_ref, hb_ref, *, alpha):
    def shifted(a):
        s = lax.broadcasted_iota(jnp.int32, a.shape, 0) % SROW
        a1 = jnp.where(s >= 1, pltpu.roll(a, 1, 0), 0.0) + f1_ref[...]
        a2 = jnp.where(s >= 2, pltpu.roll(a, 2, 0), 0.0) + f2_ref[...]
        return a1, a2

    a_ref[...] = _ffn_core(h_ref, wa_ref, wb_ref, cw_ref, cb_ref, wd_ref, g_ref, b_ref, y_ref,
                           acc_ref, hb_ref, shifted, alpha)


def _ffn_specs(tm, d, tf, nf):
    return [
        pl.BlockSpec((tm, d), lambda i, j: (i, 0)),
        pl.BlockSpec((d, tf), lambda i, j: (0, j)),
        pl.BlockSpec((d, tf), lambda i, j: (0, nf + j)),
        pl.BlockSpec((SUBLANE, tf), lambda i, j: (0, j)),
        pl.BlockSpec((1, tf), lambda i, j: (0, j)),
        pl.BlockSpec((tf, d), lambda i, j: (j, 0)),
        pl.BlockSpec((1, d), lambda i, j: (0, 0)),
        pl.BlockSpec((1, d), lambda i, j: (0, 0)),
    ]


def _ffn_prompt(h2d, fw, seq, tm, tf, alpha):
    w_up, conv_w8, conv_b, w_down, g, b = fw
    m, d = h2d.shape
    dff = w_down.shape[0]
    nf = dff // tf
    return pl.pallas_call(
        functools.partial(_ffn_prompt_kernel, tiles_per_seq=seq // tm, alpha=alpha),
        grid=(m // tm, nf),
        in_specs=_ffn_specs(tm, d, tf, nf),
        out_specs=[pl.BlockSpec((tm, d), lambda i, j: (i, 0)),
                   pl.BlockSpec((None, SUBLANE, tf), lambda i, j: (i, 0, j))],
        out_shape=[jax.ShapeDtypeStruct((m, d), F32),
                   jax.ShapeDtypeStruct((m // tm, SUBLANE, dff), F32)],
        scratch_shapes=[pltpu.VMEM((tm, d), F32), pltpu.VMEM((tm, d), BF),
                        pltpu.VMEM((nf, SUBLANE, tf), F32)],
        compiler_params=_cparams(("arbitrary", "arbitrary")),
        name="conv_ffn_prompt",
    )(h2d, w_up, w_up, conv_w8, conv_b, w_down, g, b)


def _ffn_sample(h2d, fw, fill1, fill2, tf, alpha):
    w_up, conv_w8, conv_b, w_down, g, b = fw
    m, d = h2d.shape
    dff = w_down.shape[0]
    nf = dff // tf
    return pl.pallas_call(
        functools.partial(_ffn_sample_kernel, alpha=alpha),
        grid=(1, nf),
        in_specs=_ffn_specs(m, d, tf, nf) + [pl.BlockSpec((m, tf), lambda i, j: (0, j)),
                                             pl.BlockSpec((m, tf), lambda i, j: (0, j))],
        out_specs=[pl.BlockSpec((m, d), lambda i, j: (0, 0)),
                   pl.BlockSpec((m, tf), lambda i, j: (0, j))],
        out_shape=[jax.ShapeDtypeStruct((m, d), F32), jax.ShapeDtypeStruct((m, dff), F32)],
        scratch_shapes=[pltpu.VMEM((m, d), F32), pltpu.VMEM((m, d), BF)],
        compiler_params=_cparams(("arbitrary", "arbitrary")),
        name="conv_ffn_sample",
    )(h2d, w_up, w_up, conv_w8, conv_b, w_down, g, b, fill1, fill2)


def _nsa_sample_a_kernel(qraw_ref, qrot_ref, kc_ref, vc_ref, swa_ref, kwn_ref, vwn_ref,
                         gate_ref, part_ref, sel_ref, *, past, n_blocks):
    rep = NSA_REP
    g = pl.program_id(1)
    nc = kc_ref.shape[0]
    nsp = -(-n_blocks // LANE) * LANE
    s_col = lax.broadcasted_iota(jnp.int32, (SROW, 1), 0) + past

    q_raw = _stack_heads(qraw_ref, rep)
    s_c = (_dot_nt(q_raw, kc_ref[...]) * ATTN_SCALE).reshape(rep, SROW, nc)
    n_row = lax.broadcasted_iota(jnp.int32, (1, nc), 1)
    vis = (n_row >= 1) & (n_row * CMP_STRIDE + (CMP_BLOCK - CMP_STRIDE - 1) <= s_col)
    p_c = _masked_softmax(s_c, vis[None])
    o_cmp = _dot(p_c.reshape(rep * SROW, nc), vc_ref[...])
    imp = _dot_exact01(jnp.sum(p_c, 0), _overlap_matrix(nc, nsp))
    score = _block_scores(imp, s_col, n_blocks)
    rank = _topk_rank(score, n_blocks)
    col1 = lax.broadcasted_iota(jnp.int32, score.shape, 1).astype(F32) + 1.0
    lane = lax.broadcasted_iota(jnp.int32, (SROW, LANE), 1)
    ids = jnp.full((SROW, LANE), -1.0, F32)
    for r in range(min(SLC_TOPK, n_blocks)):
        hit = jnp.where(rank == r, jnp.where(score > 0.5 * NEG_INF, col1, 0.0), 0.0)
        ids = jnp.where(lane == r, jnp.sum(hit, -1, keepdims=True) - 1.0, ids)
    sel_ref[...] = ids.astype(jnp.int32)

    q_rot = _stack_heads(qrot_ref, rep)
    wbuf = swa_ref.shape[0] // N_KV_SLOTS
    keys = jnp.concatenate([swa_ref[pl.ds(g, wbuf, stride=N_KV_SLOTS), :], kwn_ref[...]], 0)
    vals = jnp.concatenate([swa_ref[pl.ds(NSA_KV_HEADS + g, wbuf, stride=N_KV_SLOTS), :],
                            vwn_ref[...]], 0)
    w_pos = past - wbuf + lax.broadcasted_iota(jnp.int32, (1, wbuf + SROW), 1)
    dist = s_col - w_pos
    mask = (dist >= 0) & (dist < WINDOW) & (w_pos >= 0)
    s_w = (_dot_nt(q_rot, keys) * ATTN_SCALE).reshape(rep, SROW, wbuf + SROW)
    p_w = _masked_softmax(s_w, mask[None])
    o_swa = _dot(p_w.reshape(rep * SROW, wbuf + SROW), vals)

    gt = gate_ref[...]
    for r in range(rep):
        rows = slice(r * SROW, (r + 1) * SROW)
        part_ref[rows, :] = (gt[:, 3 * r:3 * r + 1] * o_cmp[rows]
                             + gt[:, 3 * r + 2:3 * r + 3] * o_swa[rows])


def _nsa_sample_a(p_s, qrot_s, kc_all, swa_rows, sig_s, past, n_blocks):
    dec_batch, nc, _ = kc_all.shape
    gw = NSA_REP * HEAD_DIM
    bg = lambda b, g: (b, g)
    return pl.pallas_call(
        functools.partial(_nsa_sample_a_kernel, past=past, n_blocks=n_blocks),
        grid=(dec_batch, NSA_KV_HEADS),
        in_specs=[
            pl.BlockSpec((SROW, gw), bg),
            pl.BlockSpec((SROW, gw), bg),
            pl.BlockSpec((None, nc, HEAD_DIM), lambda b, g: (b, 0, g)),
            pl.BlockSpec((None, nc, HEAD_DIM), lambda b, g: (b, 0, NSA_KV_HEADS + g)),
            pl.BlockSpec((None, swa_rows.shape[1], HEAD_DIM), lambda b, g: (b, 0, 0)),
            pl.BlockSpec((SROW, HEAD_DIM), lambda b, g: (b, COL_KVW + g)),
            pl.BlockSpec((SROW, HEAD_DIM), lambda b, g: (b, COL_KVW + NSA_KV_HEADS + g)),
            pl.BlockSpec((SROW, LANE), bg),
        ],
        out_specs=[pl.BlockSpec((None, None, NSA_REP * SROW, HEAD_DIM), lambda b, g: (b, g, 0, 0)),
                   pl.BlockSpec((None, None, SROW, LANE), lambda b, g: (b, g, 0, 0))],
        out_shape=[jax.ShapeDtypeStruct((dec_batch, NSA_KV_HEADS, NSA_REP * SROW, HEAD_DIM), F32),
                   jax.ShapeDtypeStruct((dec_batch, NSA_KV_HEADS, SROW, LANE), jnp.int32)],
        compiler_params=_cparams(("arbitrary", "arbitrary")),
        name="nsa_sample_cmp_swa",
    )(p_s, qrot_s, kc_all, kc_all, swa_rows, p_s, p_s, sig_s)


def _nsa_sample_gather_kernel(pt_ref, ix_ref, *refs, n_blocks, s_len):
    del pt_ref, ix_ref
    n_sel = min(SLC_TOPK, n_blocks)
    n_fetch = s_len * n_sel
    blocks = refs[:n_fetch]
    (qrot_ref, ids_ref, knew_ref, vnew_ref, gate_ref, part_ref, o_ref) = refs[n_fetch:]
    g = pl.program_id(1)
    rep = NSA_REP
    n_keys = n_fetch * SLC_BLOCK

    def block_rows(slot):
        return jnp.concatenate([r[pl.ds(slot, SLC_BLOCK, stride=N_KV_SLOTS), :].astype(BF)
                                for r in blocks], 0)

    q = _stack_heads(qrot_ref, rep)
    s_cache = _dot_nt(q, block_rows(g)) * ATTN_SCALE

    ids = ids_ref[...]
    cached = jnp.where((ids >= 0) & (ids < n_blocks - 1), 1.0, 0.0).astype(BF)
    rank_row = lax.broadcasted_iota(jnp.int32, (LANE, n_keys), 0)
    key_col = lax.broadcasted_iota(jnp.int32, (LANE, n_keys), 1)
    by_rank = jnp.where(rank_row == (key_col // SLC_BLOCK) % n_sel, 1.0, 0.0).astype(BF)
    tok = lax.broadcasted_iota(jnp.int32, (SROW, n_keys), 0)
    owner = lax.broadcasted_iota(jnp.int32, (SROW, n_keys), 1) // (SLC_BLOCK * n_sel)
    keep_cache = jnp.where(owner == tok, jnp.dot(cached, by_rank, preferred_element_type=F32), 0.0)

    s_q = lax.broadcasted_iota(jnp.int32, (SROW, LANE), 0)
    s_k = lax.broadcasted_iota(jnp.int32, (SROW, LANE), 1)
    newest = jnp.sum(jnp.where(ids == n_blocks - 1, 1.0, 0.0), -1, keepdims=True)
    keep_new = jnp.where((s_k <= s_q) & (s_k < SROW), newest, 0.0)
    pad = jnp.zeros((LANE - SROW, HEAD_DIM), F32)
    s_new = _dot_nt(q, jnp.concatenate([knew_ref[...], pad], 0)) * ATTN_SCALE

    over_heads = lambda x: jnp.concatenate([x] * rep, 0) > 0.5
    mask_c, mask_n = over_heads(keep_cache), over_heads(keep_new)
    s_cache = jnp.where(mask_c, s_cache, NEG_INF)
    s_new = jnp.where(mask_n, s_new, NEG_INF)
    m = jnp.maximum(jnp.max(s_cache, -1, keepdims=True), jnp.max(s_new, -1, keepdims=True))
    e_c = jnp.where(mask_c, jnp.exp(s_cache - m), 0.0)
    e_n = jnp.where(mask_n, jnp.exp(s_new - m), 0.0)
    denom = jnp.maximum(jnp.sum(e_c, -1, keepdims=True) + jnp.sum(e_n, -1, keepdims=True), 1e-30)
    o_slc = (_dot(e_c, block_rows(NSA_KV_HEADS + g))
             + _dot(e_n, jnp.concatenate([vnew_ref[...], pad], 0))) / denom
    gt = gate_ref[...]
    for r in range(rep):
        rows = slice(r * SROW, (r + 1) * SROW)
        o_ref[:, r * HEAD_DIM:(r + 1) * HEAD_DIM] = (
            part_ref[rows, :] + gt[:, 3 * r + 1:3 * r + 2] * o_slc[rows])


def _nsa_sample_gather(cache_rows, page_table, ids, p_s, qrot_s, sig_s, part, n_blocks, s_len):
    n_pool, page_rows, _ = cache_rows.shape
    dec_batch, n_pages = page_table.shape
    n_sel = min(SLC_TOPK, n_blocks)
    per_page = page_rows // (N_KV_SLOTS * SLC_BLOCK)
    half_rows = N_KV_SLOTS * SLC_BLOCK
    gw = NSA_REP * HEAD_DIM
    flat_ids = ids[:, :, :s_len, :n_sel].reshape(dec_batch, NSA_KV_HEADS * s_len * n_sel)
    pool_page = jnp.take_along_axis(page_table, jnp.clip(flat_ids // per_page, 0, n_pages - 1), 1)
    in_page = jnp.maximum(flat_ids, 0) % per_page

    def fetch_spec(n):
        def index(b, g, pp, ip):
            return (pp[b, g * (s_len * n_sel) + n], ip[b, g * (s_len * n_sel) + n], 0)
        return pl.BlockSpec((None, half_rows, HEAD_DIM), index)

    bg = lambda b, g, pt, ix: (b, g)
    return pl.pallas_call(
        functools.partial(_nsa_sample_gather_kernel, n_blocks=n_blocks, s_len=s_len),
        grid_spec=pltpu.PrefetchScalarGridSpec(
            num_scalar_prefetch=2,
            grid=(dec_batch, NSA_KV_HEADS),
            in_specs=[fetch_spec(n) for n in range(s_len * n_sel)] + [
                pl.BlockSpec((SROW, gw), bg),
                pl.BlockSpec((None, None, SROW, LANE), lambda b, g, pt, ix: (b, g, 0, 0)),
                pl.BlockSpec((SROW, HEAD_DIM), lambda b, g, pt, ix: (b, COL_KVS + g)),
                pl.BlockSpec((SROW, HEAD_DIM), lambda b, g, pt, ix: (b, COL_KVS + NSA_KV_HEADS + g)),
                pl.BlockSpec((SROW, LANE), bg),
                pl.BlockSpec((None, None, NSA_REP * SROW, HEAD_DIM), lambda b, g, pt, ix: (b, g, 0, 0)),
            ],
            out_specs=pl.BlockSpec((SROW, gw), bg),
        ),
        out_shape=jax.ShapeDtypeStruct((dec_batch * SROW, NSA_HEADS * HEAD_DIM), F32),
        compiler_params=_cparams(("arbitrary", "arbitrary")),
        name="nsa_sample_selected",
    )(pool_page, in_page, *([cache_rows] * (s_len * n_sel)), qrot_s, ids, p_s, p_s, sig_s, part)


def _fox_sample_kernel(pt_ref, *refs, s_len):
    n = PAGES_PER_STEP
    kpages, vpages = refs[:n], refs[n:2 * n]
    (logf_ref, q_ref, knew_ref, vnew_ref, lnew_ref,
     o_ref, m_ref, l_ref, acc_ref, carry_ref) = refs[2 * n:]
    b = pl.program_id(0)
    c = pl.program_id(1)
    page = kpages[0].shape[0]
    head = lambda ref, h: ref[:, h * HEAD_DIM:(h + 1) * HEAD_DIM]

    def cache_rows(pages, h):
        return jnp.concatenate([_head_rows(p, h).astype(BF) for p in pages], 0)

    def update(scores, mask, values):
        s = jnp.concatenate(scores, 0)
        if mask is not None:
            s = jnp.where(mask, s, NEG_INF)
        m_old = m_ref[...]
        m_new = jnp.maximum(m_old, jnp.max(s, -1, keepdims=True))
        p = jnp.exp(s - m_new)
        if mask is not None:
            p = jnp.where(mask, p, 0.0)
        alpha = jnp.exp(m_old - m_new)
        pv = jnp.concatenate([_dot(p[h * SROW:(h + 1) * SROW], values[h])
                              for h in range(FOX_HEADS)], 0)
        m_ref[...] = m_new
        l_ref[...] = alpha * l_ref[...] + jnp.sum(p, -1, keepdims=True)
        acc_ref[...] = alpha * acc_ref[...] + pv

    @pl.when(c == 0)
    def _():
        m_ref[...] = jnp.full(m_ref.shape, NEG_INF, F32)
        l_ref[...] = jnp.zeros(l_ref.shape, F32)
        acc_ref[...] = jnp.zeros(acc_ref.shape, F32)
        s_q = lax.broadcasted_iota(jnp.int32, (SROW, LANE), 0)
        s_k = lax.broadcasted_iota(jnp.int32, (SROW, LANE), 1)
        later = jnp.where((s_q > s_k) & (s_q < s_len), 1.0, 0.0)
        real = lax.broadcasted_iota(jnp.int32, (SROW, 1), 0) < s_len
        lnew = lnew_ref[...]
        pad = jnp.zeros((LANE - SROW, HEAD_DIM), F32)
        rows_q = lax.broadcasted_iota(jnp.int32, (FOX_HEADS * SROW, LANE), 0) % SROW
        cols_k = lax.broadcasted_iota(jnp.int32, (FOX_HEADS * SROW, LANE), 1)
        mask = (cols_k <= rows_q) & (cols_k < s_len)
        totals, scores, values = [], [], []
        for h in range(FOX_HEADS):
            lf = jnp.where(real, lnew[:, h:h + 1], 0.0)
            d_row = jnp.sum(lf * later, 0, keepdims=True)
            totals.append(jnp.sum(lf, 0, keepdims=True))
            k = jnp.concatenate([head(knew_ref, h), pad], 0)
            values.append(jnp.concatenate([head(vnew_ref, h), pad], 0))
            scores.append(_dot_nt(head(q_ref, h), k) * ATTN_SCALE + d_row)
        update(scores, mask, values)
        totals += [jnp.zeros((1, 1), F32)] * (SUBLANE - FOX_HEADS)
        carry_ref[...] = jnp.broadcast_to(jnp.concatenate(totals, 0), carry_ref.shape)

    @pl.when(c > 0)
    def _():
        ids = [pt_ref[b, c * n + i] for i in range(n)]
        lf = jnp.concatenate(
            [jnp.concatenate([logf_ref[h, pl.ds(pg, 1), :] for pg in ids], 1) for h in range(FOX_HEADS)]
            + [jnp.zeros((SUBLANE - FOX_HEADS, n * page), F32)], 0)
        incl = _suffix_sum_lanes(lf)
        carry = carry_ref[:, 0:1]
        decay = carry + (incl - lf)
        carry_ref[...] = jnp.broadcast_to(carry + incl[:, 0:1], carry_ref.shape)
        scores = [_dot_nt(head(q_ref, h), cache_rows(kpages, h)) * ATTN_SCALE + decay[h:h + 1, :]
                  for h in range(FOX_HEADS)]
        update(scores, None, [cache_rows(vpages, h) for h in range(FOX_HEADS)])

    @pl.when(c == pl.num_programs(1) - 1)
    def _():
        o = acc_ref[...] / jnp.maximum(l_ref[...], 1e-30)
        for h in range(FOX_HEADS):
            o_ref[:, h * HEAD_DIM:(h + 1) * HEAD_DIM] = o[h * SROW:(h + 1) * SROW]


def _fox_sample(cache, logf_t, page_table, p_s, lsig_s, s_len):
    n_pool, page = cache.shape[:2]
    dec_batch, n_pages = page_table.shape
    nchunk = n_pages // PAGES_PER_STEP
    fw = FOX_HEADS * HEAD_DIM
    row = lambda w, col: pl.BlockSpec((SROW, w), lambda b, c, pt: (b, col))
    chunks = page_table.reshape(dec_batch, nchunk, PAGES_PER_STEP)[:, ::-1]
    steps = jnp.concatenate([chunks[:, :1], chunks], 1).reshape(dec_batch, (nchunk + 1) * PAGES_PER_STEP)

    def half_specs(kv):
        def spec(i):
            return pl.BlockSpec((None, page, None, FOX_HEADS, HEAD_DIM),
                                lambda b, c, pt: (pt[b, c * PAGES_PER_STEP + i], 0, kv, 0, 0))
        return [spec(i) for i in range(PAGES_PER_STEP)]

    return pl.pallas_call(
        functools.partial(_fox_sample_kernel, s_len=s_len),
        grid_spec=pltpu.PrefetchScalarGridSpec(
            num_scalar_prefetch=1,
            grid=(dec_batch, nchunk + 1),
            in_specs=half_specs(0) + half_specs(1) + [
                pl.BlockSpec(logf_t.shape, lambda b, c, pt: (0, 0, 0), pipeline_mode=pl.Buffered(1)),
                row(fw, COL_FOX * LANE // fw),
                row(fw, COL_FOX * LANE // fw + 1),
                row(fw, COL_FOX * LANE // fw + 2),
                row(LANE, 0),
            ],
            out_specs=row(fw, 0),
            scratch_shapes=[pltpu.VMEM((FOX_HEADS * SROW, 1), F32),
                            pltpu.VMEM((FOX_HEADS * SROW, 1), F32),
                            pltpu.VMEM((FOX_HEADS * SROW, HEAD_DIM), F32),
                            pltpu.VMEM((SUBLANE, LANE), F32)],
        ),
        out_shape=jax.ShapeDtypeStruct((dec_batch * SROW, fw), F32),
        compiler_params=_cparams(("arbitrary", "arbitrary")),
        name="fox_attention_sample",
    )(steps, *([cache] * (2 * PAGES_PER_STEP)), logf_t, p_s, p_s, p_s, lsig_s)


def _rope_tables(pos):
    half = ROT_DIM // 2
    inv = jnp.power(ROPE_THETA, -jnp.arange(half, dtype=F32) * (2.0 / ROT_DIM))
    ang = pos.astype(F32)[:, None] * inv[None, :]
    cos, sin = jnp.cos(ang), jnp.sin(ang)
    n = pos.shape[0]
    c = jnp.concatenate([cos, cos, jnp.ones((n, HEAD_DIM - ROT_DIM), F32)], 1)
    s = jnp.concatenate([-sin, sin, jnp.zeros((n, HEAD_DIM - ROT_DIM), F32)], 1)
    return c, s


def _layer_weights(w_in, b_gate, b_forget, cmp_w1, cmp_b1, cmp_w2, cmp_pos, w_mem_kv, w_out,
                   ln1_g, ln1_b, w_up, conv_w, conv_b, w_down, ln2_g, ln2_b):
    d = w_in.shape[0]
    nq = NSA_HEADS * HEAD_DIM
    kvw = 2 * NSA_KV_HEADS * HEAD_DIM
    o_g = nq + 3 * kvw
    o_fox = o_g + 3 * NSA_HEADS
    o_f = o_fox + 3 * FOX_HEADS * HEAD_DIM
    o_qm = o_f + FOX_HEADS
    w_main = jnp.concatenate([w_in[:, :o_g], w_in[:, o_fox:o_f], w_in[:, o_qm:]], 1).astype(BF)
    per_group = 3 * NSA_REP
    zpad = lambda n: jnp.zeros((d, n), w_in.dtype)
    w_small = jnp.concatenate([
        w_in[:, o_g:o_g + per_group], zpad(LANE - per_group),
        w_in[:, o_g + per_group:o_fox], zpad(LANE - per_group),
        w_in[:, o_f:o_qm], zpad(LANE - FOX_HEADS)], 1).astype(BF)
    bpad = lambda n: jnp.zeros((n,), F32)
    b_small = jnp.concatenate([
        b_gate[:per_group], bpad(LANE - per_group), b_gate[per_group:], bpad(LANE - per_group),
        b_forget, bpad(LANE - FOX_HEADS)])[None, :].astype(F32)
    kdim = CMP_STRIDE * HEAD_DIM
    w1cat = jnp.concatenate([cmp_w1[:, :CMP_STRIDE].reshape(2, kdim, CMP_HIDDEN),
                             cmp_w1[:, CMP_STRIDE:].reshape(2, kdim, CMP_HIDDEN)], 2).astype(BF)
    pe = jnp.concatenate([cmp_pos[:, :CMP_STRIDE].reshape(2, 1, kdim),
                          cmp_pos[:, CMP_STRIDE:].reshape(2, 1, kdim),
                          jnp.zeros((2, SUBLANE - 2, kdim), F32)], 1)
    cw = (w1cat, pe, cmp_b1[:, None, :], cmp_w2.astype(BF))
    conv_w8 = jnp.concatenate([conv_w, jnp.zeros((SUBLANE - CONV_W, conv_w.shape[1]), F32)], 0)
    fw = (w_up.astype(BF), conv_w8, conv_b[None, :], w_down.astype(BF), ln2_g[None, :], ln2_b[None, :])
    return dict(w_main=w_main, w_small=w_small, b_small=b_small, cw=cw,
                w_mem_kv=w_mem_kv.astype(BF), w_out=w_out.astype(BF),
                ln1_g=ln1_g[None, :], ln1_b=ln1_b[None, :], fw=fw)


def _prompt_layer(x, mem, lw, alpha):
    batch, seq, d = x.shape
    x2d = x.reshape(batch * seq, d)
    rc, rs = _rope_tables(jnp.tile(jnp.arange(seq, dtype=jnp.int32), batch))
    tm = min(512, seq)
    p_main, q_rot, sig, lsig, st_c, st_s, st_w, st_f = _input_projection(
        x2d, lw["w_main"], lw["w_small"], lw["b_small"], rc, rs, tm)
    kc_all = _compress_prompt(p_main, batch, seq, lw["cw"])
    o_nsa = _nsa_prompt(p_main, q_rot, kc_all, sig, batch, seq)
    decay = _decay_prompt(lsig, batch, seq)
    o_fox = _fox_prompt(p_main, decay, batch, seq)
    mlen = mem.shape[1]
    mem_kv = _matmul(mem.reshape(batch * mlen, d), lw["w_mem_kv"], min(256, batch * mlen), 512)
    o_mem = _mem_attention(p_main, mem_kv, batch, seq, min(512, seq))
    h = _out_projection(o_nsa, o_fox, o_mem, x2d, lw["w_out"], lw["ln1_g"], lw["ln1_b"],
                        min(256, seq), alpha)
    y, tail = _ffn_prompt(h, lw["fw"], seq, tm, 512, alpha)
    kv_state = lambda st: st.reshape(batch, seq, 2, -1, HEAD_DIM)
    n_win = min(WINDOW, seq)
    tiles = seq // tm
    conv_state = tail.reshape(batch, tiles, SUBLANE, -1)[:, -1, SUBLANE - (CONV_W - 1):]
    states = (kv_state(st_c), kv_state(st_s), kv_state(st_f),
              lsig[:, :FOX_HEADS].reshape(batch, seq, FOX_HEADS),
              kv_state(st_w)[:, seq - n_win:],
              mem_kv.reshape(batch, mlen, 2, MEM_HEADS, HEAD_DIM), conv_state)
    return y.reshape(batch, seq, d), states


def _sample_layer(x, c_cmp, c_slc, c_fox, c_logf, c_swa, c_mem, s_conv, page_table, lw, alpha):
    dec_batch, s_len, d = x.shape
    n_pool, page = c_cmp.shape[0], c_cmp.shape[1]
    n_pages = page_table.shape[1]
    past = n_pages * page
    t_all = past + s_len
    assert s_len <= SROW and n_pages % PAGES_PER_STEP == 0
    assert (t_all // CMP_STRIDE) * CMP_STRIDE == past and past % SLC_BLOCK == 0
    n_blocks = -(-t_all // SLC_BLOCK)
    xp = jnp.pad(x, ((0, 0), (0, SROW - s_len), (0, 0))).reshape(dec_batch * SROW, d)
    pos = past + jnp.tile(jnp.arange(SROW, dtype=jnp.int32), dec_batch)
    rc, rs = _rope_tables(pos)
    p_s, qrot_s, sig_s, lsig_s = _input_projection(xp, lw["w_main"], lw["w_small"], lw["b_small"],
                                                   rc, rs, dec_batch * SROW)[:4]
    kvw = 2 * NSA_KV_HEADS * HEAD_DIM
    as_rows = lambda a: a.reshape(a.shape[0], -1, HEAD_DIM)
    kc_all = _compress_paged(as_rows(c_cmp), page_table, lw["cw"])
    part, ids = _nsa_sample_a(p_s, qrot_s, kc_all, as_rows(c_swa), sig_s, past, n_blocks)
    o_nsa = _nsa_sample_gather(as_rows(c_slc), page_table, ids, p_s, qrot_s, sig_s, part,
                               n_blocks, s_len)
    logf_t = jnp.transpose(c_logf, (2, 0, 1))
    o_fox = _fox_sample(c_fox, logf_t, page_table, p_s, lsig_s, s_len)
    o_mem = _mem_attention_rows(p_s, as_rows(c_mem))
    h = _out_projection(o_nsa, o_fox, o_mem, xp, lw["w_out"], lw["ln1_g"], lw["ln1_b"],
                        dec_batch * SROW, alpha)
    dff = s_conv.shape[-1]
    zrow = jnp.zeros((dec_batch, 1, dff), F32)
    fill1 = jnp.concatenate([s_conv[:, 1:2]] + [zrow] * (SROW - 1), 1).reshape(dec_batch * SROW, dff)
    fill2 = jnp.concatenate([s_conv[:, 0:1], s_conv[:, 1:2]] + [zrow] * (SROW - 2), 1)
    y, a = _ffn_sample(h, lw["fw"], fill1, fill2.reshape(dec_batch * SROW, dff), 512, alpha)

    def rows(arr):
        return arr.reshape(dec_batch, SROW, -1)[:, :s_len]
    cols = lambda c0, w: rows(p_s[:, c0 * LANE:c0 * LANE + w]).reshape(dec_batch, s_len, 2, -1, HEAD_DIM)
    new_kv_swa = cols(COL_KVW, kvw)
    new_swa = jnp.concatenate([c_swa, new_kv_swa], 1)[:, s_len:]
    conv_state = jnp.concatenate([s_conv, rows(a)], 1)[:, -(CONV_W - 1):]
    states = (cols(COL_KVC, kvw), cols(COL_KVS, kvw),
              cols(COL_FOX + FOX_HEADS, 2 * FOX_HEADS * HEAD_DIM),
              rows(lsig_s[:, :FOX_HEADS]), new_swa, conv_state)
    return rows(y), states


def kernel(x_prompt, x_sample, mem_prompt, cache_nsa_cmp, cache_nsa_slc, cache_fox_kv, cache_fox_logf, cache_nsa_swa, cache_mem, state_conv, page_table, w_in, b_gate, b_forget, cmp_w1, cmp_b1, cmp_w2, cmp_pos, w_mem_kv, w_out, ln1_g, ln1_b, w_up, conv_w, conv_b, w_down, ln2_g, ln2_b):
    depth = w_in.shape[0]
    alpha = float((2 * depth) ** 0.25)
    yp, ys = x_prompt, x_sample
    acc_p = [[] for _ in range(7)]
    acc_s = [[] for _ in range(6)]
    for l in range(depth):
        lw = _layer_weights(w_in[l], b_gate[l], b_forget[l], cmp_w1[l], cmp_b1[l], cmp_w2[l],
                            cmp_pos[l], w_mem_kv[l], w_out[l], ln1_g[l], ln1_b[l], w_up[l],
                            conv_w[l], conv_b[l], w_down[l], ln2_g[l], ln2_b[l])
        yp, st_p = _prompt_layer(yp, mem_prompt, lw, alpha)
        ys, st_s = _sample_layer(ys, cache_nsa_cmp[l], cache_nsa_slc[l], cache_fox_kv[l],
                                 cache_fox_logf[l], cache_nsa_swa[l], cache_mem[l], state_conv[l],
                                 page_table, lw, alpha)
        for lst, a in zip(acc_p, st_p):
            lst.append(a)
        for lst, a in zip(acc_s, st_s):
            lst.append(a)
    outs_p = [jnp.stack(a, 0) for a in acc_p]
    outs_s = [jnp.stack(a, 0) for a in acc_s]
    return (yp, ys, *outs_p, *outs_s)
```

```python
import functools

import jax
import jax.numpy as jnp
import numpy as np
from jax import lax
from jax.experimental import pallas as pl
from jax.experimental.pallas import tpu as pltpu

HEAD_DIM = 128
NSA_KV_HEADS = 2
NSA_REP = 4
NSA_HEADS = NSA_KV_HEADS * NSA_REP
FOX_HEADS = 4
MEM_HEADS = 4
CMP_BLOCK = 32
CMP_STRIDE = 16
CMP_HIDDEN = 256
SLC_BLOCK = 64
SLC_TOPK = 16
WINDOW = 512
Q_BLOCK = 128
ROT_DIM = HEAD_DIM // 4
ROPE_THETA = 500000.0
CONV_W = 3
LN_EPS = 1e-5
ATTN_SCALE = HEAD_DIM ** -0.5
LOG2E = 1.4426950408889634
NEG_INF = -1e30
FORCE_BONUS = 1e3

LANE = 128
SUBLANE = 8
SROW = SUBLANE
PAGES_PER_STEP = 32
CMP_PAGES_PER_STEP = 32
VMEM_LIMIT = 56 * 1024 * 1024

COL_Q = 0
COL_KVC = 8
COL_KVS = 12
COL_KVW = 16
COL_FOX = 20
COL_QM = 32
MAIN_W = 36 * LANE
IN_TILE = 512

BF = jnp.bfloat16
F32 = jnp.float32


def _dot(a, b):
    return jnp.dot(a.astype(BF), b.astype(BF), preferred_element_type=F32)


def _dot_nt(a, b):
    return lax.dot_general(a.astype(BF), b.astype(BF), (((1,), (1,)), ((), ())),
                           preferred_element_type=F32)


def _dot_exact01(x, m01, m01_first=False):
    hi = x.astype(BF)
    r1 = x - hi.astype(F32)
    mid = r1.astype(BF)
    lo = (r1 - mid.astype(F32)).astype(BF)
    m = m01.astype(BF)
    if m01_first:
        d = lambda a: lax.dot_general(m, a, (((1,), (1,)), ((), ())), preferred_element_type=F32)
    else:
        d = lambda a: jnp.dot(a, m, preferred_element_type=F32)
    return d(hi) + d(mid) + d(lo)


def _cparams(sem):
    return pltpu.CompilerParams(dimension_semantics=sem, vmem_limit_bytes=VMEM_LIMIT)


def _masked_softmax(s, mask):
    s = jnp.where(mask, s, NEG_INF)
    m = jnp.max(s, -1, keepdims=True)
    e = jnp.where(mask, jnp.exp(s - m), 0.0)
    return e / jnp.maximum(jnp.sum(e, -1, keepdims=True), 1e-30)


def _flash_update(carry, s2, v):
    m, l, acc = carry
    m_new = jnp.maximum(m, jnp.max(s2, -1, keepdims=True))
    p = jnp.exp2(s2 - m_new)
    alpha = jnp.exp2(m - m_new)
    return (m_new, alpha * l + jnp.sum(p, -1, keepdims=True),
            alpha * acc + jnp.dot(p.astype(BF), v, preferred_element_type=F32))


def _rope_tile(x, c, s):
    lane = lax.broadcasted_iota(jnp.int32, x.shape, 1)
    half = ROT_DIM // 2
    swapped = jnp.where(lane < half, pltpu.roll(x, LANE - half, 1), pltpu.roll(x, half, 1))
    return x * c + swapped * s


def _topk_rank(score, n_valid_cols):
    col = lax.broadcasted_iota(jnp.int32, score.shape, 1)
    rank = jnp.zeros(score.shape, F32)
    for i in range(n_valid_cols):
        ci = score[:, i:i + 1]
        rank = rank + jnp.where(col > i, jnp.where(ci >= score, 1.0, 0.0),
                                jnp.where(ci > score, 1.0, 0.0))
    return rank


def _topk_select_rows(score_t, n_sel):
    n, width = score_t.shape
    rank = jnp.zeros(score_t.shape, F32)
    for i in range(n):
        ci = jnp.broadcast_to(score_t[i:i + 1, :], (SUBLANE, width))
        parts = []
        for j0 in range(0, n, SUBLANE):
            sj = score_t[j0:j0 + SUBLANE, :]
            if j0 > i:
                ahead = ci >= sj
            elif j0 + SUBLANE - 1 <= i:
                ahead = ci > sj
            else:
                row = lax.broadcasted_iota(jnp.int32, (SUBLANE, width), 0) + j0
                ahead = jnp.where(row > i, jnp.where(ci >= sj, 1.0, 0.0),
                                  jnp.where(ci > sj, 1.0, 0.0)) > 0.5
            parts.append(jnp.where(ahead, 1.0, 0.0))
        rank = rank + jnp.concatenate(parts, 0)
    return jnp.where(rank < n_sel, 1.0, 0.0)


def _inproj_kernel(x_ref, w_ref, ws_ref, bs_ref, c_ref, s_ref,
                   p_ref, qrot_ref, sig_ref, lsig_ref, stc_ref, sts_ref, stw_ref, stf_ref, xb_ref):
    j = pl.program_id(1)
    tm = x_ref.shape[0]

    def store_rows(ref, tile, first_slot, slots):
        for k in range(IN_TILE // LANE):
            ref[pl.ds(first_slot + k, tm, stride=slots), :] = tile[k]

    @pl.when(j == 0)
    def _():
        xb_ref[...] = x_ref[...].astype(BF)
        z = jnp.dot(xb_ref[...], ws_ref[...], preferred_element_type=F32) + bs_ref[...]
        sig_ref[...] = jax.nn.sigmoid(z[:, :2 * LANE])
        zf = z[:, 2 * LANE:]
        lsig_ref[...] = jnp.minimum(zf, 0.0) - jnp.log1p(jnp.exp(-jnp.abs(zf)))

    acc = jnp.dot(xb_ref[...], w_ref[...], preferred_element_type=F32)
    c = c_ref[...]
    s = s_ref[...]
    heads = IN_TILE // LANE

    @pl.when(j < 2)
    def _():
        p_ref[...] = acc
        for h in range(heads):
            qrot_ref[:, h * LANE:(h + 1) * LANE] = _rope_tile(acc[:, h * LANE:(h + 1) * LANE], c, s)

    lane_tiles = lambda a: [a[:, h * LANE:(h + 1) * LANE] for h in range(heads)]

    def roped_kv(st_ref):
        tiles = [_rope_tile(t, c, s) if h < NSA_KV_HEADS else t for h, t in enumerate(lane_tiles(acc))]
        for h, t in enumerate(tiles):
            p_ref[:, h * LANE:(h + 1) * LANE] = t
        store_rows(st_ref, tiles, 0, N_KV_SLOTS)

    @pl.when(j == COL_KVC // heads)
    def _():
        p_ref[...] = acc
        store_rows(stc_ref, lane_tiles(acc), 0, N_KV_SLOTS)

    @pl.when(j == COL_KVS // heads)
    def _():
        roped_kv(sts_ref)

    @pl.when(j == COL_KVW // heads)
    def _():
        roped_kv(stw_ref)

    fox_k = COL_FOX // heads + 1
    @pl.when((j == fox_k) | (j == fox_k + 1))
    def _():
        p_ref[...] = acc

    @pl.when(j == fox_k)
    def _():
        store_rows(stf_ref, lane_tiles(acc), 0, 2 * FOX_HEADS)

    @pl.when(j == fox_k + 1)
    def _():
        store_rows(stf_ref, lane_tiles(acc), FOX_HEADS, 2 * FOX_HEADS)

    @pl.when((j == COL_FOX // heads) | (j == COL_QM // heads))
    def _():
        p_ref[...] = acc


def _input_projection(x2d, w_main, w_small, b_small, rope_c, rope_s, tm):
    m, d = x2d.shape
    nj = MAIN_W // IN_TILE
    state_slots = (N_KV_SLOTS, N_KV_SLOTS, N_KV_SLOTS, 2 * FOX_HEADS)
    return pl.pallas_call(
        _inproj_kernel,
        grid=(m // tm, nj),
        in_specs=[
            pl.BlockSpec((tm, d), lambda i, j: (i, 0)),
            pl.BlockSpec((d, IN_TILE), lambda i, j: (0, j)),
            pl.BlockSpec((d, 3 * LANE), lambda i, j: (0, 0)),
            pl.BlockSpec((1, 3 * LANE), lambda i, j: (0, 0)),
            pl.BlockSpec((tm, LANE), lambda i, j: (i, 0)),
            pl.BlockSpec((tm, LANE), lambda i, j: (i, 0)),
        ],
        out_specs=[
            pl.BlockSpec((tm, IN_TILE), lambda i, j: (i, j)),
            pl.BlockSpec((tm, IN_TILE), lambda i, j: (i, jnp.minimum(j, 1))),
            pl.BlockSpec((tm, 2 * LANE), lambda i, j: (i, 0)),
            pl.BlockSpec((tm, LANE), lambda i, j: (i, 0)),
        ] + [pl.BlockSpec((tm * slots, LANE), lambda i, j: (i, 0)) for slots in state_slots],
        out_shape=[
            jax.ShapeDtypeStruct((m, MAIN_W), F32),
            jax.ShapeDtypeStruct((m, NSA_HEADS * HEAD_DIM), F32),
            jax.ShapeDtypeStruct((m, 2 * LANE), F32),
            jax.ShapeDtypeStruct((m, LANE), F32),
        ] + [jax.ShapeDtypeStruct((m * slots, LANE), F32) for slots in state_slots],
        scratch_shapes=[pltpu.VMEM((tm, d), BF)],
        compiler_params=_cparams(("arbitrary", "arbitrary")),
        name="input_projection",
    )(x2d, w_main, w_small, b_small, rope_c, rope_s)


def _gelu_tanh(x):
    k = np.sqrt(2.0 / np.pi).astype(np.float32)
    return x * (0.5 * (1.0 + jnp.tanh(k * (x + 0.044715 * (x ** 3)))))


def _compress_body(load, nb, w1_ref, pe_ref, b1_ref, w2_ref, out_ref, carry_ref, first_step):
    @pl.when(first_step)
    def _():
        carry_ref[...] = jnp.zeros(carry_ref.shape, F32)

    row = lax.broadcasted_iota(jnp.int32, (nb, CMP_HIDDEN), 0)
    for kv in range(2):
        xs = []
        for g in range(NSA_KV_HEADS):
            per_l = [load(kv * NSA_KV_HEADS + g, l) for l in range(CMP_STRIDE)]
            xs.append(jnp.concatenate(per_l, 1).astype(BF))
        xs.append(pe_ref[kv].astype(BF))
        fs = jnp.dot(jnp.concatenate(xs, 0), w1_ref[kv], preferred_element_type=F32)
        pos = (fs[2 * nb:2 * nb + 1, :CMP_HIDDEN] + fs[2 * nb + 1:2 * nb + 2, CMP_HIDDEN:]
               + b1_ref[kv])
        for g in range(NSA_KV_HEADS):
            col = (kv * NSA_KV_HEADS + g) * HEAD_DIM
            first = fs[g * nb:(g + 1) * nb, :CMP_HIDDEN]
            second = fs[g * nb:(g + 1) * nb, CMP_HIDDEN:]
            slot = kv * NSA_KV_HEADS + g
            prev = jnp.where(row == 0, carry_ref[slot], pltpu.roll(first, 1, 0))
            carry_ref[slot] = first[nb - 1:nb, :]
            h = _gelu_tanh(prev + second + pos)
            out_ref[:, col:col + HEAD_DIM] = jnp.dot(h.astype(BF), w2_ref[kv],
                                                     preferred_element_type=F32)


N_KV_SLOTS = 2 * NSA_KV_HEADS


def _compress_prompt_kernel(*refs):
    slots = refs[:N_KV_SLOTS]
    w1_ref, pe_ref, b1_ref, w2_ref, out_ref, carry_ref = refs[N_KV_SLOTS:]
    nb = slots[0].shape[0] // CMP_STRIDE

    def load(slot, l):
        return slots[slot][pl.ds(l, nb, stride=CMP_STRIDE), :]

    _compress_body(load, nb, w1_ref, pe_ref, b1_ref, w2_ref, out_ref, carry_ref,
                   pl.program_id(1) == 0)


def _compress_paged_kernel(pt_ref, *refs):
    del pt_ref
    pages = refs[:CMP_PAGES_PER_STEP]
    w1_ref, pe_ref, b1_ref, w2_ref, out_ref, carry_ref = refs[CMP_PAGES_PER_STEP:]
    group = N_KV_SLOTS * CMP_STRIDE
    per_page = pages[0].shape[0] // group
    blocks = jnp.concatenate([p[...].reshape(per_page, group, HEAD_DIM) for p in pages], 0)
    by_row = jnp.swapaxes(blocks, 0, 1)

    def load(slot, l):
        return by_row[l * N_KV_SLOTS + slot]

    _compress_body(load, per_page * CMP_PAGES_PER_STEP, w1_ref, pe_ref, b1_ref, w2_ref, out_ref,
                   carry_ref, pl.program_id(1) == 0)


def _compress_weight_specs():
    const3 = (lambda *a: (0, 0, 0))
    return [
        pl.BlockSpec((2, CMP_STRIDE * HEAD_DIM, 2 * CMP_HIDDEN), const3),
        pl.BlockSpec((2, SUBLANE, CMP_STRIDE * HEAD_DIM), const3),
        pl.BlockSpec((2, 1, CMP_HIDDEN), const3),
        pl.BlockSpec((2, CMP_HIDDEN, HEAD_DIM), const3),
    ]


def _compress_prompt(p_main, batch, seq, cw):
    chunk = min(seq, 2048)
    nchunk = seq // chunk
    kvw = 2 * NSA_KV_HEADS * HEAD_DIM
    return pl.pallas_call(
        _compress_prompt_kernel,
        grid=(batch, nchunk),
        in_specs=[pl.BlockSpec((chunk, HEAD_DIM), (lambda b, c, s=s: (b * nchunk + c, COL_KVC + s)))
                  for s in range(N_KV_SLOTS)] + _compress_weight_specs(),
        out_specs=pl.BlockSpec((None, chunk // CMP_STRIDE, kvw), lambda b, c: (b, c, 0)),
        out_shape=jax.ShapeDtypeStruct((batch, seq // CMP_STRIDE, kvw), F32),
        scratch_shapes=[pltpu.VMEM((2 * NSA_KV_HEADS, 1, CMP_HIDDEN), F32)],
        compiler_params=_cparams(("arbitrary", "arbitrary")),
        name="nsa_compress_prompt",
    )(*([p_main] * N_KV_SLOTS), *cw)


def _page_specs(page_rows, width, chunk_of, per_step):
    def spec(i):
        return pl.BlockSpec((None, page_rows, width),
                            lambda b, c, pt: (pt[b, chunk_of(c) * per_step + i], 0, 0))
    return [spec(i) for i in range(per_step)]


def _compress_paged(cache_rows, page_table, cw):
    n_pool, page_rows, _ = cache_rows.shape
    page = page_rows // N_KV_SLOTS
    kvw = N_KV_SLOTS * HEAD_DIM
    dec_batch, n_pages = page_table.shape
    nchunk = n_pages // CMP_PAGES_PER_STEP
    rows = CMP_PAGES_PER_STEP * page // CMP_STRIDE
    return pl.pallas_call(
        _compress_paged_kernel,
        grid_spec=pltpu.PrefetchScalarGridSpec(
            num_scalar_prefetch=1,
            grid=(dec_batch, nchunk),
            in_specs=(_page_specs(page_rows, HEAD_DIM, lambda c: c, CMP_PAGES_PER_STEP)
                      + _compress_weight_specs()),
            out_specs=pl.BlockSpec((None, rows, kvw), lambda b, c, pt: (b, c, 0)),
            scratch_shapes=[pltpu.VMEM((2 * NSA_KV_HEADS, 1, CMP_HIDDEN), F32)],
        ),
        out_shape=jax.ShapeDtypeStruct((dec_batch, nchunk * rows, kvw), F32),
        compiler_params=_cparams(("arbitrary", "arbitrary")),
        name="nsa_compress_paged",
    )(page_table, *([cache_rows] * CMP_PAGES_PER_STEP), *cw)


def _overlap_matrix(n_rows, n_blocks, blocks_first=False):
    shape = (n_blocks, n_rows) if blocks_first else (n_rows, n_blocks)
    n = lax.broadcasted_iota(jnp.int32, shape, 1 if blocks_first else 0)
    s = lax.broadcasted_iota(jnp.int32, shape, 0 if blocks_first else 1)
    c0 = (n - 1) * CMP_STRIDE
    hit = (n >= 1) & (c0 < s * SLC_BLOCK + SLC_BLOCK) & (c0 + CMP_BLOCK > s * SLC_BLOCK)
    return jnp.where(hit, 1.0, 0.0)


def _block_scores(imp, q_pos, n_blocks, block_axis=1):
    blk = lax.broadcasted_iota(jnp.int32, imp.shape, block_axis)
    cur = q_pos // SLC_BLOCK
    forced = (blk == 0) | (blk == cur) | (blk == cur - 1)
    valid = (blk * SLC_BLOCK <= q_pos) & (blk < n_blocks)
    return jnp.where(valid, imp + jnp.where(forced, FORCE_BONUS, 0.0), NEG_INF)


def _head_rows(tile_ref, h):
    tokens, heads, width = tile_ref.shape
    return tile_ref.reshape(tokens * heads, width)[pl.ds(h, tokens, stride=heads), :]


def _stack_heads(ref, n):
    return jnp.concatenate([ref[:, r * HEAD_DIM:(r + 1) * HEAD_DIM] for r in range(n)], 0)


def _nsa_prompt_kernel(qraw_ref, qrot_ref, kc_ref, vc_ref, ks_ref, vs_ref, kw_ref, vw_ref,
                       gate_ref, o_ref, ksb_ref, vsb_ref, kwb_ref, vwb_ref, *, seq, kchunk):
    i = pl.program_id(2)
    q0 = i * Q_BLOCK
    nc = kc_ref.shape[0]
    ns = seq // SLC_BLOCK
    rep = NSA_REP
    t_col = lax.broadcasted_iota(jnp.int32, (Q_BLOCK, 1), 0) + q0
    over_heads = lambda x: jnp.concatenate([x] * rep, 0)
    nt = (((1,), (1,)), ((), ()))

    @pl.when(i == 0)
    def _():
        ksb_ref[...] = ks_ref[...].astype(BF)
        vsb_ref[...] = vs_ref[...].astype(BF)
        kwb_ref[...] = kw_ref[...].astype(BF)
        vwb_ref[...] = vw_ref[...].astype(BF)

    q_raw = (_stack_heads(qraw_ref, rep) * (ATTN_SCALE * LOG2E)).astype(BF)
    n_row = lax.broadcasted_iota(jnp.int32, (1, nc), 1)
    vis = (n_row >= 1) & (n_row * CMP_STRIDE + (CMP_BLOCK - CMP_STRIDE - 1) <= t_col)
    s_c = (lax.dot_general(q_raw, kc_ref[...].astype(BF), nt, preferred_element_type=F32)
           + over_heads(jnp.where(vis, 0.0, NEG_INF)))
    e_c = jnp.exp2(s_c - jnp.max(s_c, -1, keepdims=True)) * over_heads(jnp.where(vis, 1.0, 0.0))
    p_c = e_c / jnp.maximum(jnp.sum(e_c, -1, keepdims=True), 1e-30)
    o_cmp = _dot(p_c, vc_ref[...])
    p_sum = p_c[0:Q_BLOCK]
    for r in range(1, rep):
        p_sum = p_sum + p_c[r * Q_BLOCK:(r + 1) * Q_BLOCK]
    q_rot = (_stack_heads(qrot_ref, rep) * (ATTN_SCALE * LOG2E)).astype(BF)
    wk = WINDOW + Q_BLOCK
    w0 = pl.multiple_of(jnp.maximum(q0 - WINDOW, 0), Q_BLOCK)
    dist = t_col - (w0 + lax.broadcasted_iota(jnp.int32, (1, wk), 1))
    s_w = (lax.dot_general(q_rot, kwb_ref[pl.ds(w0, wk), :], nt, preferred_element_type=F32)
           + over_heads(jnp.where((dist >= 0) & (dist < WINDOW), 0.0, NEG_INF)))
    e_w = jnp.exp2(s_w - jnp.max(s_w, -1, keepdims=True))
    p_w = e_w / jnp.sum(e_w, -1, keepdims=True)
    o_swa = jnp.dot(p_w.astype(BF), vwb_ref[pl.ds(w0, wk), :], preferred_element_type=F32)

    imp_t = _dot_exact01(p_sum, _overlap_matrix(nc, ns, blocks_first=True), m01_first=True)
    t_row = lax.broadcasted_iota(jnp.int32, (1, Q_BLOCK), 1) + q0
    sel_t = _topk_select_rows(_block_scores(imp_t, t_row, ns, block_axis=0), min(SLC_TOPK, ns))
    sel_b = sel_t.T.astype(BF)

    per_chunk = kchunk // SLC_BLOCK
    blk = lax.broadcasted_iota(jnp.int32, (ns, per_chunk), 0)
    slot = lax.broadcasted_iota(jnp.int32, (ns, per_chunk), 1)
    b_row = lax.broadcasted_iota(jnp.int32, (per_chunk, kchunk), 0)
    b_lane = lax.broadcasted_iota(jnp.int32, (per_chunk, kchunk), 1)
    expand = jnp.where(b_row == b_lane // SLC_BLOCK, 1.0, 0.0).astype(BF)
    k_lane = lax.broadcasted_iota(jnp.int32, (1, kchunk), 1)

    def slc_chunk(c, carry, diagonal):
        k0 = pl.multiple_of(c * kchunk, kchunk)
        pick = jnp.where(blk == c * per_chunk + slot, 1.0, 0.0).astype(BF)
        sel_c = jnp.dot(sel_b, pick, preferred_element_type=F32).astype(BF)
        keep = jnp.dot(sel_c, expand, preferred_element_type=F32) > 0.5
        if diagonal:
            keep = keep & (k0 + k_lane <= t_col)
        s2 = (lax.dot_general(q_rot, ksb_ref[pl.ds(k0, kchunk), :], nt, preferred_element_type=F32)
              + over_heads(jnp.where(keep, 0.0, NEG_INF)))
        return _flash_update(carry, s2, vsb_ref[pl.ds(k0, kchunk), :])

    init = (jnp.full((rep * Q_BLOCK, 1), NEG_INF, F32), jnp.zeros((rep * Q_BLOCK, 1), F32),
            jnp.zeros((rep * Q_BLOCK, HEAD_DIM), F32))
    last = q0 // kchunk
    carry = lax.fori_loop(0, last, lambda c, carry: slc_chunk(c, carry, False), init)
    _, l_s, acc_s = slc_chunk(last, carry, True)
    o_slc = acc_s / jnp.maximum(l_s, 1e-30)

    gt = gate_ref[...]
    for r in range(rep):
        rows = slice(r * Q_BLOCK, (r + 1) * Q_BLOCK)
        o_ref[:, r * HEAD_DIM:(r + 1) * HEAD_DIM] = (
            gt[:, 3 * r:3 * r + 1] * o_cmp[rows] + gt[:, 3 * r + 1:3 * r + 2] * o_slc[rows]
            + gt[:, 3 * r + 2:3 * r + 3] * o_swa[rows])


def _nsa_prompt(p_main, q_rot, kc_all, sig, batch, seq):
    nq = seq // Q_BLOCK
    gw = NSA_REP * HEAD_DIM
    kchunk = min(1024, seq)
    col = lambda base, kv: (lambda b, g, i: (b, base + kv * NSA_KV_HEADS + g))
    return pl.pallas_call(
        functools.partial(_nsa_prompt_kernel, seq=seq, kchunk=kchunk),
        grid=(batch, NSA_KV_HEADS, nq),
        in_specs=[
            pl.BlockSpec((Q_BLOCK, gw), lambda b, g, i: (b * nq + i, g)),
            pl.BlockSpec((Q_BLOCK, gw), lambda b, g, i: (b * nq + i, g)),
            pl.BlockSpec((None, seq // CMP_STRIDE, HEAD_DIM), lambda b, g, i: (b, 0, g)),
            pl.BlockSpec((None, seq // CMP_STRIDE, HEAD_DIM), lambda b, g, i: (b, 0, NSA_KV_HEADS + g)),
            pl.BlockSpec((seq, HEAD_DIM), col(COL_KVS, 0)),
            pl.BlockSpec((seq, HEAD_DIM), col(COL_KVS, 1)),
            pl.BlockSpec((seq, HEAD_DIM), col(COL_KVW, 0)),
            pl.BlockSpec((seq, HEAD_DIM), col(COL_KVW, 1)),
            pl.BlockSpec((Q_BLOCK, LANE), lambda b, g, i: (b * nq + i, g)),
        ],
        out_specs=pl.BlockSpec((Q_BLOCK, gw), lambda b, g, i: (b * nq + i, g)),
        out_shape=jax.ShapeDtypeStruct((batch * seq, NSA_HEADS * HEAD_DIM), F32),
        scratch_shapes=[pltpu.VMEM((seq, HEAD_DIM), BF)] * 4,
        compiler_params=_cparams(("arbitrary", "arbitrary", "arbitrary")),
        name="nsa_attention_prompt",
    )(p_main, q_rot, kc_all, kc_all, p_main, p_main, p_main, p_main, sig)


def _suffix_sum_lanes(x):
    n = x.shape[-1]
    lane = lax.broadcasted_iota(jnp.int32, x.shape, x.ndim - 1)
    k = 1
    while k < n:
        x = x + jnp.where(lane < n - k, pltpu.roll(x, n - k, x.ndim - 1), 0.0)
        k *= 2
    return x


def _decay_prompt_kernel(lf_ref, d_ref):
    lt = lf_ref[...].T
    top = lt[:SUBLANE]
    d_ref[...] = (_suffix_sum_lanes(top) - top) * LOG2E


def _decay_prompt(lsig, batch, seq):
    return pl.pallas_call(
        _decay_prompt_kernel,
        grid=(batch,),
        in_specs=[pl.BlockSpec((seq, LANE), lambda b: (b, 0))],
        out_specs=pl.BlockSpec((None, SUBLANE, seq), lambda b: (b, 0, 0)),
        out_shape=jax.ShapeDtypeStruct((batch, SUBLANE, seq), F32),
        compiler_params=_cparams(("arbitrary",)),
        name="fox_decay_prompt",
    )(lsig)


def _fox_prompt_kernel(q_ref, k_ref, v_ref, d_ref, o_ref, kb_ref, vb_ref, m_ref, l_ref, acc_ref,
                       *, tile):
    h = pl.program_id(1)
    i = pl.program_id(2)

    @pl.when(i == 0)
    def _():
        kb_ref[...] = k_ref[...].astype(BF)
        vb_ref[...] = v_ref[...].astype(BF)

    q = (q_ref[...] * (ATTN_SCALE * LOG2E)).astype(BF)
    row = lax.broadcasted_iota(jnp.int32, (tile, tile), 0)
    col = lax.broadcasted_iota(jnp.int32, (tile, tile), 1)

    def chunk(c, carry, diagonal):
        k0 = pl.multiple_of(c * tile, tile)
        s2 = lax.dot_general(q, kb_ref[pl.ds(k0, tile), :], (((1,), (1,)), ((), ())),
                             preferred_element_type=F32) + d_ref[pl.ds(h, 1), pl.ds(k0, tile)]
        if diagonal:
            s2 = jnp.where(col <= row, s2, NEG_INF)
        return _flash_update(carry, s2, vb_ref[pl.ds(k0, tile), :])

    init = (jnp.full((tile, 1), NEG_INF, F32), jnp.zeros((tile, 1), F32),
            jnp.zeros((tile, HEAD_DIM), F32))
    carry = lax.fori_loop(0, i, lambda c, carry: chunk(c, carry, False), init)
    _, l, acc = chunk(i, carry, True)
    o_ref[...] = acc / jnp.maximum(l, 1e-30)


def _fox_prompt(p_main, decay, batch, seq):
    tq = min(1024, seq)
    nq = seq // tq
    return pl.pallas_call(
        functools.partial(_fox_prompt_kernel, tile=tq),
        grid=(batch, FOX_HEADS, nq),
        scratch_shapes=[pltpu.VMEM((seq, HEAD_DIM), BF), pltpu.VMEM((seq, HEAD_DIM), BF),
                        pltpu.VMEM((tq, 1), F32), pltpu.VMEM((tq, 1), F32),
                        pltpu.VMEM((tq, HEAD_DIM), F32)],
        in_specs=[
            pl.BlockSpec((tq, HEAD_DIM), lambda b, h, i: (b * nq + i, COL_FOX + h)),
            pl.BlockSpec((seq, HEAD_DIM), lambda b, h, i: (b, COL_FOX + FOX_HEADS + h)),
            pl.BlockSpec((seq, HEAD_DIM), lambda b, h, i: (b, COL_FOX + 2 * FOX_HEADS + h)),
            pl.BlockSpec((None, SUBLANE, seq), lambda b, h, i: (b, 0, 0)),
        ],
        out_specs=pl.BlockSpec((tq, HEAD_DIM), lambda b, h, i: (b * nq + i, h)),
        out_shape=jax.ShapeDtypeStruct((batch * seq, FOX_HEADS * HEAD_DIM), F32),
        compiler_params=_cparams(("arbitrary", "arbitrary", "arbitrary")),
        name="fox_attention_prompt",
    )(p_main, p_main, p_main, decay)


def _mem_attn_kernel(q_ref, k_ref, v_ref, o_ref):
    s = _dot_nt(q_ref[...], k_ref[...]) * ATTN_SCALE
    m = jnp.max(s, -1, keepdims=True)
    e = jnp.exp(s - m)
    p = e / jnp.sum(e, -1, keepdims=True)
    o_ref[...] = _dot(p, v_ref[...])


def _mem_attention(p_main, mem_kv2d, batch, rows_per_batch, tq):
    nq = rows_per_batch // tq
    mlen = mem_kv2d.shape[0] // batch
    return pl.pallas_call(
        _mem_attn_kernel,
        grid=(batch, MEM_HEADS, nq),
        in_specs=[
            pl.BlockSpec((tq, HEAD_DIM), lambda b, h, i: (b * nq + i, COL_QM + h)),
            pl.BlockSpec((mlen, HEAD_DIM), lambda b, h, i: (b, h)),
            pl.BlockSpec((mlen, HEAD_DIM), lambda b, h, i: (b, MEM_HEADS + h)),
        ],
        out_specs=pl.BlockSpec((tq, HEAD_DIM), lambda b, h, i: (b * nq + i, h)),
        out_shape=jax.ShapeDtypeStruct((batch * rows_per_batch, MEM_HEADS * HEAD_DIM), F32),
        compiler_params=_cparams(("arbitrary", "arbitrary", "arbitrary")),
        name="mem_attention",
    )(p_main, mem_kv2d, mem_kv2d)


def _mem_attn_rows_kernel(q_ref, kv_ref, o_ref):
    slots = 2 * MEM_HEADS
    mlen = kv_ref.shape[0] // slots
    scores = [_dot_nt(q_ref[:, h * HEAD_DIM:(h + 1) * HEAD_DIM],
                      kv_ref[pl.ds(h, mlen, stride=slots), :]) for h in range(MEM_HEADS)]
    s = jnp.concatenate(scores, 0) * ATTN_SCALE
    m = jnp.max(s, -1, keepdims=True)
    e = jnp.exp(s - m)
    p = e / jnp.sum(e, -1, keepdims=True)
    for h in range(MEM_HEADS):
        o_ref[:, h * HEAD_DIM:(h + 1) * HEAD_DIM] = _dot(
            p[h * SROW:(h + 1) * SROW], kv_ref[pl.ds(MEM_HEADS + h, mlen, stride=slots), :])


def _mem_attention_rows(p_s, mem_rows):
    dec_batch, rows, _ = mem_rows.shape
    mw = MEM_HEADS * HEAD_DIM
    return pl.pallas_call(
        _mem_attn_rows_kernel,
        grid=(dec_batch,),
        in_specs=[pl.BlockSpec((SROW, mw), lambda b: (b, COL_QM * LANE // mw)),
                  pl.BlockSpec((None, rows, HEAD_DIM), lambda b: (b, 0, 0))],
        out_specs=pl.BlockSpec((SROW, mw), lambda b: (b, 0)),
        out_shape=jax.ShapeDtypeStruct((dec_batch * SROW, mw), F32),
        compiler_params=_cparams(("arbitrary",)),
        name="mem_attention_sample",
    )(p_s, mem_rows)


def _matmul_kernel(x_ref, w_ref, o_ref):
    o_ref[...] = jnp.dot(x_ref[...].astype(BF), w_ref[...], preferred_element_type=F32)


def _matmul(x2d, w_bf, tm, tn):
    m, k = x2d.shape
    n = w_bf.shape[1]
    return pl.pallas_call(
        _matmul_kernel,
        grid=(m // tm, n // tn),
        in_specs=[pl.BlockSpec((tm, k), lambda i, j: (i, 0)),
                  pl.BlockSpec((k, tn), lambda i, j: (0, j))],
        out_specs=pl.BlockSpec((tm, tn), lambda i, j: (i, j)),
        out_shape=jax.ShapeDtypeStruct((m, n), F32),
        compiler_params=_cparams(("arbitrary", "arbitrary")),
        name="projection_matmul",
    )(x2d, w_bf)


def _layer_norm(z, g, b):
    zc = z - jnp.mean(z, -1, keepdims=True)
    var = jnp.mean(zc * zc, -1, keepdims=True)
    return zc * lax.rsqrt(var + LN_EPS) * g + b


def _outproj_kernel(on_ref, of_ref, om_ref, x_ref, w_ref, g_ref, b_ref, h_ref, *, alpha):
    mix = jnp.concatenate([on_ref[...].astype(BF), of_ref[...].astype(BF),
                           om_ref[...].astype(BF)], 1)
    y = jnp.dot(mix, w_ref[...], preferred_element_type=F32)
    h_ref[...] = _layer_norm(alpha * x_ref[...] + y, g_ref[...], b_ref[...])


def _out_projection(o_nsa, o_fox, o_mem, x2d, w_out, g, b, tm, alpha):
    m, d = x2d.shape
    row = lambda w: pl.BlockSpec((tm, w), lambda i: (i, 0))
    const = lambda shape: pl.BlockSpec(shape, lambda i: (0, 0))
    return pl.pallas_call(
        functools.partial(_outproj_kernel, alpha=alpha),
        grid=(m // tm,),
        in_specs=[row(o_nsa.shape[1]), row(o_fox.shape[1]), row(o_mem.shape[1]), row(d),
                  const(w_out.shape), const((1, d)), const((1, d))],
        out_specs=row(d),
        out_shape=jax.ShapeDtypeStruct((m, d), F32),
        compiler_params=_cparams(("arbitrary",)),
        name="out_projection_ln",
    )(o_nsa, o_fox, o_mem, x2d, w_out, g, b)


def _ffn_core(h_ref, wa_ref, wb_ref, cw_ref, cb_ref, wd_ref, g_ref, b_ref, y_ref, acc_ref, hb_ref,
              shifted, alpha):
    j = pl.program_id(1)

    @pl.when(j == 0)
    def _():
        hb_ref[...] = h_ref[...].astype(BF)

    hb = hb_ref[...]
    a = jnp.dot(hb, wa_ref[...], preferred_element_type=F32)
    gate_in = jnp.dot(hb, wb_ref[...], preferred_element_type=F32)
    a1, a2 = shifted(a)
    cw = cw_ref[...]
    c = cb_ref[...] + a2 * cw[0:1] + a1 * cw[1:2] + a * cw[2:3]
    act = (c * jax.nn.sigmoid(c)) * gate_in
    part = jnp.dot(act.astype(BF), wd_ref[...], preferred_element_type=F32)

    @pl.when(j == 0)
    def _():
        acc_ref[...] = part

    @pl.when(j > 0)
    def _():
        acc_ref[...] += part

    @pl.when(j == pl.num_programs(1) - 1)
    def _():
        y_ref[...] = _layer_norm(alpha * h_ref[...] + acc_ref[...], g_ref[...], b_ref[...])
    return a


def _ffn_prompt_kernel(h_ref, wa_ref, wb_ref, cw_ref, cb_ref, wd_ref, g_ref, b_ref,
                       y_ref, tail_ref, acc_ref, hb_ref, halo_ref, *, tiles_per_seq, alpha):
    i = pl.program_id(0)
    j = pl.program_id(1)
    tm = h_ref.shape[0]
    seq_start = (i % tiles_per_seq) == 0

    @pl.when(seq_start)
    def _():
        halo_ref[j] = jnp.zeros(halo_ref.shape[1:], F32)

    def shifted(a):
        row = lax.broadcasted_iota(jnp.int32, a.shape, 0)
        halo = halo_ref[j]
        h1 = halo[SUBLANE - 1:SUBLANE]
        h2 = halo[SUBLANE - 2:SUBLANE - 1]
        a1 = jnp.where(row == 0, h1, pltpu.roll(a, 1, 0))
        a2 = jnp.where(row == 0, h2, jnp.where(row == 1, h1, pltpu.roll(a, 2, 0)))
        return a1, a2

    a = _ffn_core(h_ref, wa_ref, wb_ref, cw_ref, cb_ref, wd_ref, g_ref, b_ref, y_ref, acc_ref,
                  hb_ref, shifted, alpha)
    halo_ref[j] = a[tm - SUBLANE:]
    tail_ref[...] = a[tm - SUBLANE:]


def _ffn_sample_kernel(h_ref, wa_ref, wb_ref, cw_ref, cb_ref, wd_ref, g_ref, b_ref, f1_ref, f2_ref,
                       y_ref, a_ref, acc_ref, hb_ref, *, alpha):
    def shifted(a):
        s = lax.broadcasted_iota(jnp.int32, a.shape, 0) % SROW
        a1 = jnp.where(s >= 1, pltpu.roll(a, 1, 0), 0.0) + f1_ref[...]
        a2 = jnp.where(s >= 2, pltpu.roll(a, 2, 0), 0.0) + f2_ref[...]
        return a1, a2

    a_ref[...] = _ffn_core(h_ref, wa_ref, wb_ref, cw_ref, cb_ref, wd_ref, g_ref, b_ref, y_ref,
                           acc_ref, hb_ref, shifted, alpha)


def _ffn_specs(tm, d, tf, nf):
    return [
        pl.BlockSpec((tm, d), lambda i, j: (i, 0)),
        pl.BlockSpec((d, tf), lambda i, j: (0, j)),
        pl.BlockSpec((d, tf), lambda i, j: (0, nf + j)),
        pl.BlockSpec((SUBLANE, tf), lambda i, j: (0, j)),
        pl.BlockSpec((1, tf), lambda i, j: (0, j)),
        pl.BlockSpec((tf, d), lambda i, j: (j, 0)),
        pl.BlockSpec((1, d), lambda i, j: (0, 0)),
        pl.BlockSpec((1, d), lambda i, j: (0, 0)),
    ]


def _ffn_prompt(h2d, fw, seq, tm, tf, alpha):
    w_up, conv_w8, conv_b, w_down, g, b = fw
    m, d = h2d.shape
    dff = w_down.shape[0]
    nf = dff // tf
    return pl.pallas_call(
        functools.partial(_ffn_prompt_kernel, tiles_per_seq=seq // tm, alpha=alpha),
        grid=(m // tm, nf),
        in_specs=_ffn_specs(tm, d, tf, nf),
        out_specs=[pl.BlockSpec((tm, d), lambda i, j: (i, 0)),
                   pl.BlockSpec((None, SUBLANE, tf), lambda i, j: (i, 0, j))],
        out_shape=[jax.ShapeDtypeStruct((m, d), F32),
                   jax.ShapeDtypeStruct((m // tm, SUBLANE, dff), F32)],
        scratch_shapes=[pltpu.VMEM((tm, d), F32), pltpu.VMEM((tm, d), BF),
                        pltpu.VMEM((nf, SUBLANE, tf), F32)],
        compiler_params=_cparams(("arbitrary", "arbitrary")),
        name="conv_ffn_prompt",
    )(h2d, w_up, w_up, conv_w8, conv_b, w_down, g, b)


def _ffn_sample(h2d, fw, fill1, fill2, tf, alpha):
    w_up, conv_w8, conv_b, w_down, g, b = fw
    m, d = h2d.shape
    dff = w_down.shape[0]
    nf = dff // tf
    return pl.pallas_call(
        functools.partial(_ffn_sample_kernel, alpha=alpha),
        grid=(1, nf),
        in_specs=_ffn_specs(m, d, tf, nf) + [pl.BlockSpec((m, tf), lambda i, j: (0, j)),
                                             pl.BlockSpec((m, tf), lambda i, j: (0, j))],
        out_specs=[pl.BlockSpec((m, d), lambda i, j: (0, 0)),
                   pl.BlockSpec((m, tf), lambda i, j: (0, j))],
        out_shape=[jax.ShapeDtypeStruct((m, d), F32), jax.ShapeDtypeStruct((m, dff), F32)],
        scratch_shapes=[pltpu.VMEM((m, d), F32), pltpu.VMEM((m, d), BF)],
        compiler_params=_cparams(("arbitrary", "arbitrary")),
        name="conv_ffn_sample",
    )(h2d, w_up, w_up, conv_w8, conv_b, w_down, g, b, fill1, fill2)


def _nsa_sample_a_kernel(qraw_ref, qrot_ref, kc_ref, vc_ref, swa_ref, kwn_ref, vwn_ref,
                         gate_ref, part_ref, sel_ref, *, past, n_blocks):
    rep = NSA_REP
    g = pl.program_id(1)
    nc = kc_ref.shape[0]
    nsp = -(-n_blocks // LANE) * LANE
    s_col = lax.broadcasted_iota(jnp.int32, (SROW, 1), 0) + past

    q_raw = _stack_heads(qraw_ref, rep)
    s_c = (_dot_nt(q_raw, kc_ref[...]) * ATTN_SCALE).reshape(rep, SROW, nc)
    n_row = lax.broadcasted_iota(jnp.int32, (1, nc), 1)
    vis = (n_row >= 1) & (n_row * CMP_STRIDE + (CMP_BLOCK - CMP_STRIDE - 1) <= s_col)
    p_c = _masked_softmax(s_c, vis[None])
    o_cmp = _dot(p_c.reshape(rep * SROW, nc), vc_ref[...])
    imp = _dot_exact01(jnp.sum(p_c, 0), _overlap_matrix(nc, nsp))
    score = _block_scores(imp, s_col, n_blocks)
    rank = _topk_rank(score, n_blocks)
    col1 = lax.broadcasted_iota(jnp.int32, score.shape, 1).astype(F32) + 1.0
    lane = lax.broadcasted_iota(jnp.int32, (SROW, LANE), 1)
    ids = jnp.full((SROW, LANE), -1.0, F32)
    for r in range(min(SLC_TOPK, n_blocks)):
        hit = jnp.where(rank == r, jnp.where(score > 0.5 * NEG_INF, col1, 0.0), 0.0)
        ids = jnp.where(lane == r, jnp.sum(hit, -1, keepdims=True) - 1.0, ids)
    sel_ref[...] = ids.astype(jnp.int32)

    q_rot = _stack_heads(qrot_ref, rep)
    wbuf = swa_ref.shape[0] // N_KV_SLOTS
    keys = jnp.concatenate([swa_ref[pl.ds(g, wbuf, stride=N_KV_SLOTS), :], kwn_ref[...]], 0)
    vals = jnp.concatenate([swa_ref[pl.ds(NSA_KV_HEADS + g, wbuf, stride=N_KV_SLOTS), :],
                            vwn_ref[...]], 0)
    w_pos = past - wbuf + lax.broadcasted_iota(jnp.int32, (1, wbuf + SROW), 1)
    dist = s_col - w_pos
    mask = (dist >= 0) & (dist < WINDOW) & (w_pos >= 0)
    s_w = (_dot_nt(q_rot, keys) * ATTN_SCALE).reshape(rep, SROW, wbuf + SROW)
    p_w = _masked_softmax(s_w, mask[None])
    o_swa = _dot(p_w.reshape(rep * SROW, wbuf + SROW), vals)

    gt = gate_ref[...]
    for r in range(rep):
        rows = slice(r * SROW, (r + 1) * SROW)
        part_ref[rows, :] = (gt[:, 3 * r:3 * r + 1] * o_cmp[rows]
                             + gt[:, 3 * r + 2:3 * r + 3] * o_swa[rows])


def _nsa_sample_a(p_s, qrot_s, kc_all, swa_rows, sig_s, past, n_blocks):
    dec_batch, nc, _ = kc_all.shape
    gw = NSA_REP * HEAD_DIM
    bg = lambda b, g: (b, g)
    return pl.pallas_call(
        functools.partial(_nsa_sample_a_kernel, past=past, n_blocks=n_blocks),
        grid=(dec_batch, NSA_KV_HEADS),
        in_specs=[
            pl.BlockSpec((SROW, gw), bg),
            pl.BlockSpec((SROW, gw), bg),
            pl.BlockSpec((None, nc, HEAD_DIM), lambda b, g: (b, 0, g)),
            pl.BlockSpec((None, nc, HEAD_DIM), lambda b, g: (b, 0, NSA_KV_HEADS + g)),
            pl.BlockSpec((None, swa_rows.shape[1], HEAD_DIM), lambda b, g: (b, 0, 0)),
            pl.BlockSpec((SROW, HEAD_DIM), lambda b, g: (b, COL_KVW + g)),
            pl.BlockSpec((SROW, HEAD_DIM), lambda b, g: (b, COL_KVW + NSA_KV_HEADS + g)),
            pl.BlockSpec((SROW, LANE), bg),
        ],
        out_specs=[pl.BlockSpec((None, None, NSA_REP * SROW, HEAD_DIM), lambda b, g: (b, g, 0, 0)),
                   pl.BlockSpec((None, None, SROW, LANE), lambda b, g: (b, g, 0, 0))],
        out_shape=[jax.ShapeDtypeStruct((dec_batch, NSA_KV_HEADS, NSA_REP * SROW, HEAD_DIM), F32),
                   jax.ShapeDtypeStruct((dec_batch, NSA_KV_HEADS, SROW, LANE), jnp.int32)],
        compiler_params=_cparams(("arbitrary", "arbitrary")),
        name="nsa_sample_cmp_swa",
    )(p_s, qrot_s, kc_all, kc_all, swa_rows, p_s, p_s, sig_s)


def _nsa_sample_gather_kernel(pt_ref, ix_ref, *refs, n_blocks, s_len):
    del pt_ref, ix_ref
    n_sel = min(SLC_TOPK, n_blocks)
    n_fetch = s_len * n_sel
    blocks = refs[:n_fetch]
    (qrot_ref, ids_ref, knew_ref, vnew_ref, gate_ref, part_ref, o_ref) = refs[n_fetch:]
    g = pl.program_id(1)
    rep = NSA_REP
    n_keys = n_fetch * SLC_BLOCK

    def block_rows(slot):
        return jnp.concatenate([r[pl.ds(slot, SLC_BLOCK, stride=N_KV_SLOTS), :].astype(BF)
                                for r in blocks], 0)

    q = _stack_heads(qrot_ref, rep)
    s_cache = _dot_nt(q, block_rows(g)) * ATTN_SCALE

    ids = ids_ref[...]
    cached = jnp.where((ids >= 0) & (ids < n_blocks - 1), 1.0, 0.0).astype(BF)
    rank_row = lax.broadcasted_iota(jnp.int32, (LANE, n_keys), 0)
    key_col = lax.broadcasted_iota(jnp.int32, (LANE, n_keys), 1)
    by_rank = jnp.where(rank_row == (key_col // SLC_BLOCK) % n_sel, 1.0, 0.0).astype(BF)
    tok = lax.broadcasted_iota(jnp.int32, (SROW, n_keys), 0)
    owner = lax.broadcasted_iota(jnp.int32, (SROW, n_keys), 1) // (SLC_BLOCK * n_sel)
    keep_cache = jnp.where(owner == tok, jnp.dot(cached, by_rank, preferred_element_type=F32), 0.0)

    s_q = lax.broadcasted_iota(jnp.int32, (SROW, LANE), 0)
    s_k = lax.broadcasted_iota(jnp.int32, (SROW, LANE), 1)
    newest = jnp.sum(jnp.where(ids == n_blocks - 1, 1.0, 0.0), -1, keepdims=True)
    keep_new = jnp.where((s_k <= s_q) & (s_k < SROW), newest, 0.0)
    pad = jnp.zeros((LANE - SROW, HEAD_DIM), F32)
    s_new = _dot_nt(q, jnp.concatenate([knew_ref[...], pad], 0)) * ATTN_SCALE

    over_heads = lambda x: jnp.concatenate([x] * rep, 0) > 0.5
    mask_c, mask_n = over_heads(keep_cache), over_heads(keep_new)
    s_cache = jnp.where(mask_c, s_cache, NEG_INF)
    s_new = jnp.where(mask_n, s_new, NEG_INF)
    m = jnp.maximum(jnp.max(s_cache, -1, keepdims=True), jnp.max(s_new, -1, keepdims=True))
    e_c = jnp.where(mask_c, jnp.exp(s_cache - m), 0.0)
    e_n = jnp.where(mask_n, jnp.exp(s_new - m), 0.0)
    denom = jnp.maximum(jnp.sum(e_c, -1, keepdims=True) + jnp.sum(e_n, -1, keepdims=True), 1e-30)
    o_slc = (_dot(e_c, block_rows(NSA_KV_HEADS + g))
             + _dot(e_n, jnp.concatenate([vnew_ref[...], pad], 0))) / denom
    gt = gate_ref[...]
    for r in range(rep):
        rows = slice(r * SROW, (r + 1) * SROW)
        o_ref[:, r * HEAD_DIM:(r + 1) * HEAD_DIM] = (
            part_ref[rows, :] + gt[:, 3 * r + 1:3 * r + 2] * o_slc[rows])


def _nsa_sample_gather(cache_rows, page_table, ids, p_s, qrot_s, sig_s, part, n_blocks, s_len):
    n_pool, page_rows, _ = cache_rows.shape
    dec_batch, n_pages = page_table.shape
    n_sel = min(SLC_TOPK, n_blocks)
    per_page = page_rows // (N_KV_SLOTS * SLC_BLOCK)
    half_rows = N_KV_SLOTS * SLC_BLOCK
    gw = NSA_REP * HEAD_DIM
    flat_ids = ids[:, :, :s_len, :n_sel].reshape(dec_batch, NSA_KV_HEADS * s_len * n_sel)
    pool_page = jnp.take_along_axis(page_table, jnp.clip(flat_ids // per_page, 0, n_pages - 1), 1)
    in_page = jnp.maximum(flat_ids, 0) % per_page

    def fetch_spec(n):
        def index(b, g, pp, ip):
            return (pp[b, g * (s_len * n_sel) + n], ip[b, g * (s_len * n_sel) + n], 0)
        return pl.BlockSpec((None, half_rows, HEAD_DIM), index)

    bg = lambda b, g, pt, ix: (b, g)
    return pl.pallas_call(
        functools.partial(_nsa_sample_gather_kernel, n_blocks=n_blocks, s_len=s_len),
        grid_spec=pltpu.PrefetchScalarGridSpec(
            num_scalar_prefetch=2,
            grid=(dec_batch, NSA_KV_HEADS),
            in_specs=[fetch_spec(n) for n in range(s_len * n_sel)] + [
                pl.BlockSpec((SROW, gw), bg),
                pl.BlockSpec((None, None, SROW, LANE), lambda b, g, pt, ix: (b, g, 0, 0)),
                pl.BlockSpec((SROW, HEAD_DIM), lambda b, g, pt, ix: (b, COL_KVS + g)),
                pl.BlockSpec((SROW, HEAD_DIM), lambda b, g, pt, ix: (b, COL_KVS + NSA_KV_HEADS + g)),
                pl.BlockSpec((SROW, LANE), bg),
                pl.BlockSpec((None, None, NSA_REP * SROW, HEAD_DIM), lambda b, g, pt, ix: (b, g, 0, 0)),
            ],
            out_specs=pl.BlockSpec((SROW, gw), bg),
        ),
        out_shape=jax.ShapeDtypeStruct((dec_batch * SROW, NSA_HEADS * HEAD_DIM), F32),
        compiler_params=_cparams(("arbitrary", "arbitrary")),
        name="nsa_sample_selected",
    )(pool_page, in_page, *([cache_rows] * (s_len * n_sel)), qrot_s, ids, p_s, p_s, sig_s, part)


def _fox_sample_kernel(pt_ref, *refs, s_len):
    n = PAGES_PER_STEP
    kpages, vpages = refs[:n], refs[n:2 * n]
    (logf_ref, q_ref, knew_ref, vnew_ref, lnew_ref,
     o_ref, m_ref, l_ref, acc_ref, carry_ref) = refs[2 * n:]
    b = pl.program_id(0)
    c = pl.program_id(1)
    page = kpages[0].shape[0]
    head = lambda ref, h: ref[:, h * HEAD_DIM:(h + 1) * HEAD_DIM]

    def cache_rows(pages, h):
        return jnp.concatenate([_head_rows(p, h).astype(BF) for p in pages], 0)

    def update(scores, mask, values):
        s = jnp.concatenate(scores, 0)
        if mask is not None:
            s = jnp.where(mask, s, NEG_INF)
        m_old = m_ref[...]
        m_new = jnp.maximum(m_old, jnp.max(s, -1, keepdims=True))
        p = jnp.exp(s - m_new)
        if mask is not None:
            p = jnp.where(mask, p, 0.0)
        alpha = jnp.exp(m_old - m_new)
        pv = jnp.concatenate([_dot(p[h * SROW:(h + 1) * SROW], values[h])
                              for h in range(FOX_HEADS)], 0)
        m_ref[...] = m_new
        l_ref[...] = alpha * l_ref[...] + jnp.sum(p, -1, keepdims=True)
        acc_ref[...] = alpha * acc_ref[...] + pv

    @pl.when(c == 0)
    def _():
        m_ref[...] = jnp.full(m_ref.shape, NEG_INF, F32)
        l_ref[...] = jnp.zeros(l_ref.shape, F32)
        acc_ref[...] = jnp.zeros(acc_ref.shape, F32)
        s_q = lax.broadcasted_iota(jnp.int32, (SROW, LANE), 0)
        s_k = lax.broadcasted_iota(jnp.int32, (SROW, LANE), 1)
        later = jnp.where((s_q > s_k) & (s_q < s_len), 1.0, 0.0)
        real = lax.broadcasted_iota(jnp.int32, (SROW, 1), 0) < s_len
        lnew = lnew_ref[...]
        pad = jnp.zeros((LANE - SROW, HEAD_DIM), F32)
        rows_q = lax.broadcasted_iota(jnp.int32, (FOX_HEADS * SROW, LANE), 0) % SROW
        cols_k = lax.broadcasted_iota(jnp.int32, (FOX_HEADS * SROW, LANE), 1)
        mask = (cols_k <= rows_q) & (cols_k < s_len)
        totals, scores, values = [], [], []
        for h in range(FOX_HEADS):
            lf = jnp.where(real, lnew[:, h:h + 1], 0.0)
            d_row = jnp.sum(lf * later, 0, keepdims=True)
            totals.append(jnp.sum(lf, 0, keepdims=True))
            k = jnp.concatenate([head(knew_ref, h), pad], 0)
            values.append(jnp.concatenate([head(vnew_ref, h), pad], 0))
            scores.append(_dot_nt(head(q_ref, h), k) * ATTN_SCALE + d_row)
        update(scores, mask, values)
        totals += [jnp.zeros((1, 1), F32)] * (SUBLANE - FOX_HEADS)
        carry_ref[...] = jnp.broadcast_to(jnp.concatenate(totals, 0), carry_ref.shape)

    @pl.when(c > 0)
    def _():
        ids = [pt_ref[b, c * n + i] for i in range(n)]
        lf = jnp.concatenate(
            [jnp.concatenate([logf_ref[h, pl.ds(pg, 1), :] for pg in ids], 1) for h in range(FOX_HEADS)]
            + [jnp.zeros((SUBLANE - FOX_HEADS, n * page), F32)], 0)
        incl = _suffix_sum_lanes(lf)
        carry = carry_ref[:, 0:1]
        decay = carry + (incl - lf)
        carry_ref[...] = jnp.broadcast_to(carry + incl[:, 0:1], carry_ref.shape)
        scores = [_dot_nt(head(q_ref, h), cache_rows(kpages, h)) * ATTN_SCALE + decay[h:h + 1, :]
                  for h in range(FOX_HEADS)]
        update(scores, None, [cache_rows(vpages, h) for h in range(FOX_HEADS)])

    @pl.when(c == pl.num_programs(1) - 1)
    def _():
        o = acc_ref[...] / jnp.maximum(l_ref[...], 1e-30)
        for h in range(FOX_HEADS):
            o_ref[:, h * HEAD_DIM:(h + 1) * HEAD_DIM] = o[h * SROW:(h + 1) * SROW]


def _fox_sample(cache, logf_t, page_table, p_s, lsig_s, s_len):
    n_pool, page = cache.shape[:2]
    dec_batch, n_pages = page_table.shape
    nchunk = n_pages // PAGES_PER_STEP
    fw = FOX_HEADS * HEAD_DIM
    row = lambda w, col: pl.BlockSpec((SROW, w), lambda b, c, pt: (b, col))
    chunks = page_table.reshape(dec_batch, nchunk, PAGES_PER_STEP)[:, ::-1]
    steps = jnp.concatenate([chunks[:, :1], chunks], 1).reshape(dec_batch, (nchunk + 1) * PAGES_PER_STEP)

    def half_specs(kv):
        def spec(i):
            return pl.BlockSpec((None, page, None, FOX_HEADS, HEAD_DIM),
                                lambda b, c, pt: (pt[b, c * PAGES_PER_STEP + i], 0, kv, 0, 0))
        return [spec(i) for i in range(PAGES_PER_STEP)]

    return pl.pallas_call(
        functools.partial(_fox_sample_kernel, s_len=s_len),
        grid_spec=pltpu.PrefetchScalarGridSpec(
            num_scalar_prefetch=1,
            grid=(dec_batch, nchunk + 1),
            in_specs=half_specs(0) + half_specs(1) + [
                pl.BlockSpec(logf_t.shape, lambda b, c, pt: (0, 0, 0), pipeline_mode=pl.Buffered(1)),
                row(fw, COL_FOX * LANE // fw),
                row(fw, COL_FOX * LANE // fw + 1),
                row(fw, COL_FOX * LANE // fw + 2),
                row(LANE, 0),
            ],
            out_specs=row(fw, 0),
            scratch_shapes=[pltpu.VMEM((FOX_HEADS * SROW, 1), F32),
                            pltpu.VMEM((FOX_HEADS * SROW, 1), F32),
                            pltpu.VMEM((FOX_HEADS * SROW, HEAD_DIM), F32),
                            pltpu.VMEM((SUBLANE, LANE), F32)],
        ),
        out_shape=jax.ShapeDtypeStruct((dec_batch * SROW, fw), F32),
        compiler_params=_cparams(("arbitrary", "arbitrary")),
        name="fox_attention_sample",
    )(steps, *([cache] * (2 * PAGES_PER_STEP)), logf_t, p_s, p_s, p_s, lsig_s)


def _rope_tables(pos):
    half = ROT_DIM // 2
    inv = jnp.power(ROPE_THETA, -jnp.arange(half, dtype=F32) * (2.0 / ROT_DIM))
    ang = pos.astype(F32)[:, None] * inv[None, :]
    cos, sin = jnp.cos(ang), jnp.sin(ang)
    n = pos.shape[0]
    c = jnp.concatenate([cos, cos, jnp.ones((n, HEAD_DIM - ROT_DIM), F32)], 1)
    s = jnp.concatenate([-sin, sin, jnp.zeros((n, HEAD_DIM - ROT_DIM), F32)], 1)
    return c, s


def _layer_weights(w_in, b_gate, b_forget, cmp_w1, cmp_b1, cmp_w2, cmp_pos, w_mem_kv, w_out,
                   ln1_g, ln1_b, w_up, conv_w, conv_b, w_down, ln2_g, ln2_b):
    d = w_in.shape[0]
    nq = NSA_HEADS * HEAD_DIM
    kvw = 2 * NSA_KV_HEADS * HEAD_DIM
    o_g = nq + 3 * kvw
    o_fox = o_g + 3 * NSA_HEADS
    o_f = o_fox + 3 * FOX_HEADS * HEAD_DIM
    o_qm = o_f + FOX_HEADS
    w_main = jnp.concatenate([w_in[:, :o_g], w_in[:, o_fox:o_f], w_in[:, o_qm:]], 1).astype(BF)
    per_group = 3 * NSA_REP
    zpad = lambda n: jnp.zeros((d, n), w_in.dtype)
    w_small = jnp.concatenate([
        w_in[:, o_g:o_g + per_group], zpad(LANE - per_group),
        w_in[:, o_g + per_group:o_fox], zpad(LANE - per_group),
        w_in[:, o_f:o_qm], zpad(LANE - FOX_HEADS)], 1).astype(BF)
    bpad = lambda n: jnp.zeros((n,), F32)
    b_small = jnp.concatenate([
        b_gate[:per_group], bpad(LANE - per_group), b_gate[per_group:], bpad(LANE - per_group),
        b_forget, bpad(LANE - FOX_HEADS)])[None, :].astype(F32)
    kdim = CMP_STRIDE * HEAD_DIM
    w1cat = jnp.concatenate([cmp_w1[:, :CMP_STRIDE].reshape(2, kdim, CMP_HIDDEN),
                             cmp_w1[:, CMP_STRIDE:].reshape(2, kdim, CMP_HIDDEN)], 2).astype(BF)
    pe = jnp.concatenate([cmp_pos[:, :CMP_STRIDE].reshape(2, 1, kdim),
                          cmp_pos[:, CMP_STRIDE:].reshape(2, 1, kdim),
                          jnp.zeros((2, SUBLANE - 2, kdim), F32)], 1)
    cw = (w1cat, pe, cmp_b1[:, None, :], cmp_w2.astype(BF))
    conv_w8 = jnp.concatenate([conv_w, jnp.zeros((SUBLANE - CONV_W, conv_w.shape[1]), F32)], 0)
    fw = (w_up.astype(BF), conv_w8, conv_b[None, :], w_down.astype(BF), ln2_g[None, :], ln2_b[None, :])
    return dict(w_main=w_main, w_small=w_small, b_small=b_small, cw=cw,
                w_mem_kv=w_mem_kv.astype(BF), w_out=w_out.astype(BF),
                ln1_g=ln1_g[None, :], ln1_b=ln1_b[None, :], fw=fw)


def _prompt_layer(x, mem, lw, alpha):
    batch, seq, d = x.shape
    x2d = x.reshape(batch * seq, d)
    rc, rs = _rope_tables(jnp.tile(jnp.arange(seq, dtype=jnp.int32), batch))
    tm = min(512, seq)
    p_main, q_rot, sig, lsig, st_c, st_s, st_w, st_f = _input_projection(
        x2d, lw["w_main"], lw["w_small"], lw["b_small"], rc, rs, tm)
    kc_all = _compress_prompt(p_main, batch, seq, lw["cw"])
    o_nsa = _nsa_prompt(p_main, q_rot, kc_all, sig, batch, seq)
    decay = _decay_prompt(lsig, batch, seq)
    o_fox = _fox_prompt(p_main, decay, batch, seq)
    mlen = mem.shape[1]
    mem_kv = _matmul(mem.reshape(batch * mlen, d), lw["w_mem_kv"], min(256, batch * mlen), 512)
    o_mem = _mem_attention(p_main, mem_kv, batch, seq, min(2048, seq))
    h = _out_projection(o_nsa, o_fox, o_mem, x2d, lw["w_out"], lw["ln1_g"], lw["ln1_b"],
                        min(512, seq), alpha)
    y, tail = _ffn_prompt(h, lw["fw"], seq, tm, 512, alpha)
    kv_state = lambda st: st.reshape(batch, seq, 2, -1, HEAD_DIM)
    n_win = min(WINDOW, seq)
    tiles = seq // tm
    conv_state = tail.reshape(batch, tiles, SUBLANE, -1)[:, -1, SUBLANE - (CONV_W - 1):]
    states = (kv_state(st_c), kv_state(st_s), kv_state(st_f),
              lsig[:, :FOX_HEADS].reshape(batch, seq, FOX_HEADS),
              kv_state(st_w)[:, seq - n_win:],
              mem_kv.reshape(batch, mlen, 2, MEM_HEADS, HEAD_DIM), conv_state)
    return y.reshape(batch, seq, d), states


def _sample_layer(x, c_cmp, c_slc, c_fox, c_logf, c_swa, c_mem, s_conv, page_table, lw, alpha):
    dec_batch, s_len, d = x.shape
    n_pool, page = c_cmp.shape[0], c_cmp.shape[1]
    n_pages = page_table.shape[1]
    past = n_pages * page
    t_all = past + s_len
    assert s_len <= SROW and n_pages % PAGES_PER_STEP == 0
    assert (t_all // CMP_STRIDE) * CMP_STRIDE == past and past % SLC_BLOCK == 0
    n_blocks = -(-t_all // SLC_BLOCK)
    xp = jnp.pad(x, ((0, 0), (0, SROW - s_len), (0, 0))).reshape(dec_batch * SROW, d)
    pos = past + jnp.tile(jnp.arange(SROW, dtype=jnp.int32), dec_batch)
    rc, rs = _rope_tables(pos)
    p_s, qrot_s, sig_s, lsig_s = _input_projection(xp, lw["w_main"], lw["w_small"], lw["b_small"],
                                                   rc, rs, dec_batch * SROW)[:4]
    kvw = 2 * NSA_KV_HEADS * HEAD_DIM
    as_rows = lambda a: a.reshape(a.shape[0], -1, HEAD_DIM)
    kc_all = _compress_paged(as_rows(c_cmp), page_table, lw["cw"])
    part, ids = _nsa_sample_a(p_s, qrot_s, kc_all, as_rows(c_swa), sig_s, past, n_blocks)
    o_nsa = _nsa_sample_gather(as_rows(c_slc), page_table, ids, p_s, qrot_s, sig_s, part,
                               n_blocks, s_len)
    logf_t = jnp.transpose(c_logf, (2, 0, 1))
    o_fox = _fox_sample(c_fox, logf_t, page_table, p_s, lsig_s, s_len)
    o_mem = _mem_attention_rows(p_s, as_rows(c_mem))
    h = _out_projection(o_nsa, o_fox, o_mem, xp, lw["w_out"], lw["ln1_g"], lw["ln1_b"],
                        dec_batch * SROW, alpha)
    dff = s_conv.shape[-1]
    zrow = jnp.zeros((dec_batch, 1, dff), F32)
    fill1 = jnp.concatenate([s_conv[:, 1:2]] + [zrow] * (SROW - 1), 1).reshape(dec_batch * SROW, dff)
    fill2 = jnp.concatenate([s_conv[:, 0:1], s_conv[:, 1:2]] + [zrow] * (SROW - 2), 1)
    y, a = _ffn_sample(h, lw["fw"], fill1, fill2.reshape(dec_batch * SROW, dff), 512, alpha)

    def rows(arr):
        return arr.reshape(dec_batch, SROW, -1)[:, :s_len]
    cols = lambda c0, w: rows(p_s[:, c0 * LANE:c0 * LANE + w]).reshape(dec_batch, s_len, 2, -1, HEAD_DIM)
    new_kv_swa = cols(COL_KVW, kvw)
    new_swa = jnp.concatenate([c_swa, new_kv_swa], 1)[:, s_len:]
    conv_state = jnp.concatenate([s_conv, rows(a)], 1)[:, -(CONV_W - 1):]
    states = (cols(COL_KVC, kvw), cols(COL_KVS, kvw),
              cols(COL_FOX + FOX_HEADS, 2 * FOX_HEADS * HEAD_DIM),
              rows(lsig_s[:, :FOX_HEADS]), new_swa, conv_state)
    return rows(y), states


def kernel(x_prompt, x_sample, mem_prompt, cache_nsa_cmp, cache_nsa_slc, cache_fox_kv, cache_fox_logf, cache_nsa_swa, cache_mem, state_conv, page_table, w_in, b_gate, b_forget, cmp_w1, cmp_b1, cmp_w2, cmp_pos, w_mem_kv, w_out, ln1_g, ln1_b, w_up, conv_w, conv_b, w_down, ln2_g, ln2_b):
    depth = w_in.shape[0]
    alpha = float((2 * depth) ** 0.25)
    yp, ys = x_prompt, x_sample
    acc_p = [[] for _ in range(7)]
    acc_s = [[] for _ in range(6)]
    for l in range(depth):
        lw = _layer_weights(w_in[l], b_gate[l], b_forget[l], cmp_w1[l], cmp_b1[l], cmp_w2[l],
                            cmp_pos[l], w_mem_kv[l], w_out[l], ln1_g[l], ln1_b[l], w_up[l],
                            conv_w[l], conv_b[l], w_down[l], ln2_g[l], ln2_b[l])
        yp, st_p = _prompt_layer(yp, mem_prompt, lw, alpha)
        ys, st_s = _sample_layer(ys, cache_nsa_cmp[l], cache_nsa_slc[l], cache_fox_kv[l],
                                 cache_fox_logf[l], cache_nsa_swa[l], cache_mem[l], state_conv[l],
                                 page_table, lw, alpha)
        for lst, a in zip(acc_p, st_p):
            lst.append(a)
        for lst, a in zip(acc_s, st_s):
            lst.append(a)
    outs_p = [jnp.stack(a, 0) for a in acc_p]
    outs_s = [jnp.stack(a, 0) for a in acc_s]
    return (yp, ys, *outs_p, *outs_s)
```

```python
import functools

import jax
import jax.numpy as jnp
import numpy as np
from jax import lax
from jax.experimental import pallas as pl
from jax.experimental.pallas import tpu as pltpu

HEAD_DIM = 128
NSA_KV_HEADS = 2
NSA_REP = 4
NSA_HEADS = NSA_KV_HEADS * NSA_REP
FOX_HEADS = 4
MEM_HEADS = 4
CMP_BLOCK = 32
CMP_STRIDE = 16
CMP_HIDDEN = 256
SLC_BLOCK = 64
SLC_TOPK = 16
WINDOW = 512
Q_BLOCK = 128
ROT_DIM = HEAD_DIM // 4
ROPE_THETA = 500000.0
CONV_W = 3
LN_EPS = 1e-5
ATTN_SCALE = HEAD_DIM ** -0.5
LOG2E = 1.4426950408889634
NEG_INF = -1e30
FORCE_BONUS = 1e3

LANE = 128
SUBLANE = 8
SROW = SUBLANE
PAGES_PER_STEP = 32
CMP_PAGES_PER_STEP = 32
VMEM_LIMIT = 56 * 1024 * 1024

COL_Q = 0
COL_KVC = 8
COL_KVS = 12
COL_KVW = 16
COL_FOX = 20
COL_QM = 32
MAIN_W = 36 * LANE
IN_TILE = 512

BF = jnp.bfloat16
F32 = jnp.float32


def _dot(a, b):
    return jnp.dot(a.astype(BF), b.astype(BF), preferred_element_type=F32)


def _dot_nt(a, b):
    return lax.dot_general(a.astype(BF), b.astype(BF), (((1,), (1,)), ((), ())),
                           preferred_element_type=F32)


def _dot_exact01(x, m01, m01_first=False):
    hi = x.astype(BF)
    r1 = x - hi.astype(F32)
    mid = r1.astype(BF)
    lo = (r1 - mid.astype(F32)).astype(BF)
    m = m01.astype(BF)
    if m01_first:
        d = lambda a: lax.dot_general(m, a, (((1,), (1,)), ((), ())), preferred_element_type=F32)
    else:
        d = lambda a: jnp.dot(a, m, preferred_element_type=F32)
    return d(hi) + d(mid) + d(lo)


def _cparams(sem):
    return pltpu.CompilerParams(dimension_semantics=sem, vmem_limit_bytes=VMEM_LIMIT)


def _masked_softmax(s, mask):
    s = jnp.where(mask, s, NEG_INF)
    m = jnp.max(s, -1, keepdims=True)
    e = jnp.where(mask, jnp.exp(s - m), 0.0)
    return e / jnp.maximum(jnp.sum(e, -1, keepdims=True), 1e-30)


def _flash_update(carry, s2, v):
    m, l, acc = carry
    m_new = jnp.maximum(m, jnp.max(s2, -1, keepdims=True))
    p = jnp.exp2(s2 - m_new)
    alpha = jnp.exp2(m - m_new)
    return (m_new, alpha * l + jnp.sum(p, -1, keepdims=True),
            alpha * acc + jnp.dot(p.astype(BF), v, preferred_element_type=F32))


def _rope_tile(x, c, s):
    lane = lax.broadcasted_iota(jnp.int32, x.shape, 1)
    half = ROT_DIM // 2
    swapped = jnp.where(lane < half, pltpu.roll(x, LANE - half, 1), pltpu.roll(x, half, 1))
    return x * c + swapped * s


def _topk_rank(score, n_valid_cols):
    col = lax.broadcasted_iota(jnp.int32, score.shape, 1)
    rank = jnp.zeros(score.shape, F32)
    for i in range(n_valid_cols):
        ci = score[:, i:i + 1]
        rank = rank + jnp.where(col > i, jnp.where(ci >= score, 1.0, 0.0),
                                jnp.where(ci > score, 1.0, 0.0))
    return rank


def _topk_select_rows(score_t, n_sel):
    n, width = score_t.shape
    rank = jnp.zeros(score_t.shape, F32)
    for i in range(n):
        ci = jnp.broadcast_to(score_t[i:i + 1, :], (SUBLANE, width))
        parts = []
        for j0 in range(0, n, SUBLANE):
            sj = score_t[j0:j0 + SUBLANE, :]
            if j0 > i:
                ahead = ci >= sj
            elif j0 + SUBLANE - 1 <= i:
                ahead = ci > sj
            else:
                row = lax.broadcasted_iota(jnp.int32, (SUBLANE, width), 0) + j0
                ahead = jnp.where(row > i, jnp.where(ci >= sj, 1.0, 0.0),
                                  jnp.where(ci > sj, 1.0, 0.0)) > 0.5
            parts.append(jnp.where(ahead, 1.0, 0.0))
        rank = rank + jnp.concatenate(parts, 0)
    return jnp.where(rank < n_sel, 1.0, 0.0)


def _inproj_kernel(x_ref, w_ref, ws_ref, bs_ref, c_ref, s_ref,
                   p_ref, qrot_ref, sig_ref, lsig_ref, stc_ref, sts_ref, stw_ref, stf_ref, xb_ref):
    j = pl.program_id(1)
    tm = x_ref.shape[0]

    def store_rows(ref, tile, first_slot, slots):
        for k in range(IN_TILE // LANE):
            ref[pl.ds(first_slot + k, tm, stride=slots), :] = tile[k]

    @pl.when(j == 0)
    def _():
        xb_ref[...] = x_ref[...].astype(BF)
        z = jnp.dot(xb_ref[...], ws_ref[...], preferred_element_type=F32) + bs_ref[...]
        sig_ref[...] = jax.nn.sigmoid(z[:, :2 * LANE])
        zf = z[:, 2 * LANE:]
        lsig_ref[...] = jnp.minimum(zf, 0.0) - jnp.log1p(jnp.exp(-jnp.abs(zf)))

    acc = jnp.dot(xb_ref[...], w_ref[...], preferred_element_type=F32)
    c = c_ref[...]
    s = s_ref[...]
    heads = IN_TILE // LANE

    @pl.when(j < 2)
    def _():
        p_ref[...] = acc
        for h in range(heads):
            qrot_ref[:, h * LANE:(h + 1) * LANE] = _rope_tile(acc[:, h * LANE:(h + 1) * LANE], c, s)

    lane_tiles = lambda a: [a[:, h * LANE:(h + 1) * LANE] for h in range(heads)]

    def roped_kv(st_ref):
        tiles = [_rope_tile(t, c, s) if h < NSA_KV_HEADS else t for h, t in enumerate(lane_tiles(acc))]
        for h, t in enumerate(tiles):
            p_ref[:, h * LANE:(h + 1) * LANE] = t
        store_rows(st_ref, tiles, 0, N_KV_SLOTS)

    @pl.when(j == COL_KVC // heads)
    def _():
        p_ref[...] = acc
        store_rows(stc_ref, lane_tiles(acc), 0, N_KV_SLOTS)

    @pl.when(j == COL_KVS // heads)
    def _():
        roped_kv(sts_ref)

    @pl.when(j == COL_KVW // heads)
    def _():
        roped_kv(stw_ref)

    fox_k = COL_FOX // heads + 1
    @pl.when((j == fox_k) | (j == fox_k + 1))
    def _():
        p_ref[...] = acc

    @pl.when(j == fox_k)
    def _():
        store_rows(stf_ref, lane_tiles(acc), 0, 2 * FOX_HEADS)

    @pl.when(j == fox_k + 1)
    def _():
        store_rows(stf_ref, lane_tiles(acc), FOX_HEADS, 2 * FOX_HEADS)

    @pl.when((j == COL_FOX // heads) | (j == COL_QM // heads))
    def _():
        p_ref[...] = acc


def _input_projection(x2d, w_main, w_small, b_small, rope_c, rope_s, tm):
    m, d = x2d.shape
    nj = MAIN_W // IN_TILE
    state_slots = (N_KV_SLOTS, N_KV_SLOTS, N_KV_SLOTS, 2 * FOX_HEADS)
    return pl.pallas_call(
        _inproj_kernel,
        grid=(m // tm, nj),
        in_specs=[
            pl.BlockSpec((tm, d), lambda i, j: (i, 0)),
            pl.BlockSpec((d, IN_TILE), lambda i, j: (0, j)),
            pl.BlockSpec((d, 3 * LANE), lambda i, j: (0, 0)),
            pl.BlockSpec((1, 3 * LANE), lambda i, j: (0, 0)),
            pl.BlockSpec((tm, LANE), lambda i, j: (i, 0)),
            pl.BlockSpec((tm, LANE), lambda i, j: (i, 0)),
        ],
        out_specs=[
            pl.BlockSpec((tm, IN_TILE), lambda i, j: (i, j)),
            pl.BlockSpec((tm, IN_TILE), lambda i, j: (i, jnp.minimum(j, 1))),
            pl.BlockSpec((tm, 2 * LANE), lambda i, j: (i, 0)),
            pl.BlockSpec((tm, LANE), lambda i, j: (i, 0)),
        ] + [pl.BlockSpec((tm * slots, LANE), lambda i, j: (i, 0)) for slots in state_slots],
        out_shape=[
            jax.ShapeDtypeStruct((m, MAIN_W), F32),
            jax.ShapeDtypeStruct((m, NSA_HEADS * HEAD_DIM), F32),
            jax.ShapeDtypeStruct((m, 2 * LANE), F32),
            jax.ShapeDtypeStruct((m, LANE), F32),
        ] + [jax.ShapeDtypeStruct((m * slots, LANE), F32) for slots in state_slots],
        scratch_shapes=[pltpu.VMEM((tm, d), BF)],
        compiler_params=_cparams(("arbitrary", "arbitrary")),
        name="input_projection",
    )(x2d, w_main, w_small, b_small, rope_c, rope_s)


def _gelu_tanh(x):
    k = np.sqrt(2.0 / np.pi).astype(np.float32)
    return x * (0.5 * (1.0 + jnp.tanh(k * (x + 0.044715 * (x ** 3)))))


def _compress_body(load, nb, w1_ref, pe_ref, b1_ref, w2_ref, out_ref, carry_ref, first_step):
    @pl.when(first_step)
    def _():
        carry_ref[...] = jnp.zeros(carry_ref.shape, F32)

    row = lax.broadcasted_iota(jnp.int32, (nb, CMP_HIDDEN), 0)
    for kv in range(2):
        xs = []
        for g in range(NSA_KV_HEADS):
            per_l = [load(kv * NSA_KV_HEADS + g, l) for l in range(CMP_STRIDE)]
            xs.append(jnp.concatenate(per_l, 1).astype(BF))
        xs.append(pe_ref[kv].astype(BF))
        fs = jnp.dot(jnp.concatenate(xs, 0), w1_ref[kv], preferred_element_type=F32)
        pos = (fs[2 * nb:2 * nb + 1, :CMP_HIDDEN] + fs[2 * nb + 1:2 * nb + 2, CMP_HIDDEN:]
               + b1_ref[kv])
        for g in range(NSA_KV_HEADS):
            col = (kv * NSA_KV_HEADS + g) * HEAD_DIM
            first = fs[g * nb:(g + 1) * nb, :CMP_HIDDEN]
            second = fs[g * nb:(g + 1) * nb, CMP_HIDDEN:]
            slot = kv * NSA_KV_HEADS + g
            prev = jnp.where(row == 0, carry_ref[slot], pltpu.roll(first, 1, 0))
            carry_ref[slot] = first[nb - 1:nb, :]
            h = _gelu_tanh(prev + second + pos)
            out_ref[:, col:col + HEAD_DIM] = jnp.dot(h.astype(BF), w2_ref[kv],
                                                     preferred_element_type=F32)


N_KV_SLOTS = 2 * NSA_KV_HEADS


def _compress_prompt_kernel(*refs):
    slots = refs[:N_KV_SLOTS]
    w1_ref, pe_ref, b1_ref, w2_ref, out_ref, carry_ref = refs[N_KV_SLOTS:]
    nb = slots[0].shape[0] // CMP_STRIDE

    def load(slot, l):
        return slots[slot][pl.ds(l, nb, stride=CMP_STRIDE), :]

    _compress_body(load, nb, w1_ref, pe_ref, b1_ref, w2_ref, out_ref, carry_ref,
                   pl.program_id(1) == 0)


def _compress_paged_kernel(pt_ref, *refs):
    del pt_ref
    pages = refs[:CMP_PAGES_PER_STEP]
    w1_ref, pe_ref, b1_ref, w2_ref, out_ref, carry_ref = refs[CMP_PAGES_PER_STEP:]
    group = N_KV_SLOTS * CMP_STRIDE
    per_page = pages[0].shape[0] // group
    blocks = jnp.concatenate([p[...].reshape(per_page, group, HEAD_DIM) for p in pages], 0)
    by_row = jnp.swapaxes(blocks, 0, 1)

    def load(slot, l):
        return by_row[l * N_KV_SLOTS + slot]

    _compress_body(load, per_page * CMP_PAGES_PER_STEP, w1_ref, pe_ref, b1_ref, w2_ref, out_ref,
                   carry_ref, pl.program_id(1) == 0)


def _compress_weight_specs():
    const3 = (lambda *a: (0, 0, 0))
    return [
        pl.BlockSpec((2, CMP_STRIDE * HEAD_DIM, 2 * CMP_HIDDEN), const3),
        pl.BlockSpec((2, SUBLANE, CMP_STRIDE * HEAD_DIM), const3),
        pl.BlockSpec((2, 1, CMP_HIDDEN), const3),
        pl.BlockSpec((2, CMP_HIDDEN, HEAD_DIM), const3),
    ]


def _compress_prompt(p_main, batch, seq, cw):
    chunk = min(seq, 2048)
    nchunk = seq // chunk
    kvw = 2 * NSA_KV_HEADS * HEAD_DIM
    return pl.pallas_call(
        _compress_prompt_kernel,
        grid=(batch, nchunk),
        in_specs=[pl.BlockSpec((chunk, HEAD_DIM), (lambda b, c, s=s: (b * nchunk + c, COL_KVC + s)))
                  for s in range(N_KV_SLOTS)] + _compress_weight_specs(),
        out_specs=pl.BlockSpec((None, chunk // CMP_STRIDE, kvw), lambda b, c: (b, c, 0)),
        out_shape=jax.ShapeDtypeStruct((batch, seq // CMP_STRIDE, kvw), F32),
        scratch_shapes=[pltpu.VMEM((2 * NSA_KV_HEADS, 1, CMP_HIDDEN), F32)],
        compiler_params=_cparams(("arbitrary", "arbitrary")),
        name="nsa_compress_prompt",
    )(*([p_main] * N_KV_SLOTS), *cw)


def _page_specs(page_rows, width, chunk_of, per_step):
    def spec(i):
        return pl.BlockSpec((None, page_rows, width),
                            lambda b, c, pt: (pt[b, chunk_of(c) * per_step + i], 0, 0))
    return [spec(i) for i in range(per_step)]


def _compress_paged(cache_rows, page_table, cw):
    n_pool, page_rows, _ = cache_rows.shape
    page = page_rows // N_KV_SLOTS
    kvw = N_KV_SLOTS * HEAD_DIM
    dec_batch, n_pages = page_table.shape
    nchunk = n_pages // CMP_PAGES_PER_STEP
    rows = CMP_PAGES_PER_STEP * page // CMP_STRIDE
    return pl.pallas_call(
        _compress_paged_kernel,
        grid_spec=pltpu.PrefetchScalarGridSpec(
            num_scalar_prefetch=1,
            grid=(dec_batch, nchunk),
            in_specs=(_page_specs(page_rows, HEAD_DIM, lambda c: c, CMP_PAGES_PER_STEP)
                      + _compress_weight_specs()),
            out_specs=pl.BlockSpec((None, rows, kvw), lambda b, c, pt: (b, c, 0)),
            scratch_shapes=[pltpu.VMEM((2 * NSA_KV_HEADS, 1, CMP_HIDDEN), F32)],
        ),
        out_shape=jax.ShapeDtypeStruct((dec_batch, nchunk * rows, kvw), F32),
        compiler_params=_cparams(("arbitrary", "arbitrary")),
        name="nsa_compress_paged",
    )(page_table, *([cache_rows] * CMP_PAGES_PER_STEP), *cw)


def _overlap_matrix(n_rows, n_blocks, blocks_first=False):
    shape = (n_blocks, n_rows) if blocks_first else (n_rows, n_blocks)
    n = lax.broadcasted_iota(jnp.int32, shape, 1 if blocks_first else 0)
    s = lax.broadcasted_iota(jnp.int32, shape, 0 if blocks_first else 1)
    c0 = (n - 1) * CMP_STRIDE
    hit = (n >= 1) & (c0 < s * SLC_BLOCK + SLC_BLOCK) & (c0 + CMP_BLOCK > s * SLC_BLOCK)
    return jnp.where(hit, 1.0, 0.0)


def _block_scores(imp, q_pos, n_blocks, block_axis=1):
    blk = lax.broadcasted_iota(jnp.int32, imp.shape, block_axis)
    cur = q_pos // SLC_BLOCK
    forced = (blk == 0) | (blk == cur) | (blk == cur - 1)
    valid = (blk * SLC_BLOCK <= q_pos) & (blk < n_blocks)
    return jnp.where(valid, imp + jnp.where(forced, FORCE_BONUS, 0.0), NEG_INF)


def _head_rows(tile_ref, h):
    tokens, heads, width = tile_ref.shape
    return tile_ref.reshape(tokens * heads, width)[pl.ds(h, tokens, stride=heads), :]


def _stack_heads(ref, n):
    return jnp.concatenate([ref[:, r * HEAD_DIM:(r + 1) * HEAD_DIM] for r in range(n)], 0)


def _nsa_prompt_kernel(qraw_ref, qrot_ref, kc_ref, vc_ref, ks_ref, vs_ref, kw_ref, vw_ref,
                       gate_ref, o_ref, ksb_ref, vsb_ref, kwb_ref, vwb_ref, *, seq, kchunk):
    i = pl.program_id(2)
    q0 = i * Q_BLOCK
    nc = kc_ref.shape[0]
    ns = seq // SLC_BLOCK
    rep = NSA_REP
    t_col = lax.broadcasted_iota(jnp.int32, (Q_BLOCK, 1), 0) + q0
    over_heads = lambda x: jnp.concatenate([x] * rep, 0)
    nt = (((1,), (1,)), ((), ()))

    @pl.when(i == 0)
    def _():
        ksb_ref[...] = ks_ref[...].astype(BF)
        vsb_ref[...] = vs_ref[...].astype(BF)
        kwb_ref[...] = kw_ref[...].astype(BF)
        vwb_ref[...] = vw_ref[...].astype(BF)

    q_raw = (_stack_heads(qraw_ref, rep) * (ATTN_SCALE * LOG2E)).astype(BF)
    n_row = lax.broadcasted_iota(jnp.int32, (1, nc), 1)
    vis = (n_row >= 1) & (n_row * CMP_STRIDE + (CMP_BLOCK - CMP_STRIDE - 1) <= t_col)
    s_c = (lax.dot_general(q_raw, kc_ref[...].astype(BF), nt, preferred_element_type=F32)
           + over_heads(jnp.where(vis, 0.0, NEG_INF)))
    e_c = jnp.exp2(s_c - jnp.max(s_c, -1, keepdims=True)) * over_heads(jnp.where(vis, 1.0, 0.0))
    p_c = e_c / jnp.maximum(jnp.sum(e_c, -1, keepdims=True), 1e-30)
    o_cmp = _dot(p_c, vc_ref[...])
    p_sum = p_c[0:Q_BLOCK]
    for r in range(1, rep):
        p_sum = p_sum + p_c[r * Q_BLOCK:(r + 1) * Q_BLOCK]
    q_rot = (_stack_heads(qrot_ref, rep) * (ATTN_SCALE * LOG2E)).astype(BF)
    wk = WINDOW + Q_BLOCK
    w0 = pl.multiple_of(jnp.maximum(q0 - WINDOW, 0), Q_BLOCK)
    dist = t_col - (w0 + lax.broadcasted_iota(jnp.int32, (1, wk), 1))
    s_w = (lax.dot_general(q_rot, kwb_ref[pl.ds(w0, wk), :], nt, preferred_element_type=F32)
           + over_heads(jnp.where((dist >= 0) & (dist < WINDOW), 0.0, NEG_INF)))
    e_w = jnp.exp2(s_w - jnp.max(s_w, -1, keepdims=True))
    p_w = e_w / jnp.sum(e_w, -1, keepdims=True)
    o_swa = jnp.dot(p_w.astype(BF), vwb_ref[pl.ds(w0, wk), :], preferred_element_type=F32)

    imp_t = _dot_exact01(p_sum, _overlap_matrix(nc, ns, blocks_first=True), m01_first=True)
    t_row = lax.broadcasted_iota(jnp.int32, (1, Q_BLOCK), 1) + q0
    sel_t = _topk_select_rows(_block_scores(imp_t, t_row, ns, block_axis=0), min(SLC_TOPK, ns))
    sel_b = sel_t.T.astype(BF)

    per_chunk = kchunk // SLC_BLOCK
    blk = lax.broadcasted_iota(jnp.int32, (ns, per_chunk), 0)
    slot = lax.broadcasted_iota(jnp.int32, (ns, per_chunk), 1)
    b_row = lax.broadcasted_iota(jnp.int32, (per_chunk, kchunk), 0)
    b_lane = lax.broadcasted_iota(jnp.int32, (per_chunk, kchunk), 1)
    expand = jnp.where(b_row == b_lane // SLC_BLOCK, 1.0, 0.0).astype(BF)
    k_lane = lax.broadcasted_iota(jnp.int32, (1, kchunk), 1)

    def slc_chunk(c, carry, diagonal):
        k0 = pl.multiple_of(c * kchunk, kchunk)
        pick = jnp.where(blk == c * per_chunk + slot, 1.0, 0.0).astype(BF)
        sel_c = jnp.dot(sel_b, pick, preferred_element_type=F32).astype(BF)
        keep = jnp.dot(sel_c, expand, preferred_element_type=F32) > 0.5
        if diagonal:
            keep = keep & (k0 + k_lane <= t_col)
        s2 = (lax.dot_general(q_rot, ksb_ref[pl.ds(k0, kchunk), :], nt, preferred_element_type=F32)
              + over_heads(jnp.where(keep, 0.0, NEG_INF)))
        return _flash_update(carry, s2, vsb_ref[pl.ds(k0, kchunk), :])

    init = (jnp.full((rep * Q_BLOCK, 1), NEG_INF, F32), jnp.zeros((rep * Q_BLOCK, 1), F32),
            jnp.zeros((rep * Q_BLOCK, HEAD_DIM), F32))
    last = q0 // kchunk
    carry = lax.fori_loop(0, last, lambda c, carry: slc_chunk(c, carry, False), init)
    _, l_s, acc_s = slc_chunk(last, carry, True)
    o_slc = acc_s / jnp.maximum(l_s, 1e-30)

    gt = gate_ref[...]
    for r in range(rep):
        rows = slice(r * Q_BLOCK, (r + 1) * Q_BLOCK)
        o_ref[:, r * HEAD_DIM:(r + 1) * HEAD_DIM] = (
            gt[:, 3 * r:3 * r + 1] * o_cmp[rows] + gt[:, 3 * r + 1:3 * r + 2] * o_slc[rows]
            + gt[:, 3 * r + 2:3 * r + 3] * o_swa[rows])


def _nsa_prompt(p_main, q_rot, kc_all, sig, batch, seq):
    nq = seq // Q_BLOCK
    gw = NSA_REP * HEAD_DIM
    kchunk = min(1024, seq)
    col = lambda base, kv: (lambda b, g, i: (b, base + kv * NSA_KV_HEADS + g))
    return pl.pallas_call(
        functools.partial(_nsa_prompt_kernel, seq=seq, kchunk=kchunk),
        grid=(batch, NSA_KV_HEADS, nq),
        in_specs=[
            pl.BlockSpec((Q_BLOCK, gw), lambda b, g, i: (b * nq + i, g)),
            pl.BlockSpec((Q_BLOCK, gw), lambda b, g, i: (b * nq + i, g)),
            pl.BlockSpec((None, seq // CMP_STRIDE, HEAD_DIM), lambda b, g, i: (b, 0, g)),
            pl.BlockSpec((None, seq // CMP_STRIDE, HEAD_DIM), lambda b, g, i: (b, 0, NSA_KV_HEADS + g)),
            pl.BlockSpec((seq, HEAD_DIM), col(COL_KVS, 0)),
            pl.BlockSpec((seq, HEAD_DIM), col(COL_KVS, 1)),
            pl.BlockSpec((seq, HEAD_DIM), col(COL_KVW, 0)),
            pl.BlockSpec((seq, HEAD_DIM), col(COL_KVW, 1)),
            pl.BlockSpec((Q_BLOCK, LANE), lambda b, g, i: (b * nq + i, g)),
        ],
        out_specs=pl.BlockSpec((Q_BLOCK, gw), lambda b, g, i: (b * nq + i, g)),
        out_shape=jax.ShapeDtypeStruct((batch * seq, NSA_HEADS * HEAD_DIM), F32),
        scratch_shapes=[pltpu.VMEM((seq, HEAD_DIM), BF)] * 4,
        compiler_params=_cparams(("arbitrary", "arbitrary", "arbitrary")),
        name="nsa_attention_prompt",
    )(p_main, q_rot, kc_all, kc_all, p_main, p_main, p_main, p_main, sig)


def _suffix_sum_lanes(x):
    n = x.shape[-1]
    lane = lax.broadcasted_iota(jnp.int32, x.shape, x.ndim - 1)
    k = 1
    while k < n:
        x = x + jnp.where(lane < n - k, pltpu.roll(x, n - k, x.ndim - 1), 0.0)
        k *= 2
    return x


def _decay_prompt_kernel(lf_ref, d_ref):
    lt = lf_ref[...].T
    top = lt[:SUBLANE]
    d_ref[...] = (_suffix_sum_lanes(top) - top) * LOG2E


def _decay_prompt(lsig, batch, seq):
    return pl.pallas_call(
        _decay_prompt_kernel,
        grid=(batch,),
        in_specs=[pl.BlockSpec((seq, LANE), lambda b: (b, 0))],
        out_specs=pl.BlockSpec((None, SUBLANE, seq), lambda b: (b, 0, 0)),
        out_shape=jax.ShapeDtypeStruct((batch, SUBLANE, seq), F32),
        compiler_params=_cparams(("arbitrary",)),
        name="fox_decay_prompt",
    )(lsig)


def _fox_prompt_kernel(q_ref, k_ref, v_ref, d_ref, o_ref, kb_ref, vb_ref, m_ref, l_ref, acc_ref,
                       *, tile):
    h = pl.program_id(1)
    i = pl.program_id(2)

    @pl.when(i == 0)
    def _():
        kb_ref[...] = k_ref[...].astype(BF)
        vb_ref[...] = v_ref[...].astype(BF)

    q = (q_ref[...] * (ATTN_SCALE * LOG2E)).astype(BF)
    row = lax.broadcasted_iota(jnp.int32, (tile, tile), 0)
    col = lax.broadcasted_iota(jnp.int32, (tile, tile), 1)

    def chunk(c, carry, diagonal):
        k0 = pl.multiple_of(c * tile, tile)
        s2 = lax.dot_general(q, kb_ref[pl.ds(k0, tile), :], (((1,), (1,)), ((), ())),
                             preferred_element_type=F32) + d_ref[pl.ds(h, 1), pl.ds(k0, tile)]
        if diagonal:
            s2 = jnp.where(col <= row, s2, NEG_INF)
        return _flash_update(carry, s2, vb_ref[pl.ds(k0, tile), :])

    init = (jnp.full((tile, 1), NEG_INF, F32), jnp.zeros((tile, 1), F32),
            jnp.zeros((tile, HEAD_DIM), F32))
    carry = lax.fori_loop(0, i, lambda c, carry: chunk(c, carry, False), init)
    _, l, acc = chunk(i, carry, True)
    o_ref[...] = acc / jnp.maximum(l, 1e-30)


def _fox_prompt(p_main, decay, batch, seq):
    tq = min(1024, seq)
    nq = seq // tq
    return pl.pallas_call(
        functools.partial(_fox_prompt_kernel, tile=tq),
        grid=(batch, FOX_HEADS, nq),
        scratch_shapes=[pltpu.VMEM((seq, HEAD_DIM), BF), pltpu.VMEM((seq, HEAD_DIM), BF),
                        pltpu.VMEM((tq, 1), F32), pltpu.VMEM((tq, 1), F32),
                        pltpu.VMEM((tq, HEAD_DIM), F32)],
        in_specs=[
            pl.BlockSpec((tq, HEAD_DIM), lambda b, h, i: (b * nq + i, COL_FOX + h)),
            pl.BlockSpec((seq, HEAD_DIM), lambda b, h, i: (b, COL_FOX + FOX_HEADS + h)),
            pl.BlockSpec((seq, HEAD_DIM), lambda b, h, i: (b, COL_FOX + 2 * FOX_HEADS + h)),
            pl.BlockSpec((None, SUBLANE, seq), lambda b, h, i: (b, 0, 0)),
        ],
        out_specs=pl.BlockSpec((tq, HEAD_DIM), lambda b, h, i: (b * nq + i, h)),
        out_shape=jax.ShapeDtypeStruct((batch * seq, FOX_HEADS * HEAD_DIM), F32),
        compiler_params=_cparams(("arbitrary", "arbitrary", "arbitrary")),
        name="fox_attention_prompt",
    )(p_main, p_main, p_main, decay)


def _mem_attn_kernel(q_ref, k_ref, v_ref, o_ref):
    s = _dot_nt(q_ref[...], k_ref[...]) * ATTN_SCALE
    m = jnp.max(s, -1, keepdims=True)
    e = jnp.exp(s - m)
    p = e / jnp.sum(e, -1, keepdims=True)
    o_ref[...] = _dot(p, v_ref[...])


def _mem_attention(p_main, mem_kv2d, batch, rows_per_batch, tq):
    nq = rows_per_batch // tq
    mlen = mem_kv2d.shape[0] // batch
    return pl.pallas_call(
        _mem_attn_kernel,
        grid=(batch, MEM_HEADS, nq),
        in_specs=[
            pl.BlockSpec((tq, HEAD_DIM), lambda b, h, i: (b * nq + i, COL_QM + h)),
            pl.BlockSpec((mlen, HEAD_DIM), lambda b, h, i: (b, h)),
            pl.BlockSpec((mlen, HEAD_DIM), lambda b, h, i: (b, MEM_HEADS + h)),
        ],
        out_specs=pl.BlockSpec((tq, HEAD_DIM), lambda b, h, i: (b * nq + i, h)),
        out_shape=jax.ShapeDtypeStruct((batch * rows_per_batch, MEM_HEADS * HEAD_DIM), F32),
        compiler_params=_cparams(("arbitrary", "arbitrary", "arbitrary")),
        name="mem_attention",
    )(p_main, mem_kv2d, mem_kv2d)


def _mem_attn_rows_kernel(q_ref, kv_ref, o_ref):
    slots = 2 * MEM_HEADS
    mlen = kv_ref.shape[0] // slots
    scores = [_dot_nt(q_ref[:, h * HEAD_DIM:(h + 1) * HEAD_DIM],
                      kv_ref[pl.ds(h, mlen, stride=slots), :]) for h in range(MEM_HEADS)]
    s = jnp.concatenate(scores, 0) * ATTN_SCALE
    m = jnp.max(s, -1, keepdims=True)
    e = jnp.exp(s - m)
    p = e / jnp.sum(e, -1, keepdims=True)
    for h in range(MEM_HEADS):
        o_ref[:, h * HEAD_DIM:(h + 1) * HEAD_DIM] = _dot(
            p[h * SROW:(h + 1) * SROW], kv_ref[pl.ds(MEM_HEADS + h, mlen, stride=slots), :])


def _mem_attention_rows(p_s, mem_rows):
    dec_batch, rows, _ = mem_rows.shape
    mw = MEM_HEADS * HEAD_DIM
    return pl.pallas_call(
        _mem_attn_rows_kernel,
        grid=(dec_batch,),
        in_specs=[pl.BlockSpec((SROW, mw), lambda b: (b, COL_QM * LANE // mw)),
                  pl.BlockSpec((None, rows, HEAD_DIM), lambda b: (b, 0, 0))],
        out_specs=pl.BlockSpec((SROW, mw), lambda b: (b, 0)),
        out_shape=jax.ShapeDtypeStruct((dec_batch * SROW, mw), F32),
        compiler_params=_cparams(("arbitrary",)),
        name="mem_attention_sample",
    )(p_s, mem_rows)


def _matmul_kernel(x_ref, w_ref, o_ref):
    o_ref[...] = jnp.dot(x_ref[...].astype(BF), w_ref[...], preferred_element_type=F32)


def _matmul(x2d, w_bf, tm, tn):
    m, k = x2d.shape
    n = w_bf.shape[1]
    return pl.pallas_call(
        _matmul_kernel,
        grid=(m // tm, n // tn),
        in_specs=[pl.BlockSpec((tm, k), lambda i, j: (i, 0)),
                  pl.BlockSpec((k, tn), lambda i, j: (0, j))],
        out_specs=pl.BlockSpec((tm, tn), lambda i, j: (i, j)),
        out_shape=jax.ShapeDtypeStruct((m, n), F32),
        compiler_params=_cparams(("arbitrary", "arbitrary")),
        name="projection_matmul",
    )(x2d, w_bf)


def _layer_norm(z, g, b):
    zc = z - jnp.mean(z, -1, keepdims=True)
    var = jnp.mean(zc * zc, -1, keepdims=True)
    return zc * lax.rsqrt(var + LN_EPS) * g + b


def _outproj_kernel(on_ref, of_ref, om_ref, x_ref, w_ref, g_ref, b_ref, h_ref, *, alpha):
    mix = jnp.concatenate([on_ref[...].astype(BF), of_ref[...].astype(BF),
                           om_ref[...].astype(BF)], 1)
    y = jnp.dot(mix, w_ref[...], preferred_element_type=F32)
    h_ref[...] = _layer_norm(alpha * x_ref[...] + y, g_ref[...], b_ref[...])


def _out_projection(o_nsa, o_fox, o_mem, x2d, w_out, g, b, tm, alpha):
    m, d = x2d.shape
    row = lambda w: pl.BlockSpec((tm, w), lambda i: (i, 0))
    const = lambda shape: pl.BlockSpec(shape, lambda i: (0, 0))
    return pl.pallas_call(
        functools.partial(_outproj_kernel, alpha=alpha),
        grid=(m // tm,),
        in_specs=[row(o_nsa.shape[1]), row(o_fox.shape[1]), row(o_mem.shape[1]), row(d),
                  const(w_out.shape), const((1, d)), const((1, d))],
        out_specs=row(d),
        out_shape=jax.ShapeDtypeStruct((m, d), F32),
        compiler_params=_cparams(("arbitrary",)),
        name="out_projection_ln",
    )(o_nsa, o_fox, o_mem, x2d, w_out, g, b)


def _ffn_core(h_ref, wa_ref, wb_ref, cw_ref, cb_ref, wd_ref, g_ref, b_ref, y_ref, acc_ref, hb_ref,
              shifted, alpha):
    j = pl.program_id(1)

    @pl.when(j == 0)
    def _():
        hb_ref[...] = h_ref[...].astype(BF)

    hb = hb_ref[...]
    a = jnp.dot(hb, wa_ref[...], preferred_element_type=F32)
    gate_in = jnp.dot(hb, wb_ref[...], preferred_element_type=F32)
    a1, a2 = shifted(a)
    cw = cw_ref[...]
    c = cb_ref[...] + a2 * cw[0:1] + a1 * cw[1:2] + a * cw[2:3]
    act = (c * jax.nn.sigmoid(c)) * gate_in
    part = jnp.dot(act.astype(BF), wd_ref[...], preferred_element_type=F32)

    @pl.when(j == 0)
    def _():
        acc_ref[...] = part

    @pl.when(j > 0)
    def _():
        acc_ref[...] += part

    @pl.when(j == pl.num_programs(1) - 1)
    def _():
        y_ref[...] = _layer_norm(alpha * h_ref[...] + acc_ref[...], g_ref[...], b_ref[...])
    return a


def _ffn_prompt_kernel(h_ref, wa_ref, wb_ref, cw_ref, cb_ref, wd_ref, g_ref, b_ref,
                       y_ref, tail_ref, acc_ref, hb_ref, halo_ref, *, tiles_per_seq, alpha):
    i = pl.program_id(0)
    j = pl.program_id(1)
    tm = h_ref.shape[0]
    seq_start = (i % tiles_per_seq) == 0

    @pl.when(seq_start)
    def _():
        halo_ref[j] = jnp.zeros(halo_ref.shape[1:], F32)

    def shifted(a):
        row = lax.broadcasted_iota(jnp.int32, a.shape, 0)
        halo = halo_ref[j]
        h1 = halo[SUBLANE - 1:SUBLANE]
        h2 = halo[SUBLANE - 2:SUBLANE - 1]
        a1 = jnp.where(row == 0, h1, pltpu.roll(a, 1, 0))
        a2 = jnp.where(row == 0, h2, jnp.where(row == 1, h1, pltpu.roll(a, 2, 0)))
        return a1, a2

    a = _ffn_core(h_ref, wa_ref, wb_ref, cw_ref, cb_ref, wd_ref, g_ref, b_ref, y_ref, acc_ref,
                  hb_ref, shifted, alpha)
    halo_ref[j] = a[tm - SUBLANE:]
    tail_ref[...] = a[tm - SUBLANE:]


def _ffn_sample_kernel(h_ref, wa_ref, wb_ref, cw_ref, cb_ref, wd_ref, g_ref, b_ref, f1_ref, f2_ref,
                       y_ref, a_ref, acc_ref, hb_ref, *, alpha):
    def shifted(a):
        s = lax.broadcasted_iota(jnp.int32, a.shape, 0) % SROW
        a1 = jnp.where(s >= 1, pltpu.roll(a, 1, 0), 0.0) + f1_ref[...]
        a2 = jnp.where(s >= 2, pltpu.roll(a, 2, 0), 0.0) + f2_ref[...]
        return a1, a2

    a_ref[...] = _ffn_core(h_ref, wa_ref, wb_ref, cw_ref, cb_ref, wd_ref, g_ref, b_ref, y_ref,
                           acc_ref, hb_ref, shifted, alpha)


def _ffn_specs(tm, d, tf, nf):
    return [
        pl.BlockSpec((tm, d), lambda i, j: (i, 0)),
        pl.BlockSpec((d, tf), lambda i, j: (0, j)),
        pl.BlockSpec((d, tf), lambda i, j: (0, nf + j)),
        pl.BlockSpec((SUBLANE, tf), lambda i, j: (0, j)),
        pl.BlockSpec((1, tf), lambda i, j: (0, j)),
        pl.BlockSpec((tf, d), lambda i, j: (j, 0)),
        pl.BlockSpec((1, d), lambda i, j: (0, 0)),
        pl.BlockSpec((1, d), lambda i, j: (0, 0)),
    ]


def _ffn_prompt(h2d, fw, seq, tm, tf, alpha):
    w_up, conv_w8, conv_b, w_down, g, b = fw
    m, d = h2d.shape
    dff = w_down.shape[0]
    nf = dff // tf
    return pl.pallas_call(
        functools.partial(_ffn_prompt_kernel, tiles_per_seq=seq // tm, alpha=alpha),
        grid=(m // tm, nf),
        in_specs=_ffn_specs(tm, d, tf, nf),
        out_specs=[pl.BlockSpec((tm, d), lambda i, j: (i, 0)),
                   pl.BlockSpec((None, SUBLANE, tf), lambda i, j: (i, 0, j))],
        out_shape=[jax.ShapeDtypeStruct((m, d), F32),
                   jax.ShapeDtypeStruct((m // tm, SUBLANE, dff), F32)],
        scratch_shapes=[pltpu.VMEM((tm, d), F32), pltpu.VMEM((tm, d), BF),
                        pltpu.VMEM((nf, SUBLANE, tf), F32)],
        compiler_params=_cparams(("arbitrary", "arbitrary")),
        name="conv_ffn_prompt",
    )(h2d, w_up, w_up, conv_w8, conv_b, w_down, g, b)


def _ffn_sample(h2d, fw, fill1, fill2, tf, alpha):
    w_up, conv_w8, conv_b, w_down, g, b = fw
    m, d = h2d.shape
    dff = w_down.shape[0]
    nf = dff // tf
    return pl.pallas_call(
        functools.partial(_ffn_sample_kernel, alpha=alpha),
        grid=(1, nf),
        in_specs=_ffn_specs(m, d, tf, nf) + [pl.BlockSpec((m, tf), lambda i, j: (0, j)),
                                             pl.BlockSpec((m, tf), lambda i, j: (0, j))],
        out_specs=[pl.BlockSpec((m, d), lambda i, j: (0, 0)),
                   pl.BlockSpec((m, tf), lambda i, j: (0, j))],
        out_shape=[jax.ShapeDtypeStruct((m, d), F32), jax.ShapeDtypeStruct((m, dff), F32)],
        scratch_shapes=[pltpu.VMEM((m, d), F32), pltpu.VMEM((m, d), BF)],
        compiler_params=_cparams(("arbitrary", "arbitrary")),
        name="conv_ffn_sample",
    )(h2d, w_up, w_up, conv_w8, conv_b, w_down, g, b, fill1, fill2)


def _nsa_sample_a_kernel(qraw_ref, qrot_ref, kc_ref, swa_ref, new_ref, gate_ref, part_ref, sel_ref,
                         *, past, n_blocks):
    gw = NSA_REP * HEAD_DIM
    lanes = lambda ref, i, w: ref.at[:, i * w:(i + 1) * w]
    for g in range(NSA_KV_HEADS):
        _nsa_sample_a_group(
            g, lanes(qraw_ref, g, gw), lanes(qrot_ref, g, gw), lanes(kc_ref, g, HEAD_DIM),
            lanes(kc_ref, NSA_KV_HEADS + g, HEAD_DIM), swa_ref, lanes(new_ref, g, HEAD_DIM),
            lanes(new_ref, NSA_KV_HEADS + g, HEAD_DIM), lanes(gate_ref, g, LANE),
            part_ref.at[g], sel_ref.at[g], past=past, n_blocks=n_blocks)


def _nsa_sample_a_group(g, qraw_ref, qrot_ref, kc_ref, vc_ref, swa_ref, kwn_ref, vwn_ref,
                        gate_ref, part_ref, sel_ref, *, past, n_blocks):
    rep = NSA_REP
    nc = kc_ref.shape[0]
    nsp = -(-n_blocks // LANE) * LANE
    s_col = lax.broadcasted_iota(jnp.int32, (SROW, 1), 0) + past

    q_raw = _stack_heads(qraw_ref, rep)
    s_c = (_dot_nt(q_raw, kc_ref[...]) * ATTN_SCALE).reshape(rep, SROW, nc)
    n_row = lax.broadcasted_iota(jnp.int32, (1, nc), 1)
    vis = (n_row >= 1) & (n_row * CMP_STRIDE + (CMP_BLOCK - CMP_STRIDE - 1) <= s_col)
    p_c = _masked_softmax(s_c, vis[None])
    o_cmp = _dot(p_c.reshape(rep * SROW, nc), vc_ref[...])
    imp = _dot_exact01(jnp.sum(p_c, 0), _overlap_matrix(nc, nsp))
    score = _block_scores(imp, s_col, n_blocks)
    rank = _topk_rank(score, n_blocks)
    col1 = lax.broadcasted_iota(jnp.int32, score.shape, 1).astype(F32) + 1.0
    lane = lax.broadcasted_iota(jnp.int32, (SROW, LANE), 1)
    ids = jnp.full((SROW, LANE), -1.0, F32)
    for r in range(min(SLC_TOPK, n_blocks)):
        hit = jnp.where(rank == r, jnp.where(score > 0.5 * NEG_INF, col1, 0.0), 0.0)
        ids = jnp.where(lane == r, jnp.sum(hit, -1, keepdims=True) - 1.0, ids)
    sel_ref[...] = ids.astype(jnp.int32)

    q_rot = _stack_heads(qrot_ref, rep)
    wbuf = swa_ref.shape[0] // N_KV_SLOTS
    keys = jnp.concatenate([swa_ref[pl.ds(g, wbuf, stride=N_KV_SLOTS), :], kwn_ref[...]], 0)
    vals = jnp.concatenate([swa_ref[pl.ds(NSA_KV_HEADS + g, wbuf, stride=N_KV_SLOTS), :],
                            vwn_ref[...]], 0)
    w_pos = past - wbuf + lax.broadcasted_iota(jnp.int32, (1, wbuf + SROW), 1)
    dist = s_col - w_pos
    mask = (dist >= 0) & (dist < WINDOW) & (w_pos >= 0)
    s_w = (_dot_nt(q_rot, keys) * ATTN_SCALE).reshape(rep, SROW, wbuf + SROW)
    p_w = _masked_softmax(s_w, mask[None])
    o_swa = _dot(p_w.reshape(rep * SROW, wbuf + SROW), vals)

    gt = gate_ref[...]
    for r in range(rep):
        rows = slice(r * SROW, (r + 1) * SROW)
        part_ref[rows, :] = (gt[:, 3 * r:3 * r + 1] * o_cmp[rows]
                             + gt[:, 3 * r + 2:3 * r + 3] * o_swa[rows])


def _nsa_sample_a(p_s, qrot_s, kc_all, swa_rows, sig_s, past, n_blocks):
    dec_batch, nc, _ = kc_all.shape
    qw = NSA_HEADS * HEAD_DIM
    kvw = N_KV_SLOTS * HEAD_DIM
    return pl.pallas_call(
        functools.partial(_nsa_sample_a_kernel, past=past, n_blocks=n_blocks),
        grid=(dec_batch,),
        in_specs=[
            pl.BlockSpec((SROW, qw), lambda b: (b, 0)),
            pl.BlockSpec((SROW, qw), lambda b: (b, 0)),
            pl.BlockSpec((None, nc, kvw), lambda b: (b, 0, 0)),
            pl.BlockSpec((None, swa_rows.shape[1], HEAD_DIM), lambda b: (b, 0, 0)),
            pl.BlockSpec((SROW, kvw), lambda b: (b, COL_KVW * LANE // kvw)),
            pl.BlockSpec((SROW, NSA_KV_HEADS * LANE), lambda b: (b, 0)),
        ],
        out_specs=[pl.BlockSpec((None, NSA_KV_HEADS, NSA_REP * SROW, HEAD_DIM), lambda b: (b, 0, 0, 0)),
                   pl.BlockSpec((None, NSA_KV_HEADS, SROW, LANE), lambda b: (b, 0, 0, 0))],
        out_shape=[jax.ShapeDtypeStruct((dec_batch, NSA_KV_HEADS, NSA_REP * SROW, HEAD_DIM), F32),
                   jax.ShapeDtypeStruct((dec_batch, NSA_KV_HEADS, SROW, LANE), jnp.int32)],
        compiler_params=_cparams(("arbitrary",)),
        name="nsa_sample_cmp_swa",
    )(p_s, qrot_s, kc_all, swa_rows, p_s, sig_s)


def _nsa_sample_gather_kernel(pt_ref, ix_ref, *refs, n_blocks, s_len):
    del pt_ref, ix_ref
    n_sel = min(SLC_TOPK, n_blocks)
    n_fetch = s_len * n_sel
    blocks = refs[:n_fetch]
    (qrot_ref, ids_ref, knew_ref, vnew_ref, gate_ref, part_ref, o_ref) = refs[n_fetch:]
    g = pl.program_id(1)
    rep = NSA_REP
    n_keys = n_fetch * SLC_BLOCK

    def block_rows(slot):
        return jnp.concatenate([r[pl.ds(slot, SLC_BLOCK, stride=N_KV_SLOTS), :].astype(BF)
                                for r in blocks], 0)

    q = _stack_heads(qrot_ref, rep)
    s_cache = _dot_nt(q, block_rows(g)) * ATTN_SCALE

    ids = ids_ref[...]
    cached = jnp.where((ids >= 0) & (ids < n_blocks - 1), 1.0, 0.0).astype(BF)
    rank_row = lax.broadcasted_iota(jnp.int32, (LANE, n_keys), 0)
    key_col = lax.broadcasted_iota(jnp.int32, (LANE, n_keys), 1)
    by_rank = jnp.where(rank_row == (key_col // SLC_BLOCK) % n_sel, 1.0, 0.0).astype(BF)
    tok = lax.broadcasted_iota(jnp.int32, (SROW, n_keys), 0)
    owner = lax.broadcasted_iota(jnp.int32, (SROW, n_keys), 1) // (SLC_BLOCK * n_sel)
    keep_cache = jnp.where(owner == tok, jnp.dot(cached, by_rank, preferred_element_type=F32), 0.0)

    s_q = lax.broadcasted_iota(jnp.int32, (SROW, LANE), 0)
    s_k = lax.broadcasted_iota(jnp.int32, (SROW, LANE), 1)
    newest = jnp.sum(jnp.where(ids == n_blocks - 1, 1.0, 0.0), -1, keepdims=True)
    keep_new = jnp.where((s_k <= s_q) & (s_k < SROW), newest, 0.0)
    pad = jnp.zeros((LANE - SROW, HEAD_DIM), F32)
    s_new = _dot_nt(q, jnp.concatenate([knew_ref[...], pad], 0)) * ATTN_SCALE

    over_heads = lambda x: jnp.concatenate([x] * rep, 0) > 0.5
    mask_c, mask_n = over_heads(keep_cache), over_heads(keep_new)
    s_cache = jnp.where(mask_c, s_cache, NEG_INF)
    s_new = jnp.where(mask_n, s_new, NEG_INF)
    m = jnp.maximum(jnp.max(s_cache, -1, keepdims=True), jnp.max(s_new, -1, keepdims=True))
    e_c = jnp.where(mask_c, jnp.exp(s_cache - m), 0.0)
    e_n = jnp.where(mask_n, jnp.exp(s_new - m), 0.0)
    denom = jnp.maximum(jnp.sum(e_c, -1, keepdims=True) + jnp.sum(e_n, -1, keepdims=True), 1e-30)
    o_slc = (_dot(e_c, block_rows(NSA_KV_HEADS + g))
             + _dot(e_n, jnp.concatenate([vnew_ref[...], pad], 0))) / denom
    gt = gate_ref[...]
    for r in range(rep):
        rows = slice(r * SROW, (r + 1) * SROW)
        o_ref[:, r * HEAD_DIM:(r + 1) * HEAD_DIM] = (
            part_ref[rows, :] + gt[:, 3 * r + 1:3 * r + 2] * o_slc[rows])


def _nsa_sample_gather(cache_rows, page_table, ids, p_s, qrot_s, sig_s, part, n_blocks, s_len):
    n_pool, page_rows, _ = cache_rows.shape
    dec_batch, n_pages = page_table.shape
    n_sel = min(SLC_TOPK, n_blocks)
    per_page = page_rows // (N_KV_SLOTS * SLC_BLOCK)
    half_rows = N_KV_SLOTS * SLC_BLOCK
    gw = NSA_REP * HEAD_DIM
    flat_ids = ids[:, :, :s_len, :n_sel].reshape(dec_batch, NSA_KV_HEADS * s_len * n_sel)
    pool_page = jnp.take_along_axis(page_table, jnp.clip(flat_ids // per_page, 0, n_pages - 1), 1)
    in_page = jnp.maximum(flat_ids, 0) % per_page

    def fetch_spec(n):
        def index(b, g, pp, ip):
            return (pp[b, g * (s_len * n_sel) + n], ip[b, g * (s_len * n_sel) + n], 0)
        return pl.BlockSpec((None, half_rows, HEAD_DIM), index)

    bg = lambda b, g, pt, ix: (b, g)
    return pl.pallas_call(
        functools.partial(_nsa_sample_gather_kernel, n_blocks=n_blocks, s_len=s_len),
        grid_spec=pltpu.PrefetchScalarGridSpec(
            num_scalar_prefetch=2,
            grid=(dec_batch, NSA_KV_HEADS),
            in_specs=[fetch_spec(n) for n in range(s_len * n_sel)] + [
                pl.BlockSpec((SROW, gw), bg),
                pl.BlockSpec((None, None, SROW, LANE), lambda b, g, pt, ix: (b, g, 0, 0)),
                pl.BlockSpec((SROW, HEAD_DIM), lambda b, g, pt, ix: (b, COL_KVS + g)),
                pl.BlockSpec((SROW, HEAD_DIM), lambda b, g, pt, ix: (b, COL_KVS + NSA_KV_HEADS + g)),
                pl.BlockSpec((SROW, LANE), bg),
                pl.BlockSpec((None, None, NSA_REP * SROW, HEAD_DIM), lambda b, g, pt, ix: (b, g, 0, 0)),
            ],
            out_specs=pl.BlockSpec((SROW, gw), bg),
        ),
        out_shape=jax.ShapeDtypeStruct((dec_batch * SROW, NSA_HEADS * HEAD_DIM), F32),
        compiler_params=_cparams(("arbitrary", "arbitrary")),
        name="nsa_sample_selected",
    )(pool_page, in_page, *([cache_rows] * (s_len * n_sel)), qrot_s, ids, p_s, p_s, sig_s, part)


def _fox_sample_kernel(pt_ref, *refs, s_len):
    n = PAGES_PER_STEP
    kpages, vpages = refs[:n], refs[n:2 * n]
    (logf_ref, q_ref, knew_ref, vnew_ref, lnew_ref,
     o_ref, m_ref, l_ref, acc_ref, carry_ref) = refs[2 * n:]
    b = pl.program_id(0)
    c = pl.program_id(1)
    page = kpages[0].shape[0]
    head = lambda ref, h: ref[:, h * HEAD_DIM:(h + 1) * HEAD_DIM]

    def cache_rows(pages, h):
        return jnp.concatenate([_head_rows(p, h).astype(BF) for p in pages], 0)

    def update(scores, mask, values):
        s = jnp.concatenate(scores, 0)
        if mask is not None:
            s = jnp.where(mask, s, NEG_INF)
        m_old = m_ref[...]
        m_new = jnp.maximum(m_old, jnp.max(s, -1, keepdims=True))
        p = jnp.exp(s - m_new)
        if mask is not None:
            p = jnp.where(mask, p, 0.0)
        alpha = jnp.exp(m_old - m_new)
        pv = jnp.concatenate([_dot(p[h * SROW:(h + 1) * SROW], values[h])
                              for h in range(FOX_HEADS)], 0)
        m_ref[...] = m_new
        l_ref[...] = alpha * l_ref[...] + jnp.sum(p, -1, keepdims=True)
        acc_ref[...] = alpha * acc_ref[...] + pv

    @pl.when(c == 0)
    def _():
        m_ref[...] = jnp.full(m_ref.shape, NEG_INF, F32)
        l_ref[...] = jnp.zeros(l_ref.shape, F32)
        acc_ref[...] = jnp.zeros(acc_ref.shape, F32)
        s_q = lax.broadcasted_iota(jnp.int32, (SROW, LANE), 0)
        s_k = lax.broadcasted_iota(jnp.int32, (SROW, LANE), 1)
        later = jnp.where((s_q > s_k) & (s_q < s_len), 1.0, 0.0)
        real = lax.broadcasted_iota(jnp.int32, (SROW, 1), 0) < s_len
        lnew = lnew_ref[...]
        pad = jnp.zeros((LANE - SROW, HEAD_DIM), F32)
        rows_q = lax.broadcasted_iota(jnp.int32, (FOX_HEADS * SROW, LANE), 0) % SROW
        cols_k = lax.broadcasted_iota(jnp.int32, (FOX_HEADS * SROW, LANE), 1)
        mask = (cols_k <= rows_q) & (cols_k < s_len)
        totals, scores, values = [], [], []
        for h in range(FOX_HEADS):
            lf = jnp.where(real, lnew[:, h:h + 1], 0.0)
            d_row = jnp.sum(lf * later, 0, keepdims=True)
            totals.append(jnp.sum(lf, 0, keepdims=True))
            k = jnp.concatenate([head(knew_ref, h), pad], 0)
            values.append(jnp.concatenate([head(vnew_ref, h), pad], 0))
            scores.append(_dot_nt(head(q_ref, h), k) * ATTN_SCALE + d_row)
        update(scores, mask, values)
        totals += [jnp.zeros((1, 1), F32)] * (SUBLANE - FOX_HEADS)
        carry_ref[...] = jnp.broadcast_to(jnp.concatenate(totals, 0), carry_ref.shape)

    @pl.when(c > 0)
    def _():
        ids = [pt_ref[b, c * n + i] for i in range(n)]
        lf = jnp.concatenate(
            [jnp.concatenate([logf_ref[h, pl.ds(pg, 1), :] for pg in ids], 1) for h in range(FOX_HEADS)]
            + [jnp.zeros((SUBLANE - FOX_HEADS, n * page), F32)], 0)
        incl = _suffix_sum_lanes(lf)
        carry = carry_ref[:, 0:1]
        decay = carry + (incl - lf)
        carry_ref[...] = jnp.broadcast_to(carry + incl[:, 0:1], carry_ref.shape)
        scores = [_dot_nt(head(q_ref, h), cache_rows(kpages, h)) * ATTN_SCALE + decay[h:h + 1, :]
                  for h in range(FOX_HEADS)]
        update(scores, None, [cache_rows(vpages, h) for h in range(FOX_HEADS)])

    @pl.when(c == pl.num_programs(1) - 1)
    def _():
        o = acc_ref[...] / jnp.maximum(l_ref[...], 1e-30)
        for h in range(FOX_HEADS):
            o_ref[:, h * HEAD_DIM:(h + 1) * HEAD_DIM] = o[h * SROW:(h + 1) * SROW]


def _fox_sample(cache, logf_t, page_table, p_s, lsig_s, s_len):
    n_pool, page = cache.shape[:2]
    dec_batch, n_pages = page_table.shape
    nchunk = n_pages // PAGES_PER_STEP
    fw = FOX_HEADS * HEAD_DIM
    row = lambda w, col: pl.BlockSpec((SROW, w), lambda b, c, pt: (b, col))
    chunks = page_table.reshape(dec_batch, nchunk, PAGES_PER_STEP)[:, ::-1]
    steps = jnp.concatenate([chunks[:, :1], chunks], 1).reshape(dec_batch, (nchunk + 1) * PAGES_PER_STEP)

    def half_specs(kv):
        def spec(i):
            return pl.BlockSpec((None, page, None, FOX_HEADS, HEAD_DIM),
                                lambda b, c, pt: (pt[b, c * PAGES_PER_STEP + i], 0, kv, 0, 0))
        return [spec(i) for i in range(PAGES_PER_STEP)]

    return pl.pallas_call(
        functools.partial(_fox_sample_kernel, s_len=s_len),
        grid_spec=pltpu.PrefetchScalarGridSpec(
            num_scalar_prefetch=1,
            grid=(dec_batch, nchunk + 1),
            in_specs=half_specs(0) + half_specs(1) + [
                pl.BlockSpec(logf_t.shape, lambda b, c, pt: (0, 0, 0), pipeline_mode=pl.Buffered(1)),
                row(fw, COL_FOX * LANE // fw),
                row(fw, COL_FOX * LANE // fw + 1),
                row(fw, COL_FOX * LANE // fw + 2),
                row(LANE, 0),
            ],
            out_specs=row(fw, 0),
            scratch_shapes=[pltpu.VMEM((FOX_HEADS * SROW, 1), F32),
                            pltpu.VMEM((FOX_HEADS * SROW, 1), F32),
                            pltpu.VMEM((FOX_HEADS * SROW, HEAD_DIM), F32),
                            pltpu.VMEM((SUBLANE, LANE), F32)],
        ),
        out_shape=jax.ShapeDtypeStruct((dec_batch * SROW, fw), F32),
        compiler_params=_cparams(("arbitrary", "arbitrary")),
        name="fox_attention_sample",
    )(steps, *([cache] * (2 * PAGES_PER_STEP)), logf_t, p_s, p_s, p_s, lsig_s)


def _rope_tables(pos):
    half = ROT_DIM // 2
    inv = jnp.power(ROPE_THETA, -jnp.arange(half, dtype=F32) * (2.0 / ROT_DIM))
    ang = pos.astype(F32)[:, None] * inv[None, :]
    cos, sin = jnp.cos(ang), jnp.sin(ang)
    n = pos.shape[0]
    c = jnp.concatenate([cos, cos, jnp.ones((n, HEAD_DIM - ROT_DIM), F32)], 1)
    s = jnp.concatenate([-sin, sin, jnp.zeros((n, HEAD_DIM - ROT_DIM), F32)], 1)
    return c, s


def _layer_weights(w_in, b_gate, b_forget, cmp_w1, cmp_b1, cmp_w2, cmp_pos, w_mem_kv, w_out,
                   ln1_g, ln1_b, w_up, conv_w, conv_b, w_down, ln2_g, ln2_b):
    d = w_in.shape[0]
    nq = NSA_HEADS * HEAD_DIM
    kvw = 2 * NSA_KV_HEADS * HEAD_DIM
    o_g = nq + 3 * kvw
    o_fox = o_g + 3 * NSA_HEADS
    o_f = o_fox + 3 * FOX_HEADS * HEAD_DIM
    o_qm = o_f + FOX_HEADS
    w_main = jnp.concatenate([w_in[:, :o_g], w_in[:, o_fox:o_f], w_in[:, o_qm:]], 1).astype(BF)
    per_group = 3 * NSA_REP
    zpad = lambda n: jnp.zeros((d, n), w_in.dtype)
    w_small = jnp.concatenate([
        w_in[:, o_g:o_g + per_group], zpad(LANE - per_group),
        w_in[:, o_g + per_group:o_fox], zpad(LANE - per_group),
        w_in[:, o_f:o_qm], zpad(LANE - FOX_HEADS)], 1).astype(BF)
    bpad = lambda n: jnp.zeros((n,), F32)
    b_small = jnp.concatenate([
        b_gate[:per_group], bpad(LANE - per_group), b_gate[per_group:], bpad(LANE - per_group),
        b_forget, bpad(LANE - FOX_HEADS)])[None, :].astype(F32)
    kdim = CMP_STRIDE * HEAD_DIM
    w1cat = jnp.concatenate([cmp_w1[:, :CMP_STRIDE].reshape(2, kdim, CMP_HIDDEN),
                             cmp_w1[:, CMP_STRIDE:].reshape(2, kdim, CMP_HIDDEN)], 2).astype(BF)
    pe = jnp.concatenate([cmp_pos[:, :CMP_STRIDE].reshape(2, 1, kdim),
                          cmp_pos[:, CMP_STRIDE:].reshape(2, 1, kdim),
                          jnp.zeros((2, SUBLANE - 2, kdim), F32)], 1)
    cw = (w1cat, pe, cmp_b1[:, None, :], cmp_w2.astype(BF))
    conv_w8 = jnp.concatenate([conv_w, jnp.zeros((SUBLANE - CONV_W, conv_w.shape[1]), F32)], 0)
    fw = (w_up.astype(BF), conv_w8, conv_b[None, :], w_down.astype(BF), ln2_g[None, :], ln2_b[None, :])
    return dict(w_main=w_main, w_small=w_small, b_small=b_small, cw=cw,
                w_mem_kv=w_mem_kv.astype(BF), w_out=w_out.astype(BF),
                ln1_g=ln1_g[None, :], ln1_b=ln1_b[None, :], fw=fw)


def _prompt_layer(x, mem, lw, alpha):
    batch, seq, d = x.shape
    x2d = x.reshape(batch * seq, d)
    rc, rs = _rope_tables(jnp.tile(jnp.arange(seq, dtype=jnp.int32), batch))
    tm = min(512, seq)
    p_main, q_rot, sig, lsig, st_c, st_s, st_w, st_f = _input_projection(
        x2d, lw["w_main"], lw["w_small"], lw["b_small"], rc, rs, tm)
    kc_all = _compress_prompt(p_main, batch, seq, lw["cw"])
    o_nsa = _nsa_prompt(p_main, q_rot, kc_all, sig, batch, seq)
    decay = _decay_prompt(lsig, batch, seq)
    o_fox = _fox_prompt(p_main, decay, batch, seq)
    mlen = mem.shape[1]
    mem_kv = _matmul(mem.reshape(batch * mlen, d), lw["w_mem_kv"], min(256, batch * mlen), 512)
    o_mem = _mem_attention(p_main, mem_kv, batch, seq, min(2048, seq))
    h = _out_projection(o_nsa, o_fox, o_mem, x2d, lw["w_out"], lw["ln1_g"], lw["ln1_b"],
                        min(512, seq), alpha)
    y, tail = _ffn_prompt(h, lw["fw"], seq, tm, 512, alpha)
    kv_state = lambda st: st.reshape(batch, seq, 2, -1, HEAD_DIM)
    n_win = min(WINDOW, seq)
    tiles = seq // tm
    conv_state = tail.reshape(batch, tiles, SUBLANE, -1)[:, -1, SUBLANE - (CONV_W - 1):]
    states = (kv_state(st_c), kv_state(st_s), kv_state(st_f),
              lsig[:, :FOX_HEADS].reshape(batch, seq, FOX_HEADS),
              kv_state(st_w)[:, seq - n_win:],
              mem_kv.reshape(batch, mlen, 2, MEM_HEADS, HEAD_DIM), conv_state)
    return y.reshape(batch, seq, d), states


def _sample_layer(x, c_cmp, c_slc, c_fox, c_logf, c_swa, c_mem, s_conv, page_table, lw, alpha):
    dec_batch, s_len, d = x.shape
    n_pool, page = c_cmp.shape[0], c_cmp.shape[1]
    n_pages = page_table.shape[1]
    past = n_pages * page
    t_all = past + s_len
    assert s_len <= SROW and n_pages % PAGES_PER_STEP == 0
    assert (t_all // CMP_STRIDE) * CMP_STRIDE == past and past % SLC_BLOCK == 0
    n_blocks = -(-t_all // SLC_BLOCK)
    xp = jnp.pad(x, ((0, 0), (0, SROW - s_len), (0, 0))).reshape(dec_batch * SROW, d)
    pos = past + jnp.tile(jnp.arange(SROW, dtype=jnp.int32), dec_batch)
    rc, rs = _rope_tables(pos)
    p_s, qrot_s, sig_s, lsig_s = _input_projection(xp, lw["w_main"], lw["w_small"], lw["b_small"],
                                                   rc, rs, dec_batch * SROW)[:4]
    kvw = 2 * NSA_KV_HEADS * HEAD_DIM
    as_rows = lambda a: a.reshape(a.shape[0], -1, HEAD_DIM)
    kc_all = _compress_paged(as_rows(c_cmp), page_table, lw["cw"])
    part, ids = _nsa_sample_a(p_s, qrot_s, kc_all, as_rows(c_swa), sig_s, past, n_blocks)
    o_nsa = _nsa_sample_gather(as_rows(c_slc), page_table, ids, p_s, qrot_s, sig_s, part,
                               n_blocks, s_len)
    logf_t = jnp.transpose(c_logf, (2, 0, 1))
    o_fox = _fox_sample(c_fox, logf_t, page_table, p_s, lsig_s, s_len)
    o_mem = _mem_attention_rows(p_s, as_rows(c_mem))
    h = _out_projection(o_nsa, o_fox, o_mem, xp, lw["w_out"], lw["ln1_g"], lw["ln1_b"],
                        dec_batch * SROW, alpha)
    dff = s_conv.shape[-1]
    zrow = jnp.zeros((dec_batch, 1, dff), F32)
    fill1 = jnp.concatenate([s_conv[:, 1:2]] + [zrow] * (SROW - 1), 1).reshape(dec_batch * SROW, dff)
    fill2 = jnp.concatenate([s_conv[:, 0:1], s_conv[:, 1:2]] + [zrow] * (SROW - 2), 1)
    y, a = _ffn_sample(h, lw["fw"], fill1, fill2.reshape(dec_batch * SROW, dff), 512, alpha)

    def rows(arr):
        return arr.reshape(dec_batch, SROW, -1)[:, :s_len]
    cols = lambda c0, w: rows(p_s[:, c0 * LANE:c0 * LANE + w]).reshape(dec_batch, s_len, 2, -1, HEAD_DIM)
    new_kv_swa = cols(COL_KVW, kvw)
    new_swa = jnp.concatenate([c_swa, new_kv_swa], 1)[:, s_len:]
    conv_state = jnp.concatenate([s_conv, rows(a)], 1)[:, -(CONV_W - 1):]
    states = (cols(COL_KVC, kvw), cols(COL_KVS, kvw),
              cols(COL_FOX + FOX_HEADS, 2 * FOX_HEADS * HEAD_DIM),
              rows(lsig_s[:, :FOX_HEADS]), new_swa, conv_state)
    return rows(y), states


def kernel(x_prompt, x_sample, mem_prompt, cache_nsa_cmp, cache_nsa_slc, cache_fox_kv, cache_fox_logf, cache_nsa_swa, cache_mem, state_conv, page_table, w_in, b_gate, b_forget, cmp_w1, cmp_b1, cmp_w2, cmp_pos, w_mem_kv, w_out, ln1_g, ln1_b, w_up, conv_w, conv_b, w_down, ln2_g, ln2_b):
    depth = w_in.shape[0]
    alpha = float((2 * depth) ** 0.25)
    yp, ys = x_prompt, x_sample
    acc_p = [[] for _ in range(7)]
    acc_s = [[] for _ in range(6)]
    for l in range(depth):
        lw = _layer_weights(w_in[l], b_gate[l], b_forget[l], cmp_w1[l], cmp_b1[l], cmp_w2[l],
                            cmp_pos[l], w_mem_kv[l], w_out[l], ln1_g[l], ln1_b[l], w_up[l],
                            conv_w[l], conv_b[l], w_down[l], ln2_g[l], ln2_b[l])
        yp, st_p = _prompt_layer(yp, mem_prompt, lw, alpha)
        ys, st_s = _sample_layer(ys, cache_nsa_cmp[l], cache_nsa_slc[l], cache_fox_kv[l],
                                 cache_fox_logf[l], cache_nsa_swa[l], cache_mem[l], state_conv[l],
                                 page_table, lw, alpha)
        for lst, a in zip(acc_p, st_p):
            lst.append(a)
        for lst, a in zip(acc_s, st_s):
            lst.append(a)
    outs_p = [jnp.stack(a, 0) for a in acc_p]
    outs_s = [jnp.stack(a, 0) for a in acc_s]
    return (yp, ys, *outs_p, *outs_s)
```
